```python
import math
import jax
import jax.numpy as jnp
from jax import lax
import numpy as np

D_MODEL = 1024
BATCH = 8
SEQ = 4096
DEPTH = 1

CHUNK = 64
N_META = 16
Q_BLOCK = 128
EPS = 1e-6
NEG_INF = -1e30

MIX_WIDTH = D_MODEL
DA_WIDTH = MIX_WIDTH // 2
DA_HEAD_DIM = 128
DA_HALF = DA_HEAD_DIM // 2
DA_HEADS = DA_WIDTH // DA_HEAD_DIM
DN_WIDTH = MIX_WIDTH - DA_WIDTH
DN_DK = 128
DN_DV = 128
DN_HEADS = DN_WIDTH // DN_DV
DN_CONV = 4
N_BUCKETS = 32
MAX_DISTANCE = 128
D_FF = ((8 * D_MODEL // 3 + 127) // 128) * 128
FFN_CONV = 3

IN_SPLITS = (DA_HEADS * DA_HEAD_DIM, DA_HEADS * DA_HEAD_DIM, DA_WIDTH,
             DN_HEADS * DN_DK, DN_HEADS * DN_DK, DN_WIDTH, DN_WIDTH, DN_HEADS, DN_HEADS)
IN_COLS = sum(IN_SPLITS)
IN_OFFSETS = tuple(int(o) for o in np.cumsum(IN_SPLITS)[:-1])

kernel_name = "hybrid_diffattn_gdn_convffn"


def rms_norm(x, g):
    xf = x.astype(jnp.float32)
    y = xf * lax.rsqrt(jnp.mean(xf * xf, axis=-1, keepdims=True) + EPS)
    return (y * g.astype(jnp.float32)).astype(x.dtype)


def l2norm(t):
    return t * lax.rsqrt(jnp.sum(t * t, axis=-1, keepdims=True) + EPS)


def causal_dwconv(x, w):
    K, C = w.shape
    return lax.conv_general_dilated(
        x, w.astype(x.dtype)[:, None, :], window_strides=(1,), padding=[(K - 1, 0)],
        dimension_numbers=('NWC', 'WIO', 'NWC'), feature_group_count=C)


def chunk_ids(pos):
    return jnp.where(pos < N_META, 0, 1 + (pos - N_META) // CHUNK)


def t5_bucket(rel):
    nb = N_BUCKETS // 2
    max_exact = nb // 2
    ret = jnp.where(rel > 0, nb, 0)
    n = jnp.abs(rel)
    nf = jnp.maximum(n, 1).astype(jnp.float32)
    large = max_exact + (jnp.log(nf / max_exact) / math.log(MAX_DISTANCE / max_exact)
                         * (nb - max_exact)).astype(jnp.int32)
    large = jnp.minimum(large, nb - 1)
    return ret + jnp.where(n < max_exact, n, large)


def diff_attention(q, k, v, lam, subln_g, rel_bias, lambda_init):
    B, Lp = q.shape[:2]
    H = DA_HEADS
    nblk = Lp // Q_BLOCK
    k = k.reshape(B, Lp, H, 2, DA_HALF)
    v = v.reshape(B, Lp, H, DA_HEAD_DIM)
    qb = jnp.moveaxis(q.reshape(B, nblk, Q_BLOCK, H, 2, DA_HALF), 1, 0)
    kpos = jnp.arange(Lp, dtype=jnp.int32)
    kcid = chunk_ids(kpos)
    table = rel_bias.astype(jnp.float32)
    scale = DA_HALF ** -0.5

    def block(args):
        qblk, bi = args
        qpos = bi * Q_BLOCK + jnp.arange(Q_BLOCK, dtype=jnp.int32)
        visible = chunk_ids(qpos)[:, None] >= kcid[None, :]
        bias = jnp.moveaxis(table[t5_bucket(kpos[None, :] - qpos[:, None])], -1, 0)
        s = jnp.einsum('bqhcd,bkhcd->bchqk', qblk, k).astype(jnp.float32) * scale + bias
        s = jnp.where(visible, s, NEG_INF)
        p = jax.nn.softmax(s, axis=-1)
        a = p[:, 0] - lam * p[:, 1]
        return jnp.einsum('bhqk,bkhd->bqhd', a.astype(v.dtype), v)

    o = lax.map(block, (qb, jnp.arange(nblk, dtype=jnp.int32)))
    o = jnp.moveaxis(o, 0, 1).reshape(B, Lp, H, DA_HEAD_DIM)
    o = rms_norm(o, subln_g) * (1.0 - lambda_init)
    return o.reshape(B, Lp, H * DA_HEAD_DIM)


def gated_deltanet(q, k, v, z, b_raw, a_raw, conv_w, A_log, dt_bias, norm_g):
    f32 = jnp.float32
    B, Lp = q.shape[:2]
    C = CHUNK
    N = Lp // C
    H = DN_HEADS
    qkv = jax.nn.silu(causal_dwconv(jnp.concatenate([q, k, v], axis=-1), conv_w))
    q, k, v = jnp.split(qkv.astype(f32), [H * DN_DK, 2 * H * DN_DK], axis=-1)
    q = l2norm(q.reshape(B, Lp, H, DN_DK)) * DN_DK ** -0.5
    k = l2norm(k.reshape(B, Lp, H, DN_DK))
    v = v.reshape(B, Lp, H, DN_DV)
    beta = jax.nn.sigmoid(b_raw.astype(f32))
    g = -jnp.exp(A_log.astype(f32)) * jax.nn.softplus(a_raw.astype(f32) + dt_bias.astype(f32))

    def to_chunks(t):
        t = t.reshape((B, N, C) + t.shape[2:])
        return jnp.moveaxis(t, 3, 1)

    q, k, v, beta, g = (to_chunks(t) for t in (q, k, v, beta, g))
    G = jnp.cumsum(g, axis=-1)
    tri = jnp.tril(jnp.ones((C, C), dtype=bool))
    strict = jnp.tril(jnp.ones((C, C), dtype=bool), -1)
    decay = jnp.exp(jnp.where(tri, G[..., :, None] - G[..., None, :], -jnp.inf))
    kk = jnp.einsum('bhnid,bhnjd->bhnij', k, k)
    M = jnp.where(strict, beta[..., None] * kk * decay, 0.0)
    T_sys = M + jnp.eye(C, dtype=f32)
    rhs = jnp.concatenate([v * beta[..., None], k * (beta * jnp.exp(G))[..., None]], axis=-1)
    sol = lax.linalg.triangular_solve(T_sys, rhs, left_side=True, lower=True, unit_diagonal=True)
    U, W = sol[..., :DN_DV], sol[..., DN_DV:]
    Aqk = jnp.einsum('bhnid,bhnjd->bhnij', q, k) * decay
    q_g = q * jnp.exp(G)[..., None]
    k_g = k * jnp.exp(G[..., -1:] - G)[..., None]
    g_last = jnp.exp(G[..., -1])

    def step(S, xs):
        qc, kc, uc, wc, ac, gc = xs
        v_new = uc - jnp.einsum('bhck,bhkv->bhcv', wc, S)
        o = jnp.einsum('bhck,bhkv->bhcv', qc, S) + jnp.einsum('bhij,bhjv->bhiv', ac, v_new)
        S = S * gc[..., None, None] + jnp.einsum('bhck,bhcv->bhkv', kc, v_new)
        return S, o

    xs = tuple(jnp.moveaxis(t, 2, 0) for t in (q_g, k_g, U, W, Aqk, g_last))
    S0 = jnp.zeros((B, H, DN_DK, DN_DV), f32)
    _, o = lax.scan(step, S0, xs)
    o = o.transpose(1, 0, 3, 2, 4).reshape(B, Lp, H, DN_DV)
    o = rms_norm(o, norm_g) * jax.nn.silu(z.astype(f32).reshape(B, Lp, H, DN_DV))
    return o.reshape(B, Lp, H * DN_DV).astype(z.dtype)


def conv_gated_ffn(u, w_up, conv_w, conv_b, w_down):
    hup = causal_dwconv(u @ w_up, conv_w) + conv_b.astype(u.dtype)
    gate, val = jnp.split(hup, 2, axis=-1)
    return (jax.nn.silu(gate) * val) @ w_down


def setup_inputs(seed: int = 0) -> dict:
    key = jax.random.key(seed)
    ks = jax.random.split(key, 20)
    f = jnp.float32
    L = DEPTH
    x = jax.random.normal(ks[0], (BATCH, SEQ, D_MODEL), f)
    meta_tokens = jax.random.normal(ks[1], (N_META, D_MODEL), f)
    rel_bias = 0.5 * jax.random.normal(ks[2], (N_BUCKETS, DA_HEADS), f)
    norm1_g = 1.0 + 0.02 * jax.random.normal(ks[3], (L, D_MODEL), f)
    w_in = jax.random.normal(ks[4], (L, D_MODEL, IN_COLS), f) * D_MODEL ** -0.5
    da_lambda = 0.1 * jax.random.normal(ks[5], (L, 4, DA_HALF), f)
    da_subln_g = 1.0 + 0.02 * jax.random.normal(ks[6], (L, DA_HEAD_DIM), f)
    dn_conv_w = jax.random.normal(ks[7], (L, DN_CONV, 2 * DN_HEADS * DN_DK + DN_WIDTH), f) * DN_CONV ** -0.5
    dn_A_log = jnp.log(jax.random.uniform(ks[8], (L, DN_HEADS), f, 1.0, 16.0))
    dt = jnp.exp(jax.random.uniform(ks[9], (L, DN_HEADS), f, math.log(1e-3), math.log(1e-1)))
    dn_dt_bias = dt + jnp.log(-jnp.expm1(-dt))
    dn_norm_g = 1.0 + 0.02 * jax.random.normal(ks[10], (L, DN_DV), f)
    w_out = jax.random.normal(ks[11], (L, MIX_WIDTH, D_MODEL), f) * MIX_WIDTH ** -0.5
    norm2_g = 1.0 + 0.02 * jax.random.normal(ks[12], (L, D_MODEL), f)
    w_up = jax.random.normal(ks[13], (L, D_MODEL, 2 * D_FF), f) * D_MODEL ** -0.5
    ffn_conv_w = jax.random.normal(ks[14], (L, FFN_CONV, 2 * D_FF), f) * FFN_CONV ** -0.5
    ffn_conv_b = 0.02 * jax.random.normal(ks[15], (L, 2 * D_FF), f)
    w_down = jax.random.normal(ks[16], (L, D_FF, D_MODEL), f) * D_FF ** -0.5
    final_norm_g = 1.0 + 0.02 * jax.random.normal(ks[17], (D_MODEL,), f)
    return {"x": x, "meta_tokens": meta_tokens, "rel_bias": rel_bias, "norm1_g": norm1_g,
            "w_in": w_in, "da_lambda": da_lambda, "da_subln_g": da_subln_g,
            "dn_conv_w": dn_conv_w, "dn_A_log": dn_A_log, "dn_dt_bias": dn_dt_bias,
            "dn_norm_g": dn_norm_g, "w_out": w_out, "norm2_g": norm2_g, "w_up": w_up,
            "ffn_conv_w": ffn_conv_w, "ffn_conv_b": ffn_conv_b, "w_down": w_down,
            "final_norm_g": final_norm_g}


def reference(x, meta_tokens, rel_bias, norm1_g, w_in, da_lambda, da_subln_g, dn_conv_w,
              dn_A_log, dn_dt_bias, dn_norm_g, w_out, norm2_g, w_up, ffn_conv_w, ffn_conv_b,
              w_down, final_norm_g):
    B, S = x.shape[0], x.shape[1]
    pad = Q_BLOCK - N_META
    meta = jnp.broadcast_to(meta_tokens.astype(x.dtype)[None], (B, N_META, D_MODEL))
    h = jnp.concatenate([meta, x, jnp.zeros((B, pad, D_MODEL), x.dtype)], axis=1)
    for l in range(DEPTH):
        lambda_init = 0.8 - 0.6 * math.exp(-0.3 * l)
        u = rms_norm(h, norm1_g[l])
        proj = u @ w_in[l]
        da_q, da_k, da_v, dn_q, dn_k, dn_v, dn_z, dn_b, dn_a = jnp.split(proj, IN_OFFSETS, axis=-1)
        lv = da_lambda[l].astype(jnp.float32)
        lam = jnp.exp(jnp.sum(lv[0] * lv[1])) - jnp.exp(jnp.sum(lv[2] * lv[3])) + lambda_init
        o_da = diff_attention(da_q, da_k, da_v, lam, da_subln_g[l], rel_bias, lambda_init)
        o_dn = gated_deltanet(dn_q, dn_k, dn_v, dn_z, dn_b, dn_a, dn_conv_w[l], dn_A_log[l],
                              dn_dt_bias[l], dn_norm_g[l])
        h = h + jnp.concatenate([o_da.astype(h.dtype), o_dn.astype(h.dtype)], axis=-1) @ w_out[l]
        u = rms_norm(h, norm2_g[l])
        h = h + conv_gated_ffn(u, w_up[l], ffn_conv_w[l], ffn_conv_b[l], w_down[l])
    h = rms_norm(h, final_norm_g)
    return h[:, N_META:N_META + S]
```

```python
import functools
import math

import numpy as np
import jax
import jax.numpy as jnp
from jax import lax
from jax.experimental import pallas as pl
from jax.experimental.pallas import tpu as pltpu

f32 = jnp.float32
bf16 = jnp.bfloat16

D_MODEL = 1024
CHUNK = 64
N_META = 16
EPS = 1e-6
NEG_INF = -1e30
LAMBDA_INIT = 0.8 - 0.6 * math.exp(-0.3 * 0)

DA_HEADS = 4
DA_HEAD_DIM = 128
DA_HALF = 64
DA_WIDTH = DA_HEADS * DA_HEAD_DIM
DN_HEADS = 4
DN_DK = 128
DN_DV = 128
DN_WIDTH = DN_HEADS * DN_DV
DN_CONV = 4
N_BUCKETS = 32
MAX_DISTANCE = 128
D_FF = 2816
FFN_CONV = 3

N_MAIN = 7 * 512
LANES = 128
SUBLANES = 8
MP = 128
TQ = 256
TK = 256
PROJ_TM = 256
GDN_ROWS = 256
FFN_TM = 256
FFN_CW = 256
HALO_ROWS = 16
VMEM_LIMIT = 52 * 1024 * 1024


def _const_spec(shape):
    nd = len(shape)
    return pl.BlockSpec(shape, lambda *_: (0,) * nd, pipeline_mode=pl.Buffered(1))


def _prep_kernel(far_ref, table_ref, lam_in_ref, bnear_in, bmeta_in, bmm_in,
                 bnear_out, bmeta_out, bmm_out, lam_out):
    h = pl.program_id(0)
    c_far = table_ref[far_ref[0], h]

    def lookup(bkt):
        out = jnp.full(bkt.shape, NEG_INF, f32)
        for b in range(N_BUCKETS):
            out = jnp.where(bkt == b, table_ref[b, h] - c_far, out)
        return out

    for i in range(2):
        bnear_out[i] = lookup(bnear_in[i])
        bmeta_out[i] = lookup(bmeta_in[i])
    bmm_out[...] = lookup(bmm_in[...])
    lv = lam_in_ref[...]
    s1 = jnp.sum(lv[0:1] * lv[1:2], axis=-1, keepdims=True)
    s2 = jnp.sum(lv[2:3] * lv[3:4], axis=-1, keepdims=True)
    lam = jnp.exp(s1) - jnp.exp(s2) + LAMBDA_INIT
    lam_out[...] = jnp.broadcast_to(lam, lam_out.shape)


def _t5_bucket(rel):
    nb = N_BUCKETS // 2
    max_exact = nb // 2
    ret = jnp.where(rel > 0, nb, 0)
    n = jnp.abs(rel)
    nf = jnp.maximum(n, 1).astype(jnp.float32)
    large = max_exact + (jnp.log(nf / max_exact) / math.log(MAX_DISTANCE / max_exact)
                         * (nb - max_exact)).astype(jnp.int32)
    large = jnp.minimum(large, nb - 1)
    return ret + jnp.where(n < max_exact, n, large)


def _bias_tiles(rel_bias, da_lambda):
    r = jnp.arange(TK, dtype=jnp.int32)[:, None]
    c = jnp.arange(TQ, dtype=jnp.int32)[None, :]
    diag = jnp.where((r // CHUNK) <= (c // CHUNK), _t5_bucket(r - c), -1)
    prev = _t5_bucket(r - c - TK)
    bnear = jnp.stack([diag, prev]).astype(jnp.int32)
    far = _t5_bucket(jnp.full((1,), -(TK + 1), jnp.int32)).astype(jnp.int32)
    rm = jnp.arange(MP, dtype=jnp.int32)[:, None]
    valid = rm >= (MP - N_META)
    kpos = rm - (MP - N_META)
    m0 = jnp.where(valid, _t5_bucket(kpos - (N_META + c)), -1)
    m1 = jnp.where(valid, jnp.broadcast_to(far[0], (MP, TQ)), -1)
    bmeta = jnp.stack([m0, m1]).astype(jnp.int32)
    cm = jnp.arange(MP, dtype=jnp.int32)[None, :]
    bmm = jnp.where(valid, _t5_bucket(rm - cm), -1).astype(jnp.int32)

    smem = pl.BlockSpec(memory_space=pltpu.SMEM)
    return pl.pallas_call(
        _prep_kernel,
        grid=(DA_HEADS,),
        in_specs=[smem, smem,
                  pl.BlockSpec((4, DA_HALF), lambda h: (0, 0)),
                  pl.BlockSpec((2, TK, TQ), lambda h: (0, 0, 0)),
                  pl.BlockSpec((2, MP, TQ), lambda h: (0, 0, 0)),
                  pl.BlockSpec((MP, MP), lambda h: (0, 0))],
        out_specs=[pl.BlockSpec((None, 2, TK, TQ), lambda h: (h, 0, 0, 0)),
                   pl.BlockSpec((None, 2, MP, TQ), lambda h: (h, 0, 0, 0)),
                   pl.BlockSpec((None, MP, MP), lambda h: (h, 0, 0)),
                   pl.BlockSpec((SUBLANES, LANES), lambda h: (0, 0))],
        out_shape=[jax.ShapeDtypeStruct((DA_HEADS, 2, TK, TQ), f32),
                   jax.ShapeDtypeStruct((DA_HEADS, 2, MP, TQ), f32),
                   jax.ShapeDtypeStruct((DA_HEADS, MP, MP), f32),
                   jax.ShapeDtypeStruct((SUBLANES, LANES), f32)],
        name="prep",
    )(far, rel_bias.astype(f32), da_lambda.astype(f32), bnear, bmeta, bmm)


def _proj_kernel(x_ref, g_ref, w_ref, ws_ref, q_ref, k_ref, v_ref, dq_ref, dk_ref, dv_ref,
                 dz_ref, ba_ref):
    x = x_ref[...]
    ms = jnp.mean(x * x, axis=-1, keepdims=True)
    u = (x * lax.rsqrt(ms + EPS) * g_ref[...]).astype(bf16)
    outs = (q_ref, k_ref, v_ref, dq_ref, dk_ref, dv_ref, dz_ref)
    for j, o in enumerate(outs):
        r = jnp.dot(u, w_ref[:, j * 512:(j + 1) * 512], preferred_element_type=f32)
        if j == 0:
            r = r * (DA_HALF ** -0.5)
        o[...] = r.astype(o.dtype)
    ba_ref[...] = jnp.dot(u, ws_ref[...], preferred_element_type=f32)


def _proj(x3, g1, w_main, w_small, tm):
    nb, rows, _ = x3.shape
    row_spec = lambda w: pl.BlockSpec((None, tm, w), lambda b, t: (b, t, 0))
    outs = [jax.ShapeDtypeStruct((nb, rows, 512), bf16)] * 7 + [
        jax.ShapeDtypeStruct((nb, rows, LANES), f32)]
    return pl.pallas_call(
        _proj_kernel,
        grid=(nb, rows // tm),
        in_specs=[row_spec(D_MODEL), _const_spec((1, D_MODEL)),
                  _const_spec((D_MODEL, N_MAIN)), _const_spec((D_MODEL, LANES))],
        out_specs=[row_spec(512)] * 7 + [row_spec(LANES)],
        out_shape=outs,
        compiler_params=pltpu.CompilerParams(
            dimension_semantics=("parallel", "parallel"), vmem_limit_bytes=VMEM_LIMIT),
        name="proj",
    )(x3, g1, w_main, w_small)


def _dot_nt(a, b):
    return lax.dot_general(a, b, (((1,), (1,)), ((), ())), preferred_element_type=f32)


def _dot_tn(a, b):
    return lax.dot_general(a, b, (((0,), (0,)), ((), ())), preferred_element_type=f32)


def _stack_components(q):
    qt = q.astype(f32).T
    row = lax.broadcasted_iota(jnp.int32, qt.shape, 0)
    q0 = jnp.where(row < DA_HALF, qt, 0.0)
    q1 = jnp.where(row >= DA_HALF, qt, 0.0)
    return jnp.concatenate([q0, q1], axis=1).astype(bf16)


def _softmax_step(carry, s, v):
    m, l, acc = carry
    m_new = jnp.maximum(m, jnp.max(s, axis=0, keepdims=True))
    alpha = jnp.exp(m - m_new)
    p = jnp.exp(s - m_new)
    l = alpha * l + jnp.sum(p, axis=0, keepdims=True)
    acc = alpha * acc + _dot_tn(v, p.astype(bf16))
    return m_new, l, acc


def _attn_finish(carry, lam, g, tq):
    m, l, acc = carry
    o = acc[:, :tq] / l[:, :tq] - lam * (acc[:, tq:] / l[:, tq:])
    ms = jnp.mean(o * o, axis=0, keepdims=True)
    y = o * lax.rsqrt(ms + EPS) * g * (1.0 - LAMBDA_INIT)
    return y.T


def _attn_init(tq):
    return (jnp.full((1, 2 * tq), -3e38, f32), jnp.zeros((1, 2 * tq), f32),
            jnp.zeros((DA_HEAD_DIM, 2 * tq), f32))


def _both(b):
    return jnp.concatenate([b, b], axis=1)


def _attn_kernel(lam_ref, q_ref, k_ref, v_ref, km_ref, vm_ref, bnear_ref, bmeta_ref, g_ref,
                 o_ref):
    i = pl.program_id(2)
    qz = _stack_components(q_ref[...])

    def tile(j):
        start = pl.multiple_of(j * TK, TK)
        return k_ref[pl.ds(start, TK), :], v_ref[pl.ds(start, TK), :]

    def far_body(j, carry):
        kt, vt = tile(j)
        return _softmax_step(carry, jnp.dot(kt, qz, preferred_element_type=f32), vt)

    carry = lax.fori_loop(0, i - 1, far_body, _attn_init(TQ))
    kt, vt = tile(jnp.maximum(i - 1, 0))
    gate = jnp.where(i >= 1, 0.0, NEG_INF).astype(f32)
    s = jnp.dot(kt, qz, preferred_element_type=f32) + _both(bnear_ref[1] + gate)
    carry = _softmax_step(carry, s, vt)
    kt, vt = tile(i)
    s = jnp.dot(kt, qz, preferred_element_type=f32) + _both(bnear_ref[0])
    carry = _softmax_step(carry, s, vt)
    s = jnp.dot(km_ref[...], qz, preferred_element_type=f32) + _both(bmeta_ref[jnp.minimum(i, 1)])
    carry = _softmax_step(carry, s, vm_ref[...])
    o_ref[...] = _attn_finish(carry, lam_ref[0], g_ref[...], TQ).astype(o_ref.dtype)


def _attn_meta_kernel(lam_ref, q_ref, km_ref, vm_ref, bmm_ref, g_ref, o_ref):
    qz = _stack_components(q_ref[...])
    s = jnp.dot(km_ref[...], qz, preferred_element_type=f32) + _both(bmm_ref[...])
    carry = _softmax_step(_attn_init(MP), s, vm_ref[...])
    o_ref[...] = _attn_finish(carry, lam_ref[0], g_ref[...], MP).astype(o_ref.dtype)


def _attention(lam1, q, k, v, km, vm, bnear, bmeta, g_tile):
    nb, s_len, _ = q.shape
    smem = pl.BlockSpec(memory_space=pltpu.SMEM)
    return pl.pallas_call(
        _attn_kernel,
        grid=(nb, DA_HEADS, s_len // TQ),
        in_specs=[smem,
                  pl.BlockSpec((None, TQ, DA_HEAD_DIM), lambda b, h, i: (b, i, h)),
                  pl.BlockSpec((None, s_len, DA_HEAD_DIM), lambda b, h, i: (b, 0, h)),
                  pl.BlockSpec((None, s_len, DA_HEAD_DIM), lambda b, h, i: (b, 0, h)),
                  pl.BlockSpec((MP, DA_HEAD_DIM), lambda b, h, i: (0, h)),
                  pl.BlockSpec((MP, DA_HEAD_DIM), lambda b, h, i: (0, h)),
                  pl.BlockSpec((None, 2, TK, TQ), lambda b, h, i: (h, 0, 0, 0)),
                  pl.BlockSpec((None, 2, MP, TQ), lambda b, h, i: (h, 0, 0, 0)),
                  pl.BlockSpec((DA_HEAD_DIM, TQ), lambda b, h, i: (0, 0))],
        out_specs=pl.BlockSpec((None, TQ, DA_HEAD_DIM), lambda b, h, i: (b, i, h)),
        out_shape=jax.ShapeDtypeStruct((nb, s_len, DA_WIDTH), bf16),
        compiler_params=pltpu.CompilerParams(
            dimension_semantics=("parallel", "parallel", "arbitrary"),
            vmem_limit_bytes=VMEM_LIMIT),
        name="attn",
    )(lam1, q, k, v, km, vm, bnear, bmeta, g_tile)


def _attention_meta(lam1, qm, km, vm, bmm, g_tile):
    smem = pl.BlockSpec(memory_space=pltpu.SMEM)
    head = pl.BlockSpec((MP, DA_HEAD_DIM), lambda h: (0, h))
    return pl.pallas_call(
        _attn_meta_kernel,
        grid=(DA_HEADS,),
        in_specs=[smem, head, head, head,
                  pl.BlockSpec((None, MP, MP), lambda h: (h, 0, 0)),
                  pl.BlockSpec((DA_HEAD_DIM, MP), lambda h: (0, 0))],
        out_specs=head,
        out_shape=jax.ShapeDtypeStruct((MP, DA_WIDTH), bf16),
        name="attn_meta",
    )(lam1, qm, km, vm, bmm, g_tile)


GDN_R = DN_HEADS * CHUNK
GDN_LEVELS = (2, 4, 8, 16, 32)
GDN_QKV = 3 * DN_WIDTH


def _gdn_masks():
    r = np.arange(GDN_R)[:, None]
    c = np.arange(GDN_R)[None, :]
    same = (r // CHUNK) == (c // CHUNK)
    tri = same & (r >= c)
    strict = same & (r > c)
    eye = (r == c)
    lv = [strict & ((r // 2) == (c // 2))]
    for s in GDN_LEVELS:
        lv.append(strict & ((r // (2 * s)) == (c // (2 * s))) & ((r & s) != 0) & ((c & s) == 0))
    return np.stack([tri, strict, eye] + lv).astype(np.float32)


def _sigmoid(x):
    return 1.0 / (1.0 + jnp.exp(-x))


def _split3(x):
    hi = x.astype(bf16)
    r1 = x - hi.astype(f32)
    mid = r1.astype(bf16)
    lo = (r1 - mid.astype(f32)).astype(bf16)
    return hi, mid, lo


def _bdot(a, b):
    return jnp.dot(a.astype(bf16), b.astype(bf16), preferred_element_type=f32)


def _gdn_prepare(xq, xk, xv, ba, cbuf, cw_ref, alog_ref, dtb_ref, mask_ref, lcum_ref):
    cbuf[SUBLANES:SUBLANES + CHUNK, :] = jnp.concatenate([xq, xk, xv], axis=1).astype(f32)
    y = cw_ref[DN_CONV - 1:DN_CONV, :] * cbuf[SUBLANES:SUBLANES + CHUNK, :]
    for d in range(1, DN_CONV):
        y = y + cw_ref[DN_CONV - 1 - d:DN_CONV - d, :] * cbuf[SUBLANES - d:SUBLANES - d + CHUNK, :]
    cbuf[0:SUBLANES, :] = cbuf[CHUNK:CHUNK + SUBLANES, :]
    y = y * _sigmoid(y)

    def stack(off):
        return jnp.concatenate(
            [y[:, off + h * DN_DK: off + (h + 1) * DN_DK] for h in range(DN_HEADS)], axis=0)

    qs, ks, vs = stack(0), stack(DN_WIDTH), stack(2 * DN_WIDTH)
    qn = qs * lax.rsqrt(jnp.sum(qs * qs, axis=-1, keepdims=True) + EPS) * (DN_DK ** -0.5)
    kn = ks * lax.rsqrt(jnp.sum(ks * ks, axis=-1, keepdims=True) + EPS)

    beta_t = _sigmoid(ba)
    xg = ba + dtb_ref[...]
    softplus = jnp.maximum(xg, 0.0) + jnp.log1p(jnp.exp(-jnp.abs(xg)))
    g_t = -jnp.exp(alog_ref[...]) * softplus
    lcum = lcum_ref[...]
    gcum = sum(jnp.dot(lcum, part, preferred_element_type=f32) for part in _split3(g_t))
    gcum_t = gcum.T
    col = lambda a, j: a[:, j:j + 1]
    beta_c = jnp.concatenate([col(beta_t, h) for h in range(DN_HEADS)], axis=0)
    g_c = jnp.concatenate([col(gcum, DN_HEADS + h) for h in range(DN_HEADS)], axis=0)
    g_l = jnp.concatenate([gcum_t[DN_HEADS + h:DN_HEADS + h + 1, :]
                           for h in range(DN_HEADS)], axis=1)
    g_last = [gcum[CHUNK - 1:CHUNK, DN_HEADS + h:DN_HEADS + h + 1] for h in range(DN_HEADS)]
    g_last_c = jnp.concatenate([jnp.broadcast_to(gl, (CHUNK, 1)) for gl in g_last], axis=0)

    tri = mask_ref[0] > 0.5
    decay = jnp.exp(jnp.where(tri, g_c - g_l, -jnp.inf))
    kn_b = kn.astype(bf16)
    kk = _dot_nt(kn_b, kn_b)
    m_mat = mask_ref[1] * (beta_c * kk * decay)
    x_inv = mask_ref[2] - m_mat * mask_ref[3]
    for lvl in range(len(GDN_LEVELS)):
        c_mat = m_mat * mask_ref[4 + lvl]
        x_inv = x_inv - _bdot(_bdot(x_inv, c_mat), x_inv)
    exp_g = jnp.exp(g_c)
    rhs = jnp.concatenate([vs * beta_c, kn * (beta_c * exp_g)], axis=1)
    sol = _bdot(x_inv, rhs)
    u_mat, w_mat = sol[:, :DN_DV], sol[:, DN_DV:]
    a_qk = _dot_nt(qn.astype(bf16), kn_b) * decay
    q_g = qn * exp_g
    k_g = kn * jnp.exp(g_last_c - g_c)
    return u_mat, w_mat, a_qk, q_g, k_g, [jnp.exp(gl) for gl in g_last]


def _gdn_recur(prep, z, s_ref, ng_ref):
    u_mat, w_mat, a_qk, q_g, k_g, g_last = prep
    v_new = []
    o_inter = []
    for h in range(DN_HEADS):
        rows = slice(h * CHUNK, (h + 1) * CHUNK)
        s_b = s_ref[h].astype(bf16)
        v_new.append(u_mat[rows] - jnp.dot(w_mat[rows].astype(bf16), s_b,
                                           preferred_element_type=f32))
        o_inter.append(jnp.dot(q_g[rows].astype(bf16), s_b, preferred_element_type=f32))
    v_new_s = jnp.concatenate(v_new, axis=0)
    o_s = jnp.concatenate(o_inter, axis=0) + _bdot(a_qk, v_new_s)
    outs = []
    for h in range(DN_HEADS):
        rows = slice(h * CHUNK, (h + 1) * CHUNK)
        s_ref[h] = s_ref[h] * g_last[h] + _dot_tn(k_g[rows].astype(bf16),
                                                  v_new[h].astype(bf16))
        o = o_s[rows]
        o = o * lax.rsqrt(jnp.mean(o * o, axis=-1, keepdims=True) + EPS) * ng_ref[...]
        zh = z[:, h * DN_DV:(h + 1) * DN_DV].astype(f32)
        outs.append(o * (zh * _sigmoid(zh)))
    return jnp.concatenate(outs, axis=1)


def _gdn_kernel(dq_ref, dk_ref, dv_ref, dz_ref, ba_ref, mq_ref, mk_ref, mv_ref, mz_ref, mba_ref,
                cw_ref, alog_ref, dtb_ref, ng_ref, mask_ref, lcum_ref,
                o_ref, om_ref, cbuf, s_ref):
    t = pl.program_id(1)
    consts = (cbuf, cw_ref, alog_ref, dtb_ref, mask_ref, lcum_ref)

    @pl.when(t == 0)
    def _():
        cbuf[0:SUBLANES, :] = jnp.zeros((SUBLANES, GDN_QKV), f32)
        s_ref[...] = jnp.zeros(s_ref.shape, f32)
        prep = _gdn_prepare(mq_ref[...], mk_ref[...], mv_ref[...], mba_ref[...], *consts)
        om_ref[...] = _gdn_recur(prep, mz_ref[...], s_ref, ng_ref).astype(om_ref.dtype)

    n_chunks = GDN_ROWS // CHUNK
    preps = []
    for c in range(n_chunks):
        rows = slice(c * CHUNK, (c + 1) * CHUNK)
        preps.append(_gdn_prepare(dq_ref[rows, :], dk_ref[rows, :], dv_ref[rows, :],
                                  ba_ref[rows, :], *consts))
    for c in range(n_chunks):
        rows = slice(c * CHUNK, (c + 1) * CHUNK)
        o_ref[rows, :] = _gdn_recur(preps[c], dz_ref[rows, :], s_ref, ng_ref).astype(o_ref.dtype)


def _gdn(dq, dk, dv, dz, ba, mq, mk, mv, mz, mba, conv_w, alog_row, dtb_row, ng_row):
    nb, s_len, _ = dq.shape
    row = lambda w: pl.BlockSpec((None, GDN_ROWS, w), lambda b, t: (b, t, 0))
    mrow = lambda w: pl.BlockSpec((CHUNK, w), lambda b, t: (MP // CHUNK - 1, 0))
    masks = jnp.asarray(_gdn_masks())
    lcum = jnp.asarray(np.tril(np.ones((CHUNK, CHUNK), np.float32))).astype(bf16)
    return pl.pallas_call(
        _gdn_kernel,
        grid=(nb, s_len // GDN_ROWS),
        in_specs=[row(DN_WIDTH)] * 4 + [row(LANES)] + [mrow(DN_WIDTH)] * 4 + [mrow(LANES)] + [
            _const_spec((DN_CONV, GDN_QKV)), _const_spec((1, LANES)), _const_spec((1, LANES)),
            _const_spec((1, DN_DV)), _const_spec(masks.shape), _const_spec((CHUNK, CHUNK))],
        out_specs=[row(DN_WIDTH), pl.BlockSpec((None, CHUNK, DN_WIDTH), lambda b, t: (b, 0, 0))],
        out_shape=[jax.ShapeDtypeStruct((nb, s_len, DN_WIDTH), bf16),
                   jax.ShapeDtypeStruct((nb, CHUNK, DN_WIDTH), bf16)],
        scratch_shapes=[pltpu.VMEM((SUBLANES + CHUNK, GDN_QKV), f32),
                        pltpu.VMEM((DN_HEADS, DN_DK, DN_DV), f32)],
        compiler_params=pltpu.CompilerParams(
            dimension_semantics=("parallel", "arbitrary"), vmem_limit_bytes=VMEM_LIMIT),
        name="gdn",
    )(dq, dk, dv, dz, ba, mq, mk, mv, mz, mba, conv_w, alog_row, dtb_row, ng_row, masks, lcum)


def _mix_and_norm(x, oda, odn, wout_ref, g2_ref):
    mix = jnp.concatenate([oda, odn], axis=1)
    h2 = x + jnp.dot(mix, wout_ref[...], preferred_element_type=f32)
    u2 = h2 * lax.rsqrt(jnp.mean(h2 * h2, axis=-1, keepdims=True) + EPS) * g2_ref[...]
    return h2, u2.astype(bf16)


def _ffn_halo_kernel(x_ref, oda_ref, odn_ref, wout_ref, g2_ref, wup_ref, halo_ref):
    _, u2 = _mix_and_norm(x_ref[...], oda_ref[...], odn_ref[...], wout_ref, g2_ref)
    halo_ref[...] = jnp.dot(u2, wup_ref[...], preferred_element_type=f32)


def _ffn_kernel(x_ref, oda_ref, odn_ref, halo_ref, wout_ref, g2_ref, wup_ref, cw_ref, cb_ref,
                wdown_ref, gf_ref, o_ref, hbuf_g, hbuf_v, carry_ref, acc_ref):
    tm = x_ref.shape[0]

    @pl.when(pl.program_id(1) == 0)
    def _():
        carry_ref[...] = halo_ref[...]

    h2, u2 = _mix_and_norm(x_ref[...], oda_ref[...], odn_ref[...], wout_ref, g2_ref)

    def conv_part(cols, hbuf):
        hbuf[0:SUBLANES, :] = carry_ref[:, cols]
        hup = jnp.dot(u2, wup_ref[:, cols], preferred_element_type=f32)
        hbuf[SUBLANES:SUBLANES + tm, :] = hup
        carry_ref[:, cols] = hup[tm - SUBLANES:, :]
        y = cb_ref[:, cols] + cw_ref[FFN_CONV - 1:FFN_CONV, cols] * hup
        for d in range(1, FFN_CONV):
            y = y + cw_ref[FFN_CONV - 1 - d:FFN_CONV - d, cols] * hbuf[SUBLANES - d:SUBLANES - d + tm, :]
        return y

    for c in range(D_FF // FFN_CW):
        gate = conv_part(slice(c * FFN_CW, (c + 1) * FFN_CW), hbuf_g)
        val = conv_part(slice(D_FF + c * FFN_CW, D_FF + (c + 1) * FFN_CW), hbuf_v)
        act = (gate * _sigmoid(gate) * val).astype(bf16)
        part = jnp.dot(act, wdown_ref[c * FFN_CW:(c + 1) * FFN_CW, :], preferred_element_type=f32)
        if c == 0:
            acc_ref[...] = part
        else:
            acc_ref[...] += part
    y = h2 + acc_ref[...]
    o_ref[...] = y * lax.rsqrt(jnp.mean(y * y, axis=-1, keepdims=True) + EPS) * gf_ref[...]


def _ffn_halo(xm, odam, odnm, w_out, g2, w_up):
    return pl.pallas_call(
        _ffn_halo_kernel,
        out_shape=jax.ShapeDtypeStruct((HALO_ROWS, 2 * D_FF), f32),
        compiler_params=pltpu.CompilerParams(vmem_limit_bytes=VMEM_LIMIT),
        name="ffn_halo",
    )(xm, odam, odnm, w_out, g2, w_up)


def _ffn(x, oda, odn, halo, w_out, g2, w_up, conv_w, conv_b, w_down, gf):
    nb, s_len, _ = x.shape
    tm = FFN_TM
    row = lambda w: pl.BlockSpec((None, tm, w), lambda b, t: (b, t, 0))
    return pl.pallas_call(
        _ffn_kernel,
        grid=(nb, s_len // tm),
        in_specs=[row(D_MODEL), row(DA_WIDTH), row(DN_WIDTH),
                  _const_spec((SUBLANES, 2 * D_FF)), _const_spec((D_MODEL, D_MODEL)),
                  _const_spec((1, D_MODEL)), _const_spec((D_MODEL, 2 * D_FF)),
                  _const_spec((FFN_CONV, 2 * D_FF)), _const_spec((1, 2 * D_FF)),
                  _const_spec((D_FF, D_MODEL)), _const_spec((1, D_MODEL))],
        out_specs=row(D_MODEL),
        out_shape=jax.ShapeDtypeStruct((nb, s_len, D_MODEL), f32),
        scratch_shapes=[pltpu.VMEM((SUBLANES + tm, FFN_CW), f32),
                        pltpu.VMEM((SUBLANES + tm, FFN_CW), f32),
                        pltpu.VMEM((SUBLANES, 2 * D_FF), f32),
                        pltpu.VMEM((tm, D_MODEL), f32)],
        compiler_params=pltpu.CompilerParams(
            dimension_semantics=("parallel", "arbitrary"), vmem_limit_bytes=VMEM_LIMIT),
        name="ffn",
    )(x, oda, odn, halo, w_out, g2, w_up, conv_w, conv_b, w_down, gf)


def kernel(x, meta_tokens, rel_bias, norm1_g, w_in, da_lambda, da_subln_g, dn_conv_w, dn_A_log,
           dn_dt_bias, dn_norm_g, w_out, norm2_g, w_up, ffn_conv_w, ffn_conv_b, w_down,
           final_norm_g):
    nb, s_len, _ = x.shape
    assert s_len % TQ == 0 and s_len % GDN_ROWS == 0 and s_len % FFN_TM == 0
    w_in0 = w_in[0]
    w_main = w_in0[:, :N_MAIN].astype(bf16)
    w_small = jnp.pad(w_in0[:, N_MAIN:], ((0, 0), (0, LANES - 2 * DN_HEADS))).astype(bf16)
    w_out_b = w_out[0].astype(bf16)
    w_up_b = w_up[0].astype(bf16)
    w_down_b = w_down[0].astype(bf16)
    g1 = norm1_g[0].reshape(1, D_MODEL).astype(f32)
    g2 = norm2_g[0].reshape(1, D_MODEL).astype(f32)
    gf = final_norm_g.reshape(1, D_MODEL).astype(f32)
    meta_pad = jnp.pad(meta_tokens.astype(x.dtype), ((MP - N_META, 0), (0, 0)))

    bnear, bmeta, bmm, lam_tile = _bias_tiles(rel_bias, da_lambda[0])
    lam1 = lam_tile[0, :1]

    q, k, v, dq, dk, dv, dz, ba = _proj(x, g1, w_main, w_small, PROJ_TM)
    mq, mk, mv, mdq, mdk, mdv, mdz, mba = [a[0] for a in _proj(meta_pad[None], g1, w_main, w_small, MP)]

    subln = da_subln_g[0].astype(f32)
    o_da = _attention(lam1, q, k, v, mk, mv, bnear, bmeta,
                      jnp.broadcast_to(subln[:, None], (DA_HEAD_DIM, TQ)))
    o_da_m = _attention_meta(lam1, mq, mk, mv, bmm,
                             jnp.broadcast_to(subln[:, None], (DA_HEAD_DIM, MP)))

    gate_row = lambda p: jnp.pad(p[0].astype(f32), (DN_HEADS, LANES - 2 * DN_HEADS)).reshape(1, LANES)
    o_dn, o_dn_m = _gdn(dq, dk, dv, dz, ba, mdq, mdk, mdv, mdz, mba,
                        dn_conv_w[0].astype(f32), gate_row(dn_A_log), gate_row(dn_dt_bias),
                        dn_norm_g[0].reshape(1, DN_DV).astype(f32))

    halo = _ffn_halo(meta_pad[MP - HALO_ROWS:], o_da_m[MP - HALO_ROWS:],
                     o_dn_m[0, CHUNK - HALO_ROWS:], w_out_b, g2, w_up_b)[HALO_ROWS - SUBLANES:]
    return _ffn(x, o_da, o_dn, halo, w_out_b, g2, w_up_b, ffn_conv_w[0].astype(f32),
                ffn_conv_b[0].reshape(1, 2 * D_FF).astype(f32), w_down_b, gf)
```

```python
import functools
import math

import numpy as np
import jax
import jax.numpy as jnp
from jax import lax
from jax.experimental import pallas as pl
from jax.experimental.pallas import tpu as pltpu

f32 = jnp.float32
bf16 = jnp.bfloat16

D_MODEL = 1024
CHUNK = 64
N_META = 16
EPS = 1e-6
NEG_INF = -1e30
LAMBDA_INIT = 0.8 - 0.6 * math.exp(-0.3 * 0)
LOG2E = math.log2(math.e)

DA_HEADS = 4
DA_HEAD_DIM = 128
DA_HALF = 64
DA_WIDTH = DA_HEADS * DA_HEAD_DIM
DN_HEADS = 4
DN_DK = 128
DN_DV = 128
DN_WIDTH = DN_HEADS * DN_DV
DN_CONV = 4
N_BUCKETS = 32
MAX_DISTANCE = 128
D_FF = 2816
FFN_CONV = 3

N_MAIN = 7 * 512
LANES = 128
SUBLANES = 8
MP = 128
TQ = 256
TK = 256
PROJ_TM = 256
GDN_ROWS = 256
FFN_TM = 256
FFN_CW = 256
FFN_NBUF = 2
HALO_ROWS = 16
VMEM_LIMIT = 52 * 1024 * 1024


def _const_spec(shape):
    nd = len(shape)
    return pl.BlockSpec(shape, lambda *_: (0,) * nd, pipeline_mode=pl.Buffered(1))


def _prep_kernel(far_ref, table_ref, lam_in_ref, bnear_in, bmeta_in, bmm_in,
                 bnear_out, bmeta_out, bmm_out, lam_out):
    h = pl.program_id(0)
    c_far = table_ref[far_ref[0], h]

    def lookup(bkt):
        out = jnp.full(bkt.shape, NEG_INF, f32)
        for b in range(N_BUCKETS):
            out = jnp.where(bkt == b, (table_ref[b, h] - c_far) * LOG2E, out)
        return out

    for i in range(2):
        bnear_out[i] = lookup(bnear_in[i])
        bmeta_out[i] = lookup(bmeta_in[i])
    bmm_out[...] = lookup(bmm_in[...])
    lv = lam_in_ref[...]
    s1 = jnp.sum(lv[0:1] * lv[1:2], axis=-1, keepdims=True)
    s2 = jnp.sum(lv[2:3] * lv[3:4], axis=-1, keepdims=True)
    lam = jnp.exp(s1) - jnp.exp(s2) + LAMBDA_INIT
    lam_out[...] = jnp.broadcast_to(lam, lam_out.shape)


def _t5_bucket(rel):
    nb = N_BUCKETS // 2
    max_exact = nb // 2
    ret = jnp.where(rel > 0, nb, 0)
    n = jnp.abs(rel)
    nf = jnp.maximum(n, 1).astype(jnp.float32)
    large = max_exact + (jnp.log(nf / max_exact) / math.log(MAX_DISTANCE / max_exact)
                         * (nb - max_exact)).astype(jnp.int32)
    large = jnp.minimum(large, nb - 1)
    return ret + jnp.where(n < max_exact, n, large)


def _bias_tiles(rel_bias, da_lambda):
    r = jnp.arange(TK, dtype=jnp.int32)[:, None]
    c = jnp.arange(TQ, dtype=jnp.int32)[None, :]
    diag = jnp.where((r // CHUNK) <= (c // CHUNK), _t5_bucket(r - c), -1)
    prev = _t5_bucket(r - c - TK)
    bnear = jnp.stack([diag, prev]).astype(jnp.int32)
    far = _t5_bucket(jnp.full((1,), -(TK + 1), jnp.int32)).astype(jnp.int32)
    rm = jnp.arange(MP, dtype=jnp.int32)[:, None]
    valid = rm >= (MP - N_META)
    kpos = rm - (MP - N_META)
    m0 = jnp.where(valid, _t5_bucket(kpos - (N_META + c)), -1)
    m1 = jnp.where(valid, jnp.broadcast_to(far[0], (MP, TQ)), -1)
    bmeta = jnp.stack([m0, m1]).astype(jnp.int32)
    cm = jnp.arange(MP, dtype=jnp.int32)[None, :]
    bmm = jnp.where(valid, _t5_bucket(rm - cm), -1).astype(jnp.int32)

    smem = pl.BlockSpec(memory_space=pltpu.SMEM)
    return pl.pallas_call(
        _prep_kernel,
        grid=(DA_HEADS,),
        in_specs=[smem, smem,
                  pl.BlockSpec((4, DA_HALF), lambda h: (0, 0)),
                  pl.BlockSpec((2, TK, TQ), lambda h: (0, 0, 0)),
                  pl.BlockSpec((2, MP, TQ), lambda h: (0, 0, 0)),
                  pl.BlockSpec((MP, MP), lambda h: (0, 0))],
        out_specs=[pl.BlockSpec((None, 2, TK, TQ), lambda h: (h, 0, 0, 0)),
                   pl.BlockSpec((None, 2, MP, TQ), lambda h: (h, 0, 0, 0)),
                   pl.BlockSpec((None, MP, MP), lambda h: (h, 0, 0)),
                   pl.BlockSpec((SUBLANES, LANES), lambda h: (0, 0))],
        out_shape=[jax.ShapeDtypeStruct((DA_HEADS, 2, TK, TQ), f32),
                   jax.ShapeDtypeStruct((DA_HEADS, 2, MP, TQ), f32),
                   jax.ShapeDtypeStruct((DA_HEADS, MP, MP), f32),
                   jax.ShapeDtypeStruct((SUBLANES, LANES), f32)],
        name="prep",
    )(far, rel_bias.astype(f32), da_lambda.astype(f32), bnear, bmeta, bmm)


def _proj_kernel(x_ref, g_ref, w_ref, ws_ref, q_ref, k_ref, v_ref, dq_ref, dk_ref, dv_ref,
                 dz_ref, ba_ref):
    x = x_ref[...]
    ms = jnp.mean(x * x, axis=-1, keepdims=True)
    u = (x * lax.rsqrt(ms + EPS) * g_ref[...]).astype(bf16)
    outs = (q_ref, k_ref, v_ref, dq_ref, dk_ref, dv_ref, dz_ref)
    for j, o in enumerate(outs):
        r = jnp.dot(u, w_ref[:, j * 512:(j + 1) * 512], preferred_element_type=f32)
        if j == 0:
            r = r * (DA_HALF ** -0.5 * LOG2E)
        o[...] = r.astype(o.dtype)
    ba_ref[...] = jnp.dot(u, ws_ref[...], preferred_element_type=f32)


def _proj(x3, g1, w_main, w_small, tm):
    nb, rows, _ = x3.shape
    row_spec = lambda w: pl.BlockSpec((None, tm, w), lambda b, t: (b, t, 0))
    outs = [jax.ShapeDtypeStruct((nb, rows, 512), bf16)] * 7 + [
        jax.ShapeDtypeStruct((nb, rows, LANES), f32)]
    return pl.pallas_call(
        _proj_kernel,
        grid=(nb, rows // tm),
        in_specs=[row_spec(D_MODEL), _const_spec((1, D_MODEL)),
                  _const_spec((D_MODEL, N_MAIN)), _const_spec((D_MODEL, LANES))],
        out_specs=[row_spec(512)] * 7 + [row_spec(LANES)],
        out_shape=outs,
        compiler_params=pltpu.CompilerParams(
            dimension_semantics=("parallel", "parallel"), vmem_limit_bytes=VMEM_LIMIT),
        name="proj",
    )(x3, g1, w_main, w_small)


def _dot_nt(a, b):
    return lax.dot_general(a, b, (((1,), (1,)), ((), ())), preferred_element_type=f32)


def _dot_tn(a, b):
    return lax.dot_general(a, b, (((0,), (0,)), ((), ())), preferred_element_type=f32)


def _stack_components(q):
    qt = q.astype(f32).T
    row = lax.broadcasted_iota(jnp.int32, qt.shape, 0)
    q0 = jnp.where(row < DA_HALF, qt, 0.0)
    q1 = jnp.where(row >= DA_HALF, qt, 0.0)
    return jnp.concatenate([q0, q1], axis=1).astype(bf16)


def _colmax(s):
    return jnp.max(s, axis=0, keepdims=True)


def _softmax_step(stats, s, smax, v_pend, p_ref, acc_ref):
    m, l, alpha_pend = stats
    pv = _dot_tn(v_pend, p_ref[...])
    m_new = jnp.maximum(m, smax)
    alpha = jnp.exp2(m - m_new)
    p = jnp.exp2(s - m_new)
    l = alpha * l + jnp.sum(p, axis=0, keepdims=True)
    acc_ref[...] = alpha_pend * acc_ref[...] + pv
    return (m_new, l, alpha), p.astype(bf16)


def _attn_finish(stats, v_pend, p_ref, acc_ref, lam, g, tq):
    m, l, alpha_pend = stats
    acc = alpha_pend * acc_ref[...] + _dot_tn(v_pend, p_ref[...])
    o = acc[:, :tq] / l[:, :tq] - lam * (acc[:, tq:] / l[:, tq:])
    ms = jnp.mean(o * o, axis=0, keepdims=True)
    y = o * lax.rsqrt(ms + EPS) * g * (1.0 - LAMBDA_INIT)
    return y.T


def _attn_init(tq, p_ref, acc_ref):
    p_ref[...] = jnp.zeros(p_ref.shape, bf16)
    acc_ref[...] = jnp.zeros(acc_ref.shape, f32)
    return (jnp.full((1, 2 * tq), -3e38, f32), jnp.zeros((1, 2 * tq), f32),
            jnp.ones((1, 2 * tq), f32))


def _both(b):
    return jnp.concatenate([b, b], axis=1)


def _attn_kernel(lam_ref, q_ref, k_ref, v_ref, km_ref, vm_ref, bnear_ref, bmeta_ref, g_ref,
                 o_ref, *scratch):
    i = pl.program_id(1)
    n_far = jnp.maximum(i - 1, 0)
    heads = range(DA_HEADS)
    cols = lambda h: slice(h * DA_HEAD_DIM, (h + 1) * DA_HEAD_DIM)
    qz = [_stack_components(q_ref[:, cols(h)]) for h in heads]

    def scores(h, j):
        start = pl.multiple_of(j * TK, TK)
        return jnp.dot(k_ref[pl.ds(start, TK), cols(h)], qz[h], preferred_element_type=f32)

    def values(h, j):
        start = pl.multiple_of(jnp.maximum(j, 0) * TK, TK)
        return v_ref[pl.ds(start, TK), cols(h)]

    s_bufs = scratch[0:DA_HEADS]
    p_bufs = scratch[DA_HEADS:2 * DA_HEADS]
    accs = scratch[2 * DA_HEADS:3 * DA_HEADS]
    stats, smax = [], []
    for h in heads:
        stats.append(_attn_init(TQ, p_bufs[h], accs[h]))
        s0 = scores(h, 0)
        s_bufs[h][0] = s0
        smax.append(_colmax(s0))

    def far_body(j, carry):
        stats, smax = carry
        slot = lax.rem(j, 2)
        new_stats, new_smax = [], []
        for h in heads:
            st, p = _softmax_step(stats[h], s_bufs[h][slot], smax[h], values(h, j - 1),
                                  p_bufs[h], accs[h])
            p_bufs[h][...] = p
            new_stats.append(st)
        for h in heads:
            s_next = scores(h, j + 1)
            s_bufs[h][1 - slot] = s_next
            new_smax.append(_colmax(s_next))
        return tuple(new_stats), tuple(new_smax)

    stats, _ = lax.fori_loop(0, n_far, far_body, (tuple(stats), tuple(smax)))
    gate = jnp.where(i >= 1, 0.0, NEG_INF).astype(f32)
    for h in heads:
        p_h, acc_h = p_bufs[h], accs[h]
        s = s_bufs[h][lax.rem(n_far, 2)] + _both(bnear_ref[h, 1] + gate)
        st, p = _softmax_step(stats[h], s, _colmax(s), values(h, n_far - 1), p_h, acc_h)
        p_h[...] = p
        s = scores(h, i) + _both(bnear_ref[h, 0])
        st, p = _softmax_step(st, s, _colmax(s), values(h, i - 1), p_h, acc_h)
        p_h[...] = p
        s = (jnp.dot(km_ref[:, cols(h)], qz[h], preferred_element_type=f32)
             + _both(bmeta_ref[h, jnp.minimum(i, 1)]))
        st, p = _softmax_step(st, s, _colmax(s), values(h, i), p_h, acc_h)
        p_meta = p_h.at[0:MP, :]
        p_meta[...] = p
        o_ref[:, cols(h)] = _attn_finish(st, vm_ref[:, cols(h)], p_meta, acc_h, lam_ref[0],
                                         g_ref[...], TQ).astype(o_ref.dtype)


def _attn_meta_kernel(lam_ref, q_ref, km_ref, vm_ref, bmm_ref, g_ref, o_ref, p_buf, acc_ref):
    qz = _stack_components(q_ref[...])
    stats = _attn_init(MP, p_buf, acc_ref)
    s = jnp.dot(km_ref[...], qz, preferred_element_type=f32) + _both(bmm_ref[...])
    stats, p = _softmax_step(stats, s, _colmax(s), vm_ref[...], p_buf, acc_ref)
    p_buf[...] = p
    o_ref[...] = _attn_finish(stats, vm_ref[...], p_buf, acc_ref, lam_ref[0], g_ref[...],
                              MP).astype(o_ref.dtype)


def _attention(lam1, q, k, v, km, vm, bnear, bmeta, g_tile):
    nb, s_len, _ = q.shape
    smem = pl.BlockSpec(memory_space=pltpu.SMEM)
    return pl.pallas_call(
        _attn_kernel,
        grid=(nb, s_len // TQ),
        in_specs=[smem,
                  pl.BlockSpec((None, TQ, DA_WIDTH), lambda b, i: (b, i, 0)),
                  pl.BlockSpec((None, s_len, DA_WIDTH), lambda b, i: (b, 0, 0)),
                  pl.BlockSpec((None, s_len, DA_WIDTH), lambda b, i: (b, 0, 0)),
                  _const_spec((MP, DA_WIDTH)), _const_spec((MP, DA_WIDTH)),
                  _const_spec((DA_HEADS, 2, TK, TQ)), _const_spec((DA_HEADS, 2, MP, TQ)),
                  _const_spec((DA_HEAD_DIM, TQ))],
        out_specs=pl.BlockSpec((None, TQ, DA_WIDTH), lambda b, i: (b, i, 0)),
        out_shape=jax.ShapeDtypeStruct((nb, s_len, DA_WIDTH), bf16),
        scratch_shapes=([pltpu.VMEM((2, TK, 2 * TQ), f32)] * DA_HEADS
                        + [pltpu.VMEM((TK, 2 * TQ), bf16)] * DA_HEADS
                        + [pltpu.VMEM((DA_HEAD_DIM, 2 * TQ), f32)] * DA_HEADS),
        compiler_params=pltpu.CompilerParams(
            dimension_semantics=("parallel", "arbitrary"),
            vmem_limit_bytes=VMEM_LIMIT),
        name="attn",
    )(lam1, q, k, v, km, vm, bnear, bmeta, g_tile)


def _attention_meta(lam1, qm, km, vm, bmm, g_tile):
    smem = pl.BlockSpec(memory_space=pltpu.SMEM)
    head = pl.BlockSpec((MP, DA_HEAD_DIM), lambda h: (0, h))
    return pl.pallas_call(
        _attn_meta_kernel,
        grid=(DA_HEADS,),
        in_specs=[smem, head, head, head,
                  pl.BlockSpec((None, MP, MP), lambda h: (h, 0, 0)),
                  pl.BlockSpec((DA_HEAD_DIM, MP), lambda h: (0, 0))],
        out_specs=head,
        out_shape=jax.ShapeDtypeStruct((MP, DA_WIDTH), bf16),
        scratch_shapes=[pltpu.VMEM((MP, 2 * MP), bf16), pltpu.VMEM((DA_HEAD_DIM, 2 * MP), f32)],
        name="attn_meta",
    )(lam1, qm, km, vm, bmm, g_tile)


GDN_R = DN_HEADS * CHUNK
GDN_LEVELS = (2, 4, 8, 16, 32)
GDN_QKV = 3 * DN_WIDTH


def _gdn_masks():
    r = np.arange(GDN_R)[:, None]
    c = np.arange(GDN_R)[None, :]
    same = (r // CHUNK) == (c // CHUNK)
    tri = same & (r >= c)
    strict = same & (r > c)
    eye = (r == c)
    lv = [strict & ((r // 2) == (c // 2))]
    for s in GDN_LEVELS:
        lv.append(strict & ((r // (2 * s)) == (c // (2 * s))) & ((r & s) != 0) & ((c & s) == 0))
    return np.stack([tri, strict, eye] + lv).astype(np.float32)


def _sigmoid(x):
    return 1.0 / (1.0 + jnp.exp(-x))


def _split3(x):
    hi = x.astype(bf16)
    r1 = x - hi.astype(f32)
    mid = r1.astype(bf16)
    lo = (r1 - mid.astype(f32)).astype(bf16)
    return hi, mid, lo


def _bdot(a, b):
    return jnp.dot(a.astype(bf16), b.astype(bf16), preferred_element_type=f32)


def _gdn_prepare(xq, xk, xv, ba, cbuf, cw_ref, alog_ref, dtb_ref, mask_ref, lcum_ref):
    cbuf[SUBLANES:SUBLANES + CHUNK, :] = jnp.concatenate([xq, xk, xv], axis=1).astype(f32)
    y = cw_ref[DN_CONV - 1:DN_CONV, :] * cbuf[SUBLANES:SUBLANES + CHUNK, :]
    for d in range(1, DN_CONV):
        y = y + cw_ref[DN_CONV - 1 - d:DN_CONV - d, :] * cbuf[SUBLANES - d:SUBLANES - d + CHUNK, :]
    cbuf[0:SUBLANES, :] = cbuf[CHUNK:CHUNK + SUBLANES, :]
    y = y * _sigmoid(y)

    def stack(off):
        return jnp.concatenate(
            [y[:, off + h * DN_DK: off + (h + 1) * DN_DK] for h in range(DN_HEADS)], axis=0)

    qs, ks, vs = stack(0), stack(DN_WIDTH), stack(2 * DN_WIDTH)
    qn = qs * lax.rsqrt(jnp.sum(qs * qs, axis=-1, keepdims=True) + EPS) * (DN_DK ** -0.5)
    kn = ks * lax.rsqrt(jnp.sum(ks * ks, axis=-1, keepdims=True) + EPS)

    beta_t = _sigmoid(ba)
    xg = ba + dtb_ref[...]
    softplus = jnp.maximum(xg, 0.0) + jnp.log1p(jnp.exp(-jnp.abs(xg)))
    g_t = -jnp.exp(alog_ref[...]) * softplus
    lcum = lcum_ref[...]
    gcum = sum(jnp.dot(lcum, part, preferred_element_type=f32) for part in _split3(g_t))
    gcum_t = gcum.T
    col = lambda a, j: a[:, j:j + 1]
    beta_c = jnp.concatenate([col(beta_t, h) for h in range(DN_HEADS)], axis=0)
    g_c = jnp.concatenate([col(gcum, DN_HEADS + h) for h in range(DN_HEADS)], axis=0)
    g_l = jnp.concatenate([gcum_t[DN_HEADS + h:DN_HEADS + h + 1, :]
                           for h in range(DN_HEADS)], axis=1)
    g_last = [gcum[CHUNK - 1:CHUNK, DN_HEADS + h:DN_HEADS + h + 1] for h in range(DN_HEADS)]
    g_last_c = jnp.concatenate([jnp.broadcast_to(gl, (CHUNK, 1)) for gl in g_last], axis=0)

    tri = mask_ref[0] > 0.5
    decay = jnp.exp(jnp.where(tri, g_c - g_l, -jnp.inf))
    kn_b = kn.astype(bf16)
    kk = _dot_nt(kn_b, kn_b)
    m_mat = mask_ref[1] * (beta_c * kk * decay)
    x_inv = mask_ref[2] - m_mat * mask_ref[3]
    for lvl in range(len(GDN_LEVELS)):
        c_mat = m_mat * mask_ref[4 + lvl]
        x_inv = x_inv - _bdot(_bdot(x_inv, c_mat), x_inv)
    exp_g = jnp.exp(g_c)
    rhs = jnp.concatenate([vs * beta_c, kn * (beta_c * exp_g)], axis=1)
    sol = _bdot(x_inv, rhs)
    u_mat, w_mat = sol[:, :DN_DV], sol[:, DN_DV:]
    a_qk = _dot_nt(qn.astype(bf16), kn_b) * decay
    q_g = qn * exp_g
    k_g = kn * jnp.exp(g_last_c - g_c)
    return u_mat, w_mat, a_qk, q_g, k_g, [jnp.exp(gl) for gl in g_last]


def _gdn_recur(prep, z, s_ref, ng_ref):
    u_mat, w_mat, a_qk, q_g, k_g, g_last = prep
    v_new = []
    o_inter = []
    for h in range(DN_HEADS):
        rows = slice(h * CHUNK, (h + 1) * CHUNK)
        s_b = s_ref[h].astype(bf16)
        v_new.append(u_mat[rows] - jnp.dot(w_mat[rows].astype(bf16), s_b,
                                           preferred_element_type=f32))
        o_inter.append(jnp.dot(q_g[rows].astype(bf16), s_b, preferred_element_type=f32))
    v_new_s = jnp.concatenate(v_new, axis=0)
    o_s = jnp.concatenate(o_inter, axis=0) + _bdot(a_qk, v_new_s)
    outs = []
    for h in range(DN_HEADS):
        rows = slice(h * CHUNK, (h + 1) * CHUNK)
        s_ref[h] = s_ref[h] * g_last[h] + _dot_tn(k_g[rows].astype(bf16),
                                                  v_new[h].astype(bf16))
        o = o_s[rows]
        o = o * lax.rsqrt(jnp.mean(o * o, axis=-1, keepdims=True) + EPS) * ng_ref[...]
        zh = z[:, h * DN_DV:(h + 1) * DN_DV].astype(f32)
        outs.append(o * (zh * _sigmoid(zh)))
    return jnp.concatenate(outs, axis=1)


def _gdn_kernel(dq_ref, dk_ref, dv_ref, dz_ref, ba_ref, mq_ref, mk_ref, mv_ref, mz_ref, mba_ref,
                cw_ref, alog_ref, dtb_ref, ng_ref, mask_ref, lcum_ref,
                o_ref, om_ref, cbuf, s_ref):
    t = pl.program_id(1)
    consts = (cbuf, cw_ref, alog_ref, dtb_ref, mask_ref, lcum_ref)

    @pl.when(t == 0)
    def _():
        cbuf[0:SUBLANES, :] = jnp.zeros((SUBLANES, GDN_QKV), f32)
        s_ref[...] = jnp.zeros(s_ref.shape, f32)
        prep = _gdn_prepare(mq_ref[...], mk_ref[...], mv_ref[...], mba_ref[...], *consts)
        om_ref[...] = _gdn_recur(prep, mz_ref[...], s_ref, ng_ref).astype(om_ref.dtype)

    n_chunks = GDN_ROWS // CHUNK
    preps = []
    for c in range(n_chunks):
        rows = slice(c * CHUNK, (c + 1) * CHUNK)
        preps.append(_gdn_prepare(dq_ref[rows, :], dk_ref[rows, :], dv_ref[rows, :],
                                  ba_ref[rows, :], *consts))
    for c in range(n_chunks):
        rows = slice(c * CHUNK, (c + 1) * CHUNK)
        o_ref[rows, :] = _gdn_recur(preps[c], dz_ref[rows, :], s_ref, ng_ref).astype(o_ref.dtype)


def _gdn(dq, dk, dv, dz, ba, mq, mk, mv, mz, mba, conv_w, alog_row, dtb_row, ng_row):
    nb, s_len, _ = dq.shape
    row = lambda w: pl.BlockSpec((None, GDN_ROWS, w), lambda b, t: (b, t, 0))
    mrow = lambda w: pl.BlockSpec((CHUNK, w), lambda b, t: (MP // CHUNK - 1, 0))
    masks = jnp.asarray(_gdn_masks())
    lcum = jnp.asarray(np.tril(np.ones((CHUNK, CHUNK), np.float32))).astype(bf16)
    return pl.pallas_call(
        _gdn_kernel,
        grid=(nb, s_len // GDN_ROWS),
        in_specs=[row(DN_WIDTH)] * 4 + [row(LANES)] + [mrow(DN_WIDTH)] * 4 + [mrow(LANES)] + [
            _const_spec((DN_CONV, GDN_QKV)), _const_spec((1, LANES)), _const_spec((1, LANES)),
            _const_spec((1, DN_DV)), _const_spec(masks.shape), _const_spec((CHUNK, CHUNK))],
        out_specs=[row(DN_WIDTH), pl.BlockSpec((None, CHUNK, DN_WIDTH), lambda b, t: (b, 0, 0))],
        out_shape=[jax.ShapeDtypeStruct((nb, s_len, DN_WIDTH), bf16),
                   jax.ShapeDtypeStruct((nb, CHUNK, DN_WIDTH), bf16)],
        scratch_shapes=[pltpu.VMEM((SUBLANES + CHUNK, GDN_QKV), f32),
                        pltpu.VMEM((DN_HEADS, DN_DK, DN_DV), f32)],
        compiler_params=pltpu.CompilerParams(
            dimension_semantics=("parallel", "arbitrary"), vmem_limit_bytes=VMEM_LIMIT),
        name="gdn",
    )(dq, dk, dv, dz, ba, mq, mk, mv, mz, mba, conv_w, alog_row, dtb_row, ng_row, masks, lcum)


def _mix_and_norm(x, oda, odn, wout_ref, g2_ref):
    mix = jnp.concatenate([oda, odn], axis=1)
    h2 = x + jnp.dot(mix, wout_ref[...], preferred_element_type=f32)
    u2 = h2 * lax.rsqrt(jnp.mean(h2 * h2, axis=-1, keepdims=True) + EPS) * g2_ref[...]
    return h2, u2.astype(bf16)


def _ffn_halo_kernel(x_ref, oda_ref, odn_ref, wout_ref, g2_ref, wup_ref, halo_ref):
    _, u2 = _mix_and_norm(x_ref[...], oda_ref[...], odn_ref[...], wout_ref, g2_ref)
    halo_ref[...] = jnp.dot(u2, wup_ref[...], preferred_element_type=f32)


def _ffn_kernel(x_ref, oda_ref, odn_ref, halo_ref, wout_ref, g2_ref, wup_ref, cw_ref, cb_ref,
                wdown_ref, gf_ref, o_ref, carry_ref, carry_next, acc_ref, *hbufs):
    tm = x_ref.shape[0]
    t = pl.program_id(1)

    @pl.when(t == 0)
    def _():
        carry_ref[...] = halo_ref[...]

    @pl.when(t > 0)
    def _():
        carry_ref[...] = carry_next[...]

    h2, u2 = _mix_and_norm(x_ref[...], oda_ref[...], odn_ref[...], wout_ref, g2_ref)

    n_chunks = D_FF // FFN_CW
    col_pair = lambda c: (slice(c * FFN_CW, (c + 1) * FFN_CW),
                          slice(D_FF + c * FFN_CW, D_FF + (c + 1) * FFN_CW))
    buf_pair = lambda c: hbufs[2 * (c % FFN_NBUF):2 * (c % FFN_NBUF) + 2]

    def up_part(c):
        for cols, hbuf in zip(col_pair(c), buf_pair(c)):
            hbuf[0:SUBLANES, :] = carry_ref[:, cols]
            hup = jnp.dot(u2, wup_ref[:, cols], preferred_element_type=f32)
            hbuf[SUBLANES:SUBLANES + tm, :] = hup
            carry_next[:, cols] = hup[tm - SUBLANES:, :]

    def conv_part(cols, hbuf):
        y = cb_ref[:, cols]
        for d in range(FFN_CONV):
            y = y + cw_ref[FFN_CONV - 1 - d:FFN_CONV - d, cols] * hbuf[SUBLANES - d:SUBLANES - d + tm, :]
        return y

    up_part(0)
    for c in range(n_chunks):
        if c + 1 < n_chunks:
            up_part(c + 1)
        gate, val = (conv_part(cols, hbuf) for cols, hbuf in zip(col_pair(c), buf_pair(c)))
        act = (gate * _sigmoid(gate) * val).astype(bf16)
        part = jnp.dot(act, wdown_ref[c * FFN_CW:(c + 1) * FFN_CW, :], preferred_element_type=f32)
        if c == 0:
            acc_ref[...] = part
        else:
            acc_ref[...] += part
    y = h2 + acc_ref[...]
    o_ref[...] = y * lax.rsqrt(jnp.mean(y * y, axis=-1, keepdims=True) + EPS) * gf_ref[...]


def _ffn_halo(xm, odam, odnm, w_out, g2, w_up):
    return pl.pallas_call(
        _ffn_halo_kernel,
        out_shape=jax.ShapeDtypeStruct((HALO_ROWS, 2 * D_FF), f32),
        compiler_params=pltpu.CompilerParams(vmem_limit_bytes=VMEM_LIMIT),
        name="ffn_halo",
    )(xm, odam, odnm, w_out, g2, w_up)


def _ffn(x, oda, odn, halo, w_out, g2, w_up, conv_w, conv_b, w_down, gf):
    nb, s_len, _ = x.shape
    tm = FFN_TM
    row = lambda w: pl.BlockSpec((None, tm, w), lambda b, t: (b, t, 0))
    return pl.pallas_call(
        _ffn_kernel,
        grid=(nb, s_len // tm),
        in_specs=[row(D_MODEL), row(DA_WIDTH), row(DN_WIDTH),
                  _const_spec((SUBLANES, 2 * D_FF)), _const_spec((D_MODEL, D_MODEL)),
                  _const_spec((1, D_MODEL)), _const_spec((D_MODEL, 2 * D_FF)),
                  _const_spec((FFN_CONV, 2 * D_FF)), _const_spec((1, 2 * D_FF)),
                  _const_spec((D_FF, D_MODEL)), _const_spec((1, D_MODEL))],
        out_specs=row(D_MODEL),
        out_shape=jax.ShapeDtypeStruct((nb, s_len, D_MODEL), f32),
        scratch_shapes=([pltpu.VMEM((SUBLANES, 2 * D_FF), f32),
                         pltpu.VMEM((SUBLANES, 2 * D_FF), f32),
                         pltpu.VMEM((tm, D_MODEL), f32)]
                        + [pltpu.VMEM((SUBLANES + tm, FFN_CW), f32)] * (2 * FFN_NBUF)),
        compiler_params=pltpu.CompilerParams(
            dimension_semantics=("parallel", "arbitrary"), vmem_limit_bytes=VMEM_LIMIT),
        name="ffn",
    )(x, oda, odn, halo, w_out, g2, w_up, conv_w, conv_b, w_down, gf)


def kernel(x, meta_tokens, rel_bias, norm1_g, w_in, da_lambda, da_subln_g, dn_conv_w, dn_A_log,
           dn_dt_bias, dn_norm_g, w_out, norm2_g, w_up, ffn_conv_w, ffn_conv_b, w_down,
           final_norm_g):
    nb, s_len, _ = x.shape
    assert s_len % TQ == 0 and s_len % GDN_ROWS == 0 and s_len % FFN_TM == 0
    w_in0 = w_in[0]
    w_main = w_in0[:, :N_MAIN].astype(bf16)
    w_small = jnp.pad(w_in0[:, N_MAIN:], ((0, 0), (0, LANES - 2 * DN_HEADS))).astype(bf16)
    w_out_b = w_out[0].astype(bf16)
    w_up_b = w_up[0].astype(bf16)
    w_down_b = w_down[0].astype(bf16)
    g1 = norm1_g[0].reshape(1, D_MODEL).astype(f32)
    g2 = norm2_g[0].reshape(1, D_MODEL).astype(f32)
    gf = final_norm_g.reshape(1, D_MODEL).astype(f32)
    meta_pad = jnp.pad(meta_tokens.astype(x.dtype), ((MP - N_META, 0), (0, 0)))

    bnear, bmeta, bmm, lam_tile = _bias_tiles(rel_bias, da_lambda[0])
    lam1 = lam_tile[0, :1]

    q, k, v, dq, dk, dv, dz, ba = _proj(x, g1, w_main, w_small, PROJ_TM)
    mq, mk, mv, mdq, mdk, mdv, mdz, mba = [a[0] for a in _proj(meta_pad[None], g1, w_main, w_small, MP)]

    subln = da_subln_g[0].astype(f32)
    o_da = _attention(lam1, q, k, v, mk, mv, bnear, bmeta,
                      jnp.broadcast_to(subln[:, None], (DA_HEAD_DIM, TQ)))
    o_da_m = _attention_meta(lam1, mq, mk, mv, bmm,
                             jnp.broadcast_to(subln[:, None], (DA_HEAD_DIM, MP)))

    gate_row = lambda p: jnp.pad(p[0].astype(f32), (DN_HEADS, LANES - 2 * DN_HEADS)).reshape(1, LANES)
    o_dn, o_dn_m = _gdn(dq, dk, dv, dz, ba, mdq, mdk, mdv, mdz, mba,
                        dn_conv_w[0].astype(f32), gate_row(dn_A_log), gate_row(dn_dt_bias),
                        dn_norm_g[0].reshape(1, DN_DV).astype(f32))

    halo = _ffn_halo(meta_pad[MP - HALO_ROWS:], o_da_m[MP - HALO_ROWS:],
                     o_dn_m[0, CHUNK - HALO_ROWS:], w_out_b, g2, w_up_b)[HALO_ROWS - SUBLANES:]
    return _ffn(x, o_da, o_dn, halo, w_out_b, g2, w_up_b, ffn_conv_w[0].astype(f32),
                ffn_conv_b[0].reshape(1, 2 * D_FF).astype(f32), w_down_b, gf)
```

```python
import functools
import math

import numpy as np
import jax
import jax.numpy as jnp
from jax import lax
from jax.experimental import pallas as pl
from jax.experimental.pallas import tpu as pltpu

f32 = jnp.float32
bf16 = jnp.bfloat16

D_MODEL = 1024
CHUNK = 64
N_META = 16
EPS = 1e-6
NEG_INF = -1e30
LAMBDA_INIT = 0.8 - 0.6 * math.exp(-0.3 * 0)
LOG2E = math.log2(math.e)

DA_HEADS = 4
DA_HEAD_DIM = 128
DA_HALF = 64
DA_WIDTH = DA_HEADS * DA_HEAD_DIM
DN_HEADS = 4
DN_DK = 128
DN_DV = 128
DN_WIDTH = DN_HEADS * DN_DV
DN_CONV = 4
N_BUCKETS = 32
MAX_DISTANCE = 128
D_FF = 2816
FFN_CONV = 3

N_MAIN = 7 * 512
LANES = 128
SUBLANES = 8
MP = 128
TQ = 256
TK = 256
PROJ_TM = 256
GDN_ROWS = 256
FFN_TM = 256
FFN_CW = 256
FFN_NBUF = 2
HALO_ROWS = 16
VMEM_LIMIT = 52 * 1024 * 1024


def _const_spec(shape):
    nd = len(shape)
    return pl.BlockSpec(shape, lambda *_: (0,) * nd, pipeline_mode=pl.Buffered(1))


def _prep_kernel(far_ref, table_ref, lam_in_ref, bnear_in, bmeta_in, bmm_in,
                 bnear_out, bmeta_out, bmm_out, lam_out):
    h = pl.program_id(0)
    c_far = table_ref[far_ref[0], h]

    def lookup(bkt):
        out = jnp.full(bkt.shape, NEG_INF, f32)
        for b in range(N_BUCKETS):
            out = jnp.where(bkt == b, (table_ref[b, h] - c_far) * LOG2E, out)
        return out

    for i in range(2):
        bnear_out[i] = lookup(bnear_in[i])
        bmeta_out[i] = lookup(bmeta_in[i])
    bmm_out[...] = lookup(bmm_in[...])
    lv = lam_in_ref[...]
    s1 = jnp.sum(lv[0:1] * lv[1:2], axis=-1, keepdims=True)
    s2 = jnp.sum(lv[2:3] * lv[3:4], axis=-1, keepdims=True)
    lam = jnp.exp(s1) - jnp.exp(s2) + LAMBDA_INIT
    lam_out[...] = jnp.broadcast_to(lam, lam_out.shape)


def _t5_bucket(rel):
    nb = N_BUCKETS // 2
    max_exact = nb // 2
    ret = jnp.where(rel > 0, nb, 0)
    n = jnp.abs(rel)
    nf = jnp.maximum(n, 1).astype(jnp.float32)
    large = max_exact + (jnp.log(nf / max_exact) / math.log(MAX_DISTANCE / max_exact)
                         * (nb - max_exact)).astype(jnp.int32)
    large = jnp.minimum(large, nb - 1)
    return ret + jnp.where(n < max_exact, n, large)


def _bias_tiles(rel_bias, da_lambda):
    r = jnp.arange(TK, dtype=jnp.int32)[:, None]
    c = jnp.arange(TQ, dtype=jnp.int32)[None, :]
    diag = jnp.where((r // CHUNK) <= (c // CHUNK), _t5_bucket(r - c), -1)
    prev = _t5_bucket(r - c - TK)
    bnear = jnp.stack([diag, prev]).astype(jnp.int32)
    far = _t5_bucket(jnp.full((1,), -(TK + 1), jnp.int32)).astype(jnp.int32)
    rm = jnp.arange(MP, dtype=jnp.int32)[:, None]
    valid = rm >= (MP - N_META)
    kpos = rm - (MP - N_META)
    m0 = jnp.where(valid, _t5_bucket(kpos - (N_META + c)), -1)
    m1 = jnp.where(valid, jnp.broadcast_to(far[0], (MP, TQ)), -1)
    bmeta = jnp.stack([m0, m1]).astype(jnp.int32)
    cm = jnp.arange(MP, dtype=jnp.int32)[None, :]
    bmm = jnp.where(valid, _t5_bucket(rm - cm), -1).astype(jnp.int32)

    smem = pl.BlockSpec(memory_space=pltpu.SMEM)
    return pl.pallas_call(
        _prep_kernel,
        grid=(DA_HEADS,),
        in_specs=[smem, smem,
                  pl.BlockSpec((4, DA_HALF), lambda h: (0, 0)),
                  pl.BlockSpec((2, TK, TQ), lambda h: (0, 0, 0)),
                  pl.BlockSpec((2, MP, TQ), lambda h: (0, 0, 0)),
                  pl.BlockSpec((MP, MP), lambda h: (0, 0))],
        out_specs=[pl.BlockSpec((None, 2, TK, TQ), lambda h: (h, 0, 0, 0)),
                   pl.BlockSpec((None, 2, MP, TQ), lambda h: (h, 0, 0, 0)),
                   pl.BlockSpec((None, MP, MP), lambda h: (h, 0, 0)),
                   pl.BlockSpec((SUBLANES, LANES), lambda h: (0, 0))],
        out_shape=[jax.ShapeDtypeStruct((DA_HEADS, 2, TK, TQ), f32),
                   jax.ShapeDtypeStruct((DA_HEADS, 2, MP, TQ), f32),
                   jax.ShapeDtypeStruct((DA_HEADS, MP, MP), f32),
                   jax.ShapeDtypeStruct((SUBLANES, LANES), f32)],
        name="prep",
    )(far, rel_bias.astype(f32), da_lambda.astype(f32), bnear, bmeta, bmm)


def _sigmoid(x):
    return 1.0 / (1.0 + jnp.exp(-x))


def _proj_kernel(x_ref, halo_ref, g_ref, w_ref, ws_ref, cw_ref, q_ref, k_ref, v_ref, dq_ref,
                 dk_ref, dv_ref, dz_ref, ba_ref, tail_ref, carry_ref, carry_next, u_ref, *hbufs):
    tm = x_ref.shape[0]
    t = pl.program_id(1)

    @pl.when(t == 0)
    def _():
        carry_ref[...] = halo_ref[...]

    @pl.when(t > 0)
    def _():
        carry_ref[...] = carry_next[...]

    x = x_ref[...]
    ms = jnp.mean(x * x, axis=-1, keepdims=True)
    u_ref[...] = (x * lax.rsqrt(ms + EPS) * g_ref[...]).astype(bf16)
    outs = (q_ref, k_ref, v_ref, dq_ref, dk_ref, dv_ref, dz_ref)
    n_out = len(outs)
    project = lambda j: jnp.dot(u_ref[...], w_ref[:, j * 512:(j + 1) * 512],
                                preferred_element_type=f32)

    def conv_silu(j, r):
        cols = slice((j - 3) * DN_WIDTH, (j - 2) * DN_WIDTH)
        hbuf = hbufs[j - 3]
        hbuf[0:SUBLANES, :] = carry_ref[:, cols]
        hbuf[SUBLANES:SUBLANES + tm, :] = r
        carry_next[:, cols] = r[tm - SUBLANES:, :]
        y = cw_ref[DN_CONV - 1:DN_CONV, cols] * r
        for d in range(1, DN_CONV):
            y = y + cw_ref[DN_CONV - 1 - d:DN_CONV - d, cols] * hbuf[SUBLANES - d:SUBLANES - d + tm, :]
        return y * _sigmoid(y)

    def l2norm_heads(y, scale):
        parts = []
        for h in range(DN_HEADS):
            yh = y[:, h * DN_DK:(h + 1) * DN_DK]
            parts.append(yh * (lax.rsqrt(jnp.sum(yh * yh, axis=-1, keepdims=True) + EPS) * scale))
        return jnp.concatenate(parts, axis=1)

    order = (3, 0, 4, 1, 5, 2, 6)
    ahead = 2
    pending = [project(j) for j in order[:ahead]]
    for pos, j in enumerate(order):
        r = pending.pop(0)
        if pos + ahead < n_out:
            pending.append(project(order[pos + ahead]))
        elif pos + ahead == n_out:
            ba_ref[...] = jnp.dot(u_ref[...], ws_ref[...], preferred_element_type=f32)
        if j == 0:
            r = r * (DA_HALF ** -0.5 * LOG2E)
        elif j in (3, 4):
            r = l2norm_heads(conv_silu(j, r), DN_DK ** -0.5 if j == 3 else 1.0)
        elif j == 5:
            r = conv_silu(j, r)
        outs[j][...] = r.astype(outs[j].dtype)
    tail_ref[...] = carry_next[...]


def _proj(x3, halo, g1, w_main, w_small, conv_w, tm):
    nb, rows, _ = x3.shape
    row_spec = lambda w: pl.BlockSpec((None, tm, w), lambda b, t: (b, t, 0))
    outs = [jax.ShapeDtypeStruct((nb, rows, 512), bf16)] * 7 + [
        jax.ShapeDtypeStruct((nb, rows, LANES), f32),
        jax.ShapeDtypeStruct((nb, SUBLANES, GDN_QKV), f32)]
    return pl.pallas_call(
        _proj_kernel,
        grid=(nb, rows // tm),
        in_specs=[row_spec(D_MODEL), _const_spec((SUBLANES, GDN_QKV)), _const_spec((1, D_MODEL)),
                  _const_spec((D_MODEL, N_MAIN)), _const_spec((D_MODEL, LANES)),
                  _const_spec((DN_CONV, GDN_QKV))],
        out_specs=[row_spec(512)] * 7 + [row_spec(LANES),
                   pl.BlockSpec((None, SUBLANES, GDN_QKV), lambda b, t: (b, 0, 0))],
        out_shape=outs,
        scratch_shapes=([pltpu.VMEM((SUBLANES, GDN_QKV), f32)] * 2
                        + [pltpu.VMEM((tm, D_MODEL), bf16)]
                        + [pltpu.VMEM((SUBLANES + tm, DN_WIDTH), f32)] * 3),
        compiler_params=pltpu.CompilerParams(
            dimension_semantics=("parallel", "arbitrary"), vmem_limit_bytes=VMEM_LIMIT),
        name="proj",
    )(x3, halo, g1, w_main, w_small, conv_w)


def _dot_nt(a, b):
    return lax.dot_general(a, b, (((1,), (1,)), ((), ())), preferred_element_type=f32)


def _dot_tn(a, b):
    return lax.dot_general(a, b, (((0,), (0,)), ((), ())), preferred_element_type=f32)


def _stack_components(q):
    qt = q.astype(f32).T
    row = lax.broadcasted_iota(jnp.int32, qt.shape, 0)
    q0 = jnp.where(row < DA_HALF, qt, 0.0)
    q1 = jnp.where(row >= DA_HALF, qt, 0.0)
    return jnp.concatenate([q0, q1], axis=1).astype(bf16)


def _colmax(s):
    return jnp.max(s, axis=0, keepdims=True)


def _softmax_step(stats, s, smax, v_pend, p_ref, acc_ref):
    m, l, alpha_pend = stats
    pv = _dot_tn(v_pend, p_ref[...])
    m_new = jnp.maximum(m, smax)
    alpha = jnp.exp2(m - m_new)
    p = jnp.exp2(s - m_new)
    l = alpha * l + jnp.sum(p, axis=0, keepdims=True)
    acc_ref[...] = alpha_pend * acc_ref[...] + pv
    return (m_new, l, alpha), p.astype(bf16)


def _attn_finish(stats, v_pend, p_ref, acc_ref, lam, g, tq):
    m, l, alpha_pend = stats
    acc = alpha_pend * acc_ref[...] + _dot_tn(v_pend, p_ref[...])
    o = acc[:, :tq] / l[:, :tq] - lam * (acc[:, tq:] / l[:, tq:])
    ms = jnp.mean(o * o, axis=0, keepdims=True)
    y = o * lax.rsqrt(ms + EPS) * g * (1.0 - LAMBDA_INIT)
    return y.T


def _attn_init(tq, p_ref, acc_ref):
    p_ref[...] = jnp.zeros(p_ref.shape, bf16)
    acc_ref[...] = jnp.zeros(acc_ref.shape, f32)
    return (jnp.full((1, 2 * tq), -3e38, f32), jnp.zeros((1, 2 * tq), f32),
            jnp.ones((1, 2 * tq), f32))


def _both(b):
    return jnp.concatenate([b, b], axis=1)


def _attn_kernel(lam_ref, q_ref, k_ref, v_ref, km_ref, vm_ref, bnear_ref, bmeta_ref, g_ref,
                 o_ref, *scratch):
    i = pl.program_id(1)
    n_far = jnp.maximum(i - 1, 0)
    heads = range(DA_HEADS)
    cols = lambda h: slice(h * DA_HEAD_DIM, (h + 1) * DA_HEAD_DIM)
    qz = [_stack_components(q_ref[:, cols(h)]) for h in heads]

    def scores(h, j):
        start = pl.multiple_of(j * TK, TK)
        return jnp.dot(k_ref[pl.ds(start, TK), cols(h)], qz[h], preferred_element_type=f32)

    def values(h, j):
        start = pl.multiple_of(jnp.maximum(j, 0) * TK, TK)
        return v_ref[pl.ds(start, TK), cols(h)]

    s_bufs = scratch[0:DA_HEADS]
    p_bufs = scratch[DA_HEADS:2 * DA_HEADS]
    accs = scratch[2 * DA_HEADS:3 * DA_HEADS]
    stats, smax = [], []
    for h in heads:
        stats.append(_attn_init(TQ, p_bufs[h], accs[h]))
        s0 = scores(h, 0)
        s_bufs[h][0] = s0
        smax.append(_colmax(s0))

    def far_body(j, carry):
        stats, smax = carry
        slot = lax.rem(j, 2)
        new_stats, new_smax = [], []
        for h in heads:
            st, p = _softmax_step(stats[h], s_bufs[h][slot], smax[h], values(h, j - 1),
                                  p_bufs[h], accs[h])
            p_bufs[h][...] = p
            new_stats.append(st)
        for h in heads:
            s_next = scores(h, j + 1)
            s_bufs[h][1 - slot] = s_next
            new_smax.append(_colmax(s_next))
        return tuple(new_stats), tuple(new_smax)

    stats, _ = lax.fori_loop(0, n_far, far_body, (tuple(stats), tuple(smax)))
    gate = jnp.where(i >= 1, 0.0, NEG_INF).astype(f32)
    for h in heads:
        p_h, acc_h = p_bufs[h], accs[h]
        s = s_bufs[h][lax.rem(n_far, 2)] + _both(bnear_ref[h, 1] + gate)
        st, p = _softmax_step(stats[h], s, _colmax(s), values(h, n_far - 1), p_h, acc_h)
        p_h[...] = p
        s = scores(h, i) + _both(bnear_ref[h, 0])
        st, p = _softmax_step(st, s, _colmax(s), values(h, i - 1), p_h, acc_h)
        p_h[...] = p
        s = (jnp.dot(km_ref[:, cols(h)], qz[h], preferred_element_type=f32)
             + _both(bmeta_ref[h, jnp.minimum(i, 1)]))
        st, p = _softmax_step(st, s, _colmax(s), values(h, i), p_h, acc_h)
        p_meta = p_h.at[0:MP, :]
        p_meta[...] = p
        o_ref[:, cols(h)] = _attn_finish(st, vm_ref[:, cols(h)], p_meta, acc_h, lam_ref[0],
                                         g_ref[...], TQ).astype(o_ref.dtype)


def _attn_meta_kernel(lam_ref, q_ref, km_ref, vm_ref, bmm_ref, g_ref, o_ref, p_buf, acc_ref):
    qz = _stack_components(q_ref[...])
    stats = _attn_init(MP, p_buf, acc_ref)
    s = jnp.dot(km_ref[...], qz, preferred_element_type=f32) + _both(bmm_ref[...])
    stats, p = _softmax_step(stats, s, _colmax(s), vm_ref[...], p_buf, acc_ref)
    p_buf[...] = p
    o_ref[...] = _attn_finish(stats, vm_ref[...], p_buf, acc_ref, lam_ref[0], g_ref[...],
                              MP).astype(o_ref.dtype)


def _attention(lam1, q, k, v, km, vm, bnear, bmeta, g_tile):
    nb, s_len, _ = q.shape
    smem = pl.BlockSpec(memory_space=pltpu.SMEM)
    return pl.pallas_call(
        _attn_kernel,
        grid=(nb, s_len // TQ),
        in_specs=[smem,
                  pl.BlockSpec((None, TQ, DA_WIDTH), lambda b, i: (b, i, 0)),
                  pl.BlockSpec((None, s_len, DA_WIDTH), lambda b, i: (b, 0, 0)),
                  pl.BlockSpec((None, s_len, DA_WIDTH), lambda b, i: (b, 0, 0)),
                  _const_spec((MP, DA_WIDTH)), _const_spec((MP, DA_WIDTH)),
                  _const_spec((DA_HEADS, 2, TK, TQ)), _const_spec((DA_HEADS, 2, MP, TQ)),
                  _const_spec((DA_HEAD_DIM, TQ))],
        out_specs=pl.BlockSpec((None, TQ, DA_WIDTH), lambda b, i: (b, i, 0)),
        out_shape=jax.ShapeDtypeStruct((nb, s_len, DA_WIDTH), bf16),
        scratch_shapes=([pltpu.VMEM((2, TK, 2 * TQ), f32)] * DA_HEADS
                        + [pltpu.VMEM((TK, 2 * TQ), bf16)] * DA_HEADS
                        + [pltpu.VMEM((DA_HEAD_DIM, 2 * TQ), f32)] * DA_HEADS),
        compiler_params=pltpu.CompilerParams(
            dimension_semantics=("parallel", "arbitrary"),
            vmem_limit_bytes=VMEM_LIMIT),
        name="attn",
    )(lam1, q, k, v, km, vm, bnear, bmeta, g_tile)


def _attention_meta(lam1, qm, km, vm, bmm, g_tile):
    smem = pl.BlockSpec(memory_space=pltpu.SMEM)
    head = pl.BlockSpec((MP, DA_HEAD_DIM), lambda h: (0, h))
    return pl.pallas_call(
        _attn_meta_kernel,
        grid=(DA_HEADS,),
        in_specs=[smem, head, head, head,
                  pl.BlockSpec((None, MP, MP), lambda h: (h, 0, 0)),
                  pl.BlockSpec((DA_HEAD_DIM, MP), lambda h: (0, 0))],
        out_specs=head,
        out_shape=jax.ShapeDtypeStruct((MP, DA_WIDTH), bf16),
        scratch_shapes=[pltpu.VMEM((MP, 2 * MP), bf16), pltpu.VMEM((DA_HEAD_DIM, 2 * MP), f32)],
        name="attn_meta",
    )(lam1, qm, km, vm, bmm, g_tile)


GDN_R = DN_HEADS * CHUNK
GDN_LEVELS = (2, 4, 8, 16, 32)
GDN_QKV = 3 * DN_WIDTH
GDN_NCH = GDN_ROWS // CHUNK


def _gdn_consts():
    r = np.arange(GDN_R)[:, None]
    c = np.arange(GDN_R)[None, :]
    same = (r // CHUNK) == (c // CHUNK)
    bd = np.stack([same & (r >= c), same & (r > c), same]).astype(np.float32)
    i = np.arange(CHUNK)[:, None]
    j = np.arange(GDN_R)[None, :] % CHUNK
    lv = [i == j, (i > j) & ((i // 2) == (j // 2))]
    for s in GDN_LEVELS:
        lv.append(((i // (2 * s)) == (j // (2 * s))) & ((i & s) != 0) & ((j & s) == 0))
    cat = np.stack(lv).astype(np.float32)
    rr = np.arange(GDN_ROWS)[:, None]
    cc = np.arange(GDN_ROWS)[None, :]
    lcum = (((rr // CHUNK) == (cc // CHUNK)) & (rr >= cc)).astype(np.float32)
    sel = np.zeros((2, LANES, DN_HEADS * LANES), np.float32)
    for h in range(DN_HEADS):
        sel[0, h, h * LANES:(h + 1) * LANES] = 1.0
        sel[1, DN_HEADS + h, h * LANES:(h + 1) * LANES] = 1.0
    return bd, cat, lcum, sel


def _split3(x):
    hi = x.astype(bf16)
    r1 = x - hi.astype(f32)
    mid = r1.astype(bf16)
    lo = (r1 - mid.astype(f32)).astype(bf16)
    return hi, mid, lo


def _gdn_block(n, xq, xk, xv, ba, z, s_ref, alog_ref, dtb_ref, ng_ref, bd_ref, cat_ref,
               lcum_ref, sel_ref):
    rows = n * CHUNK
    heads = range(DN_HEADS)
    chunks = range(n)
    dot = functools.partial(jnp.dot, preferred_element_type=f32)

    def stack(a):
        return jnp.concatenate(
            [a[c * CHUNK:(c + 1) * CHUNK, h * LANES:(h + 1) * LANES]
             for c in chunks for h in heads], axis=0)

    qn_b, kn_b = stack(xq), stack(xk)
    qn, kn, vs = qn_b.astype(f32), kn_b.astype(f32), stack(xv).astype(f32)

    beta_t = _sigmoid(ba)
    xg = ba + dtb_ref[...]
    softplus = jnp.maximum(xg, 0.0) + jnp.log1p(jnp.exp(-jnp.abs(xg)))
    g_t = -jnp.exp(alog_ref[...]) * softplus
    lcum = lcum_ref[0:rows, 0:rows]
    gcum = sum(dot(lcum, part) for part in _split3(g_t))
    g_rep = stack(sum(dot(part, sel_ref[1]) for part in _split3(gcum)))
    b_rep = stack(sum(dot(part, sel_ref[0]) for part in _split3(beta_t)[:2]))
    g_end = jnp.concatenate(
        [jnp.broadcast_to(g_rep[(b + 1) * CHUNK - 1:(b + 1) * CHUNK, :], (CHUNK, LANES))
         for b in range(n * DN_HEADS)], axis=0)
    exp_g = jnp.exp(g_rep)
    rhs = jnp.concatenate([vs * b_rep, kn * (b_rep * exp_g)], axis=1).astype(bf16)
    q_g = qn * exp_g
    k_g = (kn * jnp.exp(g_end - g_rep)).astype(bf16)
    g_last = jnp.exp(g_end)

    tri = bd_ref[0] > 0.5
    block_b = bd_ref[2].astype(bf16)
    both = lambda a: jnp.concatenate([a, a], axis=1)
    blk = lambda c: slice(c * GDN_R, (c + 1) * GDN_R)
    decay, m_b, x_cat = [], [], []
    for c in chunks:
        g_row = g_rep[blk(c)].T[0:1, :]
        dec = jnp.exp(jnp.where(tri, both(g_rep[blk(c)]) - g_row, -jnp.inf))
        m = bd_ref[1] * both(b_rep[blk(c)]) * _dot_nt(kn_b[blk(c)], kn_b[blk(c)]) * dec
        m_cat = sum(m[h * CHUNK:(h + 1) * CHUNK] for h in heads)
        decay.append(dec)
        m_b.append(m.astype(bf16))
        x_cat.append(cat_ref[0] - m_cat * cat_ref[1])

    def to_bd(x):
        return jnp.concatenate([x.astype(bf16)] * DN_HEADS, axis=0) * block_b

    for lvl in range(len(GDN_LEVELS)):
        ys = [dot(x_cat[c].astype(bf16), m_b[c]) for c in chunks]
        zs = [dot(ys[c].astype(bf16), to_bd(x_cat[c])) for c in chunks]
        x_cat = [x_cat[c] - zs[c] * cat_ref[2 + lvl] for c in chunks]
    sol = [dot(to_bd(x_cat[c]), rhs[blk(c)]).astype(bf16) for c in chunks]
    a_qk = [(_dot_nt(qn_b[blk(c)], kn_b[blk(c)]) * decay[c]).astype(bf16) for c in chunks]
    a_uw = [dot(a_qk[c], sol[c]) for c in chunks]
    hrows = lambda h: slice(h * CHUNK, (h + 1) * CHUNK)
    k_uw = [[_dot_tn(k_g[blk(c)][hrows(h)], sol[c][hrows(h)]) for h in heads] for c in chunks]

    out_rows = []
    for c in chunks:
        out_cols = []
        for h in heads:
            s_old = s_ref[h]
            s_b = s_old.astype(bf16)
            q_eff = (q_g[blk(c)][hrows(h)] - a_uw[c][hrows(h), DN_DV:]).astype(bf16)
            o = a_uw[c][hrows(h), :DN_DV] + dot(q_eff, s_b)
            s_ref[h] = (s_old * g_last[blk(c)][hrows(h)][0:1, :] + k_uw[c][h][:, :DN_DV]
                        - dot(k_uw[c][h][:, DN_DV:].astype(bf16), s_b))
            o = o * lax.rsqrt(jnp.mean(o * o, axis=-1, keepdims=True) + EPS) * ng_ref[...]
            zh = z[c * CHUNK:(c + 1) * CHUNK, h * DN_DV:(h + 1) * DN_DV].astype(f32)
            out_cols.append(o * (zh * _sigmoid(zh)))
        out_rows.append(jnp.concatenate(out_cols, axis=1))
    return jnp.concatenate(out_rows, axis=0)


def _gdn_kernel(dq_ref, dk_ref, dv_ref, dz_ref, ba_ref, mq_ref, mk_ref, mv_ref, mz_ref, mba_ref,
                alog_ref, dtb_ref, ng_ref, bd_ref, cat_ref, lcum_ref, sel_ref,
                o_ref, om_ref, s_ref):
    consts = (s_ref, alog_ref, dtb_ref, ng_ref, bd_ref, cat_ref, lcum_ref, sel_ref)

    @pl.when(pl.program_id(1) == 0)
    def _():
        s_ref[...] = jnp.zeros(s_ref.shape, f32)
        om_ref[...] = _gdn_block(1, mq_ref[...], mk_ref[...], mv_ref[...], mba_ref[...],
                                 mz_ref[...], *consts).astype(om_ref.dtype)

    o_ref[...] = _gdn_block(GDN_NCH, dq_ref[...], dk_ref[...], dv_ref[...], ba_ref[...],
                            dz_ref[...], *consts).astype(o_ref.dtype)


def _gdn(dq, dk, dv, dz, ba, mq, mk, mv, mz, mba, alog_row, dtb_row, ng_row):
    nb, s_len, _ = dq.shape
    row = lambda w: pl.BlockSpec((None, GDN_ROWS, w), lambda b, t: (b, t, 0))
    mrow = lambda w: pl.BlockSpec((CHUNK, w), lambda b, t: (MP // CHUNK - 1, 0))
    bd, cat, lcum, sel = _gdn_consts()
    bd, cat = jnp.asarray(bd), jnp.asarray(cat)
    lcum, sel = jnp.asarray(lcum).astype(bf16), jnp.asarray(sel).astype(bf16)
    return pl.pallas_call(
        _gdn_kernel,
        grid=(nb, s_len // GDN_ROWS),
        in_specs=[row(DN_WIDTH)] * 4 + [row(LANES)] + [mrow(DN_WIDTH)] * 4 + [mrow(LANES)] + [
            _const_spec((1, LANES)), _const_spec((1, LANES)),
            _const_spec((1, DN_DV)), _const_spec(bd.shape), _const_spec(cat.shape),
            _const_spec(lcum.shape), _const_spec(sel.shape)],
        out_specs=[row(DN_WIDTH), pl.BlockSpec((None, CHUNK, DN_WIDTH), lambda b, t: (b, 0, 0))],
        out_shape=[jax.ShapeDtypeStruct((nb, s_len, DN_WIDTH), bf16),
                   jax.ShapeDtypeStruct((nb, CHUNK, DN_WIDTH), bf16)],
        scratch_shapes=[pltpu.VMEM((DN_HEADS, DN_DK, DN_DV), f32)],
        compiler_params=pltpu.CompilerParams(
            dimension_semantics=("parallel", "arbitrary"), vmem_limit_bytes=VMEM_LIMIT),
        name="gdn",
    )(dq, dk, dv, dz, ba, mq, mk, mv, mz, mba, alog_row, dtb_row, ng_row, bd, cat, lcum, sel)


def _mix_and_norm(x, oda, odn, wout_ref, g2_ref):
    mix = jnp.concatenate([oda, odn], axis=1)
    h2 = x + jnp.dot(mix, wout_ref[...], preferred_element_type=f32)
    u2 = h2 * lax.rsqrt(jnp.mean(h2 * h2, axis=-1, keepdims=True) + EPS) * g2_ref[...]
    return h2, u2.astype(bf16)


def _ffn_halo_kernel(x_ref, oda_ref, odn_ref, wout_ref, g2_ref, wup_ref, halo_ref):
    _, u2 = _mix_and_norm(x_ref[...], oda_ref[...], odn_ref[...], wout_ref, g2_ref)
    halo_ref[...] = jnp.dot(u2, wup_ref[...], preferred_element_type=f32)


def _ffn_kernel(x_ref, oda_ref, odn_ref, halo_ref, wout_ref, g2_ref, wup_ref, cw_ref, cb_ref,
                wdown_ref, gf_ref, o_ref, carry_ref, carry_next, acc_ref, *hbufs):
    tm = x_ref.shape[0]
    t = pl.program_id(1)

    @pl.when(t == 0)
    def _():
        carry_ref[...] = halo_ref[...]

    @pl.when(t > 0)
    def _():
        carry_ref[...] = carry_next[...]

    h2, u2 = _mix_and_norm(x_ref[...], oda_ref[...], odn_ref[...], wout_ref, g2_ref)

    n_chunks = D_FF // FFN_CW
    col_pair = lambda c: (slice(c * FFN_CW, (c + 1) * FFN_CW),
                          slice(D_FF + c * FFN_CW, D_FF + (c + 1) * FFN_CW))
    buf_pair = lambda c: hbufs[2 * (c % FFN_NBUF):2 * (c % FFN_NBUF) + 2]

    def up_part(c):
        for cols, hbuf in zip(col_pair(c), buf_pair(c)):
            hbuf[0:SUBLANES, :] = carry_ref[:, cols]
            hup = jnp.dot(u2, wup_ref[:, cols], preferred_element_type=f32)
            hbuf[SUBLANES:SUBLANES + tm, :] = hup
            carry_next[:, cols] = hup[tm - SUBLANES:, :]

    def conv_part(cols, hbuf):
        y = cb_ref[:, cols]
        for d in range(FFN_CONV):
            y = y + cw_ref[FFN_CONV - 1 - d:FFN_CONV - d, cols] * hbuf[SUBLANES - d:SUBLANES - d + tm, :]
        return y

    up_part(0)
    for c in range(n_chunks):
        if c + 1 < n_chunks:
            up_part(c + 1)
        gate, val = (conv_part(cols, hbuf) for cols, hbuf in zip(col_pair(c), buf_pair(c)))
        act = (gate * _sigmoid(gate) * val).astype(bf16)
        part = jnp.dot(act, wdown_ref[c * FFN_CW:(c + 1) * FFN_CW, :], preferred_element_type=f32)
        if c == 0:
            acc_ref[...] = part
        else:
            acc_ref[...] += part
    y = h2 + acc_ref[...]
    o_ref[...] = y * lax.rsqrt(jnp.mean(y * y, axis=-1, keepdims=True) + EPS) * gf_ref[...]


def _ffn_halo(xm, odam, odnm, w_out, g2, w_up):
    return pl.pallas_call(
        _ffn_halo_kernel,
        out_shape=jax.ShapeDtypeStruct((HALO_ROWS, 2 * D_FF), f32),
        compiler_params=pltpu.CompilerParams(vmem_limit_bytes=VMEM_LIMIT),
        name="ffn_halo",
    )(xm, odam, odnm, w_out, g2, w_up)


def _ffn(x, oda, odn, halo, w_out, g2, w_up, conv_w, conv_b, w_down, gf):
    nb, s_len, _ = x.shape
    tm = FFN_TM
    row = lambda w: pl.BlockSpec((None, tm, w), lambda b, t: (b, t, 0))
    return pl.pallas_call(
        _ffn_kernel,
        grid=(nb, s_len // tm),
        in_specs=[row(D_MODEL), row(DA_WIDTH), row(DN_WIDTH),
                  _const_spec((SUBLANES, 2 * D_FF)), _const_spec((D_MODEL, D_MODEL)),
                  _const_spec((1, D_MODEL)), _const_spec((D_MODEL, 2 * D_FF)),
                  _const_spec((FFN_CONV, 2 * D_FF)), _const_spec((1, 2 * D_FF)),
                  _const_spec((D_FF, D_MODEL)), _const_spec((1, D_MODEL))],
        out_specs=row(D_MODEL),
        out_shape=jax.ShapeDtypeStruct((nb, s_len, D_MODEL), f32),
        scratch_shapes=([pltpu.VMEM((SUBLANES, 2 * D_FF), f32),
                         pltpu.VMEM((SUBLANES, 2 * D_FF), f32),
                         pltpu.VMEM((tm, D_MODEL), f32)]
                        + [pltpu.VMEM((SUBLANES + tm, FFN_CW), f32)] * (2 * FFN_NBUF)),
        compiler_params=pltpu.CompilerParams(
            dimension_semantics=("parallel", "arbitrary"), vmem_limit_bytes=VMEM_LIMIT),
        name="ffn",
    )(x, oda, odn, halo, w_out, g2, w_up, conv_w, conv_b, w_down, gf)


def kernel(x, meta_tokens, rel_bias, norm1_g, w_in, da_lambda, da_subln_g, dn_conv_w, dn_A_log,
           dn_dt_bias, dn_norm_g, w_out, norm2_g, w_up, ffn_conv_w, ffn_conv_b, w_down,
           final_norm_g):
    nb, s_len, _ = x.shape
    assert s_len % TQ == 0 and s_len % GDN_ROWS == 0 and s_len % FFN_TM == 0
    w_in0 = w_in[0]
    w_main = w_in0[:, :N_MAIN].astype(bf16)
    w_small = jnp.pad(w_in0[:, N_MAIN:], ((0, 0), (0, LANES - 2 * DN_HEADS))).astype(bf16)
    w_out_b = w_out[0].astype(bf16)
    w_up_b = w_up[0].astype(bf16)
    w_down_b = w_down[0].astype(bf16)
    g1 = norm1_g[0].reshape(1, D_MODEL).astype(f32)
    g2 = norm2_g[0].reshape(1, D_MODEL).astype(f32)
    gf = final_norm_g.reshape(1, D_MODEL).astype(f32)
    meta_pad = jnp.pad(meta_tokens.astype(x.dtype), ((MP - N_META, 0), (0, 0)))

    bnear, bmeta, bmm, lam_tile = _bias_tiles(rel_bias, da_lambda[0])
    lam1 = lam_tile[0, :1]

    conv_w = dn_conv_w[0].astype(f32)
    no_halo = jnp.zeros((SUBLANES, GDN_QKV), f32)
    mq, mk, mv, mdq, mdk, mdv, mdz, mba, mtail = [
        a[0] for a in _proj(meta_pad[None], no_halo, g1, w_main, w_small, conv_w, MP)]
    q, k, v, dq, dk, dv, dz, ba, _ = _proj(x, mtail, g1, w_main, w_small, conv_w, PROJ_TM)

    subln = da_subln_g[0].astype(f32)
    o_da = _attention(lam1, q, k, v, mk, mv, bnear, bmeta,
                      jnp.broadcast_to(subln[:, None], (DA_HEAD_DIM, TQ)))
    o_da_m = _attention_meta(lam1, mq, mk, mv, bmm,
                             jnp.broadcast_to(subln[:, None], (DA_HEAD_DIM, MP)))

    gate_row = lambda p: jnp.pad(p[0].astype(f32), (DN_HEADS, LANES - 2 * DN_HEADS)).reshape(1, LANES)
    o_dn, o_dn_m = _gdn(dq, dk, dv, dz, ba, mdq, mdk, mdv, mdz, mba,
                        gate_row(dn_A_log), gate_row(dn_dt_bias),
                        dn_norm_g[0].reshape(1, DN_DV).astype(f32))

    halo = _ffn_halo(meta_pad[MP - HALO_ROWS:], o_da_m[MP - HALO_ROWS:],
                     o_dn_m[0, CHUNK - HALO_ROWS:], w_out_b, g2, w_up_b)[HALO_ROWS - SUBLANES:]
    return _ffn(x, o_da, o_dn, halo, w_out_b, g2, w_up_b, ffn_conv_w[0].astype(f32),
                ffn_conv_b[0].reshape(1, 2 * D_FF).astype(f32), w_down_b, gf)
```

```python
import functools
import math

import numpy as np
import jax
import jax.numpy as jnp
from jax import lax
from jax.experimental import pallas as pl
from jax.experimental.pallas import tpu as pltpu

f32 = jnp.float32
bf16 = jnp.bfloat16

D_MODEL = 1024
CHUNK = 64
N_META = 16
EPS = 1e-6
NEG_INF = -1e30
LAMBDA_INIT = 0.8 - 0.6 * math.exp(-0.3 * 0)
LOG2E = math.log2(math.e)

DA_HEADS = 4
DA_HEAD_DIM = 128
DA_HALF = 64
DA_WIDTH = DA_HEADS * DA_HEAD_DIM
DN_HEADS = 4
DN_DK = 128
DN_DV = 128
DN_WIDTH = DN_HEADS * DN_DV
DN_CONV = 4
N_BUCKETS = 32
MAX_DISTANCE = 128
D_FF = 2816
FFN_CONV = 3

N_MAIN = 7 * 512
LANES = 128
SUBLANES = 8
MP = 128
TQ = 256
TK = 256
PROJ_TM = 256
GDN_ROWS = 256
FFN_TILES = 1
FFN_TM = 256
FFN_CW = 256
FFN_AHEAD = 4
FFN_NBUF = FFN_AHEAD + 1
HALO_ROWS = 16
VMEM_LIMIT = 52 * 1024 * 1024


def _const_spec(shape):
    nd = len(shape)
    return pl.BlockSpec(shape, lambda *_: (0,) * nd, pipeline_mode=pl.Buffered(1))


def _prep_kernel(far_ref, table_ref, lam_in_ref, bnear_in, bmeta_in, bmm_in,
                 bnear_out, bmeta_out, bmm_out, lam_out):
    h = pl.program_id(0)
    c_far = table_ref[far_ref[0], h]

    def lookup(bkt):
        out = jnp.full(bkt.shape, NEG_INF, f32)
        for b in range(N_BUCKETS):
            out = jnp.where(bkt == b, (table_ref[b, h] - c_far) * LOG2E, out)
        return out

    for i in range(2):
        bnear_out[i] = lookup(bnear_in[i])
        bmeta_out[i] = lookup(bmeta_in[i])
    bmm_out[...] = lookup(bmm_in[...])
    lv = lam_in_ref[...]
    s1 = jnp.sum(lv[0:1] * lv[1:2], axis=-1, keepdims=True)
    s2 = jnp.sum(lv[2:3] * lv[3:4], axis=-1, keepdims=True)
    lam = jnp.exp(s1) - jnp.exp(s2) + LAMBDA_INIT
    lam_out[...] = jnp.broadcast_to(lam, lam_out.shape)


def _t5_bucket(rel):
    nb = N_BUCKETS // 2
    max_exact = nb // 2
    ret = jnp.where(rel > 0, nb, 0)
    n = jnp.abs(rel)
    nf = jnp.maximum(n, 1).astype(jnp.float32)
    large = max_exact + (jnp.log(nf / max_exact) / math.log(MAX_DISTANCE / max_exact)
                         * (nb - max_exact)).astype(jnp.int32)
    large = jnp.minimum(large, nb - 1)
    return ret + jnp.where(n < max_exact, n, large)


def _bias_tiles(rel_bias, da_lambda):
    r = jnp.arange(TK, dtype=jnp.int32)[:, None]
    c = jnp.arange(TQ, dtype=jnp.int32)[None, :]
    diag = jnp.where((r // CHUNK) <= (c // CHUNK), _t5_bucket(r - c), -1)
    prev = _t5_bucket(r - c - TK)
    bnear = jnp.stack([diag, prev]).astype(jnp.int32)
    far = _t5_bucket(jnp.full((1,), -(TK + 1), jnp.int32)).astype(jnp.int32)
    rm = jnp.arange(MP, dtype=jnp.int32)[:, None]
    valid = rm >= (MP - N_META)
    kpos = rm - (MP - N_META)
    m0 = jnp.where(valid, _t5_bucket(kpos - (N_META + c)), -1)
    m1 = jnp.where(valid, jnp.broadcast_to(far[0], (MP, TQ)), -1)
    bmeta = jnp.stack([m0, m1]).astype(jnp.int32)
    cm = jnp.arange(MP, dtype=jnp.int32)[None, :]
    bmm = jnp.where(valid, _t5_bucket(rm - cm), -1).astype(jnp.int32)

    smem = pl.BlockSpec(memory_space=pltpu.SMEM)
    return pl.pallas_call(
        _prep_kernel,
        grid=(DA_HEADS,),
        in_specs=[smem, smem,
                  pl.BlockSpec((4, DA_HALF), lambda h: (0, 0)),
                  pl.BlockSpec((2, TK, TQ), lambda h: (0, 0, 0)),
                  pl.BlockSpec((2, MP, TQ), lambda h: (0, 0, 0)),
                  pl.BlockSpec((MP, MP), lambda h: (0, 0))],
        out_specs=[pl.BlockSpec((None, 2, TK, TQ), lambda h: (h, 0, 0, 0)),
                   pl.BlockSpec((None, 2, MP, TQ), lambda h: (h, 0, 0, 0)),
                   pl.BlockSpec((None, MP, MP), lambda h: (h, 0, 0)),
                   pl.BlockSpec((SUBLANES, LANES), lambda h: (0, 0))],
        out_shape=[jax.ShapeDtypeStruct((DA_HEADS, 2, TK, TQ), f32),
                   jax.ShapeDtypeStruct((DA_HEADS, 2, MP, TQ), f32),
                   jax.ShapeDtypeStruct((DA_HEADS, MP, MP), f32),
                   jax.ShapeDtypeStruct((SUBLANES, LANES), f32)],
        name="prep",
    )(far, rel_bias.astype(f32), da_lambda.astype(f32), bnear, bmeta, bmm)


def _sigmoid(x):
    return 1.0 / (1.0 + jnp.exp(-x))


def _proj_kernel(x_ref, halo_ref, g_ref, w_ref, ws_ref, cw_ref, q_ref, k_ref, v_ref, dq_ref,
                 dk_ref, dv_ref, dz_ref, ba_ref, tail_ref, carry_ref, carry_next, u_ref):
    tm = x_ref.shape[0]
    t = pl.program_id(1)

    @pl.when(t == 0)
    def _():
        carry_ref[...] = halo_ref[...]

    @pl.when(t > 0)
    def _():
        carry_ref[...] = carry_next[...]

    x = x_ref[...]
    ms = jnp.mean(x * x, axis=-1, keepdims=True)
    u_ref[...] = (x * lax.rsqrt(ms + EPS) * g_ref[...]).astype(bf16)
    outs = (q_ref, k_ref, v_ref, dq_ref, dk_ref, dv_ref, dz_ref)
    n_out = len(outs)
    project = lambda j: jnp.dot(u_ref[...], w_ref[:, j * 512:(j + 1) * 512],
                                preferred_element_type=f32)

    def conv_silu(j, r):
        cols = slice((j - 3) * DN_WIDTH, (j - 2) * DN_WIDTH)
        carry_next[:, cols] = r[tm - SUBLANES:, :]
        ext = jnp.concatenate([carry_ref[:, cols], r], axis=0)
        y = cw_ref[0:1, cols] * ext
        for d in range(1, DN_CONV):
            y = cw_ref[d:d + 1, cols] * ext + pltpu.roll(y, 1, 0)
        y = y[SUBLANES:, :]
        return y * _sigmoid(y)

    def l2norm_heads(y, scale):
        parts = []
        for h in range(DN_HEADS):
            yh = y[:, h * DN_DK:(h + 1) * DN_DK]
            parts.append(yh * (lax.rsqrt(jnp.sum(yh * yh, axis=-1, keepdims=True) + EPS) * scale))
        return jnp.concatenate(parts, axis=1)

    order = (3, 0, 4, 1, 5, 2, 6)
    ahead = 2
    pending =[project(j) for j in order[:ahead]]
    for pos, j in enumerate(order):
        r = pending.pop(0)
        if pos + ahead < n_out:
            pending.append(project(order[pos + ahead]))
        elif pos + ahead == n_out:
            ba_ref[...] = jnp.dot(u_ref[...], ws_ref[...], preferred_element_type=f32)
        if j == 0:
            r = r * (DA_HALF ** -0.5 * LOG2E)
        elif j in (3, 4):
            r = l2norm_heads(conv_silu(j, r), DN_DK ** -0.5 if j == 3 else 1.0)
        elif j == 5:
            r = conv_silu(j, r)
        outs[j][...] = r.astype(outs[j].dtype)
    tail_ref[...] = carry_next[...]


def _proj(x3, halo, g1, w_main, w_small, conv_w, tm):
    nb, rows, _ = x3.shape
    row_spec = lambda w: pl.BlockSpec((None, tm, w), lambda b, t: (b, t, 0))
    outs = [jax.ShapeDtypeStruct((nb, rows, 512), bf16)] * 7 + [
        jax.ShapeDtypeStruct((nb, rows, LANES), f32),
        jax.ShapeDtypeStruct((nb, SUBLANES, GDN_QKV), f32)]
    return pl.pallas_call(
        _proj_kernel,
        grid=(nb, rows // tm),
        in_specs=[row_spec(D_MODEL), _const_spec((SUBLANES, GDN_QKV)), _const_spec((1, D_MODEL)),
                  _const_spec((D_MODEL, N_MAIN)), _const_spec((D_MODEL, LANES)),
                  _const_spec((DN_CONV, GDN_QKV))],
        out_specs=[row_spec(512)] * 7 + [row_spec(LANES),
                   pl.BlockSpec((None, SUBLANES, GDN_QKV), lambda b, t: (b, 0, 0))],
        out_shape=outs,
        scratch_shapes=([pltpu.VMEM((SUBLANES, GDN_QKV), f32)] * 2
                        + [pltpu.VMEM((tm, D_MODEL), bf16)]),
        compiler_params=pltpu.CompilerParams(
            dimension_semantics=("parallel", "arbitrary"), vmem_limit_bytes=VMEM_LIMIT),
        name="proj",
    )(x3, halo, g1, w_main, w_small, conv_w)


def _dot_nt(a, b):
    return lax.dot_general(a, b, (((1,), (1,)), ((), ())), preferred_element_type=f32)


def _dot_tn(a, b):
    return lax.dot_general(a, b, (((0,), (0,)), ((), ())), preferred_element_type=f32)


def _stack_components(q):
    qt = q.astype(f32).T
    row = lax.broadcasted_iota(jnp.int32, qt.shape, 0)
    q0 = jnp.where(row < DA_HALF, qt, 0.0)
    q1 = jnp.where(row >= DA_HALF, qt, 0.0)
    return jnp.concatenate([q0, q1], axis=1).astype(bf16)


def _colmax(s):
    return jnp.max(s, axis=0, keepdims=True)


def _softmax_step(stats, s, smax, v_pend, p_ref, acc_ref):
    m, l, alpha_pend = stats
    pv = _dot_tn(v_pend, p_ref[...])
    m_new = jnp.maximum(m, smax)
    alpha = jnp.exp2(m - m_new)
    p = jnp.exp2(s - m_new)
    l = alpha * l + jnp.sum(p, axis=0, keepdims=True)
    acc_ref[...] = alpha_pend * acc_ref[...] + pv
    return (m_new, l, alpha), p.astype(bf16)


def _attn_finish(stats, v_pend, p_ref, acc_ref, lam, g, tq):
    m, l, alpha_pend = stats
    acc = alpha_pend * acc_ref[...] + _dot_tn(v_pend, p_ref[...])
    o = acc[:, :tq] / l[:, :tq] - lam * (acc[:, tq:] / l[:, tq:])
    ms = jnp.mean(o * o, axis=0, keepdims=True)
    y = o * lax.rsqrt(ms + EPS) * g * (1.0 - LAMBDA_INIT)
    return y.T


def _attn_init(tq, p_ref, acc_ref):
    p_ref[...] = jnp.zeros(p_ref.shape, bf16)
    acc_ref[...] = jnp.zeros(acc_ref.shape, f32)
    return (jnp.full((1, 2 * tq), -3e38, f32), jnp.zeros((1, 2 * tq), f32),
            jnp.ones((1, 2 * tq), f32))


def _both(b):
    return jnp.concatenate([b, b], axis=1)


def _attn_kernel(lam_ref, q_ref, k_ref, v_ref, km_ref, vm_ref, bnear_ref, bmeta_ref, g_ref,
                 o_ref, *scratch):
    i = pl.program_id(1)
    n_far = jnp.maximum(i - 1, 0)
    heads = range(DA_HEADS)
    cols = lambda h: slice(h * DA_HEAD_DIM, (h + 1) * DA_HEAD_DIM)
    qz = [_stack_components(q_ref[:, cols(h)]) for h in heads]

    def scores(h, j):
        start = pl.multiple_of(j * TK, TK)
        return jnp.dot(k_ref[pl.ds(start, TK), cols(h)], qz[h], preferred_element_type=f32)

    def values(h, j):
        start = pl.multiple_of(jnp.maximum(j, 0) * TK, TK)
        return v_ref[pl.ds(start, TK), cols(h)]

    s_bufs = scratch[0:DA_HEADS]
    p_bufs = scratch[DA_HEADS:2 * DA_HEADS]
    accs = scratch[2 * DA_HEADS:3 * DA_HEADS]
    stats, smax = [], []
    for h in heads:
        stats.append(_attn_init(TQ, p_bufs[h], accs[h]))
        s0 = scores(h, 0)
        s_bufs[h][0] = s0
        smax.append(_colmax(s0))

    def far_body(j, carry):
        stats, smax = carry
        slot = lax.rem(j, 2)
        new_stats, new_smax = [], []
        for h in heads:
            st, p = _softmax_step(stats[h], s_bufs[h][slot], smax[h], values(h, j - 1),
                                  p_bufs[h], accs[h])
            p_bufs[h][...] = p
            new_stats.append(st)
        for h in heads:
            s_next = scores(h, j + 1)
            s_bufs[h][1 - slot] = s_next
            new_smax.append(_colmax(s_next))
        return tuple(new_stats), tuple(new_smax)

    stats, _ = lax.fori_loop(0, n_far, far_body, (tuple(stats), tuple(smax)))
    gate = jnp.where(i >= 1, 0.0, NEG_INF).astype(f32)
    for h in heads:
        p_h, acc_h = p_bufs[h], accs[h]
        s = s_bufs[h][lax.rem(n_far, 2)] + _both(bnear_ref[h, 1] + gate)
        st, p = _softmax_step(stats[h], s, _colmax(s), values(h, n_far - 1), p_h, acc_h)
        p_h[...] = p
        s = scores(h, i) + _both(bnear_ref[h, 0])
        st, p = _softmax_step(st, s, _colmax(s), values(h, i - 1), p_h, acc_h)
        p_h[...] = p
        s = (jnp.dot(km_ref[:, cols(h)], qz[h], preferred_element_type=f32)
             + _both(bmeta_ref[h, jnp.minimum(i, 1)]))
        st, p = _softmax_step(st, s, _colmax(s), values(h, i), p_h, acc_h)
        p_meta = p_h.at[0:MP, :]
        p_meta[...] = p
        o_ref[:, cols(h)] = _attn_finish(st, vm_ref[:, cols(h)], p_meta, acc_h, lam_ref[0],
                                         g_ref[...], TQ).astype(o_ref.dtype)


def _attn_meta_kernel(lam_ref, q_ref, km_ref, vm_ref, bmm_ref, g_ref, o_ref, p_buf, acc_ref):
    qz = _stack_components(q_ref[...])
    stats = _attn_init(MP, p_buf, acc_ref)
    s = jnp.dot(km_ref[...], qz, preferred_element_type=f32) + _both(bmm_ref[...])
    stats, p = _softmax_step(stats, s, _colmax(s), vm_ref[...], p_buf, acc_ref)
    p_buf[...] = p
    o_ref[...] = _attn_finish(stats, vm_ref[...], p_buf, acc_ref, lam_ref[0], g_ref[...],
                              MP).astype(o_ref.dtype)


def _attention(lam1, q, k, v, km, vm, bnear, bmeta, g_tile):
    nb, s_len, _ = q.shape
    smem = pl.BlockSpec(memory_space=pltpu.SMEM)
    return pl.pallas_call(
        _attn_kernel,
        grid=(nb, s_len // TQ),
        in_specs=[smem,
                  pl.BlockSpec((None, TQ, DA_WIDTH), lambda b, i: (b, i, 0)),
                  pl.BlockSpec((None, s_len, DA_WIDTH), lambda b, i: (b, 0, 0)),
                  pl.BlockSpec((None, s_len, DA_WIDTH), lambda b, i: (b, 0, 0)),
                  _const_spec((MP, DA_WIDTH)), _const_spec((MP, DA_WIDTH)),
                  _const_spec((DA_HEADS, 2, TK, TQ)), _const_spec((DA_HEADS, 2, MP, TQ)),
                  _const_spec((DA_HEAD_DIM, TQ))],
        out_specs=pl.BlockSpec((None, TQ, DA_WIDTH), lambda b, i: (b, i, 0)),
        out_shape=jax.ShapeDtypeStruct((nb, s_len, DA_WIDTH), bf16),
        scratch_shapes=([pltpu.VMEM((2, TK, 2 * TQ), f32)] * DA_HEADS
                        + [pltpu.VMEM((TK, 2 * TQ), bf16)] * DA_HEADS
                        + [pltpu.VMEM((DA_HEAD_DIM, 2 * TQ), f32)] * DA_HEADS),
        compiler_params=pltpu.CompilerParams(
            dimension_semantics=("parallel", "arbitrary"),
            vmem_limit_bytes=VMEM_LIMIT),
        name="attn",
    )(lam1, q, k, v, km, vm, bnear, bmeta, g_tile)


def _attention_meta(lam1, qm, km, vm, bmm, g_tile):
    smem = pl.BlockSpec(memory_space=pltpu.SMEM)
    head = pl.BlockSpec((MP, DA_HEAD_DIM), lambda h: (0, h))
    return pl.pallas_call(
        _attn_meta_kernel,
        grid=(DA_HEADS,),
        in_specs=[smem, head, head, head,
                  pl.BlockSpec((None, MP, MP), lambda h: (h, 0, 0)),
                  pl.BlockSpec((DA_HEAD_DIM, MP), lambda h: (0, 0))],
        out_specs=head,
        out_shape=jax.ShapeDtypeStruct((MP, DA_WIDTH), bf16),
        scratch_shapes=[pltpu.VMEM((MP, 2 * MP), bf16), pltpu.VMEM((DA_HEAD_DIM, 2 * MP), f32)],
        name="attn_meta",
    )(lam1, qm, km, vm, bmm, g_tile)


GDN_R = DN_HEADS * CHUNK
GDN_LEVELS = (2, 4, 8, 16, 32)
GDN_QKV = 3 * DN_WIDTH
GDN_NCH = GDN_ROWS // CHUNK
GDN_NSEQ = 2


def _gdn_consts():
    r = np.arange(GDN_R)[:, None]
    c = np.arange(GDN_R)[None, :]
    same = (r // CHUNK) == (c // CHUNK)
    bd = np.stack([same & (r >= c), same & (r > c), same]).astype(np.float32)
    i = np.arange(CHUNK)[:, None]
    j = np.arange(GDN_R)[None, :] % CHUNK
    lv = [i == j, (i > j) & ((i // 2) == (j // 2))]
    for s in GDN_LEVELS:
        lv.append(((i // (2 * s)) == (j // (2 * s))) & ((i & s) != 0) & ((j & s) == 0))
    cat = np.stack(lv).astype(np.float32)
    rr = np.arange(GDN_ROWS)[:, None]
    cc = np.arange(GDN_ROWS)[None, :]
    lcum = (((rr // CHUNK) == (cc // CHUNK)) & (rr >= cc)).astype(np.float32)
    sel = np.zeros((2, LANES, DN_HEADS * LANES), np.float32)
    for h in range(DN_HEADS):
        sel[0, h, h * LANES:(h + 1) * LANES] = 1.0
        sel[1, DN_HEADS + h, h * LANES:(h + 1) * LANES] = 1.0
    return bd, cat, lcum, sel


def _split3(x):
    hi = x.astype(bf16)
    r1 = x - hi.astype(f32)
    mid = r1.astype(bf16)
    lo = (r1 - mid.astype(f32)).astype(bf16)
    return hi, mid, lo


def _gdn_block(n_per_seq, xq, xk, xv, ba, z, s_refs, alog_ref, dtb_ref, ng_ref, bd_ref,
               cat_ref, lcum_ref, sel_ref):
    n_seq = len(s_refs)
    n = n_seq * n_per_seq
    seq_rows = n_per_seq * CHUNK
    heads = range(DN_HEADS)
    chunks = range(n)
    dot = functools.partial(jnp.dot, preferred_element_type=f32)

    def stack(a):
        return jnp.concatenate(
            [a[c * CHUNK:(c + 1) * CHUNK, h * LANES:(h + 1) * LANES]
             for c in chunks for h in heads], axis=0)

    qn_b, kn_b = stack(xq), stack(xk)
    qn, kn, vs = qn_b.astype(f32), kn_b.astype(f32), stack(xv).astype(f32)

    beta_t = _sigmoid(ba)
    xg = ba + dtb_ref[...]
    softplus = jnp.maximum(xg, 0.0) + jnp.log1p(jnp.exp(-jnp.abs(xg)))
    g_t = -jnp.exp(alog_ref[...]) * softplus
    lcum = lcum_ref[0:seq_rows, 0:seq_rows]
    g_parts = _split3(g_t)
    gcum = jnp.concatenate(
        [sum(dot(lcum, part[q * seq_rows:(q + 1) * seq_rows]) for part in g_parts)
         for q in range(n_seq)], axis=0)
    g_rep = stack(sum(dot(part, sel_ref[1]) for part in _split3(gcum)))
    b_rep = stack(sum(dot(part, sel_ref[0]) for part in _split3(beta_t)[:2]))
    g_end = jnp.concatenate(
        [jnp.broadcast_to(g_rep[(b + 1) * CHUNK - 1:(b + 1) * CHUNK, :], (CHUNK, LANES))
         for b in range(n * DN_HEADS)], axis=0)
    exp_g = jnp.exp(g_rep)
    rhs = jnp.concatenate([vs * b_rep, kn * (b_rep * exp_g)], axis=1).astype(bf16)
    q_g = qn * exp_g
    k_g = (kn * jnp.exp(g_end - g_rep)).astype(bf16)
    g_last = jnp.exp(g_end)

    tri = bd_ref[0] > 0.5
    block_b = bd_ref[2].astype(bf16)
    both = lambda a: jnp.concatenate([a, a], axis=1)
    blk = lambda c: slice(c * GDN_R, (c + 1) * GDN_R)
    decay, m_b, x_cat = [], [], []
    for c in chunks:
        g_row = g_rep[blk(c)].T[0:1, :]
        dec = jnp.exp(jnp.where(tri, both(g_rep[blk(c)]) - g_row, -jnp.inf))
        m = bd_ref[1] * both(b_rep[blk(c)]) * _dot_nt(kn_b[blk(c)], kn_b[blk(c)]) * dec
        m_cat = sum(m[h * CHUNK:(h + 1) * CHUNK] for h in heads)
        decay.append(dec)
        m_b.append(m.astype(bf16))
        x_cat.append(cat_ref[0] - m_cat * cat_ref[1])

    def to_bd(x):
        return jnp.concatenate([x.astype(bf16)] * DN_HEADS, axis=0) * block_b

    for lvl in range(len(GDN_LEVELS)):
        ys = [dot(x_cat[c].astype(bf16), m_b[c]) for c in chunks]
        zs = [dot(ys[c].astype(bf16), to_bd(x_cat[c])) for c in chunks]
        x_cat = [x_cat[c] - zs[c] * cat_ref[2 + lvl] for c in chunks]
    sol = [dot(to_bd(x_cat[c]), rhs[blk(c)]).astype(bf16) for c in chunks]
    a_qk = [(_dot_nt(qn_b[blk(c)], kn_b[blk(c)]) * decay[c]).astype(bf16) for c in chunks]
    a_uw = [dot(a_qk[c], sol[c]) for c in chunks]
    hrows = lambda h: slice(h * CHUNK, (h + 1) * CHUNK)
    k_uw = [[_dot_tn(k_g[blk(c)][hrows(h)], sol[c][hrows(h)]) for h in heads] for c in chunks]

    out_rows = [None] * n
    for step in range(n_per_seq):
        for q in range(n_seq):
            c = q * n_per_seq + step
            s_ref = s_refs[q]
            out_cols = []
            for h in heads:
                s_old = s_ref[h]
                s_b = s_old.astype(bf16)
                q_eff = (q_g[blk(c)][hrows(h)] - a_uw[c][hrows(h), DN_DV:]).astype(bf16)
                o = a_uw[c][hrows(h), :DN_DV] + dot(q_eff, s_b)
                s_ref[h] = (s_old * g_last[blk(c)][hrows(h)][0:1, :] + k_uw[c][h][:, :DN_DV]
                            - dot(k_uw[c][h][:, DN_DV:].astype(bf16), s_b))
                o = o * lax.rsqrt(jnp.mean(o * o, axis=-1, keepdims=True) + EPS) * ng_ref[...]
                zh = z[c * CHUNK:(c + 1) * CHUNK, h * DN_DV:(h + 1) * DN_DV].astype(f32)
                out_cols.append(o * (zh * _sigmoid(zh)))
            out_rows[c] = jnp.concatenate(out_cols, axis=1)
    return jnp.concatenate(out_rows, axis=0)


def _gdn_kernel(dq_ref, dk_ref, dv_ref, dz_ref, ba_ref, mq_ref, mk_ref, mv_ref, mz_ref, mba_ref,
                alog_ref, dtb_ref, ng_ref, bd_ref, cat_ref, lcum_ref, sel_ref,
                o_ref, om_ref, s_ref):
    consts = (alog_ref, dtb_ref, ng_ref, bd_ref, cat_ref, lcum_ref, sel_ref)
    s_refs = [s_ref.at[q] for q in range(GDN_NSEQ)]

    @pl.when(pl.program_id(1) == 0)
    def _():
        s_ref[0] = jnp.zeros(s_ref.shape[1:], f32)
        om_ref[...] = _gdn_block(1, mq_ref[...], mk_ref[...], mv_ref[...], mba_ref[...],
                                 mz_ref[...], s_refs[:1], *consts).astype(om_ref.dtype)
        for q in range(1, GDN_NSEQ):
            s_ref[q] = s_ref[0]

    merge = lambda ref: ref[...].reshape(GDN_NSEQ * GDN_ROWS, ref.shape[-1])
    out = _gdn_block(GDN_NCH, merge(dq_ref), merge(dk_ref), merge(dv_ref), merge(ba_ref),
                     merge(dz_ref), s_refs, *consts)
    o_ref[...] = out.reshape(o_ref.shape).astype(o_ref.dtype)


def _gdn(dq, dk, dv, dz, ba, mq, mk, mv, mz, mba, alog_row, dtb_row, ng_row):
    nb, s_len, _ = dq.shape
    assert nb % GDN_NSEQ == 0
    row = lambda w: pl.BlockSpec((GDN_NSEQ, GDN_ROWS, w), lambda b, t: (b, t, 0))
    mrow = lambda w: pl.BlockSpec((CHUNK, w), lambda b, t: (MP // CHUNK - 1, 0))
    bd, cat, lcum, sel = _gdn_consts()
    bd, cat = jnp.asarray(bd), jnp.asarray(cat)
    lcum, sel = jnp.asarray(lcum).astype(bf16), jnp.asarray(sel).astype(bf16)
    return pl.pallas_call(
        _gdn_kernel,
        grid=(nb // GDN_NSEQ, s_len // GDN_ROWS),
        in_specs=[row(DN_WIDTH)] * 4 + [row(LANES)] + [mrow(DN_WIDTH)] * 4 + [mrow(LANES)] + [
            _const_spec((1, LANES)), _const_spec((1, LANES)),
            _const_spec((1, DN_DV)), _const_spec(bd.shape), _const_spec(cat.shape),
            _const_spec(lcum.shape), _const_spec(sel.shape)],
        out_specs=[row(DN_WIDTH), pl.BlockSpec((None, CHUNK, DN_WIDTH), lambda b, t: (b, 0, 0))],
        out_shape=[jax.ShapeDtypeStruct((nb, s_len, DN_WIDTH), bf16),
                   jax.ShapeDtypeStruct((nb // GDN_NSEQ, CHUNK, DN_WIDTH), bf16)],
        scratch_shapes=[pltpu.VMEM((GDN_NSEQ, DN_HEADS, DN_DK, DN_DV), f32)],
        compiler_params=pltpu.CompilerParams(
            dimension_semantics=("parallel", "arbitrary"), vmem_limit_bytes=VMEM_LIMIT),
        name="gdn",
    )(dq, dk, dv, dz, ba, mq, mk, mv, mz, mba, alog_row, dtb_row, ng_row, bd, cat, lcum, sel)


def _mix_and_norm(x, oda, odn, wout_ref, g2_ref):
    mix = jnp.concatenate([oda, odn], axis=1)
    h2 = x + jnp.dot(mix, wout_ref[...], preferred_element_type=f32)
    u2 = h2 * lax.rsqrt(jnp.mean(h2 * h2, axis=-1, keepdims=True) + EPS) * g2_ref[...]
    return h2, u2.astype(bf16)


def _ffn_halo_kernel(x_ref, oda_ref, odn_ref, wout_ref, g2_ref, wup_ref, halo_ref):
    _, u2 = _mix_and_norm(x_ref[...], oda_ref[...], odn_ref[...], wout_ref, g2_ref)
    halo_ref[...] = jnp.dot(u2, wup_ref[...], preferred_element_type=f32)


def _ffn_kernel(x_ref, oda_ref, odn_ref, halo_ref, wout_ref, g2_ref, wup_ref, cw_ref, cb_ref,
                wdown_ref, gf_ref, o_ref, *scratch):
    tm = FFN_TM
    carries = scratch[0:FFN_TILES + 1]
    accs = scratch[FFN_TILES + 1:2 * FFN_TILES + 1]
    hbufs = scratch[2 * FFN_TILES + 1:]
    t = pl.program_id(1)

    @pl.when(t == 0)
    def _():
        carries[0][...] = halo_ref[...]

    @pl.when(t > 0)
    def _():
        carries[0][...] = carries[FFN_TILES][...]

    n_chunks = D_FF // FFN_CW
    stages = [(a, c) for a in range(FFN_TILES) for c in range(n_chunks)]
    col_pair = lambda c: (slice(c * FFN_CW, (c + 1) * FFN_CW),
                          slice(D_FF + c * FFN_CW, D_FF + (c + 1) * FFN_CW))
    buf_pair = lambda s: hbufs[2 * (s % FFN_NBUF):2 * (s % FFN_NBUF) + 2]
    rows = lambda a: slice(a * tm, (a + 1) * tm)
    normed = {}

    def up_part(s):
        a, c = stages[s]
        if c == 0:
            normed[a] = _mix_and_norm(x_ref[rows(a), :], oda_ref[rows(a), :],
                                      odn_ref[rows(a), :], wout_ref, g2_ref)
        u2 = normed[a][1]
        for cols, hbuf in zip(col_pair(c), buf_pair(s)):
            hbuf[0:SUBLANES, :] = carries[a][:, cols]
            hup = jnp.dot(u2, wup_ref[:, cols], preferred_element_type=f32)
            hbuf[SUBLANES:SUBLANES + tm, :] = hup
            carries[a + 1][:, cols] = hup[tm - SUBLANES:, :]

    def conv_part(cols, hbuf):
        y = cb_ref[:, cols]
        for d in range(FFN_CONV):
            y = y + cw_ref[FFN_CONV - 1 - d:FFN_CONV - d, cols] * hbuf[SUBLANES - d:SUBLANES - d + tm, :]
        return y

    for s in range(FFN_AHEAD):
        up_part(s)
    for s, (a, c) in enumerate(stages):
        if s + FFN_AHEAD < len(stages):
            up_part(s + FFN_AHEAD)
        gate, val = (conv_part(cols, hbuf) for cols, hbuf in zip(col_pair(c), buf_pair(s)))
        act = (gate * _sigmoid(gate) * val).astype(bf16)
        part = jnp.dot(act, wdown_ref[c * FFN_CW:(c + 1) * FFN_CW, :], preferred_element_type=f32)
        if c == 0:
            accs[a][...] = part
        else:
            accs[a][...] += part
        if c == n_chunks - 1:
            y = normed[a][0] + accs[a][...]
            o_ref[rows(a), :] = (y * lax.rsqrt(jnp.mean(y * y, axis=-1, keepdims=True) + EPS)
                                 * gf_ref[...])


def _ffn_halo(xm, odam, odnm, w_out, g2, w_up):
    return pl.pallas_call(
        _ffn_halo_kernel,
        out_shape=jax.ShapeDtypeStruct((HALO_ROWS, 2 * D_FF), f32),
        compiler_params=pltpu.CompilerParams(vmem_limit_bytes=VMEM_LIMIT),
        name="ffn_halo",
    )(xm, odam, odnm, w_out, g2, w_up)


def _ffn(x, oda, odn, halo, w_out, g2, w_up, conv_w, conv_b, w_down, gf):
    nb, s_len, _ = x.shape
    tm = FFN_TM
    step_rows = FFN_TILES * tm
    row = lambda w: pl.BlockSpec((None, step_rows, w), lambda b, t: (b, t, 0))
    return pl.pallas_call(
        _ffn_kernel,
        grid=(nb, s_len // step_rows),
        in_specs=[row(D_MODEL), row(DA_WIDTH), row(DN_WIDTH),
                  _const_spec((SUBLANES, 2 * D_FF)), _const_spec((D_MODEL, D_MODEL)),
                  _const_spec((1, D_MODEL)), _const_spec((D_MODEL, 2 * D_FF)),
                  _const_spec((FFN_CONV, 2 * D_FF)), _const_spec((1, 2 * D_FF)),
                  _const_spec((D_FF, D_MODEL)), _const_spec((1, D_MODEL))],
        out_specs=row(D_MODEL),
        out_shape=jax.ShapeDtypeStruct((nb, s_len, D_MODEL), f32),
        scratch_shapes=([pltpu.VMEM((SUBLANES, 2 * D_FF), f32)] * (FFN_TILES + 1)
                        + [pltpu.VMEM((tm, D_MODEL), f32)] * FFN_TILES
                        + [pltpu.VMEM((SUBLANES + tm, FFN_CW), f32)] * (2 * FFN_NBUF)),
        compiler_params=pltpu.CompilerParams(
            dimension_semantics=("parallel", "arbitrary"), vmem_limit_bytes=VMEM_LIMIT),
        name="ffn",
    )(x, oda, odn, halo, w_out, g2, w_up, conv_w, conv_b, w_down, gf)


def kernel(x, meta_tokens, rel_bias, norm1_g, w_in, da_lambda, da_subln_g, dn_conv_w, dn_A_log,
           dn_dt_bias, dn_norm_g, w_out, norm2_g, w_up, ffn_conv_w, ffn_conv_b, w_down,
           final_norm_g):
    nb, s_len, _ = x.shape
    assert s_len % TQ == 0 and s_len % GDN_ROWS == 0 and s_len % FFN_TM == 0
    w_in0 = w_in[0]
    w_main = w_in0[:, :N_MAIN].astype(bf16)
    w_small = jnp.pad(w_in0[:, N_MAIN:], ((0, 0), (0, LANES - 2 * DN_HEADS))).astype(bf16)
    w_out_b = w_out[0].astype(bf16)
    w_up_b = w_up[0].astype(bf16)
    w_down_b = w_down[0].astype(bf16)
    g1 = norm1_g[0].reshape(1, D_MODEL).astype(f32)
    g2 = norm2_g[0].reshape(1, D_MODEL).astype(f32)
    gf = final_norm_g.reshape(1, D_MODEL).astype(f32)
    meta_pad = jnp.pad(meta_tokens.astype(x.dtype), ((MP - N_META, 0), (0, 0)))

    bnear, bmeta, bmm, lam_tile = _bias_tiles(rel_bias, da_lambda[0])
    lam1 = lam_tile[0, :1]

    conv_w = dn_conv_w[0].astype(f32)
    no_halo = jnp.zeros((SUBLANES, GDN_QKV), f32)
    mq, mk, mv, mdq, mdk, mdv, mdz, mba, mtail = [
        a[0] for a in _proj(meta_pad[None], no_halo, g1, w_main, w_small, conv_w, MP)]
    q, k, v, dq, dk, dv, dz, ba, _ = _proj(x, mtail, g1, w_main, w_small, conv_w, PROJ_TM)

    subln = da_subln_g[0].astype(f32)
    o_da = _attention(lam1, q, k, v, mk, mv, bnear, bmeta,
                      jnp.broadcast_to(subln[:, None], (DA_HEAD_DIM, TQ)))
    o_da_m = _attention_meta(lam1, mq, mk, mv, bmm,
                             jnp.broadcast_to(subln[:, None], (DA_HEAD_DIM, MP)))

    gate_row = lambda p: jnp.pad(p[0].astype(f32), (DN_HEADS, LANES - 2 * DN_HEADS)).reshape(1, LANES)
    o_dn, o_dn_m = _gdn(dq, dk, dv, dz, ba, mdq, mdk, mdv, mdz, mba,
                        gate_row(dn_A_log), gate_row(dn_dt_bias),
                        dn_norm_g[0].reshape(1, DN_DV).astype(f32))

    halo = _ffn_halo(meta_pad[MP - HALO_ROWS:], o_da_m[MP - HALO_ROWS:],
                     o_dn_m[0, CHUNK - HALO_ROWS:], w_out_b, g2, w_up_b)[HALO_ROWS - SUBLANES:]
    return _ffn(x, o_da, o_dn, halo, w_out_b, g2, w_up_b, ffn_conv_w[0].astype(f32),
                ffn_conv_b[0].reshape(1, 2 * D_FF).astype(f32), w_down_b, gf)
```

```python
import functools
import math

import numpy as np
import jax
import jax.numpy as jnp
from jax import lax
from jax.experimental import pallas as pl
from jax.experimental.pallas import tpu as pltpu

f32 = jnp.float32
bf16 = jnp.bfloat16

D_MODEL = 1024
CHUNK = 64
N_META = 16
EPS = 1e-6
NEG_INF = -1e30
LAMBDA_INIT = 0.8 - 0.6 * math.exp(-0.3 * 0)
LOG2E = math.log2(math.e)

DA_HEADS = 4
DA_HEAD_DIM = 128
DA_HALF = 64
DA_WIDTH = DA_HEADS * DA_HEAD_DIM
DN_HEADS = 4
DN_DK = 128
DN_DV = 128
DN_WIDTH = DN_HEADS * DN_DV
DN_CONV = 4
N_BUCKETS = 32
MAX_DISTANCE = 128
D_FF = 2816
FFN_CONV = 3

N_MAIN = 7 * 512
LANES = 128
SUBLANES = 8
MP = 128
TQ = 256
TK = 256
V_EXT = DA_HEAD_DIM + 16
VT_CHUNK = 512
PROJ_TM = 256
GDN_ROWS = 256
FFN_TILES = 1
FFN_TM = 256
FFN_CW = 256
FFN_AHEAD = 4
FFN_NBUF = FFN_AHEAD + 1
HALO_ROWS = 16
VMEM_LIMIT = 52 * 1024 * 1024


def _const_spec(shape):
    nd = len(shape)
    return pl.BlockSpec(shape, lambda *_: (0,) * nd, pipeline_mode=pl.Buffered(1))


def _prep_kernel(far_ref, table_ref, lam_in_ref, bnear_in, bmeta_in, bmm_in,
                 bnear_out, bmeta_out, bmm_out, lam_out):
    h = pl.program_id(0)
    c_far = table_ref[far_ref[0], h]

    def lookup(bkt):
        out = jnp.full(bkt.shape, NEG_INF, f32)
        for b in range(N_BUCKETS):
            out = jnp.where(bkt == b, (table_ref[b, h] - c_far) * LOG2E, out)
        return out

    for i in range(2):
        bnear_out[i] = lookup(bnear_in[i])
        bmeta_out[i] = lookup(bmeta_in[i])
    bmm_out[...] = lookup(bmm_in[...])
    lv = lam_in_ref[...]
    s1 = jnp.sum(lv[0:1] * lv[1:2], axis=-1, keepdims=True)
    s2 = jnp.sum(lv[2:3] * lv[3:4], axis=-1, keepdims=True)
    lam = jnp.exp(s1) - jnp.exp(s2) + LAMBDA_INIT
    lam_out[...] = jnp.broadcast_to(lam, lam_out.shape)


def _t5_bucket(rel):
    nb = N_BUCKETS // 2
    max_exact = nb // 2
    ret = jnp.where(rel > 0, nb, 0)
    n = jnp.abs(rel)
    nf = jnp.maximum(n, 1).astype(jnp.float32)
    large = max_exact + (jnp.log(nf / max_exact) / math.log(MAX_DISTANCE / max_exact)
                         * (nb - max_exact)).astype(jnp.int32)
    large = jnp.minimum(large, nb - 1)
    return ret + jnp.where(n < max_exact, n, large)


def _bias_tiles(rel_bias, da_lambda):
    r = jnp.arange(TK, dtype=jnp.int32)[:, None]
    c = jnp.arange(TQ, dtype=jnp.int32)[None, :]
    diag = jnp.where((r // CHUNK) <= (c // CHUNK), _t5_bucket(r - c), -1)
    prev = _t5_bucket(r - c - TK)
    bnear = jnp.stack([diag, prev]).astype(jnp.int32)
    far = _t5_bucket(jnp.full((1,), -(TK + 1), jnp.int32)).astype(jnp.int32)
    rm = jnp.arange(MP, dtype=jnp.int32)[:, None]
    valid = rm >= (MP - N_META)
    kpos = rm - (MP - N_META)
    m0 = jnp.where(valid, _t5_bucket(kpos - (N_META + c)), -1)
    m1 = jnp.where(valid, jnp.broadcast_to(far[0], (MP, TQ)), -1)
    bmeta = jnp.stack([m0, m1]).astype(jnp.int32)
    cm = jnp.arange(MP, dtype=jnp.int32)[None, :]
    bmm = jnp.where(valid, _t5_bucket(rm - cm), -1).astype(jnp.int32)

    smem = pl.BlockSpec(memory_space=pltpu.SMEM)
    return pl.pallas_call(
        _prep_kernel,
        grid=(DA_HEADS,),
        in_specs=[smem, smem,
                  pl.BlockSpec((4, DA_HALF), lambda h: (0, 0)),
                  pl.BlockSpec((2, TK, TQ), lambda h: (0, 0, 0)),
                  pl.BlockSpec((2, MP, TQ), lambda h: (0, 0, 0)),
                  pl.BlockSpec((MP, MP), lambda h: (0, 0))],
        out_specs=[pl.BlockSpec((None, 2, TK, TQ), lambda h: (h, 0, 0, 0)),
                   pl.BlockSpec((None, 2, MP, TQ), lambda h: (h, 0, 0, 0)),
                   pl.BlockSpec((None, MP, MP), lambda h: (h, 0, 0)),
                   pl.BlockSpec((SUBLANES, LANES), lambda h: (0, 0))],
        out_shape=[jax.ShapeDtypeStruct((DA_HEADS, 2, TK, TQ), f32),
                   jax.ShapeDtypeStruct((DA_HEADS, 2, MP, TQ), f32),
                   jax.ShapeDtypeStruct((DA_HEADS, MP, MP), f32),
                   jax.ShapeDtypeStruct((SUBLANES, LANES), f32)],
        name="prep",
    )(far, rel_bias.astype(f32), da_lambda.astype(f32), bnear, bmeta, bmm)


def _sigmoid(x):
    return 1.0 / (1.0 + jnp.exp(-x))


def _proj_kernel(x_ref, halo_ref, g_ref, w_ref, ws_ref, cw_ref, q_ref, k_ref, v_ref, dq_ref,
                 dk_ref, dv_ref, dz_ref, ba_ref, tail_ref, carry_ref, carry_next, u_ref):
    tm = x_ref.shape[0]
    t = pl.program_id(1)

    @pl.when(t == 0)
    def _():
        carry_ref[...] = halo_ref[...]

    @pl.when(t > 0)
    def _():
        carry_ref[...] = carry_next[...]

    x = x_ref[...]
    ms = jnp.mean(x * x, axis=-1, keepdims=True)
    u_ref[...] = (x * lax.rsqrt(ms + EPS) * g_ref[...]).astype(bf16)
    outs = (q_ref, k_ref, v_ref, dq_ref, dk_ref, dv_ref, dz_ref)
    n_out = len(outs)
    project = lambda j: jnp.dot(u_ref[...], w_ref[:, j * 512:(j + 1) * 512],
                                preferred_element_type=f32)

    def conv_silu(j, r):
        cols = slice((j - 3) * DN_WIDTH, (j - 2) * DN_WIDTH)
        carry_next[:, cols] = r[tm - SUBLANES:, :]
        ext = jnp.concatenate([carry_ref[:, cols], r], axis=0)
        y = cw_ref[0:1, cols] * ext
        for d in range(1, DN_CONV):
            y = cw_ref[d:d + 1, cols] * ext + pltpu.roll(y, 1, 0)
        y = y[SUBLANES:, :]
        return y * _sigmoid(y)

    def l2norm_heads(y, scale):
        parts = []
        for h in range(DN_HEADS):
            yh = y[:, h * DN_DK:(h + 1) * DN_DK]
            parts.append(yh * (lax.rsqrt(jnp.sum(yh * yh, axis=-1, keepdims=True) + EPS) * scale))
        return jnp.concatenate(parts, axis=1)

    order = (3, 0, 4, 1, 5, 2, 6)
    ahead = 2
    pending =[project(j) for j in order[:ahead]]
    for pos, j in enumerate(order):
        r = pending.pop(0)
        if pos + ahead < n_out:
            pending.append(project(order[pos + ahead]))
        elif pos + ahead == n_out:
            ba_ref[...] = jnp.dot(u_ref[...], ws_ref[...], preferred_element_type=f32)
        if j == 0:
            r = r * (DA_HALF ** -0.5 * LOG2E)
        elif j in (3, 4):
            r = l2norm_heads(conv_silu(j, r), DN_DK ** -0.5 if j == 3 else 1.0)
        elif j == 5:
            r = conv_silu(j, r)
        outs[j][...] = r.astype(outs[j].dtype)
    tail_ref[...] = carry_next[...]


def _proj(x3, halo, g1, w_main, w_small, conv_w, tm):
    nb, rows, _ = x3.shape
    row_spec = lambda w: pl.BlockSpec((None, tm, w), lambda b, t: (b, t, 0))
    outs = [jax.ShapeDtypeStruct((nb, rows, 512), bf16)] * 7 + [
        jax.ShapeDtypeStruct((nb, rows, LANES), f32),
        jax.ShapeDtypeStruct((nb, SUBLANES, GDN_QKV), f32)]
    return pl.pallas_call(
        _proj_kernel,
        grid=(nb, rows // tm),
        in_specs=[row_spec(D_MODEL), _const_spec((SUBLANES, GDN_QKV)), _const_spec((1, D_MODEL)),
                  _const_spec((D_MODEL, N_MAIN)), _const_spec((D_MODEL, LANES)),
                  _const_spec((DN_CONV, GDN_QKV))],
        out_specs=[row_spec(512)] * 7 + [row_spec(LANES),
                   pl.BlockSpec((None, SUBLANES, GDN_QKV), lambda b, t: (b, 0, 0))],
        out_shape=outs,
        scratch_shapes=([pltpu.VMEM((SUBLANES, GDN_QKV), f32)] * 2
                        + [pltpu.VMEM((tm, D_MODEL), bf16)]),
        compiler_params=pltpu.CompilerParams(
            dimension_semantics=("parallel", "arbitrary"), vmem_limit_bytes=VMEM_LIMIT),
        name="proj",
    )(x3, halo, g1, w_main, w_small, conv_w)


def _dot_nt(a, b):
    return lax.dot_general(a, b, (((1,), (1,)), ((), ())), preferred_element_type=f32)


def _dot_tn(a, b):
    return lax.dot_general(a, b, (((0,), (0,)), ((), ())), preferred_element_type=f32)


def _stack_components(q):
    qt = q.astype(f32).T
    row = lax.broadcasted_iota(jnp.int32, qt.shape, 0)
    q0 = jnp.where(row < DA_HALF, qt, 0.0)
    q1 = jnp.where(row >= DA_HALF, qt, 0.0)
    return jnp.concatenate([q0, q1], axis=1).astype(bf16)


def _colmax(s):
    return jnp.max(s, axis=0, keepdims=True)


def _values_t_ext(v):
    vt = v.astype(f32).T.astype(bf16)
    return jnp.concatenate([vt, jnp.ones((V_EXT - DA_HEAD_DIM, v.shape[0]), bf16)], axis=0)


def _softmax_step(stats, s, smax, vt_pend, p_ref, acc_ref):
    m, alpha_pend = stats
    pv = jnp.dot(vt_pend, p_ref[...], preferred_element_type=f32)
    m_new = jnp.maximum(m, smax)
    alpha = jnp.exp2(m - m_new)
    p = jnp.exp2((s - m_new).astype(bf16))
    acc_ref[...] = alpha_pend * acc_ref[...] + pv
    return (m_new, alpha), p


def _attn_finish(stats, vt_pend, p_ref, acc_ref, lam, g, tq):
    m, alpha_pend = stats
    acc = alpha_pend * acc_ref[...] + jnp.dot(vt_pend, p_ref[...], preferred_element_type=f32)
    l = acc[DA_HEAD_DIM:DA_HEAD_DIM + 1, :]
    acc = acc[:DA_HEAD_DIM, :]
    o = acc[:, :tq] / l[:, :tq] - lam * (acc[:, tq:] / l[:, tq:])
    ms = jnp.mean(o * o, axis=0, keepdims=True)
    y = o * lax.rsqrt(ms + EPS) * g * (1.0 - LAMBDA_INIT)
    return y.T


def _attn_init(tq, p_ref, acc_ref):
    p_ref[...] = jnp.zeros(p_ref.shape, bf16)
    acc_ref[...] = jnp.zeros(acc_ref.shape, f32)
    return (jnp.full((1, 2 * tq), -3e38, f32), jnp.ones((1, 2 * tq), f32))


def _both(b):
    return jnp.concatenate([b, b], axis=1)


def _attn_kernel(lam_ref, q_ref, k_ref, v_ref, km_ref, vm_ref, bnear_ref, bmeta_ref, g_ref,
                 o_ref, *scratch):
    i = pl.program_id(1)
    n_far = jnp.maximum(i - 1, 0)
    heads = range(DA_HEADS)
    cols = lambda h: slice(h * DA_HEAD_DIM, (h + 1) * DA_HEAD_DIM)
    qz = [_stack_components(q_ref[:, cols(h)]) for h in heads]

    def scores(h, j):
        start = pl.multiple_of(j * TK, TK)
        return jnp.dot(k_ref[pl.ds(start, TK), cols(h)], qz[h], preferred_element_type=f32)

    s_bufs = scratch[0:DA_HEADS]
    p_bufs = scratch[DA_HEADS:2 * DA_HEADS]
    accs = scratch[2 * DA_HEADS:3 * DA_HEADS]
    vts = scratch[3 * DA_HEADS:4 * DA_HEADS]
    vmt_ref = scratch[4 * DA_HEADS]

    @pl.when(i == 0)
    def _():
        for h in heads:
            for c0 in range(0, k_ref.shape[0], VT_CHUNK):
                vts[h][:, c0:c0 + VT_CHUNK] = _values_t_ext(v_ref[c0:c0 + VT_CHUNK, cols(h)])
            vmt_ref[h] = _values_t_ext(vm_ref[:, cols(h)])

    def values(h, j):
        start = pl.multiple_of(jnp.maximum(j, 0) * TK, TK)
        return vts[h][:, pl.ds(start, TK)]

    stats, smax = [], []
    for h in heads:
        stats.append(_attn_init(TQ, p_bufs[h], accs[h]))
        s0 = scores(h, 0)
        s_bufs[h][0] = s0
        smax.append(_colmax(s0))

    def far_body(j, carry):
        stats, smax = carry
        slot = lax.rem(j, 2)
        new_stats, new_smax = [], []
        for h in heads:
            st, p = _softmax_step(stats[h], s_bufs[h][slot], smax[h], values(h, j - 1),
                                  p_bufs[h], accs[h])
            p_bufs[h][...] = p
            new_stats.append(st)
        for h in heads:
            s_next = scores(h, j + 1)
            s_bufs[h][1 - slot] = s_next
            new_smax.append(_colmax(s_next))
        return tuple(new_stats), tuple(new_smax)

    stats, _ = lax.fori_loop(0, n_far, far_body, (tuple(stats), tuple(smax)))
    gate = jnp.where(i >= 1, 0.0, NEG_INF).astype(f32)
    for h in heads:
        p_h, acc_h = p_bufs[h], accs[h]
        s = s_bufs[h][lax.rem(n_far, 2)] + _both(bnear_ref[h, 1] + gate)
        st, p = _softmax_step(stats[h], s, _colmax(s), values(h, n_far - 1), p_h, acc_h)
        p_h[...] = p
        s = scores(h, i) + _both(bnear_ref[h, 0])
        st, p = _softmax_step(st, s, _colmax(s), values(h, i - 1), p_h, acc_h)
        p_h[...] = p
        s = (jnp.dot(km_ref[:, cols(h)], qz[h], preferred_element_type=f32)
             + _both(bmeta_ref[h, jnp.minimum(i, 1)]))
        st, p = _softmax_step(st, s, _colmax(s), values(h, i), p_h, acc_h)
        p_meta = p_h.at[0:MP, :]
        p_meta[...] = p
        o_ref[:, cols(h)] = _attn_finish(st, vmt_ref[h], p_meta, acc_h, lam_ref[0],
                                         g_ref[...], TQ).astype(o_ref.dtype)


def _attn_meta_kernel(lam_ref, q_ref, km_ref, vm_ref, bmm_ref, g_ref, o_ref, p_buf, acc_ref):
    qz = _stack_components(q_ref[...])
    vmt = _values_t_ext(vm_ref[...])
    stats = _attn_init(MP, p_buf, acc_ref)
    s = jnp.dot(km_ref[...], qz, preferred_element_type=f32) + _both(bmm_ref[...])
    stats, p = _softmax_step(stats, s, _colmax(s), vmt, p_buf, acc_ref)
    p_buf[...] = p
    o_ref[...] = _attn_finish(stats, vmt, p_buf, acc_ref, lam_ref[0], g_ref[...],
                              MP).astype(o_ref.dtype)


def _attention(lam1, q, k, v, km, vm, bnear, bmeta, g_tile):
    nb, s_len, _ = q.shape
    smem = pl.BlockSpec(memory_space=pltpu.SMEM)
    return pl.pallas_call(
        _attn_kernel,
        grid=(nb, s_len // TQ),
        in_specs=[smem,
                  pl.BlockSpec((None, TQ, DA_WIDTH), lambda b, i: (b, i, 0)),
                  pl.BlockSpec((None, s_len, DA_WIDTH), lambda b, i: (b, 0, 0)),
                  pl.BlockSpec((None, s_len, DA_WIDTH), lambda b, i: (b, 0, 0)),
                  _const_spec((MP, DA_WIDTH)), _const_spec((MP, DA_WIDTH)),
                  _const_spec((DA_HEADS, 2, TK, TQ)), _const_spec((DA_HEADS, 2, MP, TQ)),
                  _const_spec((DA_HEAD_DIM, TQ))],
        out_specs=pl.BlockSpec((None, TQ, DA_WIDTH), lambda b, i: (b, i, 0)),
        out_shape=jax.ShapeDtypeStruct((nb, s_len, DA_WIDTH), bf16),
        scratch_shapes=([pltpu.VMEM((2, TK, 2 * TQ), f32)] * DA_HEADS
                        + [pltpu.VMEM((TK, 2 * TQ), bf16)] * DA_HEADS
                        + [pltpu.VMEM((V_EXT, 2 * TQ), f32)] * DA_HEADS
                        + [pltpu.VMEM((V_EXT, s_len), bf16)] * DA_HEADS
                        + [pltpu.VMEM((DA_HEADS, V_EXT, MP), bf16)]),
        compiler_params=pltpu.CompilerParams(
            dimension_semantics=("parallel", "arbitrary"),
            vmem_limit_bytes=VMEM_LIMIT),
        name="attn",
    )(lam1, q, k, v, km, vm, bnear, bmeta, g_tile)


def _attention_meta(lam1, qm, km, vm, bmm, g_tile):
    smem = pl.BlockSpec(memory_space=pltpu.SMEM)
    head = pl.BlockSpec((MP, DA_HEAD_DIM), lambda h: (0, h))
    return pl.pallas_call(
        _attn_meta_kernel,
        grid=(DA_HEADS,),
        in_specs=[smem, head, head, head,
                  pl.BlockSpec((None, MP, MP), lambda h: (h, 0, 0)),
                  pl.BlockSpec((DA_HEAD_DIM, MP), lambda h: (0, 0))],
        out_specs=head,
        out_shape=jax.ShapeDtypeStruct((MP, DA_WIDTH), bf16),
        scratch_shapes=[pltpu.VMEM((MP, 2 * MP), bf16), pltpu.VMEM((V_EXT, 2 * MP), f32)],
        name="attn_meta",
    )(lam1, qm, km, vm, bmm, g_tile)


GDN_R = DN_HEADS * CHUNK
GDN_LEVELS = (2, 4, 8, 16, 32)
GDN_QKV = 3 * DN_WIDTH
GDN_NCH = GDN_ROWS // CHUNK
GDN_NSEQ = 2


def _gdn_consts():
    r = np.arange(GDN_R)[:, None]
    c = np.arange(GDN_R)[None, :]
    same = (r // CHUNK) == (c // CHUNK)
    bd = np.stack([same & (r >= c), same & (r > c), same]).astype(np.float32)
    i = np.arange(CHUNK)[:, None]
    j = np.arange(GDN_R)[None, :] % CHUNK
    lv = [i == j, (i > j) & ((i // 2) == (j // 2))]
    for s in GDN_LEVELS:
        lv.append(((i // (2 * s)) == (j // (2 * s))) & ((i & s) != 0) & ((j & s) == 0))
    cat = np.stack(lv).astype(np.float32)
    rr = np.arange(GDN_ROWS)[:, None]
    cc = np.arange(GDN_ROWS)[None, :]
    lcum = (((rr // CHUNK) == (cc // CHUNK)) & (rr >= cc)).astype(np.float32)
    sel = np.zeros((2, LANES, DN_HEADS * LANES), np.float32)
    for h in range(DN_HEADS):
        sel[0, h, h * LANES:(h + 1) * LANES] = 1.0
        sel[1, DN_HEADS + h, h * LANES:(h + 1) * LANES] = 1.0
    return bd, cat, lcum, sel


def _split3(x):
    hi = x.astype(bf16)
    r1 = x - hi.astype(f32)
    mid = r1.astype(bf16)
    lo = (r1 - mid.astype(f32)).astype(bf16)
    return hi, mid, lo


def _gdn_block(n_per_seq, xq, xk, xv, ba, z, s_refs, alog_ref, dtb_ref, ng_ref, bd_ref,
               cat_ref, lcum_ref, sel_ref):
    n_seq = len(s_refs)
    n = n_seq * n_per_seq
    seq_rows = n_per_seq * CHUNK
    heads = range(DN_HEADS)
    chunks = range(n)
    dot = functools.partial(jnp.dot, preferred_element_type=f32)

    def stack(a):
        return jnp.concatenate(
            [a[c * CHUNK:(c + 1) * CHUNK, h * LANES:(h + 1) * LANES]
             for c in chunks for h in heads], axis=0)

    qn_b, kn_b = stack(xq), stack(xk)
    qn, kn, vs = qn_b.astype(f32), kn_b.astype(f32), stack(xv).astype(f32)

    beta_t = _sigmoid(ba)
    xg = ba + dtb_ref[...]
    softplus = jnp.maximum(xg, 0.0) + jnp.log1p(jnp.exp(-jnp.abs(xg)))
    g_t = -jnp.exp(alog_ref[...]) * softplus
    lcum = lcum_ref[0:seq_rows, 0:seq_rows]
    g_parts = _split3(g_t)
    gcum = jnp.concatenate(
        [sum(dot(lcum, part[q * seq_rows:(q + 1) * seq_rows]) for part in g_parts)
         for q in range(n_seq)], axis=0)
    g_rep = stack(sum(dot(part, sel_ref[1]) for part in _split3(gcum)))
    b_rep = stack(sum(dot(part, sel_ref[0]) for part in _split3(beta_t)[:2]))
    g_end = jnp.concatenate(
        [jnp.broadcast_to(g_rep[(b + 1) * CHUNK - 1:(b + 1) * CHUNK, :], (CHUNK, LANES))
         for b in range(n * DN_HEADS)], axis=0)
    exp_g = jnp.exp(g_rep)
    rhs = jnp.concatenate([vs * b_rep, kn * (b_rep * exp_g)], axis=1).astype(bf16)
    q_g = qn * exp_g
    k_g = (kn * jnp.exp(g_end - g_rep)).astype(bf16)
    g_last = jnp.exp(g_end)

    tri = bd_ref[0] > 0.5
    block_b = bd_ref[2].astype(bf16)
    both = lambda a: jnp.concatenate([a, a], axis=1)
    blk = lambda c: slice(c * GDN_R, (c + 1) * GDN_R)
    decay, m_b, x_cat = [], [], []
    for c in chunks:
        g_row = g_rep[blk(c)].T[0:1, :]
        dec = jnp.exp(jnp.where(tri, both(g_rep[blk(c)]) - g_row, -jnp.inf))
        m = bd_ref[1] * both(b_rep[blk(c)]) * _dot_nt(kn_b[blk(c)], kn_b[blk(c)]) * dec
        m_cat = sum(m[h * CHUNK:(h + 1) * CHUNK] for h in heads)
        decay.append(dec)
        m_b.append(m.astype(bf16))
        x_cat.append(cat_ref[0] - m_cat * cat_ref[1])

    def to_bd(x):
        return jnp.concatenate([x.astype(bf16)] * DN_HEADS, axis=0) * block_b

    for lvl in range(len(GDN_LEVELS)):
        ys = [dot(x_cat[c].astype(bf16), m_b[c]) for c in chunks]
        zs = [dot(ys[c].astype(bf16), to_bd(x_cat[c])) for c in chunks]
        x_cat = [x_cat[c] - zs[c] * cat_ref[2 + lvl] for c in chunks]
    sol = [dot(to_bd(x_cat[c]), rhs[blk(c)]).astype(bf16) for c in chunks]
    a_qk = [(_dot_nt(qn_b[blk(c)], kn_b[blk(c)]) * decay[c]).astype(bf16) for c in chunks]
    a_uw = [dot(a_qk[c], sol[c]) for c in chunks]
    hrows = lambda h: slice(h * CHUNK, (h + 1) * CHUNK)
    k_uw = [[_dot_tn(k_g[blk(c)][hrows(h)], sol[c][hrows(h)]) for h in heads] for c in chunks]

    out_rows = [None] * n
    for step in range(n_per_seq):
        for q in range(n_seq):
            c = q * n_per_seq + step
            s_ref = s_refs[q]
            out_cols = []
            for h in heads:
                s_old = s_ref[h]
                s_b = s_old.astype(bf16)
                q_eff = (q_g[blk(c)][hrows(h)] - a_uw[c][hrows(h), DN_DV:]).astype(bf16)
                o = a_uw[c][hrows(h), :DN_DV] + dot(q_eff, s_b)
                s_ref[h] = (s_old * g_last[blk(c)][hrows(h)][0:1, :] + k_uw[c][h][:, :DN_DV]
                            - dot(k_uw[c][h][:, DN_DV:].astype(bf16), s_b))
                o = o * lax.rsqrt(jnp.mean(o * o, axis=-1, keepdims=True) + EPS) * ng_ref[...]
                zh = z[c * CHUNK:(c + 1) * CHUNK, h * DN_DV:(h + 1) * DN_DV].astype(f32)
                out_cols.append(o * (zh * _sigmoid(zh)))
            out_rows[c] = jnp.concatenate(out_cols, axis=1)
    return jnp.concatenate(out_rows, axis=0)


def _gdn_kernel(dq_ref, dk_ref, dv_ref, dz_ref, ba_ref, mq_ref, mk_ref, mv_ref, mz_ref, mba_ref,
                alog_ref, dtb_ref, ng_ref, bd_ref, cat_ref, lcum_ref, sel_ref,
                o_ref, om_ref, s_ref):
    consts = (alog_ref, dtb_ref, ng_ref, bd_ref, cat_ref, lcum_ref, sel_ref)
    s_refs = [s_ref.at[q] for q in range(GDN_NSEQ)]

    @pl.when(pl.program_id(1) == 0)
    def _():
        s_ref[0] = jnp.zeros(s_ref.shape[1:], f32)
        om_ref[...] = _gdn_block(1, mq_ref[...], mk_ref[...], mv_ref[...], mba_ref[...],
                                 mz_ref[...], s_refs[:1], *consts).astype(om_ref.dtype)
        for q in range(1, GDN_NSEQ):
            s_ref[q] = s_ref[0]

    merge = lambda ref: ref[...].reshape(GDN_NSEQ * GDN_ROWS, ref.shape[-1])
    out = _gdn_block(GDN_NCH, merge(dq_ref), merge(dk_ref), merge(dv_ref), merge(ba_ref),
                     merge(dz_ref), s_refs, *consts)
    o_ref[...] = out.reshape(o_ref.shape).astype(o_ref.dtype)


def _gdn(dq, dk, dv, dz, ba, mq, mk, mv, mz, mba, alog_row, dtb_row, ng_row):
    nb, s_len, _ = dq.shape
    assert nb % GDN_NSEQ == 0
    row = lambda w: pl.BlockSpec((GDN_NSEQ, GDN_ROWS, w), lambda b, t: (b, t, 0))
    mrow = lambda w: pl.BlockSpec((CHUNK, w), lambda b, t: (MP // CHUNK - 1, 0))
    bd, cat, lcum, sel = _gdn_consts()
    bd, cat = jnp.asarray(bd), jnp.asarray(cat)
    lcum, sel = jnp.asarray(lcum).astype(bf16), jnp.asarray(sel).astype(bf16)
    return pl.pallas_call(
        _gdn_kernel,
        grid=(nb // GDN_NSEQ, s_len // GDN_ROWS),
        in_specs=[row(DN_WIDTH)] * 4 + [row(LANES)] + [mrow(DN_WIDTH)] * 4 + [mrow(LANES)] + [
            _const_spec((1, LANES)), _const_spec((1, LANES)),
            _const_spec((1, DN_DV)), _const_spec(bd.shape), _const_spec(cat.shape),
            _const_spec(lcum.shape), _const_spec(sel.shape)],
        out_specs=[row(DN_WIDTH), pl.BlockSpec((None, CHUNK, DN_WIDTH), lambda b, t: (b, 0, 0))],
        out_shape=[jax.ShapeDtypeStruct((nb, s_len, DN_WIDTH), bf16),
                   jax.ShapeDtypeStruct((nb // GDN_NSEQ, CHUNK, DN_WIDTH), bf16)],
        scratch_shapes=[pltpu.VMEM((GDN_NSEQ, DN_HEADS, DN_DK, DN_DV), f32)],
        compiler_params=pltpu.CompilerParams(
            dimension_semantics=("parallel", "arbitrary"), vmem_limit_bytes=VMEM_LIMIT),
        name="gdn",
    )(dq, dk, dv, dz, ba, mq, mk, mv, mz, mba, alog_row, dtb_row, ng_row, bd, cat, lcum, sel)


def _mix_and_norm(x, oda, odn, wout_ref, g2_ref):
    mix = jnp.concatenate([oda, odn], axis=1)
    h2 = x + jnp.dot(mix, wout_ref[...], preferred_element_type=f32)
    u2 = h2 * lax.rsqrt(jnp.mean(h2 * h2, axis=-1, keepdims=True) + EPS) * g2_ref[...]
    return h2, u2.astype(bf16)


def _ffn_halo_kernel(x_ref, oda_ref, odn_ref, wout_ref, g2_ref, wup_ref, halo_ref):
    _, u2 = _mix_and_norm(x_ref[...], oda_ref[...], odn_ref[...], wout_ref, g2_ref)
    halo_ref[...] = jnp.dot(u2, wup_ref[...], preferred_element_type=f32)


def _ffn_kernel(x_ref, oda_ref, odn_ref, halo_ref, wout_ref, g2_ref, wup_ref, cw_ref, cb_ref,
                wdown_ref, gf_ref, o_ref, *scratch):
    tm = FFN_TM
    carries = scratch[0:FFN_TILES + 1]
    accs = scratch[FFN_TILES + 1:2 * FFN_TILES + 1]
    hbufs = scratch[2 * FFN_TILES + 1:]
    t = pl.program_id(1)

    @pl.when(t == 0)
    def _():
        carries[0][...] = halo_ref[...]

    @pl.when(t > 0)
    def _():
        carries[0][...] = carries[FFN_TILES][...]

    n_chunks = D_FF // FFN_CW
    stages = [(a, c) for a in range(FFN_TILES) for c in range(n_chunks)]
    col_pair = lambda c: (slice(c * FFN_CW, (c + 1) * FFN_CW),
                          slice(D_FF + c * FFN_CW, D_FF + (c + 1) * FFN_CW))
    buf_pair = lambda s: hbufs[2 * (s % FFN_NBUF):2 * (s % FFN_NBUF) + 2]
    rows = lambda a: slice(a * tm, (a + 1) * tm)
    normed = {}

    def up_part(s):
        a, c = stages[s]
        if c == 0:
            normed[a] = _mix_and_norm(x_ref[rows(a), :], oda_ref[rows(a), :],
                                      odn_ref[rows(a), :], wout_ref, g2_ref)
        u2 = normed[a][1]
        for cols, hbuf in zip(col_pair(c), buf_pair(s)):
            hbuf[0:SUBLANES, :] = carries[a][:, cols]
            hup = jnp.dot(u2, wup_ref[:, cols], preferred_element_type=f32)
            hbuf[SUBLANES:SUBLANES + tm, :] = hup
            carries[a + 1][:, cols] = hup[tm - SUBLANES:, :]

    def conv_part(cols, hbuf):
        y = cb_ref[:, cols]
        for d in range(FFN_CONV):
            y = y + cw_ref[FFN_CONV - 1 - d:FFN_CONV - d, cols] * hbuf[SUBLANES - d:SUBLANES - d + tm, :]
        return y

    for s in range(FFN_AHEAD):
        up_part(s)
    for s, (a, c) in enumerate(stages):
        if s + FFN_AHEAD < len(stages):
            up_part(s + FFN_AHEAD)
        gate, val = (conv_part(cols, hbuf) for cols, hbuf in zip(col_pair(c), buf_pair(s)))
        act = (gate * _sigmoid(gate) * val).astype(bf16)
        part = jnp.dot(act, wdown_ref[c * FFN_CW:(c + 1) * FFN_CW, :], preferred_element_type=f32)
        if c == 0:
            accs[a][...] = part
        else:
            accs[a][...] += part
        if c == n_chunks - 1:
            y = normed[a][0] + accs[a][...]
            o_ref[rows(a), :] = (y * lax.rsqrt(jnp.mean(y * y, axis=-1, keepdims=True) + EPS)
                                 * gf_ref[...])


def _ffn_halo(xm, odam, odnm, w_out, g2, w_up):
    return pl.pallas_call(
        _ffn_halo_kernel,
        out_shape=jax.ShapeDtypeStruct((HALO_ROWS, 2 * D_FF), f32),
        compiler_params=pltpu.CompilerParams(vmem_limit_bytes=VMEM_LIMIT),
        name="ffn_halo",
    )(xm, odam, odnm, w_out, g2, w_up)


def _ffn(x, oda, odn, halo, w_out, g2, w_up, conv_w, conv_b, w_down, gf):
    nb, s_len, _ = x.shape
    tm = FFN_TM
    step_rows = FFN_TILES * tm
    row = lambda w: pl.BlockSpec((None, step_rows, w), lambda b, t: (b, t, 0))
    return pl.pallas_call(
        _ffn_kernel,
        grid=(nb, s_len // step_rows),
        in_specs=[row(D_MODEL), row(DA_WIDTH), row(DN_WIDTH),
                  _const_spec((SUBLANES, 2 * D_FF)), _const_spec((D_MODEL, D_MODEL)),
                  _const_spec((1, D_MODEL)), _const_spec((D_MODEL, 2 * D_FF)),
                  _const_spec((FFN_CONV, 2 * D_FF)), _const_spec((1, 2 * D_FF)),
                  _const_spec((D_FF, D_MODEL)), _const_spec((1, D_MODEL))],
        out_specs=row(D_MODEL),
        out_shape=jax.ShapeDtypeStruct((nb, s_len, D_MODEL), f32),
        scratch_shapes=([pltpu.VMEM((SUBLANES, 2 * D_FF), f32)] * (FFN_TILES + 1)
                        + [pltpu.VMEM((tm, D_MODEL), f32)] * FFN_TILES
                        + [pltpu.VMEM((SUBLANES + tm, FFN_CW), f32)] * (2 * FFN_NBUF)),
        compiler_params=pltpu.CompilerParams(
            dimension_semantics=("parallel", "arbitrary"), vmem_limit_bytes=VMEM_LIMIT),
        name="ffn",
    )(x, oda, odn, halo, w_out, g2, w_up, conv_w, conv_b, w_down, gf)


def kernel(x, meta_tokens, rel_bias, norm1_g, w_in, da_lambda, da_subln_g, dn_conv_w, dn_A_log,
           dn_dt_bias, dn_norm_g, w_out, norm2_g, w_up, ffn_conv_w, ffn_conv_b, w_down,
           final_norm_g):
    nb, s_len, _ = x.shape
    assert s_len % TQ == 0 and s_len % GDN_ROWS == 0 and s_len % FFN_TM == 0
    w_in0 = w_in[0]
    w_main = w_in0[:, :N_MAIN].astype(bf16)
    w_small = jnp.pad(w_in0[:, N_MAIN:], ((0, 0), (0, LANES - 2 * DN_HEADS))).astype(bf16)
    w_out_b = w_out[0].astype(bf16)
    w_up_b = w_up[0].astype(bf16)
    w_down_b = w_down[0].astype(bf16)
    g1 = norm1_g[0].reshape(1, D_MODEL).astype(f32)
    g2 = norm2_g[0].reshape(1, D_MODEL).astype(f32)
    gf = final_norm_g.reshape(1, D_MODEL).astype(f32)
    meta_pad = jnp.pad(meta_tokens.astype(x.dtype), ((MP - N_META, 0), (0, 0)))

    bnear, bmeta, bmm, lam_tile = _bias_tiles(rel_bias, da_lambda[0])
    lam1 = lam_tile[0, :1]

    conv_w = dn_conv_w[0].astype(f32)
    no_halo = jnp.zeros((SUBLANES, GDN_QKV), f32)
    mq, mk, mv, mdq, mdk, mdv, mdz, mba, mtail = [
        a[0] for a in _proj(meta_pad[None], no_halo, g1, w_main, w_small, conv_w, MP)]
    q, k, v, dq, dk, dv, dz, ba, _ = _proj(x, mtail, g1, w_main, w_small, conv_w, PROJ_TM)

    subln = da_subln_g[0].astype(f32)
    o_da = _attention(lam1, q, k, v, mk, mv, bnear, bmeta,
                      jnp.broadcast_to(subln[:, None], (DA_HEAD_DIM, TQ)))
    o_da_m = _attention_meta(lam1, mq, mk, mv, bmm,
                             jnp.broadcast_to(subln[:, None], (DA_HEAD_DIM, MP)))

    gate_row = lambda p: jnp.pad(p[0].astype(f32), (DN_HEADS, LANES - 2 * DN_HEADS)).reshape(1, LANES)
    o_dn, o_dn_m = _gdn(dq, dk, dv, dz, ba, mdq, mdk, mdv, mdz, mba,
                        gate_row(dn_A_log), gate_row(dn_dt_bias),
                        dn_norm_g[0].reshape(1, DN_DV).astype(f32))

    halo = _ffn_halo(meta_pad[MP - HALO_ROWS:], o_da_m[MP - HALO_ROWS:],
                     o_dn_m[0, CHUNK - HALO_ROWS:], w_out_b, g2, w_up_b)[HALO_ROWS - SUBLANES:]
    return _ffn(x, o_da, o_dn, halo, w_out_b, g2, w_up_b, ffn_conv_w[0].astype(f32),
                ffn_conv_b[0].reshape(1, 2 * D_FF).astype(f32), w_down_b, gf)
```

```python
import functools
import math

import numpy as np
import jax
import jax.numpy as jnp
from jax import lax
from jax.experimental import pallas as pl
from jax.experimental.pallas import tpu as pltpu

f32 = jnp.float32
bf16 = jnp.bfloat16

D_MODEL = 1024
CHUNK = 64
N_META = 16
EPS = 1e-6
NEG_INF = -1e30
LAMBDA_INIT = 0.8 - 0.6 * math.exp(-0.3 * 0)
LOG2E = math.log2(math.e)

DA_HEADS = 4
DA_HEAD_DIM = 128
DA_HALF = 64
DA_WIDTH = DA_HEADS * DA_HEAD_DIM
DN_HEADS = 4
DN_DK = 128
DN_DV = 128
DN_WIDTH = DN_HEADS * DN_DV
DN_CONV = 4
N_BUCKETS = 32
MAX_DISTANCE = 128
D_FF = 2816
FFN_CONV = 3

N_MAIN = 7 * 512
LANES = 128
SUBLANES = 8
MP = 128
TQ = 256
TK = 256
V_EXT = DA_HEAD_DIM + 16
VT_CHUNK = 512
PROJ_TM = 512
GDN_ROWS = 256
FFN_TILES = 1
FFN_TM = 256
FFN_CW = 256
FFN_AHEAD = 4
FFN_NBUF = FFN_AHEAD + 1
HALO_ROWS = 16
VMEM_LIMIT = 52 * 1024 * 1024


def _const_spec(shape):
    nd = len(shape)
    return pl.BlockSpec(shape, lambda *_: (0,) * nd, pipeline_mode=pl.Buffered(1))


def _prep_kernel(far_ref, table_ref, lam_in_ref, bnear_in, bmeta_in, bmm_in,
                 bnear_out, bmeta_out, bmm_out, lam_out):
    h = pl.program_id(0)
    c_far = table_ref[far_ref[0], h]

    def lookup(bkt):
        out = jnp.full(bkt.shape, NEG_INF, f32)
        for b in range(N_BUCKETS):
            out = jnp.where(bkt == b, (table_ref[b, h] - c_far) * LOG2E, out)
        return out

    for i in range(2):
        bnear_out[i] = lookup(bnear_in[i])
        bmeta_out[i] = lookup(bmeta_in[i])
    bmm_out[...] = lookup(bmm_in[...])
    lv = lam_in_ref[...]
    s1 = jnp.sum(lv[0:1] * lv[1:2], axis=-1, keepdims=True)
    s2 = jnp.sum(lv[2:3] * lv[3:4], axis=-1, keepdims=True)
    lam = jnp.exp(s1) - jnp.exp(s2) + LAMBDA_INIT
    lam_out[...] = jnp.broadcast_to(lam, lam_out.shape)


def _t5_bucket(rel):
    nb = N_BUCKETS // 2
    max_exact = nb // 2
    ret = jnp.where(rel > 0, nb, 0)
    n = jnp.abs(rel)
    nf = jnp.maximum(n, 1).astype(jnp.float32)
    large = max_exact + (jnp.log(nf / max_exact) / math.log(MAX_DISTANCE / max_exact)
                         * (nb - max_exact)).astype(jnp.int32)
    large = jnp.minimum(large, nb - 1)
    return ret + jnp.where(n < max_exact, n, large)


def _bias_tiles(rel_bias, da_lambda):
    r = jnp.arange(TK, dtype=jnp.int32)[:, None]
    c = jnp.arange(TQ, dtype=jnp.int32)[None, :]
    diag = jnp.where((r // CHUNK) <= (c // CHUNK), _t5_bucket(r - c), -1)
    prev = _t5_bucket(r - c - TK)
    bnear = jnp.stack([diag, prev]).astype(jnp.int32)
    far = _t5_bucket(jnp.full((1,), -(TK + 1), jnp.int32)).astype(jnp.int32)
    rm = jnp.arange(MP, dtype=jnp.int32)[:, None]
    valid = rm >= (MP - N_META)
    kpos = rm - (MP - N_META)
    m0 = jnp.where(valid, _t5_bucket(kpos - (N_META + c)), -1)
    m1 = jnp.where(valid, jnp.broadcast_to(far[0], (MP, TQ)), -1)
    bmeta = jnp.stack([m0, m1]).astype(jnp.int32)
    cm = jnp.arange(MP, dtype=jnp.int32)[None, :]
    bmm = jnp.where(valid, _t5_bucket(rm - cm), -1).astype(jnp.int32)

    smem = pl.BlockSpec(memory_space=pltpu.SMEM)
    return pl.pallas_call(
        _prep_kernel,
        grid=(DA_HEADS,),
        in_specs=[smem, smem,
                  pl.BlockSpec((4, DA_HALF), lambda h: (0, 0)),
                  pl.BlockSpec((2, TK, TQ), lambda h: (0, 0, 0)),
                  pl.BlockSpec((2, MP, TQ), lambda h: (0, 0, 0)),
                  pl.BlockSpec((MP, MP), lambda h: (0, 0))],
        out_specs=[pl.BlockSpec((None, 2, TK, TQ), lambda h: (h, 0, 0, 0)),
                   pl.BlockSpec((None, 2, MP, TQ), lambda h: (h, 0, 0, 0)),
                   pl.BlockSpec((None, MP, MP), lambda h: (h, 0, 0)),
                   pl.BlockSpec((SUBLANES, LANES), lambda h: (0, 0))],
        out_shape=[jax.ShapeDtypeStruct((DA_HEADS, 2, TK, TQ), f32),
                   jax.ShapeDtypeStruct((DA_HEADS, 2, MP, TQ), f32),
                   jax.ShapeDtypeStruct((DA_HEADS, MP, MP), f32),
                   jax.ShapeDtypeStruct((SUBLANES, LANES), f32)],
        name="prep",
    )(far, rel_bias.astype(f32), da_lambda.astype(f32), bnear, bmeta, bmm)


def _sigmoid(x):
    return 1.0 / (1.0 + jnp.exp(-x))


def _proj_kernel(x_ref, halo_ref, g_ref, w_ref, ws_ref, cw_ref, q_ref, k_ref, v_ref, dq_ref,
                 dk_ref, dv_ref, dz_ref, ba_ref, tail_ref, carry_ref, carry_next, u_ref):
    tm = x_ref.shape[0]
    t = pl.program_id(1)

    @pl.when(t == 0)
    def _():
        carry_ref[...] = halo_ref[...]

    @pl.when(t > 0)
    def _():
        carry_ref[...] = carry_next[...]

    x = x_ref[...]
    ms = jnp.mean(x * x, axis=-1, keepdims=True)
    u_ref[...] = (x * lax.rsqrt(ms + EPS) * g_ref[...]).astype(bf16)
    outs = (q_ref, k_ref, v_ref, dq_ref, dk_ref, dv_ref, dz_ref)
    n_out = len(outs)
    project = lambda j: jnp.dot(u_ref[...], w_ref[:, j * 512:(j + 1) * 512],
                                preferred_element_type=f32)

    def conv_silu(j, r):
        cols = slice((j - 3) * DN_WIDTH, (j - 2) * DN_WIDTH)
        carry_next[:, cols] = r[tm - SUBLANES:, :]
        ext = jnp.concatenate([carry_ref[:, cols], r], axis=0)
        y = cw_ref[0:1, cols] * ext
        for d in range(1, DN_CONV):
            y = cw_ref[d:d + 1, cols] * ext + pltpu.roll(y, 1, 0)
        y = y[SUBLANES:, :]
        return y * _sigmoid(y)

    def l2norm_heads(y, scale):
        parts = []
        for h in range(DN_HEADS):
            yh = y[:, h * DN_DK:(h + 1) * DN_DK]
            parts.append(yh * (lax.rsqrt(jnp.sum(yh * yh, axis=-1, keepdims=True) + EPS) * scale))
        return jnp.concatenate(parts, axis=1)

    order = (3, 0, 4, 1, 5, 2, 6)
    ahead = 2
    pending =[project(j) for j in order[:ahead]]
    for pos, j in enumerate(order):
        r = pending.pop(0)
        if pos + ahead < n_out:
            pending.append(project(order[pos + ahead]))
        elif pos + ahead == n_out:
            ba_ref[...] = jnp.dot(u_ref[...], ws_ref[...], preferred_element_type=f32)
        if j == 0:
            r = r * (DA_HALF ** -0.5 * LOG2E)
        elif j in (3, 4):
            r = l2norm_heads(conv_silu(j, r), DN_DK ** -0.5 if j == 3 else 1.0)
        elif j == 5:
            r = conv_silu(j, r)
        if j == 0:
            r = r.T
        outs[j][...] = r.astype(outs[j].dtype)
    tail_ref[...] = carry_next[...]


def _proj(x3, halo, g1, w_main, w_small, conv_w, tm):
    nb, rows, _ = x3.shape
    row_spec = lambda w: pl.BlockSpec((None, tm, w), lambda b, t: (b, t, 0))
    outs = [jax.ShapeDtypeStruct((nb, 512, rows), bf16)] + [
        jax.ShapeDtypeStruct((nb, rows, 512), bf16)] * 6 + [
        jax.ShapeDtypeStruct((nb, rows, LANES), f32),
        jax.ShapeDtypeStruct((nb, SUBLANES, GDN_QKV), f32)]
    return pl.pallas_call(
        _proj_kernel,
        grid=(nb, rows // tm),
        in_specs=[row_spec(D_MODEL), _const_spec((SUBLANES, GDN_QKV)), _const_spec((1, D_MODEL)),
                  _const_spec((D_MODEL, N_MAIN)), _const_spec((D_MODEL, LANES)),
                  _const_spec((DN_CONV, GDN_QKV))],
        out_specs=[pl.BlockSpec((None, 512, tm), lambda b, t: (b, 0, t))]
                  + [row_spec(512)] * 6 + [row_spec(LANES),
                   pl.BlockSpec((None, SUBLANES, GDN_QKV), lambda b, t: (b, 0, 0))],
        out_shape=outs,
        scratch_shapes=([pltpu.VMEM((SUBLANES, GDN_QKV), f32)] * 2
                        + [pltpu.VMEM((tm, D_MODEL), bf16)]),
        compiler_params=pltpu.CompilerParams(
            dimension_semantics=("parallel", "arbitrary"), vmem_limit_bytes=VMEM_LIMIT),
        name="proj",
    )(x3, halo, g1, w_main, w_small, conv_w)


def _dot_nt(a, b):
    return lax.dot_general(a, b, (((1,), (1,)), ((), ())), preferred_element_type=f32)


def _dot_tn(a, b):
    return lax.dot_general(a, b, (((0,), (0,)), ((), ())), preferred_element_type=f32)


def _stack_components(qt):
    zero = jnp.zeros((DA_HALF, qt.shape[1]), qt.dtype)
    q0 = jnp.concatenate([qt[:DA_HALF], zero], axis=0)
    q1 = jnp.concatenate([zero, qt[DA_HALF:]], axis=0)
    return jnp.concatenate([q0, q1], axis=1)


def _colmax(s):
    return jnp.max(s, axis=0, keepdims=True)


def _values_t_ext(v):
    vt = v.astype(f32).T.astype(bf16)
    return jnp.concatenate([vt, jnp.ones((V_EXT - DA_HEAD_DIM, v.shape[0]), bf16)], axis=0)


def _softmax_step(stats, s, smax, vt_pend, p_ref, acc_ref):
    m, alpha_pend = stats
    pv = jnp.dot(vt_pend, p_ref[...], preferred_element_type=f32)
    m_new = jnp.maximum(m, smax)
    alpha = jnp.exp2(m - m_new)
    p = jnp.exp2((s - m_new).astype(bf16))
    acc_ref[...] = alpha_pend * acc_ref[...] + pv
    return (m_new, alpha), p


def _attn_finish(stats, vt_pend, p_ref, acc_ref, lam, g, tq):
    m, alpha_pend = stats
    acc = alpha_pend * acc_ref[...] + jnp.dot(vt_pend, p_ref[...], preferred_element_type=f32)
    return _attn_normalize(acc, lam, g, tq)


def _attn_normalize(acc, lam, g, tq):
    l = acc[DA_HEAD_DIM:DA_HEAD_DIM + 1, :]
    acc = acc[:DA_HEAD_DIM, :]
    o = acc[:, :tq] / l[:, :tq] - lam * (acc[:, tq:] / l[:, tq:])
    ms = jnp.mean(o * o, axis=0, keepdims=True)
    y = o * lax.rsqrt(ms + EPS) * g * (1.0 - LAMBDA_INIT)
    return y.T


def _attn_init(tq, p_ref, acc_ref):
    p_ref[...] = jnp.zeros(p_ref.shape, bf16)
    acc_ref[...] = jnp.zeros(acc_ref.shape, f32)
    return (jnp.full((1, 2 * tq), -3e38, f32), jnp.ones((1, 2 * tq), f32))


def _both(b):
    return jnp.concatenate([b, b], axis=1)


def _attn_kernel(lam_ref, q_ref, k_ref, v_ref, km_ref, vm_ref, bnear_ref, bmeta_ref, g_ref,
                 o_ref, *scratch):
    i = pl.program_id(1)
    n_far = jnp.maximum(i - 1, 0)
    heads = range(DA_HEADS)
    cols = lambda h: slice(h * DA_HEAD_DIM, (h + 1) * DA_HEAD_DIM)
    qz = [_stack_components(q_ref[cols(h), :]) for h in heads]

    def scores(h, j):
        start = pl.multiple_of(j * TK, TK)
        return jnp.dot(k_ref[pl.ds(start, TK), cols(h)], qz[h], preferred_element_type=f32)

    s_bufs = scratch[0:DA_HEADS]
    p_bufs = scratch[DA_HEADS:2 * DA_HEADS]
    accs = scratch[2 * DA_HEADS:3 * DA_HEADS]
    vts = scratch[3 * DA_HEADS:4 * DA_HEADS]
    vmt_ref = scratch[4 * DA_HEADS]

    @pl.when(i == 0)
    def _():
        for h in heads:
            for c0 in range(0, k_ref.shape[0], VT_CHUNK):
                vts[h][:, c0:c0 + VT_CHUNK] = _values_t_ext(v_ref[c0:c0 + VT_CHUNK, cols(h)])
            vmt_ref[h] = _values_t_ext(vm_ref[:, cols(h)])

    def values(h, j):
        start = pl.multiple_of(jnp.maximum(j, 0) * TK, TK)
        return vts[h][:, pl.ds(start, TK)]

    stats, smax = [], []
    for h in heads:
        stats.append(_attn_init(TQ, p_bufs[h], accs[h]))
        s0 = scores(h, 0)
        s_bufs[h][0] = s0
        smax.append(_colmax(s0))

    def consume(j, slot, stats, smax):
        new_stats = []
        for h in heads:
            st, p = _softmax_step(stats[h], s_bufs[h][slot], smax[h], values(h, j - 1),
                                  p_bufs[h], accs[h])
            p_bufs[h][...] = p
            new_stats.append(st)
        return tuple(new_stats)

    def produce(j, slot):
        new_smax = []
        for h in heads:
            s_next = scores(h, j)
            s_bufs[h][slot] = s_next
            new_smax.append(_colmax(s_next))
        return tuple(new_smax)

    def one_body(j, carry):
        stats, smax = carry
        slot = lax.rem(j, 2)
        stats = consume(j, slot, stats, smax)
        return stats, produce(j + 1, 1 - slot)

    stats, _ = lax.fori_loop(0, n_far, one_body, (tuple(stats), tuple(smax)))
    gate = jnp.where(i >= 1, 0.0, NEG_INF).astype(f32)
    for h in heads:
        p_h, acc_h = p_bufs[h], accs[h]
        s = s_bufs[h][lax.rem(n_far, 2)] + _both(bnear_ref[h, 1] + gate)
        st, p = _softmax_step(stats[h], s, _colmax(s), values(h, n_far - 1), p_h, acc_h)
        p_h[...] = p
        s = scores(h, i) + _both(bnear_ref[h, 0])
        st, p = _softmax_step(st, s, _colmax(s), values(h, i - 1), p_h, acc_h)
        p_h[...] = p
        s = (jnp.dot(km_ref[:, cols(h)], qz[h], preferred_element_type=f32)
             + _both(bmeta_ref[h, jnp.minimum(i, 1)]))
        st, p = _softmax_step(st, s, _colmax(s), values(h, i), p_h, acc_h)
        p_meta = p_h.at[0:MP, :]
        p_meta[...] = p
        o_ref[:, cols(h)] = _attn_finish(st, vmt_ref[h], p_meta, acc_h, lam_ref[0],
                                         g_ref[...], TQ).astype(o_ref.dtype)


def _attn_meta_kernel(lam_ref, q_ref, km_ref, vm_ref, bmm_ref, g_ref, o_ref, p_buf, acc_ref):
    qz = _stack_components(q_ref[...])
    vmt = _values_t_ext(vm_ref[...])
    stats = _attn_init(MP, p_buf, acc_ref)
    s = jnp.dot(km_ref[...], qz, preferred_element_type=f32) + _both(bmm_ref[...])
    stats, p = _softmax_step(stats, s, _colmax(s), vmt, p_buf, acc_ref)
    p_buf[...] = p
    o_ref[...] = _attn_finish(stats, vmt, p_buf, acc_ref, lam_ref[0], g_ref[...],
                              MP).astype(o_ref.dtype)


def _attention(lam1, q, k, v, km, vm, bnear, bmeta, g_tile):
    nb, s_len, _ = k.shape
    smem = pl.BlockSpec(memory_space=pltpu.SMEM)
    return pl.pallas_call(
        _attn_kernel,
        grid=(nb, s_len // TQ),
        in_specs=[smem,
                  pl.BlockSpec((None, DA_WIDTH, TQ), lambda b, i: (b, 0, i)),
                  pl.BlockSpec((None, s_len, DA_WIDTH), lambda b, i: (b, 0, 0)),
                  pl.BlockSpec((None, s_len, DA_WIDTH), lambda b, i: (b, 0, 0)),
                  _const_spec((MP, DA_WIDTH)), _const_spec((MP, DA_WIDTH)),
                  _const_spec((DA_HEADS, 2, TK, TQ)), _const_spec((DA_HEADS, 2, MP, TQ)),
                  _const_spec((DA_HEAD_DIM, TQ))],
        out_specs=pl.BlockSpec((None, TQ, DA_WIDTH), lambda b, i: (b, i, 0)),
        out_shape=jax.ShapeDtypeStruct((nb, s_len, DA_WIDTH), bf16),
        scratch_shapes=([pltpu.VMEM((2, TK, 2 * TQ), f32)] * DA_HEADS
                        + [pltpu.VMEM((TK, 2 * TQ), bf16)] * DA_HEADS
                        + [pltpu.VMEM((V_EXT, 2 * TQ), f32)] * DA_HEADS
                        + [pltpu.VMEM((V_EXT, s_len), bf16)] * DA_HEADS
                        + [pltpu.VMEM((DA_HEADS, V_EXT, MP), bf16)]),
        compiler_params=pltpu.CompilerParams(
            dimension_semantics=("parallel", "arbitrary"),
            vmem_limit_bytes=VMEM_LIMIT),
        name="attn",
    )(lam1, q, k, v, km, vm, bnear, bmeta, g_tile)


def _attention_meta(lam1, qm, km, vm, bmm, g_tile):
    smem = pl.BlockSpec(memory_space=pltpu.SMEM)
    head = pl.BlockSpec((MP, DA_HEAD_DIM), lambda h: (0, h))
    return pl.pallas_call(
        _attn_meta_kernel,
        grid=(DA_HEADS,),
        in_specs=[smem, pl.BlockSpec((DA_HEAD_DIM, MP), lambda h: (h, 0)), head, head,
                  pl.BlockSpec((None, MP, MP), lambda h: (h, 0, 0)),
                  pl.BlockSpec((DA_HEAD_DIM, MP), lambda h: (0, 0))],
        out_specs=head,
        out_shape=jax.ShapeDtypeStruct((MP, DA_WIDTH), bf16),
        scratch_shapes=[pltpu.VMEM((MP, 2 * MP), bf16), pltpu.VMEM((V_EXT, 2 * MP), f32)],
        name="attn_meta",
    )(lam1, qm, km, vm, bmm, g_tile)


GDN_R = DN_HEADS * CHUNK
GDN_LEVELS = (2, 4, 8, 16, 32)
GDN_QKV = 3 * DN_WIDTH
GDN_NCH = GDN_ROWS // CHUNK
GDN_NSEQ = 2


def _gdn_consts():
    r = np.arange(GDN_R)[:, None]
    c = np.arange(GDN_R)[None, :]
    same = (r // CHUNK) == (c // CHUNK)
    bd = np.stack([same & (r >= c), same & (r > c), same]).astype(np.float32)
    i = np.arange(CHUNK)[:, None]
    j = np.arange(GDN_R)[None, :] % CHUNK
    lv = [i == j, (i > j) & ((i // 2) == (j // 2))]
    for s in GDN_LEVELS:
        lv.append(((i // (2 * s)) == (j // (2 * s))) & ((i & s) != 0) & ((j & s) == 0))
    cat = np.stack(lv).astype(np.float32)
    rr = np.arange(GDN_ROWS)[:, None]
    cc = np.arange(GDN_ROWS)[None, :]
    lcum = (((rr // CHUNK) == (cc // CHUNK)) & (rr >= cc)).astype(np.float32)
    sel = np.zeros((2, LANES, DN_HEADS * LANES), np.float32)
    for h in range(DN_HEADS):
        sel[0, h, h * LANES:(h + 1) * LANES] = 1.0
        sel[1, DN_HEADS + h, h * LANES:(h + 1) * LANES] = 1.0
    return bd, cat, lcum, sel


def _split3(x):
    hi = x.astype(bf16)
    r1 = x - hi.astype(f32)
    mid = r1.astype(bf16)
    lo = (r1 - mid.astype(f32)).astype(bf16)
    return hi, mid, lo


def _gdn_block(n_per_seq, xq, xk, xv, ba, z, s_refs, alog_ref, dtb_ref, ng_ref, bd_ref,
               cat_ref, lcum_ref, sel_ref):
    n_seq = len(s_refs)
    n = n_seq * n_per_seq
    seq_rows = n_per_seq * CHUNK
    heads = range(DN_HEADS)
    chunks = range(n)
    dot = functools.partial(jnp.dot, preferred_element_type=f32)

    def stack(a):
        return jnp.concatenate(
            [a[c * CHUNK:(c + 1) * CHUNK, h * LANES:(h + 1) * LANES]
             for c in chunks for h in heads], axis=0)

    qn_b, kn_b = stack(xq), stack(xk)
    qn, kn, vs = qn_b.astype(f32), kn_b.astype(f32), stack(xv).astype(f32)
    blk = lambda c: slice(c * GDN_R, (c + 1) * GDN_R)

    beta_t = _sigmoid(ba)
    xg = ba + dtb_ref[...]
    softplus = jnp.maximum(xg, 0.0) + jnp.log1p(jnp.exp(-jnp.abs(xg)))
    g_t = -jnp.exp(alog_ref[...]) * softplus
    lcum = lcum_ref[0:seq_rows, 0:seq_rows]
    g_parts = _split3(g_t)
    gcum = jnp.concatenate(
        [sum(dot(lcum, part[q * seq_rows:(q + 1) * seq_rows]) for part in g_parts)
         for q in range(n_seq)], axis=0)
    g_rep = stack(sum(dot(part, sel_ref[1]) for part in _split3(gcum)))
    b_rep = stack(sum(dot(part, sel_ref[0]) for part in _split3(beta_t)[:2]))
    g_end = jnp.concatenate(
        [jnp.broadcast_to(g_rep[(b + 1) * CHUNK - 1:(b + 1) * CHUNK, :], (CHUNK, LANES))
         for b in range(n * DN_HEADS)], axis=0)
    exp_g = jnp.exp(g_rep)
    rhs = jnp.concatenate([vs * b_rep, kn * (b_rep * exp_g)], axis=1).astype(bf16)
    q_g = qn * exp_g
    k_g = (kn * jnp.exp(g_end - g_rep)).astype(bf16)
    g_last = jnp.exp(g_end)

    tri = bd_ref[0] > 0.5
    block_b = bd_ref[2].astype(bf16)
    both = lambda a: jnp.concatenate([a, a], axis=1)
    decay, m_b, x_cat = [], [], []
    for c in chunks:
        g_row = g_rep[blk(c)].T[0:1, :]
        dec = jnp.exp(jnp.where(tri, both(g_rep[blk(c)]) - g_row, -jnp.inf))
        m = bd_ref[1] * both(b_rep[blk(c)]) * _dot_nt(kn_b[blk(c)], kn_b[blk(c)]) * dec
        m_cat = sum(m[h * CHUNK:(h + 1) * CHUNK] for h in heads)
        decay.append(dec)
        m_b.append(m.astype(bf16))
        x_cat.append(cat_ref[0] - m_cat * cat_ref[1])

    def to_bd(x):
        return jnp.concatenate([x.astype(bf16)] * DN_HEADS, axis=0) * block_b

    for lvl in range(len(GDN_LEVELS)):
        ys = [dot(x_cat[c].astype(bf16), m_b[c]) for c in chunks]
        zs = [dot(ys[c].astype(bf16), to_bd(x_cat[c])) for c in chunks]
        x_cat = [x_cat[c] - zs[c] * cat_ref[2 + lvl] for c in chunks]
    sol = [dot(to_bd(x_cat[c]), rhs[blk(c)]).astype(bf16) for c in chunks]
    a_qk = [(_dot_nt(qn_b[blk(c)], kn_b[blk(c)]) * decay[c]).astype(bf16) for c in chunks]
    a_uw = [dot(a_qk[c], sol[c]) for c in chunks]
    hrows = lambda h: slice(h * CHUNK, (h + 1) * CHUNK)
    k_uw = [[_dot_tn(k_g[blk(c)][hrows(h)], sol[c][hrows(h)]) for h in heads] for c in chunks]

    out_rows = [None] * n
    for step in range(n_per_seq):
        for q in range(n_seq):
            c = q * n_per_seq + step
            s_ref = s_refs[q]
            out_cols = []
            for h in heads:
                s_old = s_ref[h]
                s_b = s_old.astype(bf16)
                q_eff = (q_g[blk(c)][hrows(h)] - a_uw[c][hrows(h), DN_DV:]).astype(bf16)
                o = a_uw[c][hrows(h), :DN_DV] + dot(q_eff, s_b)
                s_ref[h] = (s_old * g_last[blk(c)][hrows(h)][0:1, :] + k_uw[c][h][:, :DN_DV]
                            - dot(k_uw[c][h][:, DN_DV:].astype(bf16), s_b))
                o = o * lax.rsqrt(jnp.mean(o * o, axis=-1, keepdims=True) + EPS) * ng_ref[...]
                zh = z[c * CHUNK:(c + 1) * CHUNK, h * DN_DV:(h + 1) * DN_DV].astype(f32)
                out_cols.append(o * (zh * _sigmoid(zh)))
            out_rows[c] = jnp.concatenate(out_cols, axis=1)
    return jnp.concatenate(out_rows, axis=0)


def _gdn_kernel(dq_ref, dk_ref, dv_ref, dz_ref, ba_ref, mq_ref, mk_ref, mv_ref, mz_ref, mba_ref,
                alog_ref, dtb_ref, ng_ref, bd_ref, cat_ref, lcum_ref, sel_ref,
                o_ref, om_ref, s_ref):
    consts = (alog_ref, dtb_ref, ng_ref, bd_ref, cat_ref, lcum_ref, sel_ref)
    s_refs = [s_ref.at[q] for q in range(GDN_NSEQ)]

    @pl.when(pl.program_id(1) == 0)
    def _():
        s_ref[0] = jnp.zeros(s_ref.shape[1:], f32)
        om_ref[...] = _gdn_block(1, mq_ref[...], mk_ref[...], mv_ref[...], mba_ref[...],
                                 mz_ref[...], s_refs[:1], *consts).astype(om_ref.dtype)
        for q in range(1, GDN_NSEQ):
            s_ref[q] = s_ref[0]

    merge = lambda ref: ref[...].reshape(GDN_NSEQ * GDN_ROWS, ref.shape[-1])
    out = _gdn_block(GDN_NCH, merge(dq_ref), merge(dk_ref), merge(dv_ref), merge(ba_ref),
                     merge(dz_ref), s_refs, *consts)
    o_ref[...] = out.reshape(o_ref.shape).astype(o_ref.dtype)


def _gdn(dq, dk, dv, dz, ba, mq, mk, mv, mz, mba, alog_row, dtb_row, ng_row):
    nb, s_len, _ = dq.shape
    assert nb % GDN_NSEQ == 0
    row = lambda w: pl.BlockSpec((GDN_NSEQ, GDN_ROWS, w), lambda b, t: (b, t, 0))
    mrow = lambda w: pl.BlockSpec((CHUNK, w), lambda b, t: (MP // CHUNK - 1, 0))
    bd, cat, lcum, sel = _gdn_consts()
    bd, cat = jnp.asarray(bd), jnp.asarray(cat)
    lcum, sel = jnp.asarray(lcum).astype(bf16), jnp.asarray(sel).astype(bf16)
    return pl.pallas_call(
        _gdn_kernel,
        grid=(nb // GDN_NSEQ, s_len // GDN_ROWS),
        in_specs=[row(DN_WIDTH)] * 4 + [row(LANES)] + [mrow(DN_WIDTH)] * 4 + [mrow(LANES)] + [
            _const_spec((1, LANES)), _const_spec((1, LANES)),
            _const_spec((1, DN_DV)), _const_spec(bd.shape), _const_spec(cat.shape),
            _const_spec(lcum.shape), _const_spec(sel.shape)],
        out_specs=[row(DN_WIDTH), pl.BlockSpec((None, CHUNK, DN_WIDTH), lambda b, t: (b, 0, 0))],
        out_shape=[jax.ShapeDtypeStruct((nb, s_len, DN_WIDTH), bf16),
                   jax.ShapeDtypeStruct((nb // GDN_NSEQ, CHUNK, DN_WIDTH), bf16)],
        scratch_shapes=[pltpu.VMEM((GDN_NSEQ, DN_HEADS, DN_DK, DN_DV), f32)],
        compiler_params=pltpu.CompilerParams(
            dimension_semantics=("parallel", "arbitrary"), vmem_limit_bytes=VMEM_LIMIT),
        name="gdn",
    )(dq, dk, dv, dz, ba, mq, mk, mv, mz, mba, alog_row, dtb_row, ng_row, bd, cat, lcum, sel)


def _mix_and_norm(x, oda, odn, wout_ref, g2_ref):
    mix = jnp.concatenate([oda, odn], axis=1)
    h2 = x + jnp.dot(mix, wout_ref[...], preferred_element_type=f32)
    u2 = h2 * lax.rsqrt(jnp.mean(h2 * h2, axis=-1, keepdims=True) + EPS) * g2_ref[...]
    return h2, u2.astype(bf16)


def _ffn_halo_kernel(x_ref, oda_ref, odn_ref, wout_ref, g2_ref, wup_ref, halo_ref):
    _, u2 = _mix_and_norm(x_ref[...], oda_ref[...], odn_ref[...], wout_ref, g2_ref)
    halo_ref[...] = jnp.dot(u2, wup_ref[...], preferred_element_type=f32)


def _ffn_kernel(x_ref, oda_ref, odn_ref, halo_ref, wout_ref, g2_ref, wup_ref, cw_ref, cb_ref,
                wdown_ref, gf_ref, o_ref, *scratch):
    tm = FFN_TM
    carries = scratch[0:FFN_TILES + 1]
    accs = scratch[FFN_TILES + 1:2 * FFN_TILES + 1]
    hbufs = scratch[2 * FFN_TILES + 1:]
    t = pl.program_id(1)

    @pl.when(t == 0)
    def _():
        carries[0][...] = halo_ref[...]

    @pl.when(t > 0)
    def _():
        carries[0][...] = carries[FFN_TILES][...]

    n_chunks = D_FF // FFN_CW
    stages = [(a, c) for a in range(FFN_TILES) for c in range(n_chunks)]
    col_pair = lambda c: (slice(c * FFN_CW, (c + 1) * FFN_CW),
                          slice(D_FF + c * FFN_CW, D_FF + (c + 1) * FFN_CW))
    buf_pair = lambda s: hbufs[2 * (s % FFN_NBUF):2 * (s % FFN_NBUF) + 2]
    rows = lambda a: slice(a * tm, (a + 1) * tm)
    normed = {}

    def up_part(s):
        a, c = stages[s]
        if c == 0:
            normed[a] = _mix_and_norm(x_ref[rows(a), :], oda_ref[rows(a), :],
                                      odn_ref[rows(a), :], wout_ref, g2_ref)
        u2 = normed[a][1]
        for cols, hbuf in zip(col_pair(c), buf_pair(s)):
            hbuf[0:SUBLANES, :] = carries[a][:, cols]
            hup = jnp.dot(u2, wup_ref[:, cols], preferred_element_type=f32)
            hbuf[SUBLANES:SUBLANES + tm, :] = hup
            carries[a + 1][:, cols] = hup[tm - SUBLANES:, :]

    def conv_part(cols, hbuf):
        y = cb_ref[:, cols]
        for d in range(FFN_CONV):
            y = y + cw_ref[FFN_CONV - 1 - d:FFN_CONV - d, cols] * hbuf[SUBLANES - d:SUBLANES - d + tm, :]
        return y

    for s in range(FFN_AHEAD):
        up_part(s)
    for s, (a, c) in enumerate(stages):
        if s + FFN_AHEAD < len(stages):
            up_part(s + FFN_AHEAD)
        gate, val = (conv_part(cols, hbuf) for cols, hbuf in zip(col_pair(c), buf_pair(s)))
        act = (gate * _sigmoid(gate) * val).astype(bf16)
        part = jnp.dot(act, wdown_ref[c * FFN_CW:(c + 1) * FFN_CW, :], preferred_element_type=f32)
        if c == 0:
            accs[a][...] = part
        else:
            accs[a][...] += part
        if c == n_chunks - 1:
            y = normed[a][0] + accs[a][...]
            o_ref[rows(a), :] = (y * lax.rsqrt(jnp.mean(y * y, axis=-1, keepdims=True) + EPS)
                                 * gf_ref[...])


def _ffn_halo(xm, odam, odnm, w_out, g2, w_up):
    return pl.pallas_call(
        _ffn_halo_kernel,
        out_shape=jax.ShapeDtypeStruct((HALO_ROWS, 2 * D_FF), f32),
        compiler_params=pltpu.CompilerParams(vmem_limit_bytes=VMEM_LIMIT),
        name="ffn_halo",
    )(xm, odam, odnm, w_out, g2, w_up)


def _ffn(x, oda, odn, halo, w_out, g2, w_up, conv_w, conv_b, w_down, gf):
    nb, s_len, _ = x.shape
    tm = FFN_TM
    step_rows = FFN_TILES * tm
    row = lambda w: pl.BlockSpec((None, step_rows, w), lambda b, t: (b, t, 0))
    return pl.pallas_call(
        _ffn_kernel,
        grid=(nb, s_len // step_rows),
        in_specs=[row(D_MODEL), row(DA_WIDTH), row(DN_WIDTH),
                  _const_spec((SUBLANES, 2 * D_FF)), _const_spec((D_MODEL, D_MODEL)),
                  _const_spec((1, D_MODEL)), _const_spec((D_MODEL, 2 * D_FF)),
                  _const_spec((FFN_CONV, 2 * D_FF)), _const_spec((1, 2 * D_FF)),
                  _const_spec((D_FF, D_MODEL)), _const_spec((1, D_MODEL))],
        out_specs=row(D_MODEL),
        out_shape=jax.ShapeDtypeStruct((nb, s_len, D_MODEL), f32),
        scratch_shapes=([pltpu.VMEM((SUBLANES, 2 * D_FF), f32)] * (FFN_TILES + 1)
                        + [pltpu.VMEM((tm, D_MODEL), f32)] * FFN_TILES
                        + [pltpu.VMEM((SUBLANES + tm, FFN_CW), f32)] * (2 * FFN_NBUF)),
        compiler_params=pltpu.CompilerParams(
            dimension_semantics=("parallel", "arbitrary"), vmem_limit_bytes=VMEM_LIMIT),
        name="ffn",
    )(x, oda, odn, halo, w_out, g2, w_up, conv_w, conv_b, w_down, gf)


def kernel(x, meta_tokens, rel_bias, norm1_g, w_in, da_lambda, da_subln_g, dn_conv_w, dn_A_log,
           dn_dt_bias, dn_norm_g, w_out, norm2_g, w_up, ffn_conv_w, ffn_conv_b, w_down,
           final_norm_g):
    nb, s_len, _ = x.shape
    assert s_len % TQ == 0 and s_len % GDN_ROWS == 0 and s_len % FFN_TM == 0
    w_in0 = w_in[0]
    w_main = w_in0[:, :N_MAIN].astype(bf16)
    w_small = jnp.pad(w_in0[:, N_MAIN:], ((0, 0), (0, LANES - 2 * DN_HEADS))).astype(bf16)
    w_out_b = w_out[0].astype(bf16)
    w_up_b = w_up[0].astype(bf16)
    w_down_b = w_down[0].astype(bf16)
    g1 = norm1_g[0].reshape(1, D_MODEL).astype(f32)
    g2 = norm2_g[0].reshape(1, D_MODEL).astype(f32)
    gf = final_norm_g.reshape(1, D_MODEL).astype(f32)
    meta_pad = jnp.pad(meta_tokens.astype(x.dtype), ((MP - N_META, 0), (0, 0)))

    bnear, bmeta, bmm, lam_tile = _bias_tiles(rel_bias, da_lambda[0])
    lam1 = lam_tile[0, :1]

    conv_w = dn_conv_w[0].astype(f32)
    no_halo = jnp.zeros((SUBLANES, GDN_QKV), f32)
    mq, mk, mv, mdq, mdk, mdv, mdz, mba, mtail = [
        a[0] for a in _proj(meta_pad[None], no_halo, g1, w_main, w_small, conv_w, MP)]
    q, k, v, dq, dk, dv, dz, ba, _ = _proj(x, mtail, g1, w_main, w_small, conv_w, PROJ_TM)

    subln = da_subln_g[0].astype(f32)
    o_da = _attention(lam1, q, k, v, mk, mv, bnear, bmeta,
                      jnp.broadcast_to(subln[:, None], (DA_HEAD_DIM, TQ)))
    o_da_m = _attention_meta(lam1, mq, mk, mv, bmm,
                             jnp.broadcast_to(subln[:, None], (DA_HEAD_DIM, MP)))

    gate_row = lambda p: jnp.pad(p[0].astype(f32), (DN_HEADS, LANES - 2 * DN_HEADS)).reshape(1, LANES)
    o_dn, o_dn_m = _gdn(dq, dk, dv, dz, ba, mdq, mdk, mdv, mdz, mba,
                        gate_row(dn_A_log), gate_row(dn_dt_bias),
                        dn_norm_g[0].reshape(1, DN_DV).astype(f32))

    halo = _ffn_halo(meta_pad[MP - HALO_ROWS:], o_da_m[MP - HALO_ROWS:],
                     o_dn_m[0, CHUNK - HALO_ROWS:], w_out_b, g2, w_up_b)[HALO_ROWS - SUBLANES:]
    return _ffn(x, o_da, o_dn, halo, w_out_b, g2, w_up_b, ffn_conv_w[0].astype(f32),
                ffn_conv_b[0].reshape(1, 2 * D_FF).astype(f32), w_down_b, gf)
```

```python
import functools
import math

import numpy as np
import jax
import jax.numpy as jnp
from jax import lax
from jax.experimental import pallas as pl
from jax.experimental.pallas import tpu as pltpu

f32 = jnp.float32
bf16 = jnp.bfloat16

D_MODEL = 1024
CHUNK = 64
N_META = 16
EPS = 1e-6
NEG_INF = -1e30
LAMBDA_INIT = 0.8 - 0.6 * math.exp(-0.3 * 0)
LOG2E = math.log2(math.e)

DA_HEADS = 4
DA_HEAD_DIM = 128
DA_HALF = 64
DA_WIDTH = DA_HEADS * DA_HEAD_DIM
DN_HEADS = 4
DN_DK = 128
DN_DV = 128
DN_WIDTH = DN_HEADS * DN_DV
DN_CONV = 4
N_BUCKETS = 32
MAX_DISTANCE = 128
D_FF = 2816
FFN_CONV = 3

N_MAIN = 7 * 512
LANES = 128
SUBLANES = 8
MP = 128
TQ = 256
TK = 256
V_EXT = DA_HEAD_DIM + 16
VT_CHUNK = 512
PROJ_TM = 512
GDN_ROWS = 256
FFN_TILES = 1
FFN_TM = 256
FFN_CW = 256
FFN_AHEAD = 4
FFN_NBUF = FFN_AHEAD + 1
HALO_ROWS = 16
VMEM_LIMIT = 52 * 1024 * 1024


def _const_spec(shape):
    nd = len(shape)
    return pl.BlockSpec(shape, lambda *_: (0,) * nd, pipeline_mode=pl.Buffered(1))


def _prep_kernel(far_ref, table_ref, lam_in_ref, bnear_in, bmeta_in, bmm_in,
                 bnear_out, bmeta_out, bmm_out, lam_out):
    h = pl.program_id(0)
    c_far = table_ref[far_ref[0], h]

    def lookup(bkt):
        out = jnp.full(bkt.shape, NEG_INF, f32)
        for b in range(N_BUCKETS):
            out = jnp.where(bkt == b, (table_ref[b, h] - c_far) * LOG2E, out)
        return out

    for i in range(2):
        bnear_out[i] = lookup(bnear_in[i])
        bmeta_out[i] = lookup(bmeta_in[i])
    bmm_out[...] = lookup(bmm_in[...])
    lv = lam_in_ref[...]
    s1 = jnp.sum(lv[0:1] * lv[1:2], axis=-1, keepdims=True)
    s2 = jnp.sum(lv[2:3] * lv[3:4], axis=-1, keepdims=True)
    lam = jnp.exp(s1) - jnp.exp(s2) + LAMBDA_INIT
    lam_out[...] = jnp.broadcast_to(lam, lam_out.shape)


def _t5_bucket(rel):
    nb = N_BUCKETS // 2
    max_exact = nb // 2
    ret = jnp.where(rel > 0, nb, 0)
    n = jnp.abs(rel)
    nf = jnp.maximum(n, 1).astype(jnp.float32)
    large = max_exact + (jnp.log(nf / max_exact) / math.log(MAX_DISTANCE / max_exact)
                         * (nb - max_exact)).astype(jnp.int32)
    large = jnp.minimum(large, nb - 1)
    return ret + jnp.where(n < max_exact, n, large)


def _bias_tiles(rel_bias, da_lambda):
    r = jnp.arange(TK, dtype=jnp.int32)[:, None]
    c = jnp.arange(TQ, dtype=jnp.int32)[None, :]
    diag = jnp.where((r // CHUNK) <= (c // CHUNK), _t5_bucket(r - c), -1)
    prev = _t5_bucket(r - c - TK)
    bnear = jnp.stack([diag, prev]).astype(jnp.int32)
    far = _t5_bucket(jnp.full((1,), -(TK + 1), jnp.int32)).astype(jnp.int32)
    rm = jnp.arange(MP, dtype=jnp.int32)[:, None]
    valid = rm >= (MP - N_META)
    kpos = rm - (MP - N_META)
    m0 = jnp.where(valid, _t5_bucket(kpos - (N_META + c)), -1)
    m1 = jnp.where(valid, jnp.broadcast_to(far[0], (MP, TQ)), -1)
    bmeta = jnp.stack([m0, m1]).astype(jnp.int32)
    cm = jnp.arange(MP, dtype=jnp.int32)[None, :]
    bmm = jnp.where(valid, _t5_bucket(rm - cm), -1).astype(jnp.int32)

    smem = pl.BlockSpec(memory_space=pltpu.SMEM)
    return pl.pallas_call(
        _prep_kernel,
        grid=(DA_HEADS,),
        in_specs=[smem, smem,
                  pl.BlockSpec((4, DA_HALF), lambda h: (0, 0)),
                  pl.BlockSpec((2, TK, TQ), lambda h: (0, 0, 0)),
                  pl.BlockSpec((2, MP, TQ), lambda h: (0, 0, 0)),
                  pl.BlockSpec((MP, MP), lambda h: (0, 0))],
        out_specs=[pl.BlockSpec((None, 2, TK, TQ), lambda h: (h, 0, 0, 0)),
                   pl.BlockSpec((None, 2, MP, TQ), lambda h: (h, 0, 0, 0)),
                   pl.BlockSpec((None, MP, MP), lambda h: (h, 0, 0)),
                   pl.BlockSpec((SUBLANES, LANES), lambda h: (0, 0))],
        out_shape=[jax.ShapeDtypeStruct((DA_HEADS, 2, TK, TQ), f32),
                   jax.ShapeDtypeStruct((DA_HEADS, 2, MP, TQ), f32),
                   jax.ShapeDtypeStruct((DA_HEADS, MP, MP), f32),
                   jax.ShapeDtypeStruct((SUBLANES, LANES), f32)],
        name="prep",
    )(far, rel_bias.astype(f32), da_lambda.astype(f32), bnear, bmeta, bmm)


def _sigmoid(x):
    return 1.0 / (1.0 + jnp.exp(-x))


def _proj_kernel(x_ref, halo_ref, g_ref, w_ref, ws_ref, cw_ref, q_ref, k_ref, v_ref, dq_ref,
                 dk_ref, dv_ref, dz_ref, ba_ref, tail_ref, carry_ref, carry_next, u_ref):
    tm = x_ref.shape[0]
    t = pl.program_id(1)

    @pl.when(t == 0)
    def _():
        carry_ref[...] = halo_ref[...]

    @pl.when(t > 0)
    def _():
        carry_ref[...] = carry_next[...]

    x = x_ref[...]
    ms = jnp.mean(x * x, axis=-1, keepdims=True)
    u_ref[...] = (x * lax.rsqrt(ms + EPS) * g_ref[...]).astype(bf16)
    outs = (q_ref, k_ref, v_ref, dq_ref, dk_ref, dv_ref, dz_ref)
    n_out = len(outs)
    project = lambda j: jnp.dot(u_ref[...], w_ref[:, j * 512:(j + 1) * 512],
                                preferred_element_type=f32)

    def conv_silu(j, r):
        cols = slice((j - 3) * DN_WIDTH, (j - 2) * DN_WIDTH)
        carry_next[:, cols] = r[tm - SUBLANES:, :]
        ext = jnp.concatenate([carry_ref[:, cols], r], axis=0)
        y = cw_ref[0:1, cols] * ext
        for d in range(1, DN_CONV):
            y = cw_ref[d:d + 1, cols] * ext + pltpu.roll(y, 1, 0)
        y = y[SUBLANES:, :]
        return y * _sigmoid(y)

    def l2norm_heads(y, scale):
        parts = []
        for h in range(DN_HEADS):
            yh = y[:, h * DN_DK:(h + 1) * DN_DK]
            parts.append(yh * (lax.rsqrt(jnp.sum(yh * yh, axis=-1, keepdims=True) + EPS) * scale))
        return jnp.concatenate(parts, axis=1)

    order = (3, 0, 4, 1, 5, 2, 6)
    ahead = 2
    pending = [project(j) for j in order[:ahead]]
    for pos, j in enumerate(order):
        r = pending.pop(0)
        if pos + ahead < n_out:
            pending.append(project(order[pos + ahead]))
        elif pos + ahead == n_out:
            ba_ref[...] = jnp.dot(u_ref[...], ws_ref[...], preferred_element_type=f32)
        if j == 0:
            r = r * (DA_HALF ** -0.5 * LOG2E)
        elif j in (3, 4):
            r = l2norm_heads(conv_silu(j, r), DN_DK ** -0.5 if j == 3 else 1.0)
        elif j == 5:
            r = conv_silu(j, r)
        if j == 0:
            r = r.T
        outs[j][...] = r.astype(outs[j].dtype)
    tail_ref[...] = carry_next[...]


def _proj(x3, halo, g1, w_main, w_small, conv_w, tm):
    nb, rows, _ = x3.shape
    row_spec = lambda w: pl.BlockSpec((None, tm, w), lambda b, t: (b, t, 0))
    outs = [jax.ShapeDtypeStruct((nb, 512, rows), bf16)] + [
        jax.ShapeDtypeStruct((nb, rows, 512), bf16)] * 6 + [
        jax.ShapeDtypeStruct((nb, rows, LANES), f32),
        jax.ShapeDtypeStruct((nb, SUBLANES, GDN_QKV), f32)]
    return pl.pallas_call(
        _proj_kernel,
        grid=(nb, rows // tm),
        in_specs=[row_spec(D_MODEL), _const_spec((SUBLANES, GDN_QKV)), _const_spec((1, D_MODEL)),
                  _const_spec((D_MODEL, N_MAIN)), _const_spec((D_MODEL, LANES)),
                  _const_spec((DN_CONV, GDN_QKV))],
        out_specs=[pl.BlockSpec((None, 512, tm), lambda b, t: (b, 0, t))]
                  + [row_spec(512)] * 6 + [row_spec(LANES),
                   pl.BlockSpec((None, SUBLANES, GDN_QKV), lambda b, t: (b, 0, 0))],
        out_shape=outs,
        scratch_shapes=([pltpu.VMEM((SUBLANES, GDN_QKV), f32)] * 2
                        + [pltpu.VMEM((tm, D_MODEL), bf16)]),
        compiler_params=pltpu.CompilerParams(
            dimension_semantics=("parallel", "arbitrary"), vmem_limit_bytes=VMEM_LIMIT),
        name="proj",
    )(x3, halo, g1, w_main, w_small, conv_w)


def _dot_nt(a, b):
    return lax.dot_general(a, b, (((1,), (1,)), ((), ())), preferred_element_type=f32)


def _dot_tn(a, b):
    return lax.dot_general(a, b, (((0,), (0,)), ((), ())), preferred_element_type=f32)


def _stack_components(qt):
    zero = jnp.zeros((DA_HALF, qt.shape[1]), qt.dtype)
    q0 = jnp.concatenate([qt[:DA_HALF], zero], axis=0)
    q1 = jnp.concatenate([zero, qt[DA_HALF:]], axis=0)
    return jnp.concatenate([q0, q1], axis=1)


def _colmax(s):
    return jnp.max(s, axis=0, keepdims=True)


def _values_t_ext(v):
    vt = v.astype(f32).T.astype(bf16)
    return jnp.concatenate([vt, jnp.ones((V_EXT - DA_HEAD_DIM, v.shape[0]), bf16)], axis=0)


def _softmax_step(stats, s, smax, vt_pend, p_ref, acc_ref):
    m, alpha_pend = stats
    pv = jnp.dot(vt_pend, p_ref[...], preferred_element_type=f32)
    m_new = jnp.maximum(m, smax)
    alpha = jnp.exp2(m - m_new)
    p = jnp.exp2((s - m_new).astype(bf16))
    acc_ref[...] = alpha_pend * acc_ref[...] + pv
    return (m_new, alpha), p


def _attn_finish(stats, vt_pend, p_ref, acc_ref, lam, g, tq):
    m, alpha_pend = stats
    acc = alpha_pend * acc_ref[...] + jnp.dot(vt_pend, p_ref[...], preferred_element_type=f32)
    l = acc[DA_HEAD_DIM:DA_HEAD_DIM + 1, :]
    acc = acc[:DA_HEAD_DIM, :]
    o = acc[:, :tq] / l[:, :tq] - lam * (acc[:, tq:] / l[:, tq:])
    ms = jnp.mean(o * o, axis=0, keepdims=True)
    y = o * lax.rsqrt(ms + EPS) * g * (1.0 - LAMBDA_INIT)
    return y.T


def _attn_init(tq, p_ref, acc_ref):
    p_ref[...] = jnp.zeros(p_ref.shape, bf16)
    acc_ref[...] = jnp.zeros(acc_ref.shape, f32)
    return (jnp.full((1, 2 * tq), -3e38, f32), jnp.ones((1, 2 * tq), f32))


def _both(b):
    return jnp.concatenate([b, b], axis=1)


def _attn_kernel(lam_ref, q_ref, k_ref, v_ref, km_ref, vm_ref, bnear_ref, bmeta_ref, g_ref,
                 o_ref, *scratch):
    i = pl.program_id(1)
    n_far = jnp.maximum(i - 1, 0)
    heads = range(DA_HEADS)
    cols = lambda h: slice(h * DA_HEAD_DIM, (h + 1) * DA_HEAD_DIM)
    qz = [_stack_components(q_ref[cols(h), :]) for h in heads]

    def scores(h, j):
        start = pl.multiple_of(j * TK, TK)
        return jnp.dot(k_ref[pl.ds(start, TK), cols(h)], qz[h], preferred_element_type=f32)

    s_bufs = scratch[0:DA_HEADS]
    p_bufs = scratch[DA_HEADS:2 * DA_HEADS]
    accs = scratch[2 * DA_HEADS:3 * DA_HEADS]
    vts = scratch[3 * DA_HEADS:4 * DA_HEADS]
    vmt_ref = scratch[4 * DA_HEADS]

    @pl.when(i == 0)
    def _():
        for h in heads:
            for c0 in range(0, k_ref.shape[0], VT_CHUNK):
                vts[h][:, c0:c0 + VT_CHUNK] = _values_t_ext(v_ref[c0:c0 + VT_CHUNK, cols(h)])
            vmt_ref[h] = _values_t_ext(vm_ref[:, cols(h)])

    def values(h, j):
        start = pl.multiple_of(jnp.maximum(j, 0) * TK, TK)
        return vts[h][:, pl.ds(start, TK)]

    stats, smax = [], []
    for h in heads:
        stats.append(_attn_init(TQ, p_bufs[h], accs[h]))
        s0 = scores(h, 0)
        s_bufs[h][0] = s0
        smax.append(_colmax(s0))

    def far_body(j, carry):
        stats, smax = carry
        slot = lax.rem(j, 2)
        new_stats, new_smax = [], []
        for h in heads:
            st, p = _softmax_step(stats[h], s_bufs[h][slot], smax[h], values(h, j - 1),
                                  p_bufs[h], accs[h])
            p_bufs[h][...] = p
            new_stats.append(st)
        for h in heads:
            s_next = scores(h, j + 1)
            s_bufs[h][1 - slot] = s_next
            new_smax.append(_colmax(s_next))
        return tuple(new_stats), tuple(new_smax)

    stats, _ = lax.fori_loop(0, n_far, far_body, (tuple(stats), tuple(smax)))
    gate = jnp.where(i >= 1, 0.0, NEG_INF).astype(f32)
    for h in heads:
        p_h, acc_h = p_bufs[h], accs[h]
        s = s_bufs[h][lax.rem(n_far, 2)] + _both(bnear_ref[h, 1] + gate)
        st, p = _softmax_step(stats[h], s, _colmax(s), values(h, n_far - 1), p_h, acc_h)
        p_h[...] = p
        s = scores(h, i) + _both(bnear_ref[h, 0])
        st, p = _softmax_step(st, s, _colmax(s), values(h, i - 1), p_h, acc_h)
        p_h[...] = p
        s = (jnp.dot(km_ref[:, cols(h)], qz[h], preferred_element_type=f32)
             + _both(bmeta_ref[h, jnp.minimum(i, 1)]))
        st, p = _softmax_step(st, s, _colmax(s), values(h, i), p_h, acc_h)
        p_meta = p_h.at[0:MP, :]
        p_meta[...] = p
        o_ref[:, cols(h)] = _attn_finish(st, vmt_ref[h], p_meta, acc_h, lam_ref[0],
                                         g_ref[...], TQ).astype(o_ref.dtype)


def _attn_meta_kernel(lam_ref, q_ref, km_ref, vm_ref, bmm_ref, g_ref, o_ref, p_buf, acc_ref):
    qz = _stack_components(q_ref[...])
    vmt = _values_t_ext(vm_ref[...])
    stats = _attn_init(MP, p_buf, acc_ref)
    s = jnp.dot(km_ref[...], qz, preferred_element_type=f32) + _both(bmm_ref[...])
    stats, p = _softmax_step(stats, s, _colmax(s), vmt, p_buf, acc_ref)
    p_buf[...] = p
    o_ref[...] = _attn_finish(stats, vmt, p_buf, acc_ref, lam_ref[0], g_ref[...],
                              MP).astype(o_ref.dtype)


def _attention(lam1, q, k, v, km, vm, bnear, bmeta, g_tile):
    nb, s_len, _ = k.shape
    smem = pl.BlockSpec(memory_space=pltpu.SMEM)
    return pl.pallas_call(
        _attn_kernel,
        grid=(nb, s_len // TQ),
        in_specs=[smem,
                  pl.BlockSpec((None, DA_WIDTH, TQ), lambda b, i: (b, 0, i)),
                  pl.BlockSpec((None, s_len, DA_WIDTH), lambda b, i: (b, 0, 0)),
                  pl.BlockSpec((None, s_len, DA_WIDTH), lambda b, i: (b, 0, 0)),
                  _const_spec((MP, DA_WIDTH)), _const_spec((MP, DA_WIDTH)),
                  _const_spec((DA_HEADS, 2, TK, TQ)), _const_spec((DA_HEADS, 2, MP, TQ)),
                  _const_spec((DA_HEAD_DIM, TQ))],
        out_specs=pl.BlockSpec((None, TQ, DA_WIDTH), lambda b, i: (b, i, 0)),
        out_shape=jax.ShapeDtypeStruct((nb, s_len, DA_WIDTH), bf16),
        scratch_shapes=([pltpu.VMEM((2, TK, 2 * TQ), f32)] * DA_HEADS
                        + [pltpu.VMEM((TK, 2 * TQ), bf16)] * DA_HEADS
                        + [pltpu.VMEM((V_EXT, 2 * TQ), f32)] * DA_HEADS
                        + [pltpu.VMEM((V_EXT, s_len), bf16)] * DA_HEADS
                        + [pltpu.VMEM((DA_HEADS, V_EXT, MP), bf16)]),
        compiler_params=pltpu.CompilerParams(
            dimension_semantics=("parallel", "arbitrary"),
            vmem_limit_bytes=VMEM_LIMIT),
        name="attn",
    )(lam1, q, k, v, km, vm, bnear, bmeta, g_tile)


def _attention_meta(lam1, qm, km, vm, bmm, g_tile):
    smem = pl.BlockSpec(memory_space=pltpu.SMEM)
    head = pl.BlockSpec((MP, DA_HEAD_DIM), lambda h: (0, h))
    return pl.pallas_call(
        _attn_meta_kernel,
        grid=(DA_HEADS,),
        in_specs=[smem, pl.BlockSpec((DA_HEAD_DIM, MP), lambda h: (h, 0)), head, head,
                  pl.BlockSpec((None, MP, MP), lambda h: (h, 0, 0)),
                  pl.BlockSpec((DA_HEAD_DIM, MP), lambda h: (0, 0))],
        out_specs=head,
        out_shape=jax.ShapeDtypeStruct((MP, DA_WIDTH), bf16),
        scratch_shapes=[pltpu.VMEM((MP, 2 * MP), bf16), pltpu.VMEM((V_EXT, 2 * MP), f32)],
        name="attn_meta",
    )(lam1, qm, km, vm, bmm, g_tile)


GDN_R = DN_HEADS * CHUNK
GDN_LEVELS = (2, 4, 8, 16, 32)
GDN_QKV = 3 * DN_WIDTH
GDN_NCH = GDN_ROWS // CHUNK
GDN_NSEQ = 2


def _gdn_consts():
    r = np.arange(GDN_R)[:, None]
    c = np.arange(GDN_R)[None, :]
    same = (r // CHUNK) == (c // CHUNK)
    bd = np.stack([same & (r >= c), same & (r > c), same]).astype(np.float32)
    i = np.arange(CHUNK)[:, None]
    j = np.arange(GDN_R)[None, :] % CHUNK
    lv = [i == j, (i > j) & ((i // 2) == (j // 2))]
    for s in GDN_LEVELS:
        lv.append(((i // (2 * s)) == (j // (2 * s))) & ((i & s) != 0) & ((j & s) == 0))
    cat = np.stack(lv).astype(np.float32)
    rr = np.arange(GDN_ROWS)[:, None]
    cc = np.arange(GDN_ROWS)[None, :]
    lcum = (((rr // CHUNK) == (cc // CHUNK)) & (rr >= cc)).astype(np.float32)
    sel = np.zeros((2, LANES, DN_HEADS * LANES), np.float32)
    for h in range(DN_HEADS):
        sel[0, h, h * LANES:(h + 1) * LANES] = 1.0
        sel[1, DN_HEADS + h, h * LANES:(h + 1) * LANES] = 1.0
    return bd, cat, lcum, sel


def _split3(x):
    hi = x.astype(bf16)
    r1 = x - hi.astype(f32)
    mid = r1.astype(bf16)
    lo = (r1 - mid.astype(f32)).astype(bf16)
    return hi, mid, lo


def _gdn_block(n_per_seq, xq, xk, xv, ba, z, s_refs, alog_ref, dtb_ref, ng_ref, bd_ref,
               cat_ref, lcum_ref, sel_ref):
    n_seq = len(s_refs)
    n = n_seq * n_per_seq
    seq_rows = n_per_seq * CHUNK
    heads = range(DN_HEADS)
    chunks = range(n)
    dot = functools.partial(jnp.dot, preferred_element_type=f32)

    def stack(a):
        return jnp.concatenate(
            [a[c * CHUNK:(c + 1) * CHUNK, h * LANES:(h + 1) * LANES]
             for c in chunks for h in heads], axis=0)

    qn_b, kn_b = stack(xq), stack(xk)
    qn, kn, vs = qn_b.astype(f32), kn_b.astype(f32), stack(xv).astype(f32)
    blk = lambda c: slice(c * GDN_R, (c + 1) * GDN_R)

    beta_t = _sigmoid(ba)
    xg = ba + dtb_ref[...]
    softplus = jnp.maximum(xg, 0.0) + jnp.log1p(jnp.exp(-jnp.abs(xg)))
    g_t = -jnp.exp(alog_ref[...]) * softplus
    lcum = lcum_ref[0:seq_rows, 0:seq_rows]
    g_parts = _split3(g_t)
    gcum = jnp.concatenate(
        [sum(dot(lcum, part[q * seq_rows:(q + 1) * seq_rows]) for part in g_parts)
         for q in range(n_seq)], axis=0)
    g_rep = stack(sum(dot(part, sel_ref[1]) for part in _split3(gcum)))
    b_rep = stack(sum(dot(part, sel_ref[0]) for part in _split3(beta_t)[:2]))
    g_end = jnp.concatenate(
        [jnp.broadcast_to(g_rep[(b + 1) * CHUNK - 1:(b + 1) * CHUNK, :], (CHUNK, LANES))
         for b in range(n * DN_HEADS)], axis=0)
    exp_g = jnp.exp(g_rep)
    rhs = jnp.concatenate([vs * b_rep, kn * (b_rep * exp_g)], axis=1).astype(bf16)
    q_g = qn * exp_g
    k_g = (kn * jnp.exp(g_end - g_rep)).astype(bf16)
    g_last = jnp.exp(g_end)

    tri = bd_ref[0] > 0.5
    block_b = bd_ref[2].astype(bf16)
    both = lambda a: jnp.concatenate([a, a], axis=1)
    decay, m_b, x_cat = [], [], []
    for c in chunks:
        g_row = g_rep[blk(c)].T[0:1, :]
        dec = jnp.exp(jnp.where(tri, both(g_rep[blk(c)]) - g_row, -jnp.inf))
        m = bd_ref[1] * both(b_rep[blk(c)]) * _dot_nt(kn_b[blk(c)], kn_b[blk(c)]) * dec
        m_cat = sum(m[h * CHUNK:(h + 1) * CHUNK] for h in heads)
        decay.append(dec)
        m_b.append(m.astype(bf16))
        x_cat.append(cat_ref[0] - m_cat * cat_ref[1])

    def to_bd(x):
        return jnp.concatenate([x.astype(bf16)] * DN_HEADS, axis=0) * block_b

    for lvl in range(len(GDN_LEVELS)):
        ys = [dot(x_cat[c].astype(bf16), m_b[c]) for c in chunks]
        zs = [dot(ys[c].astype(bf16), to_bd(x_cat[c])) for c in chunks]
        x_cat = [x_cat[c] - zs[c] * cat_ref[2 + lvl] for c in chunks]
    sol = [dot(to_bd(x_cat[c]), rhs[blk(c)]).astype(bf16) for c in chunks]
    a_qk = [(_dot_nt(qn_b[blk(c)], kn_b[blk(c)]) * decay[c]).astype(bf16) for c in chunks]
    a_uw = [dot(a_qk[c], sol[c]) for c in chunks]
    hrows = lambda h: slice(h * CHUNK, (h + 1) * CHUNK)
    k_uw = [[_dot_tn(k_g[blk(c)][hrows(h)], sol[c][hrows(h)]) for h in heads] for c in chunks]

    out_rows = [None] * n
    for step in range(n_per_seq):
        for q in range(n_seq):
            c = q * n_per_seq + step
            s_ref = s_refs[q]
            out_cols = []
            for h in heads:
                s_old = s_ref[h]
                s_b = s_old.astype(bf16)
                q_eff = (q_g[blk(c)][hrows(h)] - a_uw[c][hrows(h), DN_DV:]).astype(bf16)
                o = a_uw[c][hrows(h), :DN_DV] + dot(q_eff, s_b)
                s_ref[h] = (s_old * g_last[blk(c)][hrows(h)][0:1, :] + k_uw[c][h][:, :DN_DV]
                            - dot(k_uw[c][h][:, DN_DV:].astype(bf16), s_b))
                o = o * lax.rsqrt(jnp.mean(o * o, axis=-1, keepdims=True) + EPS) * ng_ref[...]
                zh = z[c * CHUNK:(c + 1) * CHUNK, h * DN_DV:(h + 1) * DN_DV].astype(f32)
                out_cols.append(o * (zh * _sigmoid(zh)))
            out_rows[c] = jnp.concatenate(out_cols, axis=1)
    return jnp.concatenate(out_rows, axis=0)


def _gdn_kernel(dq_ref, dk_ref, dv_ref, dz_ref, ba_ref, mq_ref, mk_ref, mv_ref, mz_ref, mba_ref,
                alog_ref, dtb_ref, ng_ref, bd_ref, cat_ref, lcum_ref, sel_ref,
                o_ref, om_ref, s_ref):
    consts = (alog_ref, dtb_ref, ng_ref, bd_ref, cat_ref, lcum_ref, sel_ref)
    s_refs = [s_ref.at[q] for q in range(GDN_NSEQ)]

    @pl.when(pl.program_id(1) == 0)
    def _():
        s_ref[0] = jnp.zeros(s_ref.shape[1:], f32)
        om_ref[...] = _gdn_block(1, mq_ref[...], mk_ref[...], mv_ref[...], mba_ref[...],
                                 mz_ref[...], s_refs[:1], *consts).astype(om_ref.dtype)
        for q in range(1, GDN_NSEQ):
            s_ref[q] = s_ref[0]

    merge = lambda ref: ref[...].reshape(GDN_NSEQ * GDN_ROWS, ref.shape[-1])
    out = _gdn_block(GDN_NCH, merge(dq_ref), merge(dk_ref), merge(dv_ref), merge(ba_ref),
                     merge(dz_ref), s_refs, *consts)
    o_ref[...] = out.reshape(o_ref.shape).astype(o_ref.dtype)


def _gdn(dq, dk, dv, dz, ba, mq, mk, mv, mz, mba, alog_row, dtb_row, ng_row):
    nb, s_len, _ = dq.shape
    assert nb % GDN_NSEQ == 0
    row = lambda w: pl.BlockSpec((GDN_NSEQ, GDN_ROWS, w), lambda b, t: (b, t, 0))
    mrow = lambda w: pl.BlockSpec((CHUNK, w), lambda b, t: (MP // CHUNK - 1, 0))
    bd, cat, lcum, sel = _gdn_consts()
    bd, cat = jnp.asarray(bd), jnp.asarray(cat)
    lcum, sel = jnp.asarray(lcum).astype(bf16), jnp.asarray(sel).astype(bf16)
    return pl.pallas_call(
        _gdn_kernel,
        grid=(nb // GDN_NSEQ, s_len // GDN_ROWS),
        in_specs=[row(DN_WIDTH)] * 4 + [row(LANES)] + [mrow(DN_WIDTH)] * 4 + [mrow(LANES)] + [
            _const_spec((1, LANES)), _const_spec((1, LANES)),
            _const_spec((1, DN_DV)), _const_spec(bd.shape), _const_spec(cat.shape),
            _const_spec(lcum.shape), _const_spec(sel.shape)],
        out_specs=[row(DN_WIDTH), pl.BlockSpec((None, CHUNK, DN_WIDTH), lambda b, t: (b, 0, 0))],
        out_shape=[jax.ShapeDtypeStruct((nb, s_len, DN_WIDTH), bf16),
                   jax.ShapeDtypeStruct((nb // GDN_NSEQ, CHUNK, DN_WIDTH), bf16)],
        scratch_shapes=[pltpu.VMEM((GDN_NSEQ, DN_HEADS, DN_DK, DN_DV), f32)],
        compiler_params=pltpu.CompilerParams(
            dimension_semantics=("parallel", "arbitrary"), vmem_limit_bytes=VMEM_LIMIT),
        name="gdn",
    )(dq, dk, dv, dz, ba, mq, mk, mv, mz, mba, alog_row, dtb_row, ng_row, bd, cat, lcum, sel)


def _mix_and_norm(x, oda, odn, wout_ref, g2_ref):
    mix = jnp.concatenate([oda, odn], axis=1)
    h2 = x + jnp.dot(mix, wout_ref[...], preferred_element_type=f32)
    u2 = h2 * lax.rsqrt(jnp.mean(h2 * h2, axis=-1, keepdims=True) + EPS) * g2_ref[...]
    return h2, u2.astype(bf16)


def _ffn_halo_kernel(x_ref, oda_ref, odn_ref, wout_ref, g2_ref, wup_ref, halo_ref):
    _, u2 = _mix_and_norm(x_ref[...], oda_ref[...], odn_ref[...], wout_ref, g2_ref)
    halo_ref[...] = jnp.dot(u2, wup_ref[...], preferred_element_type=f32)


def _ffn_kernel(x_ref, oda_ref, odn_ref, halo_ref, wout_ref, g2_ref, wup_ref, cw_ref, cb_ref,
                wdown_ref, gf_ref, o_ref, *scratch):
    tm = FFN_TM
    carries = scratch[0:FFN_TILES + 1]
    accs = scratch[FFN_TILES + 1:2 * FFN_TILES + 1]
    hbufs = scratch[2 * FFN_TILES + 1:]
    t = pl.program_id(1)

    @pl.when(t == 0)
    def _():
        carries[0][...] = halo_ref[...]

    @pl.when(t > 0)
    def _():
        carries[0][...] = carries[FFN_TILES][...]

    n_chunks = D_FF // FFN_CW
    stages = [(a, c) for a in range(FFN_TILES) for c in range(n_chunks)]
    col_pair = lambda c: (slice(c * FFN_CW, (c + 1) * FFN_CW),
                          slice(D_FF + c * FFN_CW, D_FF + (c + 1) * FFN_CW))
    buf_pair = lambda s: hbufs[2 * (s % FFN_NBUF):2 * (s % FFN_NBUF) + 2]
    rows = lambda a: slice(a * tm, (a + 1) * tm)
    normed = {}

    def up_part(s):
        a, c = stages[s]
        if c == 0:
            normed[a] = _mix_and_norm(x_ref[rows(a), :], oda_ref[rows(a), :],
                                      odn_ref[rows(a), :], wout_ref, g2_ref)
        u2 = normed[a][1]
        for cols, hbuf in zip(col_pair(c), buf_pair(s)):
            hbuf[0:SUBLANES, :] = carries[a][:, cols]
            hup = jnp.dot(u2, wup_ref[:, cols], preferred_element_type=f32)
            hbuf[SUBLANES:SUBLANES + tm, :] = hup
            carries[a + 1][:, cols] = hup[tm - SUBLANES:, :]

    def conv_part(cols, hbuf):
        y = cb_ref[:, cols]
        for d in range(FFN_CONV):
            y = y + cw_ref[FFN_CONV - 1 - d:FFN_CONV - d, cols] * hbuf[SUBLANES - d:SUBLANES - d + tm, :]
        return y

    for s in range(FFN_AHEAD):
        up_part(s)
    for s, (a, c) in enumerate(stages):
        if s + FFN_AHEAD < len(stages):
            up_part(s + FFN_AHEAD)
        gate, val = (conv_part(cols, hbuf) for cols, hbuf in zip(col_pair(c), buf_pair(s)))
        act = (gate * _sigmoid(gate) * val).astype(bf16)
        part = jnp.dot(act, wdown_ref[c * FFN_CW:(c + 1) * FFN_CW, :], preferred_element_type=f32)
        if c == 0:
            accs[a][...] = part
        else:
            accs[a][...] += part
        if c == n_chunks - 1:
            y = normed[a][0] + accs[a][...]
            o_ref[rows(a), :] = (y * lax.rsqrt(jnp.mean(y * y, axis=-1, keepdims=True) + EPS)
                                 * gf_ref[...])


def _ffn_halo(xm, odam, odnm, w_out, g2, w_up):
    return pl.pallas_call(
        _ffn_halo_kernel,
        out_shape=jax.ShapeDtypeStruct((HALO_ROWS, 2 * D_FF), f32),
        compiler_params=pltpu.CompilerParams(vmem_limit_bytes=VMEM_LIMIT),
        name="ffn_halo",
    )(xm, odam, odnm, w_out, g2, w_up)


def _ffn(x, oda, odn, halo, w_out, g2, w_up, conv_w, conv_b, w_down, gf):
    nb, s_len, _ = x.shape
    tm = FFN_TM
    step_rows = FFN_TILES * tm
    row = lambda w: pl.BlockSpec((None, step_rows, w), lambda b, t: (b, t, 0))
    return pl.pallas_call(
        _ffn_kernel,
        grid=(nb, s_len // step_rows),
        in_specs=[row(D_MODEL), row(DA_WIDTH), row(DN_WIDTH),
                  _const_spec((SUBLANES, 2 * D_FF)), _const_spec((D_MODEL, D_MODEL)),
                  _const_spec((1, D_MODEL)), _const_spec((D_MODEL, 2 * D_FF)),
                  _const_spec((FFN_CONV, 2 * D_FF)), _const_spec((1, 2 * D_FF)),
                  _const_spec((D_FF, D_MODEL)), _const_spec((1, D_MODEL))],
        out_specs=row(D_MODEL),
        out_shape=jax.ShapeDtypeStruct((nb, s_len, D_MODEL), f32),
        scratch_shapes=([pltpu.VMEM((SUBLANES, 2 * D_FF), f32)] * (FFN_TILES + 1)
                        + [pltpu.VMEM((tm, D_MODEL), f32)] * FFN_TILES
                        + [pltpu.VMEM((SUBLANES + tm, FFN_CW), f32)] * (2 * FFN_NBUF)),
        compiler_params=pltpu.CompilerParams(
            dimension_semantics=("parallel", "arbitrary"), vmem_limit_bytes=VMEM_LIMIT),
        name="ffn",
    )(x, oda, odn, halo, w_out, g2, w_up, conv_w, conv_b, w_down, gf)


def kernel(x, meta_tokens, rel_bias, norm1_g, w_in, da_lambda, da_subln_g, dn_conv_w, dn_A_log,
           dn_dt_bias, dn_norm_g, w_out, norm2_g, w_up, ffn_conv_w, ffn_conv_b, w_down,
           final_norm_g):
    nb, s_len, _ = x.shape
    assert s_len % TQ == 0 and s_len % GDN_ROWS == 0 and s_len % (FFN_TILES * FFN_TM) == 0
    assert s_len % PROJ_TM == 0 and s_len % VT_CHUNK == 0
    w_in0 = w_in[0]
    w_main = w_in0[:, :N_MAIN].astype(bf16)
    w_small = jnp.pad(w_in0[:, N_MAIN:], ((0, 0), (0, LANES - 2 * DN_HEADS))).astype(bf16)
    w_out_b = w_out[0].astype(bf16)
    w_up_b = w_up[0].astype(bf16)
    w_down_b = w_down[0].astype(bf16)
    g1 = norm1_g[0].reshape(1, D_MODEL).astype(f32)
    g2 = norm2_g[0].reshape(1, D_MODEL).astype(f32)
    gf = final_norm_g.reshape(1, D_MODEL).astype(f32)
    meta_pad = jnp.pad(meta_tokens.astype(x.dtype), ((MP - N_META, 0), (0, 0)))

    bnear, bmeta, bmm, lam_tile = _bias_tiles(rel_bias, da_lambda[0])
    lam1 = lam_tile[0, :1]

    conv_w = dn_conv_w[0].astype(f32)
    no_halo = jnp.zeros((SUBLANES, GDN_QKV), f32)
    mq, mk, mv, mdq, mdk, mdv, mdz, mba, mtail = [
        a[0] for a in _proj(meta_pad[None], no_halo, g1, w_main, w_small, conv_w, MP)]
    q, k, v, dq, dk, dv, dz, ba, _ = _proj(x, mtail, g1, w_main, w_small, conv_w, PROJ_TM)

    subln = da_subln_g[0].astype(f32)
    o_da = _attention(lam1, q, k, v, mk, mv, bnear, bmeta,
                      jnp.broadcast_to(subln[:, None], (DA_HEAD_DIM, TQ)))
    o_da_m = _attention_meta(lam1, mq, mk, mv, bmm,
                             jnp.broadcast_to(subln[:, None], (DA_HEAD_DIM, MP)))

    gate_row = lambda p: jnp.pad(p[0].astype(f32), (DN_HEADS, LANES - 2 * DN_HEADS)).reshape(1, LANES)
    o_dn, o_dn_m = _gdn(dq, dk, dv, dz, ba, mdq, mdk, mdv, mdz, mba,
                        gate_row(dn_A_log), gate_row(dn_dt_bias),
                        dn_norm_g[0].reshape(1, DN_DV).astype(f32))

    halo = _ffn_halo(meta_pad[MP - HALO_ROWS:], o_da_m[MP - HALO_ROWS:],
                     o_dn_m[0, CHUNK - HALO_ROWS:], w_out_b, g2, w_up_b)[HALO_ROWS - SUBLANES:]
    return _ffn(x, o_da, o_dn, halo, w_out_b, g2, w_up_b, ffn_conv_w[0].astype(f32),
                ffn_conv_b[0].reshape(1, 2 * D_FF).astype(f32), w_down_b, gf)
```

```python
import functools
import math

import numpy as np
import jax
import jax.numpy as jnp
from jax import lax
from jax.experimental import pallas as pl
from jax.experimental.pallas import tpu as pltpu

f32 = jnp.float32
bf16 = jnp.bfloat16

D_MODEL = 1024
CHUNK = 64
N_META = 16
EPS = 1e-6
NEG_INF = -1e30
LAMBDA_INIT = 0.8 - 0.6 * math.exp(-0.3 * 0)
LOG2E = math.log2(math.e)

DA_HEADS = 4
DA_HEAD_DIM = 128
DA_HALF = 64
DA_WIDTH = DA_HEADS * DA_HEAD_DIM
DN_HEADS = 4
DN_DK = 128
DN_DV = 128
DN_WIDTH = DN_HEADS * DN_DV
DN_CONV = 4
N_BUCKETS = 32
MAX_DISTANCE = 128
D_FF = 2816
FFN_CONV = 3

N_MAIN = 7 * 512
LANES = 128
SUBLANES = 8
MP = 128
TQ = 256
TK = 256
V_EXT = DA_HEAD_DIM + 16
VT_CHUNK = 512
PROJ_TM = 512
GDN_ROWS = 256
FFN_TILES = 1
FFN_TM = 256
FFN_CW = 256
FFN_AHEAD = 4
FFN_NBUF = FFN_AHEAD + 1
HALO_ROWS = 16
VMEM_LIMIT = 52 * 1024 * 1024


def _const_spec(shape):
    nd = len(shape)
    return pl.BlockSpec(shape, lambda *_: (0,) * nd, pipeline_mode=pl.Buffered(1))


def _prep_kernel(far_ref, table_ref, lam_in_ref, bnear_in, bmeta_in, bmm_in,
                 bnear_out, bmeta_out, bmm_out, lam_out):
    h = pl.program_id(0)
    c_far = table_ref[far_ref[0], h]

    def lookup(bkt):
        out = jnp.full(bkt.shape, NEG_INF, f32)
        for b in range(N_BUCKETS):
            out = jnp.where(bkt == b, (table_ref[b, h] - c_far) * LOG2E, out)
        return out

    for i in range(2):
        bnear_out[i] = lookup(bnear_in[i])
        bmeta_out[i] = lookup(bmeta_in[i])
    bmm_out[...] = lookup(bmm_in[...])
    lv = lam_in_ref[...]
    s1 = jnp.sum(lv[0:1] * lv[1:2], axis=-1, keepdims=True)
    s2 = jnp.sum(lv[2:3] * lv[3:4], axis=-1, keepdims=True)
    lam = jnp.exp(s1) - jnp.exp(s2) + LAMBDA_INIT
    lam_out[...] = jnp.broadcast_to(lam, lam_out.shape)


def _t5_bucket(rel):
    nb = N_BUCKETS // 2
    max_exact = nb // 2
    ret = jnp.where(rel > 0, nb, 0)
    n = jnp.abs(rel)
    nf = jnp.maximum(n, 1).astype(jnp.float32)
    large = max_exact + (jnp.log(nf / max_exact) / math.log(MAX_DISTANCE / max_exact)
                         * (nb - max_exact)).astype(jnp.int32)
    large = jnp.minimum(large, nb - 1)
    return ret + jnp.where(n < max_exact, n, large)


def _bias_tiles(rel_bias, da_lambda):
    r = jnp.arange(TK, dtype=jnp.int32)[:, None]
    c = jnp.arange(TQ, dtype=jnp.int32)[None, :]
    diag = jnp.where((r // CHUNK) <= (c // CHUNK), _t5_bucket(r - c), -1)
    prev = _t5_bucket(r - c - TK)
    bnear = jnp.stack([diag, prev]).astype(jnp.int32)
    far = _t5_bucket(jnp.full((1,), -(TK + 1), jnp.int32)).astype(jnp.int32)
    rm = jnp.arange(MP, dtype=jnp.int32)[:, None]
    valid = rm >= (MP - N_META)
    kpos = rm - (MP - N_META)
    m0 = jnp.where(valid, _t5_bucket(kpos - (N_META + c)), -1)
    m1 = jnp.where(valid, jnp.broadcast_to(far[0], (MP, TQ)), -1)
    bmeta = jnp.stack([m0, m1]).astype(jnp.int32)
    cm = jnp.arange(MP, dtype=jnp.int32)[None, :]
    bmm = jnp.where(valid, _t5_bucket(rm - cm), -1).astype(jnp.int32)

    smem = pl.BlockSpec(memory_space=pltpu.SMEM)
    return pl.pallas_call(
        _prep_kernel,
        grid=(DA_HEADS,),
        in_specs=[smem, smem,
                  pl.BlockSpec((4, DA_HALF), lambda h: (0, 0)),
                  pl.BlockSpec((2, TK, TQ), lambda h: (0, 0, 0)),
                  pl.BlockSpec((2, MP, TQ), lambda h: (0, 0, 0)),
                  pl.BlockSpec((MP, MP), lambda h: (0, 0))],
        out_specs=[pl.BlockSpec((None, 2, TK, TQ), lambda h: (h, 0, 0, 0)),
                   pl.BlockSpec((None, 2, MP, TQ), lambda h: (h, 0, 0, 0)),
                   pl.BlockSpec((None, MP, MP), lambda h: (h, 0, 0)),
                   pl.BlockSpec((SUBLANES, LANES), lambda h: (0, 0))],
        out_shape=[jax.ShapeDtypeStruct((DA_HEADS, 2, TK, TQ), f32),
                   jax.ShapeDtypeStruct((DA_HEADS, 2, MP, TQ), f32),
                   jax.ShapeDtypeStruct((DA_HEADS, MP, MP), f32),
                   jax.ShapeDtypeStruct((SUBLANES, LANES), f32)],
        name="prep",
    )(far, rel_bias.astype(f32), da_lambda.astype(f32), bnear, bmeta, bmm)


def _sigmoid(x):
    return 1.0 / (1.0 + jnp.exp(-x))


def _proj_kernel(x_ref, halo_ref, g_ref, w_ref, cw_ref, q_ref, k_ref, v_ref, dq_ref,
                 dk_ref, dv_ref, dz_ref, ba_ref, tail_ref, carry_ref, carry_next, u_ref):
    tm = x_ref.shape[0]
    t = pl.program_id(1)

    @pl.when(t == 0)
    def _():
        carry_ref[...] = halo_ref[...]

    @pl.when(t > 0)
    def _():
        carry_ref[...] = carry_next[...]

    x = x_ref[...]
    ms = jnp.mean(x * x, axis=-1, keepdims=True)
    u_ref[...] = (x * lax.rsqrt(ms + EPS) * g_ref[...]).astype(bf16)
    outs = (q_ref, k_ref, v_ref, dq_ref, dk_ref, dv_ref, dz_ref)
    n_out = len(outs)
    project = lambda j: jnp.dot(u_ref[...], w_ref[:, j * 512:(j + 1) * 512],
                                preferred_element_type=f32)

    def conv_silu(j, r):
        cols = slice((j - 3) * DN_WIDTH, (j - 2) * DN_WIDTH)
        carry_next[:, cols] = r[tm - SUBLANES:, :]
        ext = jnp.concatenate([carry_ref[:, cols], r], axis=0)
        y = cw_ref[0:1, cols] * ext
        for d in range(1, DN_CONV):
            y = cw_ref[d:d + 1, cols] * ext + pltpu.roll(y, 1, 0)
        y = y[SUBLANES:, :]
        return y * _sigmoid(y)

    def l2norm_heads(y, scale):
        parts = []
        for h in range(DN_HEADS):
            yh = y[:, h * DN_DK:(h + 1) * DN_DK]
            parts.append(yh * (lax.rsqrt(jnp.sum(yh * yh, axis=-1, keepdims=True) + EPS) * scale))
        return jnp.concatenate(parts, axis=1)

    order = (3, 0, 4, 1, 5, 2, 6)
    ahead = 2
    pending = [project(j) for j in order[:ahead]]
    for pos, j in enumerate(order):
        r = pending.pop(0)
        if pos + ahead < n_out:
            pending.append(project(order[pos + ahead]))
        elif pos + ahead == n_out:
            ba_ref[...] = jnp.dot(u_ref[...], w_ref[:, N_MAIN:N_MAIN + LANES],
                                  preferred_element_type=f32)
        if j == 0:
            r = r * (DA_HALF ** -0.5 * LOG2E)
        elif j in (3, 4):
            r = l2norm_heads(conv_silu(j, r), DN_DK ** -0.5 if j == 3 else 1.0)
        elif j == 5:
            r = conv_silu(j, r)
        if j == 0:
            r = r.T
        outs[j][...] = r.astype(outs[j].dtype)
    tail_ref[...] = carry_next[...]


def _proj(x3, halo, g1, w_all, conv_w, tm):
    nb, rows, _ = x3.shape
    row_spec = lambda w: pl.BlockSpec((None, tm, w), lambda b, t: (b, t, 0))
    outs = [jax.ShapeDtypeStruct((nb, 512, rows), bf16)] + [
        jax.ShapeDtypeStruct((nb, rows, 512), bf16)] * 6 + [
        jax.ShapeDtypeStruct((nb, rows, LANES), f32),
        jax.ShapeDtypeStruct((nb, SUBLANES, GDN_QKV), f32)]
    return pl.pallas_call(
        _proj_kernel,
        grid=(nb, rows // tm),
        in_specs=[row_spec(D_MODEL), _const_spec((SUBLANES, GDN_QKV)), _const_spec((1, D_MODEL)),
                  _const_spec((D_MODEL, N_MAIN + LANES)), _const_spec((DN_CONV, GDN_QKV))],
        out_specs=[pl.BlockSpec((None, 512, tm), lambda b, t: (b, 0, t))]
                  + [row_spec(512)] * 6 + [row_spec(LANES),
                   pl.BlockSpec((None, SUBLANES, GDN_QKV), lambda b, t: (b, 0, 0))],
        out_shape=outs,
        scratch_shapes=([pltpu.VMEM((SUBLANES, GDN_QKV), f32)] * 2
                        + [pltpu.VMEM((tm, D_MODEL), bf16)]),
        compiler_params=pltpu.CompilerParams(
            dimension_semantics=("parallel", "arbitrary"), vmem_limit_bytes=VMEM_LIMIT),
        name="proj",
    )(x3, halo, g1, w_all, conv_w)


def _dot_nt(a, b):
    return lax.dot_general(a, b, (((1,), (1,)), ((), ())), preferred_element_type=f32)


def _dot_tn(a, b):
    return lax.dot_general(a, b, (((0,), (0,)), ((), ())), preferred_element_type=f32)


def _stack_components(qt):
    zero = jnp.zeros((DA_HALF, qt.shape[1]), qt.dtype)
    q0 = jnp.concatenate([qt[:DA_HALF], zero], axis=0)
    q1 = jnp.concatenate([zero, qt[DA_HALF:]], axis=0)
    return jnp.concatenate([q0, q1], axis=1)


def _colmax(s):
    return jnp.max(s, axis=0, keepdims=True)


def _values_t_ext(v):
    vt = v.astype(f32).T.astype(bf16)
    return jnp.concatenate([vt, jnp.ones((V_EXT - DA_HEAD_DIM, v.shape[0]), bf16)], axis=0)


def _softmax_step(stats, s, smax, vt_pend, p_ref, acc_ref):
    m, alpha_pend = stats
    pv = jnp.dot(vt_pend, p_ref[...], preferred_element_type=f32)
    m_new = jnp.maximum(m, smax)
    alpha = jnp.exp2(m - m_new)
    p = jnp.exp2((s - m_new).astype(bf16))
    acc_ref[...] = alpha_pend * acc_ref[...] + pv
    return (m_new, alpha), p


def _attn_finish(stats, vt_pend, p_ref, acc_ref, lam, g, tq):
    m, alpha_pend = stats
    acc = alpha_pend * acc_ref[...] + jnp.dot(vt_pend, p_ref[...], preferred_element_type=f32)
    l = acc[DA_HEAD_DIM:DA_HEAD_DIM + 1, :]
    acc = acc[:DA_HEAD_DIM, :]
    o = acc[:, :tq] / l[:, :tq] - lam * (acc[:, tq:] / l[:, tq:])
    ms = jnp.mean(o * o, axis=0, keepdims=True)
    y = o * lax.rsqrt(ms + EPS) * g * (1.0 - LAMBDA_INIT)
    return y.T


def _attn_init(tq, p_ref, acc_ref):
    p_ref[...] = jnp.zeros(p_ref.shape, bf16)
    acc_ref[...] = jnp.zeros(acc_ref.shape, f32)
    return (jnp.full((1, 2 * tq), -3e38, f32), jnp.ones((1, 2 * tq), f32))


def _both(b):
    return jnp.concatenate([b, b], axis=1)


def _attn_kernel(lam_ref, q_ref, k_ref, v_ref, km_ref, vm_ref, bnear_ref, bmeta_ref, g_ref,
                 o_ref, *scratch):
    i = pl.program_id(1)
    n_far = jnp.maximum(i - 1, 0)
    heads = range(DA_HEADS)
    cols = lambda h: slice(h * DA_HEAD_DIM, (h + 1) * DA_HEAD_DIM)
    qz = [_stack_components(q_ref[cols(h), :]) for h in heads]

    def scores(h, j):
        start = pl.multiple_of(j * TK, TK)
        return jnp.dot(k_ref[pl.ds(start, TK), cols(h)], qz[h], preferred_element_type=f32)

    s_bufs = scratch[0:DA_HEADS]
    p_bufs = scratch[DA_HEADS:2 * DA_HEADS]
    accs = scratch[2 * DA_HEADS:3 * DA_HEADS]
    vts = scratch[3 * DA_HEADS:4 * DA_HEADS]
    vmt_ref = scratch[4 * DA_HEADS]

    @pl.when(i == 0)
    def _():
        for h in heads:
            for c0 in range(0, k_ref.shape[0], VT_CHUNK):
                vts[h][:, c0:c0 + VT_CHUNK] = _values_t_ext(v_ref[c0:c0 + VT_CHUNK, cols(h)])
            vmt_ref[h] = _values_t_ext(vm_ref[:, cols(h)])

    def values(h, j):
        start = pl.multiple_of(jnp.maximum(j, 0) * TK, TK)
        return vts[h][:, pl.ds(start, TK)]

    stats, smax = [], []
    for h in heads:
        stats.append(_attn_init(TQ, p_bufs[h], accs[h]))
        s0 = scores(h, 0)
        s_bufs[h][0] = s0
        smax.append(_colmax(s0))

    def far_body(j, carry):
        stats, smax = carry
        slot = lax.rem(j, 2)
        new_stats, new_smax = [], []
        for h in heads:
            st, p = _softmax_step(stats[h], s_bufs[h][slot], smax[h], values(h, j - 1),
                                  p_bufs[h], accs[h])
            p_bufs[h][...] = p
            new_stats.append(st)
        for h in heads:
            s_next = scores(h, j + 1)
            s_bufs[h][1 - slot] = s_next
            new_smax.append(_colmax(s_next))
        return tuple(new_stats), tuple(new_smax)

    stats, _ = lax.fori_loop(0, n_far, far_body, (tuple(stats), tuple(smax)))
    gate = jnp.where(i >= 1, 0.0, NEG_INF).astype(f32)
    for h in heads:
        p_h, acc_h = p_bufs[h], accs[h]
        s = s_bufs[h][lax.rem(n_far, 2)] + _both(bnear_ref[h, 1] + gate)
        st, p = _softmax_step(stats[h], s, _colmax(s), values(h, n_far - 1), p_h, acc_h)
        p_h[...] = p
        s = scores(h, i) + _both(bnear_ref[h, 0])
        st, p = _softmax_step(st, s, _colmax(s), values(h, i - 1), p_h, acc_h)
        p_h[...] = p
        s = (jnp.dot(km_ref[:, cols(h)], qz[h], preferred_element_type=f32)
             + _both(bmeta_ref[h, jnp.minimum(i, 1)]))
        st, p = _softmax_step(st, s, _colmax(s), values(h, i), p_h, acc_h)
        p_meta = p_h.at[0:MP, :]
        p_meta[...] = p
        o_ref[:, cols(h)] = _attn_finish(st, vmt_ref[h], p_meta, acc_h, lam_ref[0],
                                         g_ref[...], TQ).astype(o_ref.dtype)


def _attn_meta_kernel(lam_ref, q_ref, km_ref, vm_ref, bmm_ref, g_ref, o_ref, p_buf, acc_ref):
    qz = _stack_components(q_ref[...])
    vmt = _values_t_ext(vm_ref[...])
    stats = _attn_init(MP, p_buf, acc_ref)
    s = jnp.dot(km_ref[...], qz, preferred_element_type=f32) + _both(bmm_ref[...])
    stats, p = _softmax_step(stats, s, _colmax(s), vmt, p_buf, acc_ref)
    p_buf[...] = p
    o_ref[...] = _attn_finish(stats, vmt, p_buf, acc_ref, lam_ref[0], g_ref[...],
                              MP).astype(o_ref.dtype)


def _attention(lam1, q, k, v, km, vm, bnear, bmeta, g_tile):
    nb, s_len, _ = k.shape
    smem = pl.BlockSpec(memory_space=pltpu.SMEM)
    return pl.pallas_call(
        _attn_kernel,
        grid=(nb, s_len // TQ),
        in_specs=[smem,
                  pl.BlockSpec((None, DA_WIDTH, TQ), lambda b, i: (b, 0, i)),
                  pl.BlockSpec((None, s_len, DA_WIDTH), lambda b, i: (b, 0, 0)),
                  pl.BlockSpec((None, s_len, DA_WIDTH), lambda b, i: (b, 0, 0)),
                  _const_spec((MP, DA_WIDTH)), _const_spec((MP, DA_WIDTH)),
                  _const_spec((DA_HEADS, 2, TK, TQ)), _const_spec((DA_HEADS, 2, MP, TQ)),
                  _const_spec((DA_HEAD_DIM, TQ))],
        out_specs=pl.BlockSpec((None, TQ, DA_WIDTH), lambda b, i: (b, i, 0)),
        out_shape=jax.ShapeDtypeStruct((nb, s_len, DA_WIDTH), bf16),
        scratch_shapes=([pltpu.VMEM((2, TK, 2 * TQ), f32)] * DA_HEADS
                        + [pltpu.VMEM((TK, 2 * TQ), bf16)] * DA_HEADS
                        + [pltpu.VMEM((V_EXT, 2 * TQ), f32)] * DA_HEADS
                        + [pltpu.VMEM((V_EXT, s_len), bf16)] * DA_HEADS
                        + [pltpu.VMEM((DA_HEADS, V_EXT, MP), bf16)]),
        compiler_params=pltpu.CompilerParams(
            dimension_semantics=("parallel", "arbitrary"),
            vmem_limit_bytes=VMEM_LIMIT),
        name="attn",
    )(lam1, q, k, v, km, vm, bnear, bmeta, g_tile)


def _attention_meta(lam1, qm, km, vm, bmm, g_tile):
    smem = pl.BlockSpec(memory_space=pltpu.SMEM)
    head = pl.BlockSpec((MP, DA_HEAD_DIM), lambda h: (0, h))
    return pl.pallas_call(
        _attn_meta_kernel,
        grid=(DA_HEADS,),
        in_specs=[smem, pl.BlockSpec((DA_HEAD_DIM, MP), lambda h: (h, 0)), head, head,
                  pl.BlockSpec((None, MP, MP), lambda h: (h, 0, 0)),
                  pl.BlockSpec((DA_HEAD_DIM, MP), lambda h: (0, 0))],
        out_specs=head,
        out_shape=jax.ShapeDtypeStruct((MP, DA_WIDTH), bf16),
        scratch_shapes=[pltpu.VMEM((MP, 2 * MP), bf16), pltpu.VMEM((V_EXT, 2 * MP), f32)],
        name="attn_meta",
    )(lam1, qm, km, vm, bmm, g_tile)


GDN_R = DN_HEADS * CHUNK
GDN_LEVELS = (2, 4, 8, 16, 32)
GDN_QKV = 3 * DN_WIDTH
GDN_NCH = GDN_ROWS // CHUNK
GDN_NSEQ = 2


def _gdn_consts():
    r = np.arange(GDN_R)[:, None]
    c = np.arange(GDN_R)[None, :]
    same = (r // CHUNK) == (c // CHUNK)
    bd = np.stack([same & (r >= c), same & (r > c), same]).astype(np.float32)
    i = np.arange(CHUNK)[:, None]
    j = np.arange(GDN_R)[None, :] % CHUNK
    lv = [i == j, (i > j) & ((i // 2) == (j // 2))]
    for s in GDN_LEVELS:
        lv.append(((i // (2 * s)) == (j // (2 * s))) & ((i & s) != 0) & ((j & s) == 0))
    cat = np.stack(lv).astype(np.float32)
    rr = np.arange(GDN_ROWS)[:, None]
    cc = np.arange(GDN_ROWS)[None, :]
    lcum = (((rr // CHUNK) == (cc // CHUNK)) & (rr >= cc)).astype(np.float32)
    sel = np.zeros((2, LANES, DN_HEADS * LANES), np.float32)
    for h in range(DN_HEADS):
        sel[0, h, h * LANES:(h + 1) * LANES] = 1.0
        sel[1, DN_HEADS + h, h * LANES:(h + 1) * LANES] = 1.0
    return bd, cat, lcum, sel


def _split3(x):
    hi = x.astype(bf16)
    r1 = x - hi.astype(f32)
    mid = r1.astype(bf16)
    lo = (r1 - mid.astype(f32)).astype(bf16)
    return hi, mid, lo


def _gdn_block(n_per_seq, xq, xk, xv, ba, z, s_refs, alog_ref, dtb_ref, ng_ref, bd_ref,
               cat_ref, lcum_ref, sel_ref):
    n_seq = len(s_refs)
    n = n_seq * n_per_seq
    seq_rows = n_per_seq * CHUNK
    heads = range(DN_HEADS)
    chunks = range(n)
    dot = functools.partial(jnp.dot, preferred_element_type=f32)

    def stack(a):
        return jnp.concatenate(
            [a[c * CHUNK:(c + 1) * CHUNK, h * LANES:(h + 1) * LANES]
             for c in chunks for h in heads], axis=0)

    qn_b, kn_b = stack(xq), stack(xk)
    qn, kn, vs = qn_b.astype(f32), kn_b.astype(f32), stack(xv).astype(f32)
    blk = lambda c: slice(c * GDN_R, (c + 1) * GDN_R)

    beta_t = _sigmoid(ba)
    xg = ba + dtb_ref[...]
    softplus = jnp.maximum(xg, 0.0) + jnp.log1p(jnp.exp(-jnp.abs(xg)))
    g_t = -jnp.exp(alog_ref[...]) * softplus
    lcum = lcum_ref[0:seq_rows, 0:seq_rows]
    g_parts = _split3(g_t)
    gcum = jnp.concatenate(
        [sum(dot(lcum, part[q * seq_rows:(q + 1) * seq_rows]) for part in g_parts)
         for q in range(n_seq)], axis=0)
    g_rep = stack(sum(dot(part, sel_ref[1]) for part in _split3(gcum)))
    b_rep = stack(sum(dot(part, sel_ref[0]) for part in _split3(beta_t)[:2]))
    g_end = jnp.concatenate(
        [jnp.broadcast_to(g_rep[(b + 1) * CHUNK - 1:(b + 1) * CHUNK, :], (CHUNK, LANES))
         for b in range(n * DN_HEADS)], axis=0)
    exp_g = jnp.exp(g_rep)
    rhs = jnp.concatenate([vs * b_rep, kn * (b_rep * exp_g)], axis=1).astype(bf16)
    q_g = qn * exp_g
    k_g = (kn * jnp.exp(g_end - g_rep)).astype(bf16)
    g_last = jnp.exp(g_end)

    tri = bd_ref[0] > 0.5
    block_b = bd_ref[2].astype(bf16)
    both = lambda a: jnp.concatenate([a, a], axis=1)
    decay, m_b, x_cat = [], [], []
    for c in chunks:
        g_row = g_rep[blk(c)].T[0:1, :]
        dec = jnp.exp(jnp.where(tri, both(g_rep[blk(c)]) - g_row, -jnp.inf))
        m = bd_ref[1] * both(b_rep[blk(c)]) * _dot_nt(kn_b[blk(c)], kn_b[blk(c)]) * dec
        m_cat = sum(m[h * CHUNK:(h + 1) * CHUNK] for h in heads)
        decay.append(dec)
        m_b.append(m.astype(bf16))
        x_cat.append(cat_ref[0] - m_cat * cat_ref[1])

    def to_bd(x):
        return jnp.concatenate([x.astype(bf16)] * DN_HEADS, axis=0) * block_b

    for lvl in range(len(GDN_LEVELS)):
        ys = [dot(x_cat[c].astype(bf16), m_b[c]) for c in chunks]
        zs = [dot(ys[c].astype(bf16), to_bd(x_cat[c])) for c in chunks]
        x_cat = [x_cat[c] - zs[c] * cat_ref[2 + lvl] for c in chunks]
    sol = [dot(to_bd(x_cat[c]), rhs[blk(c)]).astype(bf16) for c in chunks]
    a_qk = [(_dot_nt(qn_b[blk(c)], kn_b[blk(c)]) * decay[c]).astype(bf16) for c in chunks]
    a_uw = [dot(a_qk[c], sol[c]) for c in chunks]
    hrows = lambda h: slice(h * CHUNK, (h + 1) * CHUNK)
    k_uw = [[_dot_tn(k_g[blk(c)][hrows(h)], sol[c][hrows(h)]) for h in heads] for c in chunks]

    out_rows = [None] * n
    for step in range(n_per_seq):
        for q in range(n_seq):
            c = q * n_per_seq + step
            s_ref = s_refs[q]
            out_cols = []
            for h in heads:
                s_old = s_ref[h]
                s_b = s_old.astype(bf16)
                q_eff = (q_g[blk(c)][hrows(h)] - a_uw[c][hrows(h), DN_DV:]).astype(bf16)
                o = a_uw[c][hrows(h), :DN_DV] + dot(q_eff, s_b)
                s_ref[h] = (s_old * g_last[blk(c)][hrows(h)][0:1, :] + k_uw[c][h][:, :DN_DV]
                            - dot(k_uw[c][h][:, DN_DV:].astype(bf16), s_b))
                o = o * lax.rsqrt(jnp.mean(o * o, axis=-1, keepdims=True) + EPS) * ng_ref[...]
                zh = z[c * CHUNK:(c + 1) * CHUNK, h * DN_DV:(h + 1) * DN_DV].astype(f32)
                out_cols.append(o * (zh * _sigmoid(zh)))
            out_rows[c] = jnp.concatenate(out_cols, axis=1)
    return jnp.concatenate(out_rows, axis=0)


def _gdn_kernel(dq_ref, dk_ref, dv_ref, dz_ref, ba_ref, mq_ref, mk_ref, mv_ref, mz_ref, mba_ref,
                alog_ref, dtb_ref, ng_ref, bd_ref, cat_ref, lcum_ref, sel_ref,
                o_ref, om_ref, s_ref):
    consts = (alog_ref, dtb_ref, ng_ref, bd_ref, cat_ref, lcum_ref, sel_ref)
    s_refs = [s_ref.at[q] for q in range(GDN_NSEQ)]

    @pl.when(pl.program_id(1) == 0)
    def _():
        s_ref[0] = jnp.zeros(s_ref.shape[1:], f32)
        om_ref[...] = _gdn_block(1, mq_ref[...], mk_ref[...], mv_ref[...], mba_ref[...],
                                 mz_ref[...], s_refs[:1], *consts).astype(om_ref.dtype)
        for q in range(1, GDN_NSEQ):
            s_ref[q] = s_ref[0]

    merge = lambda ref: ref[...].reshape(GDN_NSEQ * GDN_ROWS, ref.shape[-1])
    out = _gdn_block(GDN_NCH, merge(dq_ref), merge(dk_ref), merge(dv_ref), merge(ba_ref),
                     merge(dz_ref), s_refs, *consts)
    o_ref[...] = out.reshape(o_ref.shape).astype(o_ref.dtype)


def _gdn(dq, dk, dv, dz, ba, mq, mk, mv, mz, mba, alog_row, dtb_row, ng_row):
    nb, s_len, _ = dq.shape
    assert nb % GDN_NSEQ == 0
    row = lambda w: pl.BlockSpec((GDN_NSEQ, GDN_ROWS, w), lambda b, t: (b, t, 0))
    mrow = lambda w: pl.BlockSpec((CHUNK, w), lambda b, t: (MP // CHUNK - 1, 0))
    bd, cat, lcum, sel = _gdn_consts()
    bd, cat = jnp.asarray(bd), jnp.asarray(cat)
    lcum, sel = jnp.asarray(lcum).astype(bf16), jnp.asarray(sel).astype(bf16)
    return pl.pallas_call(
        _gdn_kernel,
        grid=(nb // GDN_NSEQ, s_len // GDN_ROWS),
        in_specs=[row(DN_WIDTH)] * 4 + [row(LANES)] + [mrow(DN_WIDTH)] * 4 + [mrow(LANES)] + [
            _const_spec((1, LANES)), _const_spec((1, LANES)),
            _const_spec((1, DN_DV)), _const_spec(bd.shape), _const_spec(cat.shape),
            _const_spec(lcum.shape), _const_spec(sel.shape)],
        out_specs=[row(DN_WIDTH), pl.BlockSpec((None, CHUNK, DN_WIDTH), lambda b, t: (b, 0, 0))],
        out_shape=[jax.ShapeDtypeStruct((nb, s_len, DN_WIDTH), bf16),
                   jax.ShapeDtypeStruct((nb // GDN_NSEQ, CHUNK, DN_WIDTH), bf16)],
        scratch_shapes=[pltpu.VMEM((GDN_NSEQ, DN_HEADS, DN_DK, DN_DV), f32)],
        compiler_params=pltpu.CompilerParams(
            dimension_semantics=("parallel", "arbitrary"), vmem_limit_bytes=VMEM_LIMIT),
        name="gdn",
    )(dq, dk, dv, dz, ba, mq, mk, mv, mz, mba, alog_row, dtb_row, ng_row, bd, cat, lcum, sel)


def _mix_and_norm(x, oda, odn, wout_ref, g2_ref):
    mix = jnp.concatenate([oda, odn], axis=1)
    h2 = x + jnp.dot(mix, wout_ref[...], preferred_element_type=f32)
    u2 = h2 * lax.rsqrt(jnp.mean(h2 * h2, axis=-1, keepdims=True) + EPS) * g2_ref[...]
    return h2, u2.astype(bf16)


def _ffn_halo_kernel(x_ref, oda_ref, odn_ref, wout_ref, g2_ref, wup_ref, halo_ref):
    _, u2 = _mix_and_norm(x_ref[...], oda_ref[...], odn_ref[...], wout_ref, g2_ref)
    halo_ref[...] = jnp.dot(u2, wup_ref[...], preferred_element_type=f32)


def _ffn_kernel(x_ref, oda_ref, odn_ref, halo_ref, wout_ref, g2_ref, wup_ref, cw_ref, cb_ref,
                wdown_ref, gf_ref, o_ref, *scratch):
    tm = FFN_TM
    carries = scratch[0:FFN_TILES + 1]
    accs = scratch[FFN_TILES + 1:2 * FFN_TILES + 1]
    hbufs = scratch[2 * FFN_TILES + 1:]
    t = pl.program_id(1)

    @pl.when(t == 0)
    def _():
        carries[0][...] = halo_ref[...]

    @pl.when(t > 0)
    def _():
        carries[0][...] = carries[FFN_TILES][...]

    n_chunks = D_FF // FFN_CW
    stages = [(a, c) for a in range(FFN_TILES) for c in range(n_chunks)]
    col_pair = lambda c: (slice(c * FFN_CW, (c + 1) * FFN_CW),
                          slice(D_FF + c * FFN_CW, D_FF + (c + 1) * FFN_CW))
    buf_pair = lambda s: hbufs[2 * (s % FFN_NBUF):2 * (s % FFN_NBUF) + 2]
    rows = lambda a: slice(a * tm, (a + 1) * tm)
    normed = {}

    def up_part(s):
        a, c = stages[s]
        if c == 0:
            normed[a] = _mix_and_norm(x_ref[rows(a), :], oda_ref[rows(a), :],
                                      odn_ref[rows(a), :], wout_ref, g2_ref)
        u2 = normed[a][1]
        for cols, hbuf in zip(col_pair(c), buf_pair(s)):
            hbuf[0:SUBLANES, :] = carries[a][:, cols]
            hup = jnp.dot(u2, wup_ref[:, cols], preferred_element_type=f32)
            hbuf[SUBLANES:SUBLANES + tm, :] = hup
            carries[a + 1][:, cols] = hup[tm - SUBLANES:, :]

    def conv_part(cols, hbuf):
        y = cb_ref[:, cols]
        for d in range(FFN_CONV):
            y = y + cw_ref[FFN_CONV - 1 - d:FFN_CONV - d, cols] * hbuf[SUBLANES - d:SUBLANES - d + tm, :]
        return y

    for s in range(FFN_AHEAD):
        up_part(s)
    for s, (a, c) in enumerate(stages):
        if s + FFN_AHEAD < len(stages):
            up_part(s + FFN_AHEAD)
        gate, val = (conv_part(cols, hbuf) for cols, hbuf in zip(col_pair(c), buf_pair(s)))
        act = (gate * _sigmoid(gate) * val).astype(bf16)
        part = jnp.dot(act, wdown_ref[c * FFN_CW:(c + 1) * FFN_CW, :], preferred_element_type=f32)
        if c == 0:
            accs[a][...] = part
        else:
            accs[a][...] += part
        if c == n_chunks - 1:
            y = normed[a][0] + accs[a][...]
            o_ref[rows(a), :] = (y * lax.rsqrt(jnp.mean(y * y, axis=-1, keepdims=True) + EPS)
                                 * gf_ref[...])


def _ffn_halo(xm, odam, odnm, w_out, g2, w_up):
    return pl.pallas_call(
        _ffn_halo_kernel,
        out_shape=jax.ShapeDtypeStruct((HALO_ROWS, 2 * D_FF), f32),
        compiler_params=pltpu.CompilerParams(vmem_limit_bytes=VMEM_LIMIT),
        name="ffn_halo",
    )(xm, odam, odnm, w_out, g2, w_up)


def _ffn(x, oda, odn, halo, w_out, g2, w_up, conv_w, conv_b, w_down, gf):
    nb, s_len, _ = x.shape
    tm = FFN_TM
    step_rows = FFN_TILES * tm
    row = lambda w: pl.BlockSpec((None, step_rows, w), lambda b, t: (b, t, 0))
    return pl.pallas_call(
        _ffn_kernel,
        grid=(nb, s_len // step_rows),
        in_specs=[row(D_MODEL), row(DA_WIDTH), row(DN_WIDTH),
                  _const_spec((SUBLANES, 2 * D_FF)), _const_spec((D_MODEL, D_MODEL)),
                  _const_spec((1, D_MODEL)), _const_spec((D_MODEL, 2 * D_FF)),
                  _const_spec((FFN_CONV, 2 * D_FF)), _const_spec((1, 2 * D_FF)),
                  _const_spec((D_FF, D_MODEL)), _const_spec((1, D_MODEL))],
        out_specs=row(D_MODEL),
        out_shape=jax.ShapeDtypeStruct((nb, s_len, D_MODEL), f32),
        scratch_shapes=([pltpu.VMEM((SUBLANES, 2 * D_FF), f32)] * (FFN_TILES + 1)
                        + [pltpu.VMEM((tm, D_MODEL), f32)] * FFN_TILES
                        + [pltpu.VMEM((SUBLANES + tm, FFN_CW), f32)] * (2 * FFN_NBUF)),
        compiler_params=pltpu.CompilerParams(
            dimension_semantics=("parallel", "arbitrary"), vmem_limit_bytes=VMEM_LIMIT),
        name="ffn",
    )(x, oda, odn, halo, w_out, g2, w_up, conv_w, conv_b, w_down, gf)


def kernel(x, meta_tokens, rel_bias, norm1_g, w_in, da_lambda, da_subln_g, dn_conv_w, dn_A_log,
           dn_dt_bias, dn_norm_g, w_out, norm2_g, w_up, ffn_conv_w, ffn_conv_b, w_down,
           final_norm_g):
    nb, s_len, _ = x.shape
    assert s_len % TQ == 0 and s_len % GDN_ROWS == 0 and s_len % (FFN_TILES * FFN_TM) == 0
    assert s_len % PROJ_TM == 0 and s_len % VT_CHUNK == 0
    w_all = jnp.pad(w_in[0], ((0, 0), (0, LANES - 2 * DN_HEADS))).astype(bf16)
    w_out_b = w_out[0].astype(bf16)
    w_up_b = w_up[0].astype(bf16)
    w_down_b = w_down[0].astype(bf16)
    g1 = norm1_g[0].reshape(1, D_MODEL).astype(f32)
    g2 = norm2_g[0].reshape(1, D_MODEL).astype(f32)
    gf = final_norm_g.reshape(1, D_MODEL).astype(f32)
    meta_pad = jnp.pad(meta_tokens.astype(x.dtype), ((MP - N_META, 0), (0, 0)))

    bnear, bmeta, bmm, lam_tile = _bias_tiles(rel_bias, da_lambda[0])
    lam1 = lam_tile[0, :1]

    conv_w = dn_conv_w[0].astype(f32)
    no_halo = jnp.zeros((SUBLANES, GDN_QKV), f32)
    mq, mk, mv, mdq, mdk, mdv, mdz, mba, mtail = [
        a[0] for a in _proj(meta_pad[None], no_halo, g1, w_all, conv_w, MP)]
    q, k, v, dq, dk, dv, dz, ba, _ = _proj(x, mtail, g1, w_all, conv_w, PROJ_TM)

    subln = da_subln_g[0].astype(f32)
    o_da = _attention(lam1, q, k, v, mk, mv, bnear, bmeta,
                      jnp.broadcast_to(subln[:, None], (DA_HEAD_DIM, TQ)))
    o_da_m = _attention_meta(lam1, mq, mk, mv, bmm,
                             jnp.broadcast_to(subln[:, None], (DA_HEAD_DIM, MP)))

    gate_row = lambda p: jnp.pad(p[0].astype(f32), (DN_HEADS, LANES - 2 * DN_HEADS)).reshape(1, LANES)
    o_dn, o_dn_m = _gdn(dq, dk, dv, dz, ba, mdq, mdk, mdv, mdz, mba,
                        gate_row(dn_A_log), gate_row(dn_dt_bias),
                        dn_norm_g[0].reshape(1, DN_DV).astype(f32))

    halo = _ffn_halo(meta_pad[MP - HALO_ROWS:], o_da_m[MP - HALO_ROWS:],
                     o_dn_m[0, CHUNK - HALO_ROWS:], w_out_b, g2, w_up_b)[HALO_ROWS - SUBLANES:]
    return _ffn(x, o_da, o_dn, halo, w_out_b, g2, w_up_b, ffn_conv_w[0].astype(f32),
                ffn_conv_b[0].reshape(1, 2 * D_FF).astype(f32), w_down_b, gf)
```

```python
import functools
import math

import numpy as np
import jax
import jax.numpy as jnp
from jax import lax
from jax.experimental import pallas as pl
from jax.experimental.pallas import tpu as pltpu

f32 = jnp.float32
bf16 = jnp.bfloat16

D_MODEL = 1024
CHUNK = 64
N_META = 16
EPS = 1e-6
NEG_INF = -1e30
LAMBDA_INIT = 0.8 - 0.6 * math.exp(-0.3 * 0)
LOG2E = math.log2(math.e)

DA_HEADS = 4
DA_HEAD_DIM = 128
DA_HALF = 64
DA_WIDTH = DA_HEADS * DA_HEAD_DIM
DN_HEADS = 4
DN_DK = 128
DN_DV = 128
DN_WIDTH = DN_HEADS * DN_DV
DN_CONV = 4
N_BUCKETS = 32
MAX_DISTANCE = 128
D_FF = 2816
FFN_CONV = 3

N_MAIN = 7 * 512
LANES = 128
SUBLANES = 8
MP = 128
TQ = 256
TK = 256
V_EXT = DA_HEAD_DIM + 16
VT_CHUNK = 512
PROJ_TM = 512
GDN_ROWS = 256
FFN_TILES = 1
FFN_TM = 256
FFN_CW = 256
FFN_AHEAD = 10
FFN_NBUF = FFN_AHEAD + 1
HALO_ROWS = 16
VMEM_LIMIT = 52 * 1024 * 1024


def _const_spec(shape):
    nd = len(shape)
    return pl.BlockSpec(shape, lambda *_: (0,) * nd, pipeline_mode=pl.Buffered(1))


def _prep_kernel(far_ref, table_ref, lam_in_ref, bnear_in, bmeta_in, bmm_in,
                 bnear_out, bmeta_out, bmm_out, lam_out):
    h = pl.program_id(0)
    c_far = table_ref[far_ref[0], h]

    def lookup(bkt):
        out = jnp.full(bkt.shape, NEG_INF, f32)
        for b in range(N_BUCKETS):
            out = jnp.where(bkt == b, (table_ref[b, h] - c_far) * LOG2E, out)
        return out

    for i in range(2):
        bnear_out[i] = lookup(bnear_in[i])
        bmeta_out[i] = lookup(bmeta_in[i])
    bmm_out[...] = lookup(bmm_in[...])
    lv = lam_in_ref[...]
    s1 = jnp.sum(lv[0:1] * lv[1:2], axis=-1, keepdims=True)
    s2 = jnp.sum(lv[2:3] * lv[3:4], axis=-1, keepdims=True)
    lam = jnp.exp(s1) - jnp.exp(s2) + LAMBDA_INIT
    lam_out[...] = jnp.broadcast_to(lam, lam_out.shape)


def _t5_bucket(rel):
    nb = N_BUCKETS // 2
    max_exact = nb // 2
    ret = jnp.where(rel > 0, nb, 0)
    n = jnp.abs(rel)
    nf = jnp.maximum(n, 1).astype(jnp.float32)
    large = max_exact + (jnp.log(nf / max_exact) / math.log(MAX_DISTANCE / max_exact)
                         * (nb - max_exact)).astype(jnp.int32)
    large = jnp.minimum(large, nb - 1)
    return ret + jnp.where(n < max_exact, n, large)


def _bias_tiles(rel_bias, da_lambda):
    r = jnp.arange(TK, dtype=jnp.int32)[:, None]
    c = jnp.arange(TQ, dtype=jnp.int32)[None, :]
    diag = jnp.where((r // CHUNK) <= (c // CHUNK), _t5_bucket(r - c), -1)
    prev = _t5_bucket(r - c - TK)
    bnear = jnp.stack([diag, prev]).astype(jnp.int32)
    far = _t5_bucket(jnp.full((1,), -(TK + 1), jnp.int32)).astype(jnp.int32)
    rm = jnp.arange(MP, dtype=jnp.int32)[:, None]
    valid = rm >= (MP - N_META)
    kpos = rm - (MP - N_META)
    m0 = jnp.where(valid, _t5_bucket(kpos - (N_META + c)), -1)
    m1 = jnp.where(valid, jnp.broadcast_to(far[0], (MP, TQ)), -1)
    bmeta = jnp.stack([m0, m1]).astype(jnp.int32)
    cm = jnp.arange(MP, dtype=jnp.int32)[None, :]
    bmm = jnp.where(valid, _t5_bucket(rm - cm), -1).astype(jnp.int32)

    smem = pl.BlockSpec(memory_space=pltpu.SMEM)
    return pl.pallas_call(
        _prep_kernel,
        grid=(DA_HEADS,),
        in_specs=[smem, smem,
                  pl.BlockSpec((4, DA_HALF), lambda h: (0, 0)),
                  pl.BlockSpec((2, TK, TQ), lambda h: (0, 0, 0)),
                  pl.BlockSpec((2, MP, TQ), lambda h: (0, 0, 0)),
                  pl.BlockSpec((MP, MP), lambda h: (0, 0))],
        out_specs=[pl.BlockSpec((None, 2, TK, TQ), lambda h: (h, 0, 0, 0)),
                   pl.BlockSpec((None, 2, MP, TQ), lambda h: (h, 0, 0, 0)),
                   pl.BlockSpec((None, MP, MP), lambda h: (h, 0, 0)),
                   pl.BlockSpec((SUBLANES, LANES), lambda h: (0, 0))],
        out_shape=[jax.ShapeDtypeStruct((DA_HEADS, 2, TK, TQ), f32),
                   jax.ShapeDtypeStruct((DA_HEADS, 2, MP, TQ), f32),
                   jax.ShapeDtypeStruct((DA_HEADS, MP, MP), f32),
                   jax.ShapeDtypeStruct((SUBLANES, LANES), f32)],
        name="prep",
    )(far, rel_bias.astype(f32), da_lambda.astype(f32), bnear, bmeta, bmm)


def _sigmoid(x):
    return 1.0 / (1.0 + jnp.exp(-x))


def _proj_kernel(x_ref, halo_ref, g_ref, w_ref, cw_ref, q_ref, k_ref, v_ref, dq_ref,
                 dk_ref, dv_ref, dz_ref, ba_ref, tail_ref, carry_ref, carry_next, u_ref):
    tm = x_ref.shape[0]
    t = pl.program_id(1)

    @pl.when(t == 0)
    def _():
        carry_ref[...] = halo_ref[...]

    @pl.when(t > 0)
    def _():
        carry_ref[...] = carry_next[...]

    x = x_ref[...]
    ms = jnp.mean(x * x, axis=-1, keepdims=True)
    u_ref[...] = (x * lax.rsqrt(ms + EPS) * g_ref[...]).astype(bf16)
    outs = (q_ref, k_ref, v_ref, dq_ref, dk_ref, dv_ref, dz_ref)
    n_out = len(outs)
    project = lambda j: jnp.dot(u_ref[...], w_ref[:, j * 512:(j + 1) * 512],
                                preferred_element_type=f32)

    def conv_silu(j, r):
        cols = slice((j - 3) * DN_WIDTH, (j - 2) * DN_WIDTH)
        carry_next[:, cols] = r[tm - SUBLANES:, :]
        ext = jnp.concatenate([carry_ref[:, cols], r], axis=0)
        y = cw_ref[0:1, cols] * ext
        for d in range(1, DN_CONV):
            y = cw_ref[d:d + 1, cols] * ext + pltpu.roll(y, 1, 0)
        y = y[SUBLANES:, :]
        return y * _sigmoid(y)

    def l2norm_heads(y, scale):
        parts = []
        for h in range(DN_HEADS):
            yh = y[:, h * DN_DK:(h + 1) * DN_DK]
            parts.append(yh * (lax.rsqrt(jnp.sum(yh * yh, axis=-1, keepdims=True) + EPS) * scale))
        return jnp.concatenate(parts, axis=1)

    order = (3, 0, 4, 1, 5, 2, 6)
    ahead = 2
    pending = [project(j) for j in order[:ahead]]
    for pos, j in enumerate(order):
        r = pending.pop(0)
        if pos + ahead < n_out:
            pending.append(project(order[pos + ahead]))
        elif pos + ahead == n_out:
            ba_ref[...] = jnp.dot(u_ref[...], w_ref[:, N_MAIN:N_MAIN + LANES],
                                  preferred_element_type=f32)
        if j == 0:
            r = r * (DA_HALF ** -0.5 * LOG2E)
        elif j in (3, 4):
            r = l2norm_heads(conv_silu(j, r), DN_DK ** -0.5 if j == 3 else 1.0)
        elif j == 5:
            r = conv_silu(j, r)
        if j == 0:
            r = r.T
        outs[j][...] = r.astype(outs[j].dtype)
    tail_ref[...] = carry_next[...]


def _proj(x3, halo, g1, w_all, conv_w, tm):
    nb, rows, _ = x3.shape
    row_spec = lambda w: pl.BlockSpec((None, tm, w), lambda b, t: (b, t, 0))
    outs = [jax.ShapeDtypeStruct((nb, 512, rows), bf16)] + [
        jax.ShapeDtypeStruct((nb, rows, 512), bf16)] * 6 + [
        jax.ShapeDtypeStruct((nb, rows, LANES), f32),
        jax.ShapeDtypeStruct((nb, SUBLANES, GDN_QKV), f32)]
    return pl.pallas_call(
        _proj_kernel,
        grid=(nb, rows // tm),
        in_specs=[row_spec(D_MODEL), _const_spec((SUBLANES, GDN_QKV)), _const_spec((1, D_MODEL)),
                  _const_spec((D_MODEL, N_MAIN + LANES)), _const_spec((DN_CONV, GDN_QKV))],
        out_specs=[pl.BlockSpec((None, 512, tm), lambda b, t: (b, 0, t))]
                  + [row_spec(512)] * 6 + [row_spec(LANES),
                   pl.BlockSpec((None, SUBLANES, GDN_QKV), lambda b, t: (b, 0, 0))],
        out_shape=outs,
        scratch_shapes=([pltpu.VMEM((SUBLANES, GDN_QKV), f32)] * 2
                        + [pltpu.VMEM((tm, D_MODEL), bf16)]),
        compiler_params=pltpu.CompilerParams(
            dimension_semantics=("parallel", "arbitrary"), vmem_limit_bytes=VMEM_LIMIT),
        name="proj",
    )(x3, halo, g1, w_all, conv_w)


def _dot_nt(a, b):
    return lax.dot_general(a, b, (((1,), (1,)), ((), ())), preferred_element_type=f32)


def _dot_tn(a, b):
    return lax.dot_general(a, b, (((0,), (0,)), ((), ())), preferred_element_type=f32)


def _stack_components(qt):
    zero = jnp.zeros((DA_HALF, qt.shape[1]), qt.dtype)
    q0 = jnp.concatenate([qt[:DA_HALF], zero], axis=0)
    q1 = jnp.concatenate([zero, qt[DA_HALF:]], axis=0)
    return jnp.concatenate([q0, q1], axis=1)


def _colmax(s):
    return jnp.max(s, axis=0, keepdims=True)


def _values_t_ext(v):
    vt = v.astype(f32).T.astype(bf16)
    return jnp.concatenate([vt, jnp.ones((V_EXT - DA_HEAD_DIM, v.shape[0]), bf16)], axis=0)


def _softmax_step(stats, s, smax, vt_pend, p_ref, acc_ref):
    m, alpha_pend = stats
    pv = jnp.dot(vt_pend, p_ref[...], preferred_element_type=f32)
    m_new = jnp.maximum(m, smax)
    alpha = jnp.exp2(m - m_new)
    p = jnp.exp2((s - m_new).astype(bf16))
    acc_ref[...] = alpha_pend * acc_ref[...] + pv
    return (m_new, alpha), p


def _attn_finish(stats, vt_pend, p_ref, acc_ref, lam, g, tq):
    m, alpha_pend = stats
    acc = alpha_pend * acc_ref[...] + jnp.dot(vt_pend, p_ref[...], preferred_element_type=f32)
    l = acc[DA_HEAD_DIM:DA_HEAD_DIM + 1, :]
    acc = acc[:DA_HEAD_DIM, :]
    o = acc[:, :tq] / l[:, :tq] - lam * (acc[:, tq:] / l[:, tq:])
    ms = jnp.mean(o * o, axis=0, keepdims=True)
    y = o * lax.rsqrt(ms + EPS) * g * (1.0 - LAMBDA_INIT)
    return y.T


def _attn_init(tq, p_ref, acc_ref):
    p_ref[...] = jnp.zeros(p_ref.shape, bf16)
    acc_ref[...] = jnp.zeros(acc_ref.shape, f32)
    return (jnp.full((1, 2 * tq), -3e38, f32), jnp.ones((1, 2 * tq), f32))


def _both(b):
    return jnp.concatenate([b, b], axis=1)


def _attn_kernel(lam_ref, q_ref, k_ref, v_ref, km_ref, vm_ref, bnear_ref, bmeta_ref, g_ref,
                 o_ref, *scratch):
    i = pl.program_id(1)
    n_far = jnp.maximum(i - 1, 0)
    heads = range(DA_HEADS)
    cols = lambda h: slice(h * DA_HEAD_DIM, (h + 1) * DA_HEAD_DIM)
    qz = [_stack_components(q_ref[cols(h), :]) for h in heads]

    def scores(h, j):
        start = pl.multiple_of(j * TK, TK)
        return jnp.dot(k_ref[pl.ds(start, TK), cols(h)], qz[h], preferred_element_type=f32)

    s_bufs = scratch[0:DA_HEADS]
    p_bufs = scratch[DA_HEADS:2 * DA_HEADS]
    accs = scratch[2 * DA_HEADS:3 * DA_HEADS]
    vts = scratch[3 * DA_HEADS:4 * DA_HEADS]
    vmt_ref = scratch[4 * DA_HEADS]

    @pl.when(i == 0)
    def _():
        for h in heads:
            for c0 in range(0, k_ref.shape[0], VT_CHUNK):
                vts[h][:, c0:c0 + VT_CHUNK] = _values_t_ext(v_ref[c0:c0 + VT_CHUNK, cols(h)])
            vmt_ref[h] = _values_t_ext(vm_ref[:, cols(h)])

    def values(h, j):
        start = pl.multiple_of(jnp.maximum(j, 0) * TK, TK)
        return vts[h][:, pl.ds(start, TK)]

    stats, smax = [], []
    for h in heads:
        stats.append(_attn_init(TQ, p_bufs[h], accs[h]))
        s0 = scores(h, 0)
        s_bufs[h][0] = s0
        smax.append(_colmax(s0))

    def far_body(j, carry):
        stats, smax = carry
        slot = lax.rem(j, 2)
        new_stats, new_smax = [], []
        for h in heads:
            st, p = _softmax_step(stats[h], s_bufs[h][slot], smax[h], values(h, j - 1),
                                  p_bufs[h], accs[h])
            p_bufs[h][...] = p
            new_stats.append(st)
        for h in heads:
            s_next = scores(h, j + 1)
            s_bufs[h][1 - slot] = s_next
            new_smax.append(_colmax(s_next))
        return tuple(new_stats), tuple(new_smax)

    stats, _ = lax.fori_loop(0, n_far, far_body, (tuple(stats), tuple(smax)))
    gate = jnp.where(i >= 1, 0.0, NEG_INF).astype(f32)
    for h in heads:
        p_h, acc_h = p_bufs[h], accs[h]
        s = s_bufs[h][lax.rem(n_far, 2)] + _both(bnear_ref[h, 1] + gate)
        st, p = _softmax_step(stats[h], s, _colmax(s), values(h, n_far - 1), p_h, acc_h)
        p_h[...] = p
        s = scores(h, i) + _both(bnear_ref[h, 0])
        st, p = _softmax_step(st, s, _colmax(s), values(h, i - 1), p_h, acc_h)
        p_h[...] = p
        s = (jnp.dot(km_ref[:, cols(h)], qz[h], preferred_element_type=f32)
             + _both(bmeta_ref[h, jnp.minimum(i, 1)]))
        st, p = _softmax_step(st, s, _colmax(s), values(h, i), p_h, acc_h)
        p_meta = p_h.at[0:MP, :]
        p_meta[...] = p
        o_ref[:, cols(h)] = _attn_finish(st, vmt_ref[h], p_meta, acc_h, lam_ref[0],
                                         g_ref[...], TQ).astype(o_ref.dtype)


def _attn_meta_kernel(lam_ref, q_ref, km_ref, vm_ref, bmm_ref, g_ref, o_ref, p_buf, acc_ref):
    qz = _stack_components(q_ref[...])
    vmt = _values_t_ext(vm_ref[...])
    stats = _attn_init(MP, p_buf, acc_ref)
    s = jnp.dot(km_ref[...], qz, preferred_element_type=f32) + _both(bmm_ref[...])
    stats, p = _softmax_step(stats, s, _colmax(s), vmt, p_buf, acc_ref)
    p_buf[...] = p
    o_ref[...] = _attn_finish(stats, vmt, p_buf, acc_ref, lam_ref[0], g_ref[...],
                              MP).astype(o_ref.dtype)


def _attention(lam1, q, k, v, km, vm, bnear, bmeta, g_tile):
    nb, s_len, _ = k.shape
    smem = pl.BlockSpec(memory_space=pltpu.SMEM)
    return pl.pallas_call(
        _attn_kernel,
        grid=(nb, s_len // TQ),
        in_specs=[smem,
                  pl.BlockSpec((None, DA_WIDTH, TQ), lambda b, i: (b, 0, i)),
                  pl.BlockSpec((None, s_len, DA_WIDTH), lambda b, i: (b, 0, 0)),
                  pl.BlockSpec((None, s_len, DA_WIDTH), lambda b, i: (b, 0, 0)),
                  _const_spec((MP, DA_WIDTH)), _const_spec((MP, DA_WIDTH)),
                  _const_spec((DA_HEADS, 2, TK, TQ)), _const_spec((DA_HEADS, 2, MP, TQ)),
                  _const_spec((DA_HEAD_DIM, TQ))],
        out_specs=pl.BlockSpec((None, TQ, DA_WIDTH), lambda b, i: (b, i, 0)),
        out_shape=jax.ShapeDtypeStruct((nb, s_len, DA_WIDTH), bf16),
        scratch_shapes=([pltpu.VMEM((2, TK, 2 * TQ), f32)] * DA_HEADS
                        + [pltpu.VMEM((TK, 2 * TQ), bf16)] * DA_HEADS
                        + [pltpu.VMEM((V_EXT, 2 * TQ), f32)] * DA_HEADS
                        + [pltpu.VMEM((V_EXT, s_len), bf16)] * DA_HEADS
                        + [pltpu.VMEM((DA_HEADS, V_EXT, MP), bf16)]),
        compiler_params=pltpu.CompilerParams(
            dimension_semantics=("parallel", "arbitrary"),
            vmem_limit_bytes=VMEM_LIMIT),
        name="attn",
    )(lam1, q, k, v, km, vm, bnear, bmeta, g_tile)


def _attention_meta(lam1, qm, km, vm, bmm, g_tile):
    smem = pl.BlockSpec(memory_space=pltpu.SMEM)
    head = pl.BlockSpec((MP, DA_HEAD_DIM), lambda h: (0, h))
    return pl.pallas_call(
        _attn_meta_kernel,
        grid=(DA_HEADS,),
        in_specs=[smem, pl.BlockSpec((DA_HEAD_DIM, MP), lambda h: (h, 0)), head, head,
                  pl.BlockSpec((None, MP, MP), lambda h: (h, 0, 0)),
                  pl.BlockSpec((DA_HEAD_DIM, MP), lambda h: (0, 0))],
        out_specs=head,
        out_shape=jax.ShapeDtypeStruct((MP, DA_WIDTH), bf16),
        scratch_shapes=[pltpu.VMEM((MP, 2 * MP), bf16), pltpu.VMEM((V_EXT, 2 * MP), f32)],
        name="attn_meta",
    )(lam1, qm, km, vm, bmm, g_tile)


GDN_R = DN_HEADS * CHUNK
GDN_LEVELS = (2, 4, 8, 16, 32)
GDN_QKV = 3 * DN_WIDTH
GDN_NCH = GDN_ROWS // CHUNK
GDN_NSEQ = 4


def _gdn_consts():
    r = np.arange(GDN_R)[:, None]
    c = np.arange(GDN_R)[None, :]
    same = (r // CHUNK) == (c // CHUNK)
    bd = np.stack([same & (r >= c), same & (r > c), same]).astype(np.float32)
    i = np.arange(CHUNK)[:, None]
    j = np.arange(GDN_R)[None, :] % CHUNK
    lv = [i == j, (i > j) & ((i // 2) == (j // 2))]
    for s in GDN_LEVELS:
        lv.append(((i // (2 * s)) == (j // (2 * s))) & ((i & s) != 0) & ((j & s) == 0))
    cat = np.stack(lv).astype(np.float32)
    rr = np.arange(GDN_ROWS)[:, None]
    cc = np.arange(GDN_ROWS)[None, :]
    lcum = (((rr // CHUNK) == (cc // CHUNK)) & (rr >= cc)).astype(np.float32)
    sel = np.zeros((2, LANES, DN_HEADS * LANES), np.float32)
    for h in range(DN_HEADS):
        sel[0, h, h * LANES:(h + 1) * LANES] = 1.0
        sel[1, DN_HEADS + h, h * LANES:(h + 1) * LANES] = 1.0
    return bd, cat, lcum, sel


def _split3(x):
    hi = x.astype(bf16)
    r1 = x - hi.astype(f32)
    mid = r1.astype(bf16)
    lo = (r1 - mid.astype(f32)).astype(bf16)
    return hi, mid, lo


def _gdn_block(n_per_seq, xq, xk, xv, ba, z, s_refs, alog_ref, dtb_ref, ng_ref, bd_ref,
               cat_ref, lcum_ref, sel_ref):
    n_seq = len(s_refs)
    n = n_seq * n_per_seq
    seq_rows = n_per_seq * CHUNK
    heads = range(DN_HEADS)
    chunks = range(n)
    dot = functools.partial(jnp.dot, preferred_element_type=f32)

    def stack(a):
        return jnp.concatenate(
            [a[c * CHUNK:(c + 1) * CHUNK, h * LANES:(h + 1) * LANES]
             for c in chunks for h in heads], axis=0)

    qn_b, kn_b = stack(xq), stack(xk)
    qn, kn, vs = qn_b.astype(f32), kn_b.astype(f32), stack(xv).astype(f32)
    blk = lambda c: slice(c * GDN_R, (c + 1) * GDN_R)

    beta_t = _sigmoid(ba)
    xg = ba + dtb_ref[...]
    softplus = jnp.maximum(xg, 0.0) + jnp.log1p(jnp.exp(-jnp.abs(xg)))
    g_t = -jnp.exp(alog_ref[...]) * softplus
    lcum = lcum_ref[0:seq_rows, 0:seq_rows]
    g_parts = _split3(g_t)
    gcum = jnp.concatenate(
        [sum(dot(lcum, part[q * seq_rows:(q + 1) * seq_rows]) for part in g_parts)
         for q in range(n_seq)], axis=0)
    g_rep = stack(sum(dot(part, sel_ref[1]) for part in _split3(gcum)))
    b_rep = stack(sum(dot(part, sel_ref[0]) for part in _split3(beta_t)[:2]))
    g_end = jnp.concatenate(
        [jnp.broadcast_to(g_rep[(b + 1) * CHUNK - 1:(b + 1) * CHUNK, :], (CHUNK, LANES))
         for b in range(n * DN_HEADS)], axis=0)
    exp_g = jnp.exp(g_rep)
    rhs = jnp.concatenate([vs * b_rep, kn * (b_rep * exp_g)], axis=1).astype(bf16)
    q_g = qn * exp_g
    k_g = (kn * jnp.exp(g_end - g_rep)).astype(bf16)
    g_last = jnp.exp(g_end)

    tri = bd_ref[0] > 0.5
    block_b = bd_ref[2].astype(bf16)
    both = lambda a: jnp.concatenate([a, a], axis=1)
    decay, m_b, x_cat = [], [], []
    for c in chunks:
        g_row = g_rep[blk(c)].T[0:1, :]
        dec = jnp.exp(jnp.where(tri, both(g_rep[blk(c)]) - g_row, -jnp.inf))
        m = bd_ref[1] * both(b_rep[blk(c)]) * _dot_nt(kn_b[blk(c)], kn_b[blk(c)]) * dec
        m_cat = sum(m[h * CHUNK:(h + 1) * CHUNK] for h in heads)
        decay.append(dec)
        m_b.append(m.astype(bf16))
        x_cat.append(cat_ref[0] - m_cat * cat_ref[1])

    def to_bd(x):
        return jnp.concatenate([x.astype(bf16)] * DN_HEADS, axis=0) * block_b

    for lvl in range(len(GDN_LEVELS)):
        ys = [dot(x_cat[c].astype(bf16), m_b[c]) for c in chunks]
        zs = [dot(ys[c].astype(bf16), to_bd(x_cat[c])) for c in chunks]
        x_cat = [x_cat[c] - zs[c] * cat_ref[2 + lvl] for c in chunks]
    sol = [dot(to_bd(x_cat[c]), rhs[blk(c)]).astype(bf16) for c in chunks]
    a_qk = [(_dot_nt(qn_b[blk(c)], kn_b[blk(c)]) * decay[c]).astype(bf16) for c in chunks]
    a_uw = [dot(a_qk[c], sol[c]) for c in chunks]
    hrows = lambda h: slice(h * CHUNK, (h + 1) * CHUNK)
    k_uw = [[_dot_tn(k_g[blk(c)][hrows(h)], sol[c][hrows(h)]) for h in heads] for c in chunks]

    out_rows = [None] * n
    for step in range(n_per_seq):
        for q in range(n_seq):
            c = q * n_per_seq + step
            s_ref = s_refs[q]
            out_cols = []
            for h in heads:
                s_old = s_ref[h]
                s_b = s_old.astype(bf16)
                q_eff = (q_g[blk(c)][hrows(h)] - a_uw[c][hrows(h), DN_DV:]).astype(bf16)
                o = a_uw[c][hrows(h), :DN_DV] + dot(q_eff, s_b)
                s_ref[h] = (s_old * g_last[blk(c)][hrows(h)][0:1, :] + k_uw[c][h][:, :DN_DV]
                            - dot(k_uw[c][h][:, DN_DV:].astype(bf16), s_b))
                o = o * lax.rsqrt(jnp.mean(o * o, axis=-1, keepdims=True) + EPS) * ng_ref[...]
                zh = z[c * CHUNK:(c + 1) * CHUNK, h * DN_DV:(h + 1) * DN_DV].astype(f32)
                out_cols.append(o * (zh * _sigmoid(zh)))
            out_rows[c] = jnp.concatenate(out_cols, axis=1)
    return jnp.concatenate(out_rows, axis=0)


def _gdn_kernel(dq_ref, dk_ref, dv_ref, dz_ref, ba_ref, mq_ref, mk_ref, mv_ref, mz_ref, mba_ref,
                alog_ref, dtb_ref, ng_ref, bd_ref, cat_ref, lcum_ref, sel_ref,
                o_ref, om_ref, s_ref):
    consts = (alog_ref, dtb_ref, ng_ref, bd_ref, cat_ref, lcum_ref, sel_ref)
    s_refs = [s_ref.at[q] for q in range(GDN_NSEQ)]

    @pl.when(pl.program_id(1) == 0)
    def _():
        s_ref[0] = jnp.zeros(s_ref.shape[1:], f32)
        om_ref[...] = _gdn_block(1, mq_ref[...], mk_ref[...], mv_ref[...], mba_ref[...],
                                 mz_ref[...], s_refs[:1], *consts).astype(om_ref.dtype)
        for q in range(1, GDN_NSEQ):
            s_ref[q] = s_ref[0]

    merge = lambda ref: ref[...].reshape(GDN_NSEQ * GDN_ROWS, ref.shape[-1])
    out = _gdn_block(GDN_NCH, merge(dq_ref), merge(dk_ref), merge(dv_ref), merge(ba_ref),
                     merge(dz_ref), s_refs, *consts)
    o_ref[...] = out.reshape(o_ref.shape).astype(o_ref.dtype)


def _gdn(dq, dk, dv, dz, ba, mq, mk, mv, mz, mba, alog_row, dtb_row, ng_row):
    nb, s_len, _ = dq.shape
    assert nb % GDN_NSEQ == 0
    row = lambda w: pl.BlockSpec((GDN_NSEQ, GDN_ROWS, w), lambda b, t: (b, t, 0))
    mrow = lambda w: pl.BlockSpec((CHUNK, w), lambda b, t: (MP // CHUNK - 1, 0))
    bd, cat, lcum, sel = _gdn_consts()
    bd, cat = jnp.asarray(bd), jnp.asarray(cat)
    lcum, sel = jnp.asarray(lcum).astype(bf16), jnp.asarray(sel).astype(bf16)
    return pl.pallas_call(
        _gdn_kernel,
        grid=(nb // GDN_NSEQ, s_len // GDN_ROWS),
        in_specs=[row(DN_WIDTH)] * 4 + [row(LANES)] + [mrow(DN_WIDTH)] * 4 + [mrow(LANES)] + [
            _const_spec((1, LANES)), _const_spec((1, LANES)),
            _const_spec((1, DN_DV)), _const_spec(bd.shape), _const_spec(cat.shape),
            _const_spec(lcum.shape), _const_spec(sel.shape)],
        out_specs=[row(DN_WIDTH), pl.BlockSpec((None, CHUNK, DN_WIDTH), lambda b, t: (b, 0, 0))],
        out_shape=[jax.ShapeDtypeStruct((nb, s_len, DN_WIDTH), bf16),
                   jax.ShapeDtypeStruct((nb // GDN_NSEQ, CHUNK, DN_WIDTH), bf16)],
        scratch_shapes=[pltpu.VMEM((GDN_NSEQ, DN_HEADS, DN_DK, DN_DV), f32)],
        compiler_params=pltpu.CompilerParams(
            dimension_semantics=("parallel", "arbitrary"), vmem_limit_bytes=VMEM_LIMIT),
        name="gdn",
    )(dq, dk, dv, dz, ba, mq, mk, mv, mz, mba, alog_row, dtb_row, ng_row, bd, cat, lcum, sel)


def _mix_and_norm(x, oda, odn, wout_ref, g2_ref):
    mix = jnp.concatenate([oda, odn], axis=1)
    h2 = x + jnp.dot(mix, wout_ref[...], preferred_element_type=f32)
    u2 = h2 * lax.rsqrt(jnp.mean(h2 * h2, axis=-1, keepdims=True) + EPS) * g2_ref[...]
    return h2, u2.astype(bf16)


def _ffn_halo_kernel(x_ref, oda_ref, odn_ref, wout_ref, g2_ref, wup_ref, halo_ref):
    _, u2 = _mix_and_norm(x_ref[...], oda_ref[...], odn_ref[...], wout_ref, g2_ref)
    halo_ref[...] = jnp.dot(u2, wup_ref[...], preferred_element_type=f32)


def _ffn_kernel(x_ref, oda_ref, odn_ref, halo_ref, wout_ref, g2_ref, wup_ref, cw_ref, cb_ref,
                wdown_ref, gf_ref, o_ref, *scratch):
    tm = FFN_TM
    carries = scratch[0:FFN_TILES + 1]
    accs = scratch[FFN_TILES + 1:2 * FFN_TILES + 1]
    hbufs = scratch[2 * FFN_TILES + 1:]
    t = pl.program_id(1)

    @pl.when(t == 0)
    def _():
        carries[0][...] = halo_ref[...]

    @pl.when(t > 0)
    def _():
        carries[0][...] = carries[FFN_TILES][...]

    n_chunks = D_FF // FFN_CW
    stages = [(a, c) for a in range(FFN_TILES) for c in range(n_chunks)]
    col_pair = lambda c: (slice(c * FFN_CW, (c + 1) * FFN_CW),
                          slice(D_FF + c * FFN_CW, D_FF + (c + 1) * FFN_CW))
    buf_pair = lambda s: hbufs[2 * (s % FFN_NBUF):2 * (s % FFN_NBUF) + 2]
    rows = lambda a: slice(a * tm, (a + 1) * tm)
    normed = {}

    def up_part(s):
        a, c = stages[s]
        if c == 0:
            normed[a] = _mix_and_norm(x_ref[rows(a), :], oda_ref[rows(a), :],
                                      odn_ref[rows(a), :], wout_ref, g2_ref)
        u2 = normed[a][1]
        for cols, hbuf in zip(col_pair(c), buf_pair(s)):
            hbuf[0:SUBLANES, :] = carries[a][:, cols]
            hup = jnp.dot(u2, wup_ref[:, cols], preferred_element_type=f32)
            hbuf[SUBLANES:SUBLANES + tm, :] = hup
            carries[a + 1][:, cols] = hup[tm - SUBLANES:, :]

    def conv_part(cols, hbuf):
        y = cb_ref[:, cols]
        for d in range(FFN_CONV):
            y = y + cw_ref[FFN_CONV - 1 - d:FFN_CONV - d, cols] * hbuf[SUBLANES - d:SUBLANES - d + tm, :]
        return y

    for s in range(FFN_AHEAD):
        up_part(s)
    for s, (a, c) in enumerate(stages):
        if s + FFN_AHEAD < len(stages):
            up_part(s + FFN_AHEAD)
        gate, val = (conv_part(cols, hbuf) for cols, hbuf in zip(col_pair(c), buf_pair(s)))
        act = (gate * _sigmoid(gate) * val).astype(bf16)
        part = jnp.dot(act, wdown_ref[c * FFN_CW:(c + 1) * FFN_CW, :], preferred_element_type=f32)
        if c == 0:
            accs[a][...] = part
        else:
            accs[a][...] += part
        if c == n_chunks - 1:
            y = normed[a][0] + accs[a][...]
            o_ref[rows(a), :] = (y * lax.rsqrt(jnp.mean(y * y, axis=-1, keepdims=True) + EPS)
                                 * gf_ref[...])


def _ffn_halo(xm, odam, odnm, w_out, g2, w_up):
    return pl.pallas_call(
        _ffn_halo_kernel,
        out_shape=jax.ShapeDtypeStruct((HALO_ROWS, 2 * D_FF), f32),
        compiler_params=pltpu.CompilerParams(vmem_limit_bytes=VMEM_LIMIT),
        name="ffn_halo",
    )(xm, odam, odnm, w_out, g2, w_up)


def _ffn(x, oda, odn, halo, w_out, g2, w_up, conv_w, conv_b, w_down, gf):
    nb, s_len, _ = x.shape
    tm = FFN_TM
    step_rows = FFN_TILES * tm
    row = lambda w: pl.BlockSpec((None, step_rows, w), lambda b, t: (b, t, 0))
    return pl.pallas_call(
        _ffn_kernel,
        grid=(nb, s_len // step_rows),
        in_specs=[row(D_MODEL), row(DA_WIDTH), row(DN_WIDTH),
                  _const_spec((SUBLANES, 2 * D_FF)), _const_spec((D_MODEL, D_MODEL)),
                  _const_spec((1, D_MODEL)), _const_spec((D_MODEL, 2 * D_FF)),
                  _const_spec((FFN_CONV, 2 * D_FF)), _const_spec((1, 2 * D_FF)),
                  _const_spec((D_FF, D_MODEL)), _const_spec((1, D_MODEL))],
        out_specs=row(D_MODEL),
        out_shape=jax.ShapeDtypeStruct((nb, s_len, D_MODEL), f32),
        scratch_shapes=([pltpu.VMEM((SUBLANES, 2 * D_FF), f32)] * (FFN_TILES + 1)
                        + [pltpu.VMEM((tm, D_MODEL), f32)] * FFN_TILES
                        + [pltpu.VMEM((SUBLANES + tm, FFN_CW), f32)] * (2 * FFN_NBUF)),
        compiler_params=pltpu.CompilerParams(
            dimension_semantics=("parallel", "arbitrary"), vmem_limit_bytes=VMEM_LIMIT),
        name="ffn",
    )(x, oda, odn, halo, w_out, g2, w_up, conv_w, conv_b, w_down, gf)


def kernel(x, meta_tokens, rel_bias, norm1_g, w_in, da_lambda, da_subln_g, dn_conv_w, dn_A_log,
           dn_dt_bias, dn_norm_g, w_out, norm2_g, w_up, ffn_conv_w, ffn_conv_b, w_down,
           final_norm_g):
    nb, s_len, _ = x.shape
    assert s_len % TQ == 0 and s_len % GDN_ROWS == 0 and s_len % (FFN_TILES * FFN_TM) == 0
    assert s_len % PROJ_TM == 0 and s_len % VT_CHUNK == 0
    w_all = jnp.pad(w_in[0], ((0, 0), (0, LANES - 2 * DN_HEADS))).astype(bf16)
    w_out_b = w_out[0].astype(bf16)
    w_up_b = w_up[0].astype(bf16)
    w_down_b = w_down[0].astype(bf16)
    g1 = norm1_g[0].reshape(1, D_MODEL).astype(f32)
    g2 = norm2_g[0].reshape(1, D_MODEL).astype(f32)
    gf = final_norm_g.reshape(1, D_MODEL).astype(f32)
    meta_pad = jnp.pad(meta_tokens.astype(x.dtype), ((MP - N_META, 0), (0, 0)))

    bnear, bmeta, bmm, lam_tile = _bias_tiles(rel_bias, da_lambda[0])
    lam1 = lam_tile[0, :1]

    conv_w = dn_conv_w[0].astype(f32)
    no_halo = jnp.zeros((SUBLANES, GDN_QKV), f32)
    mq, mk, mv, mdq, mdk, mdv, mdz, mba, mtail = [
        a[0] for a in _proj(meta_pad[None], no_halo, g1, w_all, conv_w, MP)]
    q, k, v, dq, dk, dv, dz, ba, _ = _proj(x, mtail, g1, w_all, conv_w, PROJ_TM)

    subln = da_subln_g[0].astype(f32)
    o_da = _attention(lam1, q, k, v, mk, mv, bnear, bmeta,
                      jnp.broadcast_to(subln[:, None], (DA_HEAD_DIM, TQ)))
    o_da_m = _attention_meta(lam1, mq, mk, mv, bmm,
                             jnp.broadcast_to(subln[:, None], (DA_HEAD_DIM, MP)))

    gate_row = lambda p: jnp.pad(p[0].astype(f32), (DN_HEADS, LANES - 2 * DN_HEADS)).reshape(1, LANES)
    o_dn, o_dn_m = _gdn(dq, dk, dv, dz, ba, mdq, mdk, mdv, mdz, mba,
                        gate_row(dn_A_log), gate_row(dn_dt_bias),
                        dn_norm_g[0].reshape(1, DN_DV).astype(f32))

    halo = _ffn_halo(meta_pad[MP - HALO_ROWS:], o_da_m[MP - HALO_ROWS:],
                     o_dn_m[0, CHUNK - HALO_ROWS:], w_out_b, g2, w_up_b)[HALO_ROWS - SUBLANES:]
    return _ffn(x, o_da, o_dn, halo, w_out_b, g2, w_up_b, ffn_conv_w[0].astype(f32),
                ffn_conv_b[0].reshape(1, 2 * D_FF).astype(f32), w_down_b, gf)
```

```python
import functools
import math

import numpy as np
import jax
import jax.numpy as jnp
from jax import lax
from jax.experimental import pallas as pl
from jax.experimental.pallas import tpu as pltpu

f32 = jnp.float32
bf16 = jnp.bfloat16

D_MODEL = 1024
CHUNK = 64
N_META = 16
EPS = 1e-6
NEG_INF = -1e30
LAMBDA_INIT = 0.8 - 0.6 * math.exp(-0.3 * 0)
LOG2E = math.log2(math.e)

DA_HEADS = 4
DA_HEAD_DIM = 128
DA_HALF = 64
DA_WIDTH = DA_HEADS * DA_HEAD_DIM
DN_HEADS = 4
DN_DK = 128
DN_DV = 128
DN_WIDTH = DN_HEADS * DN_DV
DN_CONV = 4
N_BUCKETS = 32
MAX_DISTANCE = 128
D_FF = 2816
FFN_CONV = 3

N_MAIN = 7 * 512
LANES = 128
SUBLANES = 8
MP = 128
TQ = 256
TK = 256
V_EXT = DA_HEAD_DIM + 16
VT_CHUNK = 512
PROJ_TM = 512
GDN_ROWS = 256
FFN_TILES = 1
FFN_TM = 256
FFN_CW = 256
FFN_AHEAD = 10
FFN_NBUF = FFN_AHEAD + 1
HALO_ROWS = 16
VMEM_LIMIT = 52 * 1024 * 1024


def _const_spec(shape):
    nd = len(shape)
    return pl.BlockSpec(shape, lambda *_: (0,) * nd, pipeline_mode=pl.Buffered(1))


def _prep_kernel(far_ref, table_ref, lam_in_ref, bnear_in, bmeta_in, bmm_in,
                 bnear_out, bmeta_out, bmm_out, lam_out):
    h = pl.program_id(0)
    c_far = table_ref[far_ref[0], h]

    def lookup(bkt):
        out = jnp.full(bkt.shape, NEG_INF, f32)
        for b in range(N_BUCKETS):
            out = jnp.where(bkt == b, (table_ref[b, h] - c_far) * LOG2E, out)
        return out

    for i in range(2):
        bnear_out[i] = lookup(bnear_in[i])
        bmeta_out[i] = lookup(bmeta_in[i])
    bmm_out[...] = lookup(bmm_in[...])
    lv = lam_in_ref[...]
    s1 = jnp.sum(lv[0:1] * lv[1:2], axis=-1, keepdims=True)
    s2 = jnp.sum(lv[2:3] * lv[3:4], axis=-1, keepdims=True)
    lam = jnp.exp(s1) - jnp.exp(s2) + LAMBDA_INIT
    lam_out[...] = jnp.broadcast_to(lam, lam_out.shape)


def _t5_bucket(rel):
    nb = N_BUCKETS // 2
    max_exact = nb // 2
    ret = jnp.where(rel > 0, nb, 0)
    n = jnp.abs(rel)
    nf = jnp.maximum(n, 1).astype(jnp.float32)
    large = max_exact + (jnp.log(nf / max_exact) / math.log(MAX_DISTANCE / max_exact)
                         * (nb - max_exact)).astype(jnp.int32)
    large = jnp.minimum(large, nb - 1)
    return ret + jnp.where(n < max_exact, n, large)


def _bias_tiles(rel_bias, da_lambda):
    r = jnp.arange(TK, dtype=jnp.int32)[:, None]
    c = jnp.arange(TQ, dtype=jnp.int32)[None, :]
    diag = jnp.where((r // CHUNK) <= (c // CHUNK), _t5_bucket(r - c), -1)
    prev = _t5_bucket(r - c - TK)
    bnear = jnp.stack([diag, prev]).astype(jnp.int32)
    far = _t5_bucket(jnp.full((1,), -(TK + 1), jnp.int32)).astype(jnp.int32)
    rm = jnp.arange(MP, dtype=jnp.int32)[:, None]
    valid = rm >= (MP - N_META)
    kpos = rm - (MP - N_META)
    m0 = jnp.where(valid, _t5_bucket(kpos - (N_META + c)), -1)
    m1 = jnp.where(valid, jnp.broadcast_to(far[0], (MP, TQ)), -1)
    bmeta = jnp.stack([m0, m1]).astype(jnp.int32)
    cm = jnp.arange(MP, dtype=jnp.int32)[None, :]
    bmm = jnp.where(valid, _t5_bucket(rm - cm), -1).astype(jnp.int32)

    smem = pl.BlockSpec(memory_space=pltpu.SMEM)
    return pl.pallas_call(
        _prep_kernel,
        grid=(DA_HEADS,),
        in_specs=[smem, smem,
                  pl.BlockSpec((4, DA_HALF), lambda h: (0, 0)),
                  pl.BlockSpec((2, TK, TQ), lambda h: (0, 0, 0)),
                  pl.BlockSpec((2, MP, TQ), lambda h: (0, 0, 0)),
                  pl.BlockSpec((MP, MP), lambda h: (0, 0))],
        out_specs=[pl.BlockSpec((None, 2, TK, TQ), lambda h: (h, 0, 0, 0)),
                   pl.BlockSpec((None, 2, MP, TQ), lambda h: (h, 0, 0, 0)),
                   pl.BlockSpec((None, MP, MP), lambda h: (h, 0, 0)),
                   pl.BlockSpec((SUBLANES, LANES), lambda h: (0, 0))],
        out_shape=[jax.ShapeDtypeStruct((DA_HEADS, 2, TK, TQ), f32),
                   jax.ShapeDtypeStruct((DA_HEADS, 2, MP, TQ), f32),
                   jax.ShapeDtypeStruct((DA_HEADS, MP, MP), f32),
                   jax.ShapeDtypeStruct((SUBLANES, LANES), f32)],
        name="prep",
    )(far, rel_bias.astype(f32), da_lambda.astype(f32), bnear, bmeta, bmm)


def _sigmoid(x):
    return 1.0 / (1.0 + jnp.exp(-x))


def _proj_kernel(x_ref, halo_ref, g_ref, w_ref, cw_ref, q_ref, k_ref, v_ref, dq_ref,
                 dk_ref, dv_ref, dz_ref, ba_ref, tail_ref, carry_ref, carry_next, u_ref):
    tm = x_ref.shape[0]
    t = pl.program_id(1)

    @pl.when(t == 0)
    def _():
        carry_ref[...] = halo_ref[...]

    @pl.when(t > 0)
    def _():
        carry_ref[...] = carry_next[...]

    x = x_ref[...]
    ms = jnp.mean(x * x, axis=-1, keepdims=True)
    u_ref[...] = (x * lax.rsqrt(ms + EPS) * g_ref[...]).astype(bf16)
    outs = (q_ref, k_ref, v_ref, dq_ref, dk_ref, dv_ref, dz_ref)
    n_out = len(outs)
    project = lambda j: jnp.dot(u_ref[...], w_ref[:, j * 512:(j + 1) * 512],
                                preferred_element_type=f32)

    def conv_silu(j, r):
        cols = slice((j - 3) * DN_WIDTH, (j - 2) * DN_WIDTH)
        carry_next[:, cols] = r[tm - SUBLANES:, :]
        ext = jnp.concatenate([carry_ref[:, cols], r], axis=0)
        y = cw_ref[0:1, cols] * ext
        for d in range(1, DN_CONV):
            y = cw_ref[d:d + 1, cols] * ext + pltpu.roll(y, 1, 0)
        y = y[SUBLANES:, :]
        return y * _sigmoid(y)

    def l2norm_heads(y, scale):
        parts = []
        for h in range(DN_HEADS):
            yh = y[:, h * DN_DK:(h + 1) * DN_DK]
            parts.append(yh * (lax.rsqrt(jnp.sum(yh * yh, axis=-1, keepdims=True) + EPS) * scale))
        return jnp.concatenate(parts, axis=1)

    order = (3, 0, 4, 1, 5, 2, 6)
    ahead = 2
    pending = [project(j) for j in order[:ahead]]
    for pos, j in enumerate(order):
        r = pending.pop(0)
        if pos + ahead < n_out:
            pending.append(project(order[pos + ahead]))
        elif pos + ahead == n_out:
            ba_ref[...] = jnp.dot(u_ref[...], w_ref[:, N_MAIN:N_MAIN + LANES],
                                  preferred_element_type=f32)
        if j == 0:
            r = r * (DA_HALF ** -0.5 * LOG2E)
        elif j in (3, 4):
            r = l2norm_heads(conv_silu(j, r), DN_DK ** -0.5 if j == 3 else 1.0)
        elif j == 5:
            r = conv_silu(j, r)
        if j == 0:
            r = r.T
        outs[j][...] = r.astype(outs[j].dtype)
    tail_ref[...] = carry_next[...]


def _proj(x3, halo, g1, w_all, conv_w, tm):
    nb, rows, _ = x3.shape
    row_spec = lambda w: pl.BlockSpec((None, tm, w), lambda b, t: (b, t, 0))
    outs = [jax.ShapeDtypeStruct((nb, 512, rows), bf16)] + [
        jax.ShapeDtypeStruct((nb, rows, 512), bf16)] * 6 + [
        jax.ShapeDtypeStruct((nb, rows, LANES), f32),
        jax.ShapeDtypeStruct((nb, SUBLANES, GDN_QKV), f32)]
    return pl.pallas_call(
        _proj_kernel,
        grid=(nb, rows // tm),
        in_specs=[row_spec(D_MODEL), _const_spec((SUBLANES, GDN_QKV)), _const_spec((1, D_MODEL)),
                  _const_spec((D_MODEL, N_MAIN + LANES)), _const_spec((DN_CONV, GDN_QKV))],
        out_specs=[pl.BlockSpec((None, 512, tm), lambda b, t: (b, 0, t))]
                  + [row_spec(512)] * 6 + [row_spec(LANES),
                   pl.BlockSpec((None, SUBLANES, GDN_QKV), lambda b, t: (b, 0, 0))],
        out_shape=outs,
        scratch_shapes=([pltpu.VMEM((SUBLANES, GDN_QKV), f32)] * 2
                        + [pltpu.VMEM((tm, D_MODEL), bf16)]),
        compiler_params=pltpu.CompilerParams(
            dimension_semantics=("parallel", "arbitrary"), vmem_limit_bytes=VMEM_LIMIT),
        name="proj",
    )(x3, halo, g1, w_all, conv_w)


def _dot_nt(a, b):
    return lax.dot_general(a, b, (((1,), (1,)), ((), ())), preferred_element_type=f32)


def _dot_tn(a, b):
    return lax.dot_general(a, b, (((0,), (0,)), ((), ())), preferred_element_type=f32)


def _stack_components(qt):
    zero = jnp.zeros((DA_HALF, qt.shape[1]), qt.dtype)
    q0 = jnp.concatenate([qt[:DA_HALF], zero], axis=0)
    q1 = jnp.concatenate([zero, qt[DA_HALF:]], axis=0)
    return jnp.concatenate([q0, q1], axis=1)


def _colmax(s):
    return jnp.max(s, axis=0, keepdims=True)


def _values_t_ext(v):
    vt = v.astype(f32).T.astype(bf16)
    return jnp.concatenate([vt, jnp.ones((V_EXT - DA_HEAD_DIM, v.shape[0]), bf16)], axis=0)


def _softmax_step(stats, s, smax, vt_pend, p_ref, acc_ref):
    m, alpha_pend = stats
    pv = jnp.dot(vt_pend, p_ref[...], preferred_element_type=f32)
    m_new = jnp.maximum(m, smax)
    alpha = jnp.exp2(m - m_new)
    p = jnp.exp2((s - m_new).astype(bf16))
    acc_ref[...] = alpha_pend * acc_ref[...] + pv
    return (m_new, alpha), p


def _attn_finish(stats, vt_pend, p_ref, acc_ref, lam, g, tq):
    m, alpha_pend = stats
    acc = alpha_pend * acc_ref[...] + jnp.dot(vt_pend, p_ref[...], preferred_element_type=f32)
    l = acc[DA_HEAD_DIM:DA_HEAD_DIM + 1, :]
    acc = acc[:DA_HEAD_DIM, :]
    o = acc[:, :tq] / l[:, :tq] - lam * (acc[:, tq:] / l[:, tq:])
    ms = jnp.mean(o * o, axis=0, keepdims=True)
    y = o * lax.rsqrt(ms + EPS) * g * (1.0 - LAMBDA_INIT)
    return y.T


def _attn_init(tq, p_ref, acc_ref):
    p_ref[...] = jnp.zeros(p_ref.shape, bf16)
    acc_ref[...] = jnp.zeros(acc_ref.shape, f32)
    return (jnp.full((1, 2 * tq), -3e38, f32), jnp.ones((1, 2 * tq), f32))


def _both(b):
    return jnp.concatenate([b, b], axis=1)


def _attn_kernel(lam_ref, q_ref, k_ref, v_ref, km_ref, vm_ref, bnear_ref, bmeta_ref, g_ref,
                 o_ref, *scratch):
    i = pl.program_id(1)
    n_far = jnp.maximum(i - 1, 0)
    heads = range(DA_HEADS)
    cols = lambda h: slice(h * DA_HEAD_DIM, (h + 1) * DA_HEAD_DIM)
    qz = [_stack_components(q_ref[cols(h), :]) for h in heads]

    def scores(h, j):
        start = pl.multiple_of(j * TK, TK)
        return jnp.dot(k_ref[pl.ds(start, TK), cols(h)], qz[h], preferred_element_type=f32)

    s_even = scratch[0:DA_HEADS]
    s_odd = scratch[DA_HEADS:2 * DA_HEADS]
    p_bufs = scratch[2 * DA_HEADS:3 * DA_HEADS]
    accs = scratch[3 * DA_HEADS:4 * DA_HEADS]
    vts = scratch[4 * DA_HEADS:5 * DA_HEADS]
    vmt_ref = scratch[5 * DA_HEADS]

    @pl.when(i == 0)
    def _():
        for h in heads:
            for c0 in range(0, k_ref.shape[0], VT_CHUNK):
                vts[h][:, c0:c0 + VT_CHUNK] = _values_t_ext(v_ref[c0:c0 + VT_CHUNK, cols(h)])
            vmt_ref[h] = _values_t_ext(vm_ref[:, cols(h)])

    def values(h, j):
        start = pl.multiple_of(jnp.maximum(j, 0) * TK, TK)
        return vts[h][:, pl.ds(start, TK)]

    def consume(j, bufs, stats, smax):
        new_stats = []
        for h in heads:
            st, p = _softmax_step(stats[h], bufs[h][...], smax[h], values(h, j - 1),
                                  p_bufs[h], accs[h])
            p_bufs[h][...] = p
            new_stats.append(st)
        return tuple(new_stats)

    def produce(j, bufs):
        new_smax = []
        for h in heads:
            s_next = scores(h, j)
            bufs[h][...] = s_next
            new_smax.append(_colmax(s_next))
        return tuple(new_smax)

    stats = tuple(_attn_init(TQ, p_bufs[h], accs[h]) for h in heads)
    smax_even = produce(0, s_even)
    smax_odd = produce(1, s_odd)

    def pair_body(jp, carry):
        stats, smax_even, smax_odd = carry
        j = 2 * jp
        stats = consume(j, s_even, stats, smax_even)
        smax_even = produce(j + 2, s_even)
        stats = consume(j + 1, s_odd, stats, smax_odd)
        smax_odd = produce(j + 3, s_odd)
        return stats, smax_even, smax_odd

    n_pairs = n_far // 2
    stats, smax_even, _ = lax.fori_loop(0, n_pairs, pair_body, (stats, smax_even, smax_odd))

    def single_body(_, stats):
        stats = consume(2 * n_pairs, s_even, stats, smax_even)
        for h in heads:
            s_even[h][...] = s_odd[h][...]
        return stats

    stats = lax.fori_loop(0, n_far - 2 * n_pairs, single_body, stats)
    gate = jnp.where(i >= 1, 0.0, NEG_INF).astype(f32)
    for h in heads:
        p_h, acc_h = p_bufs[h], accs[h]
        s = s_even[h][...] + _both(bnear_ref[h, 1] + gate)
        st, p = _softmax_step(stats[h], s, _colmax(s), values(h, n_far - 1), p_h, acc_h)
        p_h[...] = p
        s = scores(h, i) + _both(bnear_ref[h, 0])
        st, p = _softmax_step(st, s, _colmax(s), values(h, i - 1), p_h, acc_h)
        p_h[...] = p
        s = (jnp.dot(km_ref[:, cols(h)], qz[h], preferred_element_type=f32)
             + _both(bmeta_ref[h, jnp.minimum(i, 1)]))
        st, p = _softmax_step(st, s, _colmax(s), values(h, i), p_h, acc_h)
        p_meta = p_h.at[0:MP, :]
        p_meta[...] = p
        o_ref[:, cols(h)] = _attn_finish(st, vmt_ref[h], p_meta, acc_h, lam_ref[0],
                                         g_ref[...], TQ).astype(o_ref.dtype)


def _attn_meta_kernel(lam_ref, q_ref, km_ref, vm_ref, bmm_ref, g_ref, o_ref, p_buf, acc_ref):
    qz = _stack_components(q_ref[...])
    vmt = _values_t_ext(vm_ref[...])
    stats = _attn_init(MP, p_buf, acc_ref)
    s = jnp.dot(km_ref[...], qz, preferred_element_type=f32) + _both(bmm_ref[...])
    stats, p = _softmax_step(stats, s, _colmax(s), vmt, p_buf, acc_ref)
    p_buf[...] = p
    o_ref[...] = _attn_finish(stats, vmt, p_buf, acc_ref, lam_ref[0], g_ref[...],
                              MP).astype(o_ref.dtype)


def _attention(lam1, q, k, v, km, vm, bnear, bmeta, g_tile):
    nb, s_len, _ = k.shape
    smem = pl.BlockSpec(memory_space=pltpu.SMEM)
    return pl.pallas_call(
        _attn_kernel,
        grid=(nb, s_len // TQ),
        in_specs=[smem,
                  pl.BlockSpec((None, DA_WIDTH, TQ), lambda b, i: (b, 0, i)),
                  pl.BlockSpec((None, s_len, DA_WIDTH), lambda b, i: (b, 0, 0)),
                  pl.BlockSpec((None, s_len, DA_WIDTH), lambda b, i: (b, 0, 0)),
                  _const_spec((MP, DA_WIDTH)), _const_spec((MP, DA_WIDTH)),
                  _const_spec((DA_HEADS, 2, TK, TQ)), _const_spec((DA_HEADS, 2, MP, TQ)),
                  _const_spec((DA_HEAD_DIM, TQ))],
        out_specs=pl.BlockSpec((None, TQ, DA_WIDTH), lambda b, i: (b, i, 0)),
        out_shape=jax.ShapeDtypeStruct((nb, s_len, DA_WIDTH), bf16),
        scratch_shapes=([pltpu.VMEM((TK, 2 * TQ), f32)] * (2 * DA_HEADS)
                        + [pltpu.VMEM((TK, 2 * TQ), bf16)] * DA_HEADS
                        + [pltpu.VMEM((V_EXT, 2 * TQ), f32)] * DA_HEADS
                        + [pltpu.VMEM((V_EXT, s_len), bf16)] * DA_HEADS
                        + [pltpu.VMEM((DA_HEADS, V_EXT, MP), bf16)]),
        compiler_params=pltpu.CompilerParams(
            dimension_semantics=("parallel", "arbitrary"),
            vmem_limit_bytes=VMEM_LIMIT),
        name="attn",
    )(lam1, q, k, v, km, vm, bnear, bmeta, g_tile)


def _attention_meta(lam1, qm, km, vm, bmm, g_tile):
    smem = pl.BlockSpec(memory_space=pltpu.SMEM)
    head = pl.BlockSpec((MP, DA_HEAD_DIM), lambda h: (0, h))
    return pl.pallas_call(
        _attn_meta_kernel,
        grid=(DA_HEADS,),
        in_specs=[smem, pl.BlockSpec((DA_HEAD_DIM, MP), lambda h: (h, 0)), head, head,
                  pl.BlockSpec((None, MP, MP), lambda h: (h, 0, 0)),
                  pl.BlockSpec((DA_HEAD_DIM, MP), lambda h: (0, 0))],
        out_specs=head,
        out_shape=jax.ShapeDtypeStruct((MP, DA_WIDTH), bf16),
        scratch_shapes=[pltpu.VMEM((MP, 2 * MP), bf16), pltpu.VMEM((V_EXT, 2 * MP), f32)],
        name="attn_meta",
    )(lam1, qm, km, vm, bmm, g_tile)


GDN_R = DN_HEADS * CHUNK
GDN_LEVELS = (2, 4, 8, 16, 32)
GDN_QKV = 3 * DN_WIDTH
GDN_NCH = GDN_ROWS // CHUNK
GDN_NSEQ = 4


def _gdn_consts():
    r = np.arange(GDN_R)[:, None]
    c = np.arange(GDN_R)[None, :]
    same = (r // CHUNK) == (c // CHUNK)
    bd = np.stack([same & (r >= c), same & (r > c), same]).astype(np.float32)
    i = np.arange(CHUNK)[:, None]
    j = np.arange(GDN_R)[None, :] % CHUNK
    lv = [i == j, (i > j) & ((i // 2) == (j // 2))]
    for s in GDN_LEVELS:
        lv.append(((i // (2 * s)) == (j // (2 * s))) & ((i & s) != 0) & ((j & s) == 0))
    cat = np.stack(lv).astype(np.float32)
    rr = np.arange(GDN_ROWS)[:, None]
    cc = np.arange(GDN_ROWS)[None, :]
    lcum = (((rr // CHUNK) == (cc // CHUNK)) & (rr >= cc)).astype(np.float32)
    sel = np.zeros((2, LANES, DN_HEADS * LANES), np.float32)
    for h in range(DN_HEADS):
        sel[0, h, h * LANES:(h + 1) * LANES] = 1.0
        sel[1, DN_HEADS + h, h * LANES:(h + 1) * LANES] = 1.0
    return bd, cat, lcum, sel


def _split3(x):
    hi = x.astype(bf16)
    r1 = x - hi.astype(f32)
    mid = r1.astype(bf16)
    lo = (r1 - mid.astype(f32)).astype(bf16)
    return hi, mid, lo


def _gdn_block(n_per_seq, xq, xk, xv, ba, z, s_refs, alog_ref, dtb_ref, ng_ref, bd_ref,
               cat_ref, lcum_ref, sel_ref):
    n_seq = len(s_refs)
    n = n_seq * n_per_seq
    seq_rows = n_per_seq * CHUNK
    heads = range(DN_HEADS)
    chunks = range(n)
    dot = functools.partial(jnp.dot, preferred_element_type=f32)

    def stack(a):
        return jnp.concatenate(
            [a[c * CHUNK:(c + 1) * CHUNK, h * LANES:(h + 1) * LANES]
             for c in chunks for h in heads], axis=0)

    qn_b, kn_b = stack(xq), stack(xk)
    qn, kn, vs = qn_b.astype(f32), kn_b.astype(f32), stack(xv).astype(f32)
    blk = lambda c: slice(c * GDN_R, (c + 1) * GDN_R)

    beta_t = _sigmoid(ba)
    xg = ba + dtb_ref[...]
    softplus = jnp.maximum(xg, 0.0) + jnp.log1p(jnp.exp(-jnp.abs(xg)))
    g_t = -jnp.exp(alog_ref[...]) * softplus
    lcum = lcum_ref[0:seq_rows, 0:seq_rows]
    g_parts = _split3(g_t)
    gcum = jnp.concatenate(
        [sum(dot(lcum, part[q * seq_rows:(q + 1) * seq_rows]) for part in g_parts)
         for q in range(n_seq)], axis=0)
    g_rep = stack(sum(dot(part, sel_ref[1]) for part in _split3(gcum)))
    b_rep = stack(sum(dot(part, sel_ref[0]) for part in _split3(beta_t)[:2]))
    g_end = jnp.concatenate(
        [jnp.broadcast_to(g_rep[(b + 1) * CHUNK - 1:(b + 1) * CHUNK, :], (CHUNK, LANES))
         for b in range(n * DN_HEADS)], axis=0)
    exp_g = jnp.exp(g_rep)
    rhs = jnp.concatenate([vs * b_rep, kn * (b_rep * exp_g)], axis=1).astype(bf16)
    q_g = qn * exp_g
    k_g = (kn * jnp.exp(g_end - g_rep)).astype(bf16)
    g_last = jnp.exp(g_end)

    tri = bd_ref[0] > 0.5
    block_b = bd_ref[2].astype(bf16)
    both = lambda a: jnp.concatenate([a, a], axis=1)
    decay, m_b, x_cat = [], [], []
    for c in chunks:
        g_row = g_rep[blk(c)].T[0:1, :]
        dec = jnp.exp(jnp.where(tri, both(g_rep[blk(c)]) - g_row, -jnp.inf))
        m = bd_ref[1] * both(b_rep[blk(c)]) * _dot_nt(kn_b[blk(c)], kn_b[blk(c)]) * dec
        m_cat = sum(m[h * CHUNK:(h + 1) * CHUNK] for h in heads)
        decay.append(dec)
        m_b.append(m.astype(bf16))
        x_cat.append(cat_ref[0] - m_cat * cat_ref[1])

    def to_bd(x):
        return jnp.concatenate([x.astype(bf16)] * DN_HEADS, axis=0) * block_b

    for lvl in range(len(GDN_LEVELS)):
        ys = [dot(x_cat[c].astype(bf16), m_b[c]) for c in chunks]
        zs = [dot(ys[c].astype(bf16), to_bd(x_cat[c])) for c in chunks]
        x_cat = [x_cat[c] - zs[c] * cat_ref[2 + lvl] for c in chunks]
    sol = [dot(to_bd(x_cat[c]), rhs[blk(c)]).astype(bf16) for c in chunks]
    a_qk = [(_dot_nt(qn_b[blk(c)], kn_b[blk(c)]) * decay[c]).astype(bf16) for c in chunks]
    a_uw = [dot(a_qk[c], sol[c]) for c in chunks]
    hrows = lambda h: slice(h * CHUNK, (h + 1) * CHUNK)
    k_uw = [[_dot_tn(k_g[blk(c)][hrows(h)], sol[c][hrows(h)]) for h in heads] for c in chunks]

    out_rows = [None] * n
    for step in range(n_per_seq):
        for q in range(n_seq):
            c = q * n_per_seq + step
            s_ref = s_refs[q]
            out_cols = []
            for h in heads:
                s_old = s_ref[h]
                s_b = s_old.astype(bf16)
                q_eff = (q_g[blk(c)][hrows(h)] - a_uw[c][hrows(h), DN_DV:]).astype(bf16)
                o = a_uw[c][hrows(h), :DN_DV] + dot(q_eff, s_b)
                s_ref[h] = (s_old * g_last[blk(c)][hrows(h)][0:1, :] + k_uw[c][h][:, :DN_DV]
                            - dot(k_uw[c][h][:, DN_DV:].astype(bf16), s_b))
                o = o * lax.rsqrt(jnp.mean(o * o, axis=-1, keepdims=True) + EPS) * ng_ref[...]
                zh = z[c * CHUNK:(c + 1) * CHUNK, h * DN_DV:(h + 1) * DN_DV].astype(f32)
                out_cols.append(o * (zh * _sigmoid(zh)))
            out_rows[c] = jnp.concatenate(out_cols, axis=1)
    return jnp.concatenate(out_rows, axis=0)


def _gdn_kernel(dq_ref, dk_ref, dv_ref, dz_ref, ba_ref, mq_ref, mk_ref, mv_ref, mz_ref, mba_ref,
                alog_ref, dtb_ref, ng_ref, bd_ref, cat_ref, lcum_ref, sel_ref,
                o_ref, om_ref, s_ref):
    consts = (alog_ref, dtb_ref, ng_ref, bd_ref, cat_ref, lcum_ref, sel_ref)
    s_refs = [s_ref.at[q] for q in range(GDN_NSEQ)]

    @pl.when(pl.program_id(1) == 0)
    def _():
        s_ref[0] = jnp.zeros(s_ref.shape[1:], f32)
        om_ref[...] = _gdn_block(1, mq_ref[...], mk_ref[...], mv_ref[...], mba_ref[...],
                                 mz_ref[...], s_refs[:1], *consts).astype(om_ref.dtype)
        for q in range(1, GDN_NSEQ):
            s_ref[q] = s_ref[0]

    merge = lambda ref: ref[...].reshape(GDN_NSEQ * GDN_ROWS, ref.shape[-1])
    out = _gdn_block(GDN_NCH, merge(dq_ref), merge(dk_ref), merge(dv_ref), merge(ba_ref),
                     merge(dz_ref), s_refs, *consts)
    o_ref[...] = out.reshape(o_ref.shape).astype(o_ref.dtype)


def _gdn(dq, dk, dv, dz, ba, mq, mk, mv, mz, mba, alog_row, dtb_row, ng_row):
    nb, s_len, _ = dq.shape
    assert nb % GDN_NSEQ == 0
    row = lambda w: pl.BlockSpec((GDN_NSEQ, GDN_ROWS, w), lambda b, t: (b, t, 0))
    mrow = lambda w: pl.BlockSpec((CHUNK, w), lambda b, t: (MP // CHUNK - 1, 0))
    bd, cat, lcum, sel = _gdn_consts()
    bd, cat = jnp.asarray(bd), jnp.asarray(cat)
    lcum, sel = jnp.asarray(lcum).astype(bf16), jnp.asarray(sel).astype(bf16)
    return pl.pallas_call(
        _gdn_kernel,
        grid=(nb // GDN_NSEQ, s_len // GDN_ROWS),
        in_specs=[row(DN_WIDTH)] * 4 + [row(LANES)] + [mrow(DN_WIDTH)] * 4 + [mrow(LANES)] + [
            _const_spec((1, LANES)), _const_spec((1, LANES)),
            _const_spec((1, DN_DV)), _const_spec(bd.shape), _const_spec(cat.shape),
            _const_spec(lcum.shape), _const_spec(sel.shape)],
        out_specs=[row(DN_WIDTH), pl.BlockSpec((None, CHUNK, DN_WIDTH), lambda b, t: (b, 0, 0))],
        out_shape=[jax.ShapeDtypeStruct((nb, s_len, DN_WIDTH), bf16),
                   jax.ShapeDtypeStruct((nb // GDN_NSEQ, CHUNK, DN_WIDTH), bf16)],
        scratch_shapes=[pltpu.VMEM((GDN_NSEQ, DN_HEADS, DN_DK, DN_DV), f32)],
        compiler_params=pltpu.CompilerParams(
            dimension_semantics=("parallel", "arbitrary"), vmem_limit_bytes=VMEM_LIMIT),
        name="gdn",
    )(dq, dk, dv, dz, ba, mq, mk, mv, mz, mba, alog_row, dtb_row, ng_row, bd, cat, lcum, sel)


def _mix_and_norm(x, oda, odn, wout_ref, g2_ref):
    mix = jnp.concatenate([oda, odn], axis=1)
    h2 = x + jnp.dot(mix, wout_ref[...], preferred_element_type=f32)
    u2 = h2 * lax.rsqrt(jnp.mean(h2 * h2, axis=-1, keepdims=True) + EPS) * g2_ref[...]
    return h2, u2.astype(bf16)


def _ffn_halo_kernel(x_ref, oda_ref, odn_ref, wout_ref, g2_ref, wup_ref, halo_ref):
    _, u2 = _mix_and_norm(x_ref[...], oda_ref[...], odn_ref[...], wout_ref, g2_ref)
    halo_ref[...] = jnp.dot(u2, wup_ref[...], preferred_element_type=f32)


def _ffn_kernel(x_ref, oda_ref, odn_ref, halo_ref, wout_ref, g2_ref, wup_ref, cw_ref, cb_ref,
                wdown_ref, gf_ref, o_ref, *scratch):
    tm = FFN_TM
    carries = scratch[0:FFN_TILES + 1]
    accs = scratch[FFN_TILES + 1:2 * FFN_TILES + 1]
    hbufs = scratch[2 * FFN_TILES + 1:]
    t = pl.program_id(1)

    @pl.when(t == 0)
    def _():
        carries[0][...] = halo_ref[...]

    @pl.when(t > 0)
    def _():
        carries[0][...] = carries[FFN_TILES][...]

    n_chunks = D_FF // FFN_CW
    stages = [(a, c) for a in range(FFN_TILES) for c in range(n_chunks)]
    col_pair = lambda c: (slice(c * FFN_CW, (c + 1) * FFN_CW),
                          slice(D_FF + c * FFN_CW, D_FF + (c + 1) * FFN_CW))
    buf_pair = lambda s: hbufs[2 * (s % FFN_NBUF):2 * (s % FFN_NBUF) + 2]
    rows = lambda a: slice(a * tm, (a + 1) * tm)
    normed = {}

    def up_part(s):
        a, c = stages[s]
        if c == 0:
            normed[a] = _mix_and_norm(x_ref[rows(a), :], oda_ref[rows(a), :],
                                      odn_ref[rows(a), :], wout_ref, g2_ref)
        u2 = normed[a][1]
        for cols, hbuf in zip(col_pair(c), buf_pair(s)):
            hbuf[0:SUBLANES, :] = carries[a][:, cols]
            hup = jnp.dot(u2, wup_ref[:, cols], preferred_element_type=f32)
            hbuf[SUBLANES:SUBLANES + tm, :] = hup
            carries[a + 1][:, cols] = hup[tm - SUBLANES:, :]

    def conv_part(cols, hbuf):
        y = cb_ref[:, cols]
        for d in range(FFN_CONV):
            y = y + cw_ref[FFN_CONV - 1 - d:FFN_CONV - d, cols] * hbuf[SUBLANES - d:SUBLANES - d + tm, :]
        return y

    for s in range(FFN_AHEAD):
        up_part(s)
    for s, (a, c) in enumerate(stages):
        if s + FFN_AHEAD < len(stages):
            up_part(s + FFN_AHEAD)
        gate, val = (conv_part(cols, hbuf) for cols, hbuf in zip(col_pair(c), buf_pair(s)))
        act = (gate * _sigmoid(gate) * val).astype(bf16)
        part = jnp.dot(act, wdown_ref[c * FFN_CW:(c + 1) * FFN_CW, :], preferred_element_type=f32)
        if c == 0:
            accs[a][...] = part
        else:
            accs[a][...] += part
        if c == n_chunks - 1:
            y = normed[a][0] + accs[a][...]
            o_ref[rows(a), :] = (y * lax.rsqrt(jnp.mean(y * y, axis=-1, keepdims=True) + EPS)
                                 * gf_ref[...])


def _ffn_halo(xm, odam, odnm, w_out, g2, w_up):
    return pl.pallas_call(
        _ffn_halo_kernel,
        out_shape=jax.ShapeDtypeStruct((HALO_ROWS, 2 * D_FF), f32),
        compiler_params=pltpu.CompilerParams(vmem_limit_bytes=VMEM_LIMIT),
        name="ffn_halo",
    )(xm, odam, odnm, w_out, g2, w_up)


def _ffn(x, oda, odn, halo, w_out, g2, w_up, conv_w, conv_b, w_down, gf):
    nb, s_len, _ = x.shape
    tm = FFN_TM
    step_rows = FFN_TILES * tm
    row = lambda w: pl.BlockSpec((None, step_rows, w), lambda b, t: (b, t, 0))
    return pl.pallas_call(
        _ffn_kernel,
        grid=(nb, s_len // step_rows),
        in_specs=[row(D_MODEL), row(DA_WIDTH), row(DN_WIDTH),
                  _const_spec((SUBLANES, 2 * D_FF)), _const_spec((D_MODEL, D_MODEL)),
                  _const_spec((1, D_MODEL)), _const_spec((D_MODEL, 2 * D_FF)),
                  _const_spec((FFN_CONV, 2 * D_FF)), _const_spec((1, 2 * D_FF)),
                  _const_spec((D_FF, D_MODEL)), _const_spec((1, D_MODEL))],
        out_specs=row(D_MODEL),
        out_shape=jax.ShapeDtypeStruct((nb, s_len, D_MODEL), f32),
        scratch_shapes=([pltpu.VMEM((SUBLANES, 2 * D_FF), f32)] * (FFN_TILES + 1)
                        + [pltpu.VMEM((tm, D_MODEL), f32)] * FFN_TILES
                        + [pltpu.VMEM((SUBLANES + tm, FFN_CW), f32)] * (2 * FFN_NBUF)),
        compiler_params=pltpu.CompilerParams(
            dimension_semantics=("parallel", "arbitrary"), vmem_limit_bytes=VMEM_LIMIT),
        name="ffn",
    )(x, oda, odn, halo, w_out, g2, w_up, conv_w, conv_b, w_down, gf)


def kernel(x, meta_tokens, rel_bias, norm1_g, w_in, da_lambda, da_subln_g, dn_conv_w, dn_A_log,
           dn_dt_bias, dn_norm_g, w_out, norm2_g, w_up, ffn_conv_w, ffn_conv_b, w_down,
           final_norm_g):
    nb, s_len, _ = x.shape
    assert s_len % TQ == 0 and s_len % GDN_ROWS == 0 and s_len % (FFN_TILES * FFN_TM) == 0
    assert s_len % PROJ_TM == 0 and s_len % VT_CHUNK == 0
    w_all = jnp.pad(w_in[0], ((0, 0), (0, LANES - 2 * DN_HEADS))).astype(bf16)
    w_out_b = w_out[0].astype(bf16)
    w_up_b = w_up[0].astype(bf16)
    w_down_b = w_down[0].astype(bf16)
    g1 = norm1_g[0].reshape(1, D_MODEL).astype(f32)
    g2 = norm2_g[0].reshape(1, D_MODEL).astype(f32)
    gf = final_norm_g.reshape(1, D_MODEL).astype(f32)
    meta_pad = jnp.pad(meta_tokens.astype(x.dtype), ((MP - N_META, 0), (0, 0)))

    bnear, bmeta, bmm, lam_tile = _bias_tiles(rel_bias, da_lambda[0])
    lam1 = lam_tile[0, :1]

    conv_w = dn_conv_w[0].astype(f32)
    no_halo = jnp.zeros((SUBLANES, GDN_QKV), f32)
    mq, mk, mv, mdq, mdk, mdv, mdz, mba, mtail = [
        a[0] for a in _proj(meta_pad[None], no_halo, g1, w_all, conv_w, MP)]
    q, k, v, dq, dk, dv, dz, ba, _ = _proj(x, mtail, g1, w_all, conv_w, PROJ_TM)

    subln = da_subln_g[0].astype(f32)
    o_da = _attention(lam1, q, k, v, mk, mv, bnear, bmeta,
                      jnp.broadcast_to(subln[:, None], (DA_HEAD_DIM, TQ)))
    o_da_m = _attention_meta(lam1, mq, mk, mv, bmm,
                             jnp.broadcast_to(subln[:, None], (DA_HEAD_DIM, MP)))

    gate_row = lambda p: jnp.pad(p[0].astype(f32), (DN_HEADS, LANES - 2 * DN_HEADS)).reshape(1, LANES)
    o_dn, o_dn_m = _gdn(dq, dk, dv, dz, ba, mdq, mdk, mdv, mdz, mba,
                        gate_row(dn_A_log), gate_row(dn_dt_bias),
                        dn_norm_g[0].reshape(1, DN_DV).astype(f32))

    halo = _ffn_halo(meta_pad[MP - HALO_ROWS:], o_da_m[MP - HALO_ROWS:],
                     o_dn_m[0, CHUNK - HALO_ROWS:], w_out_b, g2, w_up_b)[HALO_ROWS - SUBLANES:]
    return _ffn(x, o_da, o_dn, halo, w_out_b, g2, w_up_b, ffn_conv_w[0].astype(f32),
                ffn_conv_b[0].reshape(1, 2 * D_FF).astype(f32), w_down_b, gf)
```

```python
import functools
import math

import numpy as np
import jax
import jax.numpy as jnp
from jax import lax
from jax.experimental import pallas as pl
from jax.experimental.pallas import tpu as pltpu

f32 = jnp.float32
bf16 = jnp.bfloat16

D_MODEL = 1024
CHUNK = 64
N_META = 16
EPS = 1e-6
NEG_INF = -1e30
LAMBDA_INIT = 0.8 - 0.6 * math.exp(-0.3 * 0)
LOG2E = math.log2(math.e)

DA_HEADS = 4
DA_HEAD_DIM = 128
DA_HALF = 64
DA_WIDTH = DA_HEADS * DA_HEAD_DIM
DN_HEADS = 4
DN_DK = 128
DN_DV = 128
DN_WIDTH = DN_HEADS * DN_DV
DN_CONV = 4
N_BUCKETS = 32
MAX_DISTANCE = 128
D_FF = 2816
FFN_CONV = 3

N_MAIN = 7 * 512
LANES = 128
SUBLANES = 8
MP = 128
TQ = 256
TK = 256
V_EXT = DA_HEAD_DIM + 16
VT_CHUNK = 512
PROJ_TM = 512
GDN_ROWS = 256
FFN_TILES = 2
FFN_TM = 256
FFN_CW = 256
FFN_AHEAD = 10
FFN_NBUF = FFN_AHEAD + 1
HALO_ROWS = 16
VMEM_LIMIT = 52 * 1024 * 1024


def _const_spec(shape):
    nd = len(shape)
    return pl.BlockSpec(shape, lambda *_: (0,) * nd, pipeline_mode=pl.Buffered(1))


def _prep_kernel(far_ref, table_ref, lam_in_ref, bnear_in, bmeta_in, bmm_in,
                 bnear_out, bmeta_out, bmm_out, lam_out):
    h = pl.program_id(0)
    c_far = table_ref[far_ref[0], h]

    def lookup(bkt):
        out = jnp.full(bkt.shape, NEG_INF, f32)
        for b in range(N_BUCKETS):
            out = jnp.where(bkt == b, (table_ref[b, h] - c_far) * LOG2E, out)
        return out

    for i in range(2):
        bnear_out[i] = lookup(bnear_in[i])
        bmeta_out[i] = lookup(bmeta_in[i])
    bmm_out[...] = lookup(bmm_in[...])
    lv = lam_in_ref[...]
    s1 = jnp.sum(lv[0:1] * lv[1:2], axis=-1, keepdims=True)
    s2 = jnp.sum(lv[2:3] * lv[3:4], axis=-1, keepdims=True)
    lam = jnp.exp(s1) - jnp.exp(s2) + LAMBDA_INIT
    lam_out[...] = jnp.broadcast_to(lam, lam_out.shape)


def _t5_bucket(rel):
    nb = N_BUCKETS // 2
    max_exact = nb // 2
    ret = jnp.where(rel > 0, nb, 0)
    n = jnp.abs(rel)
    nf = jnp.maximum(n, 1).astype(jnp.float32)
    large = max_exact + (jnp.log(nf / max_exact) / math.log(MAX_DISTANCE / max_exact)
                         * (nb - max_exact)).astype(jnp.int32)
    large = jnp.minimum(large, nb - 1)
    return ret + jnp.where(n < max_exact, n, large)


def _bias_tiles(rel_bias, da_lambda):
    r = jnp.arange(TK, dtype=jnp.int32)[:, None]
    c = jnp.arange(TQ, dtype=jnp.int32)[None, :]
    diag = jnp.where((r // CHUNK) <= (c // CHUNK), _t5_bucket(r - c), -1)
    prev = _t5_bucket(r - c - TK)
    bnear = jnp.stack([diag, prev]).astype(jnp.int32)
    far = _t5_bucket(jnp.full((1,), -(TK + 1), jnp.int32)).astype(jnp.int32)
    rm = jnp.arange(MP, dtype=jnp.int32)[:, None]
    valid = rm >= (MP - N_META)
    kpos = rm - (MP - N_META)
    m0 = jnp.where(valid, _t5_bucket(kpos - (N_META + c)), -1)
    m1 = jnp.where(valid, jnp.broadcast_to(far[0], (MP, TQ)), -1)
    bmeta = jnp.stack([m0, m1]).astype(jnp.int32)
    cm = jnp.arange(MP, dtype=jnp.int32)[None, :]
    bmm = jnp.where(valid, _t5_bucket(rm - cm), -1).astype(jnp.int32)

    smem = pl.BlockSpec(memory_space=pltpu.SMEM)
    return pl.pallas_call(
        _prep_kernel,
        grid=(DA_HEADS,),
        in_specs=[smem, smem,
                  pl.BlockSpec((4, DA_HALF), lambda h: (0, 0)),
                  pl.BlockSpec((2, TK, TQ), lambda h: (0, 0, 0)),
                  pl.BlockSpec((2, MP, TQ), lambda h: (0, 0, 0)),
                  pl.BlockSpec((MP, MP), lambda h: (0, 0))],
        out_specs=[pl.BlockSpec((None, 2, TK, TQ), lambda h: (h, 0, 0, 0)),
                   pl.BlockSpec((None, 2, MP, TQ), lambda h: (h, 0, 0, 0)),
                   pl.BlockSpec((None, MP, MP), lambda h: (h, 0, 0)),
                   pl.BlockSpec((SUBLANES, LANES), lambda h: (0, 0))],
        out_shape=[jax.ShapeDtypeStruct((DA_HEADS, 2, TK, TQ), f32),
                   jax.ShapeDtypeStruct((DA_HEADS, 2, MP, TQ), f32),
                   jax.ShapeDtypeStruct((DA_HEADS, MP, MP), f32),
                   jax.ShapeDtypeStruct((SUBLANES, LANES), f32)],
        name="prep",
    )(far, rel_bias.astype(f32), da_lambda.astype(f32), bnear, bmeta, bmm)


def _sigmoid(x):
    return 1.0 / (1.0 + jnp.exp(-x))


def _proj_kernel(x_ref, halo_ref, g_ref, w_ref, cw_ref, q_ref, k_ref, v_ref, dq_ref,
                 dk_ref, dv_ref, dz_ref, ba_ref, tail_ref, carry_ref, carry_next, u_ref):
    tm = x_ref.shape[0]
    t = pl.program_id(1)

    @pl.when(t == 0)
    def _():
        carry_ref[...] = halo_ref[...]

    @pl.when(t > 0)
    def _():
        carry_ref[...] = carry_next[...]

    x = x_ref[...]
    ms = jnp.mean(x * x, axis=-1, keepdims=True)
    u_ref[...] = (x * lax.rsqrt(ms + EPS) * g_ref[...]).astype(bf16)
    outs = (q_ref, k_ref, v_ref, dq_ref, dk_ref, dv_ref, dz_ref)
    n_out = len(outs)
    project = lambda j: jnp.dot(u_ref[...], w_ref[:, j * 512:(j + 1) * 512],
                                preferred_element_type=f32)

    def conv_silu(j, r):
        cols = slice((j - 3) * DN_WIDTH, (j - 2) * DN_WIDTH)
        carry_next[:, cols] = r[tm - SUBLANES:, :]
        ext = jnp.concatenate([carry_ref[:, cols], r], axis=0)
        y = cw_ref[0:1, cols] * ext
        for d in range(1, DN_CONV):
            y = cw_ref[d:d + 1, cols] * ext + pltpu.roll(y, 1, 0)
        y = y[SUBLANES:, :]
        return y * _sigmoid(y)

    def l2norm_heads(y, scale):
        parts = []
        for h in range(DN_HEADS):
            yh = y[:, h * DN_DK:(h + 1) * DN_DK]
            parts.append(yh * (lax.rsqrt(jnp.sum(yh * yh, axis=-1, keepdims=True) + EPS) * scale))
        return jnp.concatenate(parts, axis=1)

    order = (3, 0, 4, 1, 5, 2, 6)
    ahead = 2
    pending = [project(j) for j in order[:ahead]]
    for pos, j in enumerate(order):
        r = pending.pop(0)
        if pos + ahead < n_out:
            pending.append(project(order[pos + ahead]))
        elif pos + ahead == n_out:
            ba_ref[...] = jnp.dot(u_ref[...], w_ref[:, N_MAIN:N_MAIN + LANES],
                                  preferred_element_type=f32)
        if j == 0:
            r = r * (DA_HALF ** -0.5 * LOG2E)
        elif j in (3, 4):
            r = l2norm_heads(conv_silu(j, r), DN_DK ** -0.5 if j == 3 else 1.0)
        elif j == 5:
            r = conv_silu(j, r)
        if j == 0:
            r = r.T
        outs[j][...] = r.astype(outs[j].dtype)
    tail_ref[...] = carry_next[...]


def _proj(x3, halo, g1, w_all, conv_w, tm):
    nb, rows, _ = x3.shape
    row_spec = lambda w: pl.BlockSpec((None, tm, w), lambda b, t: (b, t, 0))
    outs = [jax.ShapeDtypeStruct((nb, 512, rows), bf16)] + [
        jax.ShapeDtypeStruct((nb, rows, 512), bf16)] * 6 + [
        jax.ShapeDtypeStruct((nb, rows, LANES), f32),
        jax.ShapeDtypeStruct((nb, SUBLANES, GDN_QKV), f32)]
    return pl.pallas_call(
        _proj_kernel,
        grid=(nb, rows // tm),
        in_specs=[row_spec(D_MODEL), _const_spec((SUBLANES, GDN_QKV)), _const_spec((1, D_MODEL)),
                  _const_spec((D_MODEL, N_MAIN + LANES)), _const_spec((DN_CONV, GDN_QKV))],
        out_specs=[pl.BlockSpec((None, 512, tm), lambda b, t: (b, 0, t))]
                  + [row_spec(512)] * 6 + [row_spec(LANES),
                   pl.BlockSpec((None, SUBLANES, GDN_QKV), lambda b, t: (b, 0, 0))],
        out_shape=outs,
        scratch_shapes=([pltpu.VMEM((SUBLANES, GDN_QKV), f32)] * 2
                        + [pltpu.VMEM((tm, D_MODEL), bf16)]),
        compiler_params=pltpu.CompilerParams(
            dimension_semantics=("parallel", "arbitrary"), vmem_limit_bytes=VMEM_LIMIT),
        name="proj",
    )(x3, halo, g1, w_all, conv_w)


def _dot_nt(a, b):
    return lax.dot_general(a, b, (((1,), (1,)), ((), ())), preferred_element_type=f32)


def _dot_tn(a, b):
    return lax.dot_general(a, b, (((0,), (0,)), ((), ())), preferred_element_type=f32)


def _stack_components(qt):
    zero = jnp.zeros((DA_HALF, qt.shape[1]), qt.dtype)
    q0 = jnp.concatenate([qt[:DA_HALF], zero], axis=0)
    q1 = jnp.concatenate([zero, qt[DA_HALF:]], axis=0)
    return jnp.concatenate([q0, q1], axis=1)


def _colmax(s):
    return jnp.max(s, axis=0, keepdims=True)


def _values_t_ext(v):
    vt = v.astype(f32).T.astype(bf16)
    return jnp.concatenate([vt, jnp.ones((V_EXT - DA_HEAD_DIM, v.shape[0]), bf16)], axis=0)


def _softmax_step(stats, s, smax, vt_pend, p_ref, acc_ref):
    m, alpha_pend = stats
    pv = jnp.dot(vt_pend, p_ref[...], preferred_element_type=f32)
    m_new = jnp.maximum(m, smax)
    alpha = jnp.exp2(m - m_new)
    p = jnp.exp2((s - m_new).astype(bf16))
    acc_ref[...] = alpha_pend * acc_ref[...] + pv
    return (m_new, alpha), p


def _attn_finish(stats, vt_pend, p_ref, acc_ref, lam, g, tq):
    m, alpha_pend = stats
    acc = alpha_pend * acc_ref[...] + jnp.dot(vt_pend, p_ref[...], preferred_element_type=f32)
    l = acc[DA_HEAD_DIM:DA_HEAD_DIM + 1, :]
    acc = acc[:DA_HEAD_DIM, :]
    o = acc[:, :tq] / l[:, :tq] - lam * (acc[:, tq:] / l[:, tq:])
    ms = jnp.mean(o * o, axis=0, keepdims=True)
    y = o * lax.rsqrt(ms + EPS) * g * (1.0 - LAMBDA_INIT)
    return y.T


def _attn_init(tq, p_ref, acc_ref):
    p_ref[...] = jnp.zeros(p_ref.shape, bf16)
    acc_ref[...] = jnp.zeros(acc_ref.shape, f32)
    return (jnp.full((1, 2 * tq), -3e38, f32), jnp.ones((1, 2 * tq), f32))


def _both(b):
    return jnp.concatenate([b, b], axis=1)


def _attn_kernel(lam_ref, q_ref, k_ref, v_ref, km_ref, vm_ref, bnear_ref, bmeta_ref, g_ref,
                 o_ref, *scratch):
    i = pl.program_id(1)
    n_far = jnp.maximum(i - 1, 0)
    heads = range(DA_HEADS)
    cols = lambda h: slice(h * DA_HEAD_DIM, (h + 1) * DA_HEAD_DIM)
    qz = [_stack_components(q_ref[cols(h), :]) for h in heads]

    def scores(h, j):
        start = pl.multiple_of(j * TK, TK)
        return jnp.dot(k_ref[pl.ds(start, TK), cols(h)], qz[h], preferred_element_type=f32)

    s_even = scratch[0:DA_HEADS]
    s_odd = scratch[DA_HEADS:2 * DA_HEADS]
    p_bufs = scratch[2 * DA_HEADS:3 * DA_HEADS]
    accs = scratch[3 * DA_HEADS:4 * DA_HEADS]
    vts = scratch[4 * DA_HEADS:5 * DA_HEADS]
    vmt_ref = scratch[5 * DA_HEADS]

    @pl.when(i == 0)
    def _():
        for h in heads:
            for c0 in range(0, k_ref.shape[0], VT_CHUNK):
                vts[h][:, c0:c0 + VT_CHUNK] = _values_t_ext(v_ref[c0:c0 + VT_CHUNK, cols(h)])
            vmt_ref[h] = _values_t_ext(vm_ref[:, cols(h)])

    def values(h, j):
        start = pl.multiple_of(jnp.maximum(j, 0) * TK, TK)
        return vts[h][:, pl.ds(start, TK)]

    def consume(j, bufs, stats, smax):
        new_stats = []
        for h in heads:
            st, p = _softmax_step(stats[h], bufs[h][...], smax[h], values(h, j - 1),
                                  p_bufs[h], accs[h])
            p_bufs[h][...] = p
            new_stats.append(st)
        return tuple(new_stats)

    def produce(j, bufs):
        new_smax = []
        for h in heads:
            s_next = scores(h, j)
            bufs[h][...] = s_next
            new_smax.append(_colmax(s_next))
        return tuple(new_smax)

    stats = tuple(_attn_init(TQ, p_bufs[h], accs[h]) for h in heads)
    smax_even = produce(0, s_even)
    smax_odd = produce(1, s_odd)

    def pair_body(jp, carry):
        stats, smax_even, smax_odd = carry
        j = 2 * jp
        stats = consume(j, s_even, stats, smax_even)
        smax_even = produce(j + 2, s_even)
        stats = consume(j + 1, s_odd, stats, smax_odd)
        smax_odd = produce(j + 3, s_odd)
        return stats, smax_even, smax_odd

    n_pairs = n_far // 2
    stats, smax_even, _ = lax.fori_loop(0, n_pairs, pair_body, (stats, smax_even, smax_odd))

    def single_body(_, stats):
        stats = consume(2 * n_pairs, s_even, stats, smax_even)
        for h in heads:
            s_even[h][...] = s_odd[h][...]
        return stats

    stats = lax.fori_loop(0, n_far - 2 * n_pairs, single_body, stats)
    gate = jnp.where(i >= 1, 0.0, NEG_INF).astype(f32)
    for h in heads:
        p_h, acc_h = p_bufs[h], accs[h]
        s = s_even[h][...] + _both(bnear_ref[h, 1] + gate)
        st, p = _softmax_step(stats[h], s, _colmax(s), values(h, n_far - 1), p_h, acc_h)
        p_h[...] = p
        s = scores(h, i) + _both(bnear_ref[h, 0])
        st, p = _softmax_step(st, s, _colmax(s), values(h, i - 1), p_h, acc_h)
        p_h[...] = p
        s = (jnp.dot(km_ref[:, cols(h)], qz[h], preferred_element_type=f32)
             + _both(bmeta_ref[h, jnp.minimum(i, 1)]))
        st, p = _softmax_step(st, s, _colmax(s), values(h, i), p_h, acc_h)
        p_meta = p_h.at[0:MP, :]
        p_meta[...] = p
        o_ref[:, cols(h)] = _attn_finish(st, vmt_ref[h], p_meta, acc_h, lam_ref[0],
                                         g_ref[...], TQ).astype(o_ref.dtype)


def _attn_meta_kernel(lam_ref, q_ref, km_ref, vm_ref, bmm_ref, g_ref, o_ref, p_buf, acc_ref):
    qz = _stack_components(q_ref[...])
    vmt = _values_t_ext(vm_ref[...])
    stats = _attn_init(MP, p_buf, acc_ref)
    s = jnp.dot(km_ref[...], qz, preferred_element_type=f32) + _both(bmm_ref[...])
    stats, p = _softmax_step(stats, s, _colmax(s), vmt, p_buf, acc_ref)
    p_buf[...] = p
    o_ref[...] = _attn_finish(stats, vmt, p_buf, acc_ref, lam_ref[0], g_ref[...],
                              MP).astype(o_ref.dtype)


def _attention(lam1, q, k, v, km, vm, bnear, bmeta, g_tile):
    nb, s_len, _ = k.shape
    smem = pl.BlockSpec(memory_space=pltpu.SMEM)
    return pl.pallas_call(
        _attn_kernel,
        grid=(nb, s_len // TQ),
        in_specs=[smem,
                  pl.BlockSpec((None, DA_WIDTH, TQ), lambda b, i: (b, 0, i)),
                  pl.BlockSpec((None, s_len, DA_WIDTH), lambda b, i: (b, 0, 0)),
                  pl.BlockSpec((None, s_len, DA_WIDTH), lambda b, i: (b, 0, 0)),
                  _const_spec((MP, DA_WIDTH)), _const_spec((MP, DA_WIDTH)),
                  _const_spec((DA_HEADS, 2, TK, TQ)), _const_spec((DA_HEADS, 2, MP, TQ)),
                  _const_spec((DA_HEAD_DIM, TQ))],
        out_specs=pl.BlockSpec((None, TQ, DA_WIDTH), lambda b, i: (b, i, 0)),
        out_shape=jax.ShapeDtypeStruct((nb, s_len, DA_WIDTH), bf16),
        scratch_shapes=([pltpu.VMEM((TK, 2 * TQ), f32)] * (2 * DA_HEADS)
                        + [pltpu.VMEM((TK, 2 * TQ), bf16)] * DA_HEADS
                        + [pltpu.VMEM((V_EXT, 2 * TQ), f32)] * DA_HEADS
                        + [pltpu.VMEM((V_EXT, s_len), bf16)] * DA_HEADS
                        + [pltpu.VMEM((DA_HEADS, V_EXT, MP), bf16)]),
        compiler_params=pltpu.CompilerParams(
            dimension_semantics=("parallel", "arbitrary"),
            vmem_limit_bytes=VMEM_LIMIT),
        name="attn",
    )(lam1, q, k, v, km, vm, bnear, bmeta, g_tile)


def _attention_meta(lam1, qm, km, vm, bmm, g_tile):
    smem = pl.BlockSpec(memory_space=pltpu.SMEM)
    head = pl.BlockSpec((MP, DA_HEAD_DIM), lambda h: (0, h))
    return pl.pallas_call(
        _attn_meta_kernel,
        grid=(DA_HEADS,),
        in_specs=[smem, pl.BlockSpec((DA_HEAD_DIM, MP), lambda h: (h, 0)), head, head,
                  pl.BlockSpec((None, MP, MP), lambda h: (h, 0, 0)),
                  pl.BlockSpec((DA_HEAD_DIM, MP), lambda h: (0, 0))],
        out_specs=head,
        out_shape=jax.ShapeDtypeStruct((MP, DA_WIDTH), bf16),
        scratch_shapes=[pltpu.VMEM((MP, 2 * MP), bf16), pltpu.VMEM((V_EXT, 2 * MP), f32)],
        name="attn_meta",
    )(lam1, qm, km, vm, bmm, g_tile)


GDN_R = DN_HEADS * CHUNK
GDN_LEVELS = (2, 4, 8, 16, 32)
GDN_QKV = 3 * DN_WIDTH
GDN_NCH = GDN_ROWS // CHUNK
GDN_NSEQ = 4


def _gdn_consts():
    r = np.arange(GDN_R)[:, None]
    c = np.arange(GDN_R)[None, :]
    same = (r // CHUNK) == (c // CHUNK)
    bd = np.stack([same & (r >= c), same & (r > c), same]).astype(np.float32)
    i = np.arange(CHUNK)[:, None]
    j = np.arange(GDN_R)[None, :] % CHUNK
    lv = [i == j, (i > j) & ((i // 2) == (j // 2))]
    for s in GDN_LEVELS:
        lv.append(((i // (2 * s)) == (j // (2 * s))) & ((i & s) != 0) & ((j & s) == 0))
    cat = np.stack(lv).astype(np.float32)
    rr = np.arange(GDN_ROWS)[:, None]
    cc = np.arange(GDN_ROWS)[None, :]
    lcum = (((rr // CHUNK) == (cc // CHUNK)) & (rr >= cc)).astype(np.float32)
    sel = np.zeros((2, LANES, DN_HEADS * LANES), np.float32)
    for h in range(DN_HEADS):
        sel[0, h, h * LANES:(h + 1) * LANES] = 1.0
        sel[1, DN_HEADS + h, h * LANES:(h + 1) * LANES] = 1.0
    return bd, cat, lcum, sel


def _split3(x):
    hi = x.astype(bf16)
    r1 = x - hi.astype(f32)
    mid = r1.astype(bf16)
    lo = (r1 - mid.astype(f32)).astype(bf16)
    return hi, mid, lo


def _gdn_block(n_per_seq, xq, xk, xv, ba, z, s_refs, alog_ref, dtb_ref, ng_ref, bd_ref,
               cat_ref, lcum_ref, sel_ref):
    n_seq = len(s_refs)
    n = n_seq * n_per_seq
    seq_rows = n_per_seq * CHUNK
    heads = range(DN_HEADS)
    chunks = range(n)
    dot = functools.partial(jnp.dot, preferred_element_type=f32)

    def stack(a):
        return jnp.concatenate(
            [a[c * CHUNK:(c + 1) * CHUNK, h * LANES:(h + 1) * LANES]
             for c in chunks for h in heads], axis=0)

    qn_b, kn_b = stack(xq), stack(xk)
    qn, kn, vs = qn_b.astype(f32), kn_b.astype(f32), stack(xv).astype(f32)
    blk = lambda c: slice(c * GDN_R, (c + 1) * GDN_R)

    beta_t = _sigmoid(ba)
    xg = ba + dtb_ref[...]
    softplus = jnp.maximum(xg, 0.0) + jnp.log1p(jnp.exp(-jnp.abs(xg)))
    g_t = -jnp.exp(alog_ref[...]) * softplus
    lcum = lcum_ref[0:seq_rows, 0:seq_rows]
    g_parts = _split3(g_t)
    gcum = jnp.concatenate(
        [sum(dot(lcum, part[q * seq_rows:(q + 1) * seq_rows]) for part in g_parts)
         for q in range(n_seq)], axis=0)
    g_rep = stack(sum(dot(part, sel_ref[1]) for part in _split3(gcum)))
    b_rep = stack(sum(dot(part, sel_ref[0]) for part in _split3(beta_t)[:2]))
    g_end = jnp.concatenate(
        [jnp.broadcast_to(g_rep[(b + 1) * CHUNK - 1:(b + 1) * CHUNK, :], (CHUNK, LANES))
         for b in range(n * DN_HEADS)], axis=0)
    exp_g = jnp.exp(g_rep)
    rhs = jnp.concatenate([vs * b_rep, kn * (b_rep * exp_g)], axis=1).astype(bf16)
    q_g = qn * exp_g
    k_g = (kn * jnp.exp(g_end - g_rep)).astype(bf16)
    g_last = jnp.exp(g_end)

    tri = bd_ref[0] > 0.5
    block_b = bd_ref[2].astype(bf16)
    both = lambda a: jnp.concatenate([a, a], axis=1)
    decay, m_b, x_cat = [], [], []
    for c in chunks:
        g_row = g_rep[blk(c)].T[0:1, :]
        dec = jnp.exp(jnp.where(tri, both(g_rep[blk(c)]) - g_row, -jnp.inf))
        m = bd_ref[1] * both(b_rep[blk(c)]) * _dot_nt(kn_b[blk(c)], kn_b[blk(c)]) * dec
        m_cat = sum(m[h * CHUNK:(h + 1) * CHUNK] for h in heads)
        decay.append(dec)
        m_b.append(m.astype(bf16))
        x_cat.append(cat_ref[0] - m_cat * cat_ref[1])

    def to_bd(x):
        return jnp.concatenate([x.astype(bf16)] * DN_HEADS, axis=0) * block_b

    for lvl in range(len(GDN_LEVELS)):
        ys = [dot(x_cat[c].astype(bf16), m_b[c]) for c in chunks]
        zs = [dot(ys[c].astype(bf16), to_bd(x_cat[c])) for c in chunks]
        x_cat = [x_cat[c] - zs[c] * cat_ref[2 + lvl] for c in chunks]
    sol = [dot(to_bd(x_cat[c]), rhs[blk(c)]).astype(bf16) for c in chunks]
    a_qk = [(_dot_nt(qn_b[blk(c)], kn_b[blk(c)]) * decay[c]).astype(bf16) for c in chunks]
    a_uw = [dot(a_qk[c], sol[c]) for c in chunks]
    hrows = lambda h: slice(h * CHUNK, (h + 1) * CHUNK)
    k_uw = [[_dot_tn(k_g[blk(c)][hrows(h)], sol[c][hrows(h)]) for h in heads] for c in chunks]

    out_rows = [None] * n
    for step in range(n_per_seq):
        for q in range(n_seq):
            c = q * n_per_seq + step
            s_ref = s_refs[q]
            out_cols = []
            for h in heads:
                s_old = s_ref[h]
                s_b = s_old.astype(bf16)
                q_eff = (q_g[blk(c)][hrows(h)] - a_uw[c][hrows(h), DN_DV:]).astype(bf16)
                o = a_uw[c][hrows(h), :DN_DV] + dot(q_eff, s_b)
                s_ref[h] = (s_old * g_last[blk(c)][hrows(h)][0:1, :] + k_uw[c][h][:, :DN_DV]
                            - dot(k_uw[c][h][:, DN_DV:].astype(bf16), s_b))
                o = o * lax.rsqrt(jnp.mean(o * o, axis=-1, keepdims=True) + EPS) * ng_ref[...]
                zh = z[c * CHUNK:(c + 1) * CHUNK, h * DN_DV:(h + 1) * DN_DV].astype(f32)
                out_cols.append(o * (zh * _sigmoid(zh)))
            out_rows[c] = jnp.concatenate(out_cols, axis=1)
    return jnp.concatenate(out_rows, axis=0)


def _gdn_kernel(dq_ref, dk_ref, dv_ref, dz_ref, ba_ref, mq_ref, mk_ref, mv_ref, mz_ref, mba_ref,
                alog_ref, dtb_ref, ng_ref, bd_ref, cat_ref, lcum_ref, sel_ref,
                o_ref, om_ref, s_ref):
    consts = (alog_ref, dtb_ref, ng_ref, bd_ref, cat_ref, lcum_ref, sel_ref)
    s_refs = [s_ref.at[q] for q in range(GDN_NSEQ)]

    @pl.when(pl.program_id(1) == 0)
    def _():
        s_ref[0] = jnp.zeros(s_ref.shape[1:], f32)
        om_ref[...] = _gdn_block(1, mq_ref[...], mk_ref[...], mv_ref[...], mba_ref[...],
                                 mz_ref[...], s_refs[:1], *consts).astype(om_ref.dtype)
        for q in range(1, GDN_NSEQ):
            s_ref[q] = s_ref[0]

    merge = lambda ref: ref[...].reshape(GDN_NSEQ * GDN_ROWS, ref.shape[-1])
    out = _gdn_block(GDN_NCH, merge(dq_ref), merge(dk_ref), merge(dv_ref), merge(ba_ref),
                     merge(dz_ref), s_refs, *consts)
    o_ref[...] = out.reshape(o_ref.shape).astype(o_ref.dtype)


def _gdn(dq, dk, dv, dz, ba, mq, mk, mv, mz, mba, alog_row, dtb_row, ng_row):
    nb, s_len, _ = dq.shape
    assert nb % GDN_NSEQ == 0
    row = lambda w: pl.BlockSpec((GDN_NSEQ, GDN_ROWS, w), lambda b, t: (b, t, 0))
    mrow = lambda w: pl.BlockSpec((CHUNK, w), lambda b, t: (MP // CHUNK - 1, 0))
    bd, cat, lcum, sel = _gdn_consts()
    bd, cat = jnp.asarray(bd), jnp.asarray(cat)
    lcum, sel = jnp.asarray(lcum).astype(bf16), jnp.asarray(sel).astype(bf16)
    return pl.pallas_call(
        _gdn_kernel,
        grid=(nb // GDN_NSEQ, s_len // GDN_ROWS),
        in_specs=[row(DN_WIDTH)] * 4 + [row(LANES)] + [mrow(DN_WIDTH)] * 4 + [mrow(LANES)] + [
            _const_spec((1, LANES)), _const_spec((1, LANES)),
            _const_spec((1, DN_DV)), _const_spec(bd.shape), _const_spec(cat.shape),
            _const_spec(lcum.shape), _const_spec(sel.shape)],
        out_specs=[row(DN_WIDTH), pl.BlockSpec((None, CHUNK, DN_WIDTH), lambda b, t: (b, 0, 0))],
        out_shape=[jax.ShapeDtypeStruct((nb, s_len, DN_WIDTH), bf16),
                   jax.ShapeDtypeStruct((nb // GDN_NSEQ, CHUNK, DN_WIDTH), bf16)],
        scratch_shapes=[pltpu.VMEM((GDN_NSEQ, DN_HEADS, DN_DK, DN_DV), f32)],
        compiler_params=pltpu.CompilerParams(
            dimension_semantics=("parallel", "arbitrary"), vmem_limit_bytes=VMEM_LIMIT),
        name="gdn",
    )(dq, dk, dv, dz, ba, mq, mk, mv, mz, mba, alog_row, dtb_row, ng_row, bd, cat, lcum, sel)


def _mix_and_norm(x, oda, odn, wout_ref, g2_ref):
    mix = jnp.concatenate([oda, odn], axis=1)
    h2 = x + jnp.dot(mix, wout_ref[...], preferred_element_type=f32)
    u2 = h2 * lax.rsqrt(jnp.mean(h2 * h2, axis=-1, keepdims=True) + EPS) * g2_ref[...]
    return h2, u2.astype(bf16)


def _ffn_halo_kernel(x_ref, oda_ref, odn_ref, wout_ref, g2_ref, wup_ref, halo_ref):
    _, u2 = _mix_and_norm(x_ref[...], oda_ref[...], odn_ref[...], wout_ref, g2_ref)
    halo_ref[...] = jnp.dot(u2, wup_ref[...], preferred_element_type=f32)


def _ffn_kernel(x_ref, oda_ref, odn_ref, halo_ref, wout_ref, g2_ref, wup_ref, cw_ref, cb_ref,
                wdown_ref, gf_ref, o_ref, *scratch):
    tm = FFN_TM
    carries = scratch[0:FFN_TILES + 1]
    accs = scratch[FFN_TILES + 1:2 * FFN_TILES + 1]
    hbufs = scratch[2 * FFN_TILES + 1:]
    t = pl.program_id(1)

    @pl.when(t == 0)
    def _():
        carries[0][...] = halo_ref[...]

    @pl.when(t > 0)
    def _():
        carries[0][...] = carries[FFN_TILES][...]

    n_chunks = D_FF // FFN_CW
    stages = [(a, c) for a in range(FFN_TILES) for c in range(n_chunks)]
    col_pair = lambda c: (slice(c * FFN_CW, (c + 1) * FFN_CW),
                          slice(D_FF + c * FFN_CW, D_FF + (c + 1) * FFN_CW))
    buf_pair = lambda s: hbufs[2 * (s % FFN_NBUF):2 * (s % FFN_NBUF) + 2]
    rows = lambda a: slice(a * tm, (a + 1) * tm)
    normed = {}

    def up_part(s):
        a, c = stages[s]
        if c == 0:
            normed[a] = _mix_and_norm(x_ref[rows(a), :], oda_ref[rows(a), :],
                                      odn_ref[rows(a), :], wout_ref, g2_ref)
        u2 = normed[a][1]
        for cols, hbuf in zip(col_pair(c), buf_pair(s)):
            hbuf[0:SUBLANES, :] = carries[a][:, cols]
            hup = jnp.dot(u2, wup_ref[:, cols], preferred_element_type=f32)
            hbuf[SUBLANES:SUBLANES + tm, :] = hup
            carries[a + 1][:, cols] = hup[tm - SUBLANES:, :]

    def conv_part(cols, hbuf):
        y = cb_ref[:, cols]
        for d in range(FFN_CONV):
            y = y + cw_ref[FFN_CONV - 1 - d:FFN_CONV - d, cols] * hbuf[SUBLANES - d:SUBLANES - d + tm, :]
        return y

    for s in range(FFN_AHEAD):
        up_part(s)
    for s, (a, c) in enumerate(stages):
        if s + FFN_AHEAD < len(stages):
            up_part(s + FFN_AHEAD)
        gate, val = (conv_part(cols, hbuf) for cols, hbuf in zip(col_pair(c), buf_pair(s)))
        act = (gate * _sigmoid(gate) * val).astype(bf16)
        part = jnp.dot(act, wdown_ref[c * FFN_CW:(c + 1) * FFN_CW, :], preferred_element_type=f32)
        if c == 0:
            accs[a][...] = part
        else:
            accs[a][...] += part
        if c == n_chunks - 1:
            y = normed[a][0] + accs[a][...]
            o_ref[rows(a), :] = (y * lax.rsqrt(jnp.mean(y * y, axis=-1, keepdims=True) + EPS)
                                 * gf_ref[...])


def _ffn_halo(xm, odam, odnm, w_out, g2, w_up):
    return pl.pallas_call(
        _ffn_halo_kernel,
        out_shape=jax.ShapeDtypeStruct((HALO_ROWS, 2 * D_FF), f32),
        compiler_params=pltpu.CompilerParams(vmem_limit_bytes=VMEM_LIMIT),
        name="ffn_halo",
    )(xm, odam, odnm, w_out, g2, w_up)


def _ffn(x, oda, odn, halo, w_out, g2, w_up, conv_w, conv_b, w_down, gf):
    nb, s_len, _ = x.shape
    tm = FFN_TM
    step_rows = FFN_TILES * tm
    row = lambda w: pl.BlockSpec((None, step_rows, w), lambda b, t: (b, t, 0))
    return pl.pallas_call(
        _ffn_kernel,
        grid=(nb, s_len // step_rows),
        in_specs=[row(D_MODEL), row(DA_WIDTH), row(DN_WIDTH),
                  _const_spec((SUBLANES, 2 * D_FF)), _const_spec((D_MODEL, D_MODEL)),
                  _const_spec((1, D_MODEL)), _const_spec((D_MODEL, 2 * D_FF)),
                  _const_spec((FFN_CONV, 2 * D_FF)), _const_spec((1, 2 * D_FF)),
                  _const_spec((D_FF, D_MODEL)), _const_spec((1, D_MODEL))],
        out_specs=row(D_MODEL),
        out_shape=jax.ShapeDtypeStruct((nb, s_len, D_MODEL), f32),
        scratch_shapes=([pltpu.VMEM((SUBLANES, 2 * D_FF), f32)] * (FFN_TILES + 1)
                        + [pltpu.VMEM((tm, D_MODEL), f32)] * FFN_TILES
                        + [pltpu.VMEM((SUBLANES + tm, FFN_CW), f32)] * (2 * FFN_NBUF)),
        compiler_params=pltpu.CompilerParams(
            dimension_semantics=("parallel", "arbitrary"), vmem_limit_bytes=VMEM_LIMIT),
        name="ffn",
    )(x, oda, odn, halo, w_out, g2, w_up, conv_w, conv_b, w_down, gf)


def kernel(x, meta_tokens, rel_bias, norm1_g, w_in, da_lambda, da_subln_g, dn_conv_w, dn_A_log,
           dn_dt_bias, dn_norm_g, w_out, norm2_g, w_up, ffn_conv_w, ffn_conv_b, w_down,
           final_norm_g):
    nb, s_len, _ = x.shape
    assert s_len % TQ == 0 and s_len % GDN_ROWS == 0 and s_len % (FFN_TILES * FFN_TM) == 0
    assert s_len % PROJ_TM == 0 and s_len % VT_CHUNK == 0
    w_all = jnp.pad(w_in[0], ((0, 0), (0, LANES - 2 * DN_HEADS))).astype(bf16)
    w_out_b = w_out[0].astype(bf16)
    w_up_b = w_up[0].astype(bf16)
    w_down_b = w_down[0].astype(bf16)
    g1 = norm1_g[0].reshape(1, D_MODEL).astype(f32)
    g2 = norm2_g[0].reshape(1, D_MODEL).astype(f32)
    gf = final_norm_g.reshape(1, D_MODEL).astype(f32)
    meta_pad = jnp.pad(meta_tokens.astype(x.dtype), ((MP - N_META, 0), (0, 0)))

    bnear, bmeta, bmm, lam_tile = _bias_tiles(rel_bias, da_lambda[0])
    lam1 = lam_tile[0, :1]

    conv_w = dn_conv_w[0].astype(f32)
    no_halo = jnp.zeros((SUBLANES, GDN_QKV), f32)
    mq, mk, mv, mdq, mdk, mdv, mdz, mba, mtail = [
        a[0] for a in _proj(meta_pad[None], no_halo, g1, w_all, conv_w, MP)]
    q, k, v, dq, dk, dv, dz, ba, _ = _proj(x, mtail, g1, w_all, conv_w, PROJ_TM)

    subln = da_subln_g[0].astype(f32)
    o_da = _attention(lam1, q, k, v, mk, mv, bnear, bmeta,
                      jnp.broadcast_to(subln[:, None], (DA_HEAD_DIM, TQ)))
    o_da_m = _attention_meta(lam1, mq, mk, mv, bmm,
                             jnp.broadcast_to(subln[:, None], (DA_HEAD_DIM, MP)))

    gate_row = lambda p: jnp.pad(p[0].astype(f32), (DN_HEADS, LANES - 2 * DN_HEADS)).reshape(1, LANES)
    o_dn, o_dn_m = _gdn(dq, dk, dv, dz, ba, mdq, mdk, mdv, mdz, mba,
                        gate_row(dn_A_log), gate_row(dn_dt_bias),
                        dn_norm_g[0].reshape(1, DN_DV).astype(f32))

    halo = _ffn_halo(meta_pad[MP - HALO_ROWS:], o_da_m[MP - HALO_ROWS:],
                     o_dn_m[0, CHUNK - HALO_ROWS:], w_out_b, g2, w_up_b)[HALO_ROWS - SUBLANES:]
    return _ffn(x, o_da, o_dn, halo, w_out_b, g2, w_up_b, ffn_conv_w[0].astype(f32),
                ffn_conv_b[0].reshape(1, 2 * D_FF).astype(f32), w_down_b, gf)
```

```python
import functools
import math

import numpy as np
import jax
import jax.numpy as jnp
from jax import lax
from jax.experimental import pallas as pl
from jax.experimental.pallas import tpu as pltpu

f32 = jnp.float32
bf16 = jnp.bfloat16

D_MODEL = 1024
CHUNK = 64
N_META = 16
EPS = 1e-6
NEG_INF = -1e30
LAMBDA_INIT = 0.8 - 0.6 * math.exp(-0.3 * 0)
LOG2E = math.log2(math.e)

DA_HEADS = 4
DA_HEAD_DIM = 128
DA_HALF = 64
DA_WIDTH = DA_HEADS * DA_HEAD_DIM
DN_HEADS = 4
DN_DK = 128
DN_DV = 128
DN_WIDTH = DN_HEADS * DN_DV
DN_CONV = 4
N_BUCKETS = 32
MAX_DISTANCE = 128
D_FF = 2816
FFN_CONV = 3

N_MAIN = 7 * 512
LANES = 128
SUBLANES = 8
MP = 128
TQ = 256
TK = 256
V_EXT = DA_HEAD_DIM + 16
VT_CHUNK = 512
PROJ_TM = 512
GDN_ROWS = 256
FFN_TILES = 1
FFN_TM = 256
FFN_CW = 256
FFN_AHEAD = 10
FFN_NBUF = FFN_AHEAD + 1
HALO_ROWS = 16
VMEM_LIMIT = 52 * 1024 * 1024


def _const_spec(shape):
    nd = len(shape)
    return pl.BlockSpec(shape, lambda *_: (0,) * nd, pipeline_mode=pl.Buffered(1))


def _prep_kernel(far_ref, table_ref, lam_in_ref, bnear_in, bmeta_in, bmm_in,
                 bnear_out, bmeta_out, bmm_out, lam_out):
    h = pl.program_id(0)
    c_far = table_ref[far_ref[0], h]

    def lookup(bkt):
        out = jnp.full(bkt.shape, NEG_INF, f32)
        for b in range(N_BUCKETS):
            out = jnp.where(bkt == b, (table_ref[b, h] - c_far) * LOG2E, out)
        return out

    for i in range(bnear_in.shape[0]):
        bnear_out[i] = lookup(bnear_in[i])
    for i in range(bmeta_in.shape[0]):
        bmeta_out[i] = lookup(bmeta_in[i])
    bmm_out[...] = lookup(bmm_in[...])
    lv = lam_in_ref[...]
    s1 = jnp.sum(lv[0:1] * lv[1:2], axis=-1, keepdims=True)
    s2 = jnp.sum(lv[2:3] * lv[3:4], axis=-1, keepdims=True)
    lam = jnp.exp(s1) - jnp.exp(s2) + LAMBDA_INIT
    lam_out[...] = jnp.broadcast_to(lam, lam_out.shape)


def _t5_bucket(rel):
    nb = N_BUCKETS // 2
    max_exact = nb // 2
    ret = jnp.where(rel > 0, nb, 0)
    n = jnp.abs(rel)
    nf = jnp.maximum(n, 1).astype(jnp.float32)
    large = max_exact + (jnp.log(nf / max_exact) / math.log(MAX_DISTANCE / max_exact)
                         * (nb - max_exact)).astype(jnp.int32)
    large = jnp.minimum(large, nb - 1)
    return ret + jnp.where(n < max_exact, n, large)


def _bias_tiles(rel_bias, da_lambda):
    r = jnp.arange(TK, dtype=jnp.int32)[:, None]
    c = jnp.arange(TQ, dtype=jnp.int32)[None, :]
    diag = jnp.where((r // CHUNK) <= (c // CHUNK), _t5_bucket(r - c), -1)
    prev = _t5_bucket(r - c - TK)
    far = _t5_bucket(jnp.full((1,), -(TK + 1), jnp.int32)).astype(jnp.int32)
    bnear = jnp.stack([jnp.broadcast_to(far[0], (TK, TQ)), prev, diag]).astype(jnp.int32)
    rm = jnp.arange(MP, dtype=jnp.int32)[:, None]
    valid = rm >= (MP - N_META)
    kpos = rm - (MP - N_META)
    m0 = jnp.where(valid, _t5_bucket(kpos - (N_META + c)), -1)
    m1 = jnp.where(valid, jnp.broadcast_to(far[0], (MP, TQ)), -1)
    bmeta = jnp.stack([m0, m1]).astype(jnp.int32)
    cm = jnp.arange(MP, dtype=jnp.int32)[None, :]
    bmm = jnp.where(valid, _t5_bucket(rm - cm), -1).astype(jnp.int32)

    smem = pl.BlockSpec(memory_space=pltpu.SMEM)
    return pl.pallas_call(
        _prep_kernel,
        grid=(DA_HEADS,),
        in_specs=[smem, smem,
                  pl.BlockSpec((4, DA_HALF), lambda h: (0, 0)),
                  pl.BlockSpec((3, TK, TQ), lambda h: (0, 0, 0)),
                  pl.BlockSpec((2, MP, TQ), lambda h: (0, 0, 0)),
                  pl.BlockSpec((MP, MP), lambda h: (0, 0))],
        out_specs=[pl.BlockSpec((None, 3, TK, TQ), lambda h: (h, 0, 0, 0)),
                   pl.BlockSpec((None, 2, MP, TQ), lambda h: (h, 0, 0, 0)),
                   pl.BlockSpec((None, MP, MP), lambda h: (h, 0, 0)),
                   pl.BlockSpec((SUBLANES, LANES), lambda h: (0, 0))],
        out_shape=[jax.ShapeDtypeStruct((DA_HEADS, 3, TK, TQ), f32),
                   jax.ShapeDtypeStruct((DA_HEADS, 2, MP, TQ), f32),
                   jax.ShapeDtypeStruct((DA_HEADS, MP, MP), f32),
                   jax.ShapeDtypeStruct((SUBLANES, LANES), f32)],
        name="prep",
    )(far, rel_bias.astype(f32), da_lambda.astype(f32), bnear, bmeta, bmm)


def _sigmoid(x):
    return 1.0 / (1.0 + jnp.exp(-x))


def _proj_kernel(x_ref, halo_ref, g_ref, w_ref, cw_ref, q_ref, k_ref, v_ref, dq_ref,
                 dk_ref, dv_ref, dz_ref, ba_ref, tail_ref, carry_ref, carry_next, u_ref):
    tm = x_ref.shape[0]
    t = pl.program_id(1)

    @pl.when(t == 0)
    def _():
        carry_ref[...] = halo_ref[...]

    @pl.when(t > 0)
    def _():
        carry_ref[...] = carry_next[...]

    x = x_ref[...]
    ms = jnp.mean(x * x, axis=-1, keepdims=True)
    u_ref[...] = (x * lax.rsqrt(ms + EPS) * g_ref[...]).astype(bf16)
    outs = (q_ref, k_ref, v_ref, dq_ref, dk_ref, dv_ref, dz_ref)
    n_out = len(outs)
    project = lambda j: jnp.dot(u_ref[...], w_ref[:, j * 512:(j + 1) * 512],
                                preferred_element_type=f32)

    def conv_silu(j, r):
        cols = slice((j - 3) * DN_WIDTH, (j - 2) * DN_WIDTH)
        carry_next[:, cols] = r[tm - SUBLANES:, :]
        ext = jnp.concatenate([carry_ref[:, cols], r], axis=0)
        y = cw_ref[0:1, cols] * ext
        for d in range(1, DN_CONV):
            y = cw_ref[d:d + 1, cols] * ext + pltpu.roll(y, 1, 0)
        y = y[SUBLANES:, :]
        return y * _sigmoid(y)

    def l2norm_heads(y, scale):
        parts = []
        for h in range(DN_HEADS):
            yh = y[:, h * DN_DK:(h + 1) * DN_DK]
            parts.append(yh * (lax.rsqrt(jnp.sum(yh * yh, axis=-1, keepdims=True) + EPS) * scale))
        return jnp.concatenate(parts, axis=1)

    order = (3, 0, 4, 1, 5, 2, 6)
    ahead = 2
    pending = [project(j) for j in order[:ahead]]
    for pos, j in enumerate(order):
        r = pending.pop(0)
        if pos + ahead < n_out:
            pending.append(project(order[pos + ahead]))
        elif pos + ahead == n_out:
            ba_ref[...] = jnp.dot(u_ref[...], w_ref[:, N_MAIN:N_MAIN + LANES],
                                  preferred_element_type=f32)
        if j == 0:
            r = r * (DA_HALF ** -0.5 * LOG2E)
        elif j in (3, 4):
            r = l2norm_heads(conv_silu(j, r), DN_DK ** -0.5 if j == 3 else 1.0)
        elif j == 5:
            r = conv_silu(j, r)
        if j == 0:
            r = r.T
        outs[j][...] = r.astype(outs[j].dtype)
    tail_ref[...] = carry_next[...]


def _proj(x3, halo, g1, w_all, conv_w, tm):
    nb, rows, _ = x3.shape
    row_spec = lambda w: pl.BlockSpec((None, tm, w), lambda b, t: (b, t, 0))
    outs = [jax.ShapeDtypeStruct((nb, 512, rows), bf16)] + [
        jax.ShapeDtypeStruct((nb, rows, 512), bf16)] * 6 + [
        jax.ShapeDtypeStruct((nb, rows, LANES), f32),
        jax.ShapeDtypeStruct((nb, SUBLANES, GDN_QKV), f32)]
    return pl.pallas_call(
        _proj_kernel,
        grid=(nb, rows // tm),
        in_specs=[row_spec(D_MODEL), _const_spec((SUBLANES, GDN_QKV)), _const_spec((1, D_MODEL)),
                  _const_spec((D_MODEL, N_MAIN + LANES)), _const_spec((DN_CONV, GDN_QKV))],
        out_specs=[pl.BlockSpec((None, 512, tm), lambda b, t: (b, 0, t))]
                  + [row_spec(512)] * 6 + [row_spec(LANES),
                   pl.BlockSpec((None, SUBLANES, GDN_QKV), lambda b, t: (b, 0, 0))],
        out_shape=outs,
        scratch_shapes=([pltpu.VMEM((SUBLANES, GDN_QKV), f32)] * 2
                        + [pltpu.VMEM((tm, D_MODEL), bf16)]),
        compiler_params=pltpu.CompilerParams(
            dimension_semantics=("parallel", "arbitrary"), vmem_limit_bytes=VMEM_LIMIT),
        name="proj",
    )(x3, halo, g1, w_all, conv_w)


def _dot_nt(a, b):
    return lax.dot_general(a, b, (((1,), (1,)), ((), ())), preferred_element_type=f32)


def _dot_tn(a, b):
    return lax.dot_general(a, b, (((0,), (0,)), ((), ())), preferred_element_type=f32)


def _stack_components(qt):
    zero = jnp.zeros((DA_HALF, qt.shape[1]), qt.dtype)
    q0 = jnp.concatenate([qt[:DA_HALF], zero], axis=0)
    q1 = jnp.concatenate([zero, qt[DA_HALF:]], axis=0)
    return jnp.concatenate([q0, q1], axis=1)


def _colmax(s):
    return jnp.max(s, axis=0, keepdims=True)


def _values_t_ext(v):
    vt = v.astype(f32).T.astype(bf16)
    return jnp.concatenate([vt, jnp.ones((V_EXT - DA_HEAD_DIM, v.shape[0]), bf16)], axis=0)


def _softmax_step(stats, s, smax, vt_pend, p_ref, acc_ref):
    m, alpha_pend = stats
    pv = jnp.dot(vt_pend, p_ref[...], preferred_element_type=f32)
    m_new = jnp.maximum(m, smax)
    alpha = jnp.exp2(m - m_new)
    p = jnp.exp2((s - m_new).astype(bf16))
    acc_ref[...] = alpha_pend * acc_ref[...] + pv
    return (m_new, alpha), p


def _attn_finish(stats, vt_pend, p_ref, acc_ref, lam, g, tq):
    m, alpha_pend = stats
    acc = alpha_pend * acc_ref[...] + jnp.dot(vt_pend, p_ref[...], preferred_element_type=f32)
    l = acc[DA_HEAD_DIM:DA_HEAD_DIM + 1, :]
    acc = acc[:DA_HEAD_DIM, :]
    o = acc[:, :tq] / l[:, :tq] - lam * (acc[:, tq:] / l[:, tq:])
    ms = jnp.mean(o * o, axis=0, keepdims=True)
    y = o * lax.rsqrt(ms + EPS) * g * (1.0 - LAMBDA_INIT)
    return y.T


def _attn_init(tq, p_ref, acc_ref):
    p_ref[...] = jnp.zeros(p_ref.shape, bf16)
    acc_ref[...] = jnp.zeros(acc_ref.shape, f32)
    return (jnp.full((1, 2 * tq), -3e38, f32), jnp.ones((1, 2 * tq), f32))


def _both(b):
    return jnp.concatenate([b, b], axis=1)


def _attn_kernel(lam_ref, q_ref, k_ref, v_ref, km_ref, vm_ref, bnear_ref, bmeta_ref, g_ref,
                 o_ref, *scratch):
    i = pl.program_id(1)
    heads = range(DA_HEADS)
    cols = lambda h: slice(h * DA_HEAD_DIM, (h + 1) * DA_HEAD_DIM)
    qz = [_stack_components(q_ref[cols(h), :]) for h in heads]

    def scores(h, j):
        start = pl.multiple_of(j * TK, TK)
        return jnp.dot(k_ref[pl.ds(start, TK), cols(h)], qz[h], preferred_element_type=f32)

    s_even = scratch[0:DA_HEADS]
    s_odd = scratch[DA_HEADS:2 * DA_HEADS]
    p_bufs = scratch[2 * DA_HEADS:3 * DA_HEADS]
    accs = scratch[3 * DA_HEADS:4 * DA_HEADS]
    vts = scratch[4 * DA_HEADS:5 * DA_HEADS]
    vmt_ref = scratch[5 * DA_HEADS]

    @pl.when(i == 0)
    def _():
        for h in heads:
            for c0 in range(0, k_ref.shape[0], VT_CHUNK):
                vts[h][:, c0:c0 + VT_CHUNK] = _values_t_ext(v_ref[c0:c0 + VT_CHUNK, cols(h)])
            vmt_ref[h] = _values_t_ext(vm_ref[:, cols(h)])

    def values(h, j):
        start = pl.multiple_of(jnp.maximum(j, 0) * TK, TK)
        return vts[h][:, pl.ds(start, TK)]

    def consume(j, bufs, stats, smax):
        new_stats = []
        for h in heads:
            st, p = _softmax_step(stats[h], bufs[h][...], smax[h], values(h, j - 1),
                                  p_bufs[h], accs[h])
            p_bufs[h][...] = p
            new_stats.append(st)
        return tuple(new_stats)

    last_tile = k_ref.shape[0] // TK - 1

    def produce(j, bufs):
        kind = jnp.clip(j - (i - 2), 0, 2)
        j = jnp.minimum(j, last_tile)
        new_smax = []
        for h in heads:
            s_next = scores(h, j) + _both(bnear_ref[h, kind])
            bufs[h][...] = s_next
            new_smax.append(_colmax(s_next))
        return tuple(new_smax)

    stats = tuple(_attn_init(TQ, p_bufs[h], accs[h]) for h in heads)
    smax_even = produce(0, s_even)
    smax_odd = produce(1, s_odd)

    def pair_body(jp, carry):
        stats, smax_even, smax_odd = carry
        j = 2 * jp
        stats = consume(j, s_even, stats, smax_even)
        smax_even = produce(j + 2, s_even)
        stats = consume(j + 1, s_odd, stats, smax_odd)
        smax_odd = produce(j + 3, s_odd)
        return stats, smax_even, smax_odd

    n_tiles = i + 1
    n_pairs = n_tiles // 2
    stats, smax_even, _ = lax.fori_loop(0, n_pairs, pair_body, (stats, smax_even, smax_odd))
    stats = lax.fori_loop(0, n_tiles - 2 * n_pairs,
                          lambda _, st: consume(2 * n_pairs, s_even, st, smax_even), stats)

    for h in heads:
        p_h, acc_h = p_bufs[h], accs[h]
        s = (jnp.dot(km_ref[:, cols(h)], qz[h], preferred_element_type=f32)
             + _both(bmeta_ref[h, jnp.minimum(i, 1)]))
        st, p = _softmax_step(stats[h], s, _colmax(s), values(h, i), p_h, acc_h)
        p_meta = p_h.at[0:MP, :]
        p_meta[...] = p
        o_ref[:, cols(h)] = _attn_finish(st, vmt_ref[h], p_meta, acc_h, lam_ref[0],
                                         g_ref[...], TQ).astype(o_ref.dtype)


def _attn_meta_kernel(lam_ref, q_ref, km_ref, vm_ref, bmm_ref, g_ref, o_ref, p_buf, acc_ref):
    qz = _stack_components(q_ref[...])
    vmt = _values_t_ext(vm_ref[...])
    stats = _attn_init(MP, p_buf, acc_ref)
    s = jnp.dot(km_ref[...], qz, preferred_element_type=f32) + _both(bmm_ref[...])
    stats, p = _softmax_step(stats, s, _colmax(s), vmt, p_buf, acc_ref)
    p_buf[...] = p
    o_ref[...] = _attn_finish(stats, vmt, p_buf, acc_ref, lam_ref[0], g_ref[...],
                              MP).astype(o_ref.dtype)


def _attention(lam1, q, k, v, km, vm, bnear, bmeta, g_tile):
    nb, s_len, _ = k.shape
    smem = pl.BlockSpec(memory_space=pltpu.SMEM)
    return pl.pallas_call(
        _attn_kernel,
        grid=(nb, s_len // TQ),
        in_specs=[smem,
                  pl.BlockSpec((None, DA_WIDTH, TQ), lambda b, i: (b, 0, i)),
                  pl.BlockSpec((None, s_len, DA_WIDTH), lambda b, i: (b, 0, 0)),
                  pl.BlockSpec((None, s_len, DA_WIDTH), lambda b, i: (b, 0, 0)),
                  _const_spec((MP, DA_WIDTH)), _const_spec((MP, DA_WIDTH)),
                  _const_spec((DA_HEADS, 3, TK, TQ)), _const_spec((DA_HEADS, 2, MP, TQ)),
                  _const_spec((DA_HEAD_DIM, TQ))],
        out_specs=pl.BlockSpec((None, TQ, DA_WIDTH), lambda b, i: (b, i, 0)),
        out_shape=jax.ShapeDtypeStruct((nb, s_len, DA_WIDTH), bf16),
        scratch_shapes=([pltpu.VMEM((TK, 2 * TQ), f32)] * (2 * DA_HEADS)
                        + [pltpu.VMEM((TK, 2 * TQ), bf16)] * DA_HEADS
                        + [pltpu.VMEM((V_EXT, 2 * TQ), f32)] * DA_HEADS
                        + [pltpu.VMEM((V_EXT, s_len), bf16)] * DA_HEADS
                        + [pltpu.VMEM((DA_HEADS, V_EXT, MP), bf16)]),
        compiler_params=pltpu.CompilerParams(
            dimension_semantics=("parallel", "arbitrary"),
            vmem_limit_bytes=VMEM_LIMIT),
        name="attn",
    )(lam1, q, k, v, km, vm, bnear, bmeta, g_tile)


def _attention_meta(lam1, qm, km, vm, bmm, g_tile):
    smem = pl.BlockSpec(memory_space=pltpu.SMEM)
    head = pl.BlockSpec((MP, DA_HEAD_DIM), lambda h: (0, h))
    return pl.pallas_call(
        _attn_meta_kernel,
        grid=(DA_HEADS,),
        in_specs=[smem, pl.BlockSpec((DA_HEAD_DIM, MP), lambda h: (h, 0)), head, head,
                  pl.BlockSpec((None, MP, MP), lambda h: (h, 0, 0)),
                  pl.BlockSpec((DA_HEAD_DIM, MP), lambda h: (0, 0))],
        out_specs=head,
        out_shape=jax.ShapeDtypeStruct((MP, DA_WIDTH), bf16),
        scratch_shapes=[pltpu.VMEM((MP, 2 * MP), bf16), pltpu.VMEM((V_EXT, 2 * MP), f32)],
        name="attn_meta",
    )(lam1, qm, km, vm, bmm, g_tile)


GDN_R = DN_HEADS * CHUNK
GDN_LEVELS = (2, 4, 8, 16, 32)
GDN_QKV = 3 * DN_WIDTH
GDN_NCH = GDN_ROWS // CHUNK
GDN_NSEQ = 4


def _gdn_consts():
    r = np.arange(GDN_R)[:, None]
    c = np.arange(GDN_R)[None, :]
    same = (r // CHUNK) == (c // CHUNK)
    bd = np.stack([same & (r >= c), same & (r > c), same]).astype(np.float32)
    i = np.arange(CHUNK)[:, None]
    j = np.arange(GDN_R)[None, :] % CHUNK
    lv = [i == j, (i > j) & ((i // 2) == (j // 2))]
    for s in GDN_LEVELS:
        lv.append(((i // (2 * s)) == (j // (2 * s))) & ((i & s) != 0) & ((j & s) == 0))
    cat = np.stack(lv).astype(np.float32)
    rr = np.arange(GDN_ROWS)[:, None]
    cc = np.arange(GDN_ROWS)[None, :]
    lcum = (((rr // CHUNK) == (cc // CHUNK)) & (rr >= cc)).astype(np.float32)
    sel = np.zeros((2, LANES, DN_HEADS * LANES), np.float32)
    for h in range(DN_HEADS):
        sel[0, h, h * LANES:(h + 1) * LANES] = 1.0
        sel[1, DN_HEADS + h, h * LANES:(h + 1) * LANES] = 1.0
    return bd, cat, lcum, sel


def _split3(x):
    hi = x.astype(bf16)
    r1 = x - hi.astype(f32)
    mid = r1.astype(bf16)
    lo = (r1 - mid.astype(f32)).astype(bf16)
    return hi, mid, lo


def _gdn_block(n_per_seq, xq, xk, xv, ba, z, s_refs, alog_ref, dtb_ref, ng_ref, bd_ref,
               cat_ref, lcum_ref, sel_ref):
    n_seq = len(s_refs)
    n = n_seq * n_per_seq
    seq_rows = n_per_seq * CHUNK
    heads = range(DN_HEADS)
    chunks = range(n)
    dot = functools.partial(jnp.dot, preferred_element_type=f32)

    def stack(a):
        return jnp.concatenate(
            [a[c * CHUNK:(c + 1) * CHUNK, h * LANES:(h + 1) * LANES]
             for c in chunks for h in heads], axis=0)

    qn_b, kn_b = stack(xq), stack(xk)
    qn, kn, vs = qn_b.astype(f32), kn_b.astype(f32), stack(xv).astype(f32)
    blk = lambda c: slice(c * GDN_R, (c + 1) * GDN_R)

    beta_t = _sigmoid(ba)
    xg = ba + dtb_ref[...]
    softplus = jnp.maximum(xg, 0.0) + jnp.log1p(jnp.exp(-jnp.abs(xg)))
    g_t = -jnp.exp(alog_ref[...]) * softplus
    lcum = lcum_ref[0:seq_rows, 0:seq_rows]
    g_parts = _split3(g_t)
    gcum = jnp.concatenate(
        [sum(dot(lcum, part[q * seq_rows:(q + 1) * seq_rows]) for part in g_parts)
         for q in range(n_seq)], axis=0)
    g_rep = stack(sum(dot(part, sel_ref[1]) for part in _split3(gcum)))
    b_rep = stack(sum(dot(part, sel_ref[0]) for part in _split3(beta_t)[:2]))
    g_end = jnp.concatenate(
        [jnp.broadcast_to(g_rep[(b + 1) * CHUNK - 1:(b + 1) * CHUNK, :], (CHUNK, LANES))
         for b in range(n * DN_HEADS)], axis=0)
    exp_g = jnp.exp(g_rep)
    rhs = jnp.concatenate([vs * b_rep, kn * (b_rep * exp_g)], axis=1).astype(bf16)
    q_g = qn * exp_g
    k_g = (kn * jnp.exp(g_end - g_rep)).astype(bf16)
    g_last = jnp.exp(g_end)

    tri = bd_ref[0] > 0.5
    block_b = bd_ref[2].astype(bf16)
    both = lambda a: jnp.concatenate([a, a], axis=1)
    decay, m_b, x_cat = [], [], []
    for c in chunks:
        g_row = g_rep[blk(c)].T[0:1, :]
        dec = jnp.exp(jnp.where(tri, both(g_rep[blk(c)]) - g_row, -jnp.inf))
        m = bd_ref[1] * both(b_rep[blk(c)]) * _dot_nt(kn_b[blk(c)], kn_b[blk(c)]) * dec
        m_cat = sum(m[h * CHUNK:(h + 1) * CHUNK] for h in heads)
        decay.append(dec)
        m_b.append(m.astype(bf16))
        x_cat.append(cat_ref[0] - m_cat * cat_ref[1])

    def to_bd(x):
        return jnp.concatenate([x.astype(bf16)] * DN_HEADS, axis=0) * block_b

    for lvl in range(len(GDN_LEVELS)):
        ys = [dot(x_cat[c].astype(bf16), m_b[c]) for c in chunks]
        zs = [dot(ys[c].astype(bf16), to_bd(x_cat[c])) for c in chunks]
        x_cat = [x_cat[c] - zs[c] * cat_ref[2 + lvl] for c in chunks]
    sol = [dot(to_bd(x_cat[c]), rhs[blk(c)]).astype(bf16) for c in chunks]
    a_qk = [(_dot_nt(qn_b[blk(c)], kn_b[blk(c)]) * decay[c]).astype(bf16) for c in chunks]
    a_uw = [dot(a_qk[c], sol[c]) for c in chunks]
    hrows = lambda h: slice(h * CHUNK, (h + 1) * CHUNK)
    k_uw = [[_dot_tn(k_g[blk(c)][hrows(h)], sol[c][hrows(h)]) for h in heads] for c in chunks]

    out_rows = [None] * n
    for step in range(n_per_seq):
        for q in range(n_seq):
            c = q * n_per_seq + step
            s_ref = s_refs[q]
            out_cols = []
            for h in heads:
                s_old = s_ref[h]
                s_b = s_old.astype(bf16)
                q_eff = (q_g[blk(c)][hrows(h)] - a_uw[c][hrows(h), DN_DV:]).astype(bf16)
                o = a_uw[c][hrows(h), :DN_DV] + dot(q_eff, s_b)
                s_ref[h] = (s_old * g_last[blk(c)][hrows(h)][0:1, :] + k_uw[c][h][:, :DN_DV]
                            - dot(k_uw[c][h][:, DN_DV:].astype(bf16), s_b))
                o = o * lax.rsqrt(jnp.mean(o * o, axis=-1, keepdims=True) + EPS) * ng_ref[...]
                zh = z[c * CHUNK:(c + 1) * CHUNK, h * DN_DV:(h + 1) * DN_DV].astype(f32)
                out_cols.append(o * (zh * _sigmoid(zh)))
            out_rows[c] = jnp.concatenate(out_cols, axis=1)
    return jnp.concatenate(out_rows, axis=0)


def _gdn_kernel(dq_ref, dk_ref, dv_ref, dz_ref, ba_ref, mq_ref, mk_ref, mv_ref, mz_ref, mba_ref,
                alog_ref, dtb_ref, ng_ref, bd_ref, cat_ref, lcum_ref, sel_ref,
                o_ref, om_ref, s_ref):
    consts = (alog_ref, dtb_ref, ng_ref, bd_ref, cat_ref, lcum_ref, sel_ref)
    s_refs = [s_ref.at[q] for q in range(GDN_NSEQ)]

    @pl.when(pl.program_id(1) == 0)
    def _():
        s_ref[0] = jnp.zeros(s_ref.shape[1:], f32)
        om_ref[...] = _gdn_block(1, mq_ref[...], mk_ref[...], mv_ref[...], mba_ref[...],
                                 mz_ref[...], s_refs[:1], *consts).astype(om_ref.dtype)
        for q in range(1, GDN_NSEQ):
            s_ref[q] = s_ref[0]

    merge = lambda ref: ref[...].reshape(GDN_NSEQ * GDN_ROWS, ref.shape[-1])
    out = _gdn_block(GDN_NCH, merge(dq_ref), merge(dk_ref), merge(dv_ref), merge(ba_ref),
                     merge(dz_ref), s_refs, *consts)
    o_ref[...] = out.reshape(o_ref.shape).astype(o_ref.dtype)


def _gdn(dq, dk, dv, dz, ba, mq, mk, mv, mz, mba, alog_row, dtb_row, ng_row):
    nb, s_len, _ = dq.shape
    assert nb % GDN_NSEQ == 0
    row = lambda w: pl.BlockSpec((GDN_NSEQ, GDN_ROWS, w), lambda b, t: (b, t, 0))
    mrow = lambda w: pl.BlockSpec((CHUNK, w), lambda b, t: (MP // CHUNK - 1, 0))
    bd, cat, lcum, sel = _gdn_consts()
    bd, cat = jnp.asarray(bd), jnp.asarray(cat)
    lcum, sel = jnp.asarray(lcum).astype(bf16), jnp.asarray(sel).astype(bf16)
    return pl.pallas_call(
        _gdn_kernel,
        grid=(nb // GDN_NSEQ, s_len // GDN_ROWS),
        in_specs=[row(DN_WIDTH)] * 4 + [row(LANES)] + [mrow(DN_WIDTH)] * 4 + [mrow(LANES)] + [
            _const_spec((1, LANES)), _const_spec((1, LANES)),
            _const_spec((1, DN_DV)), _const_spec(bd.shape), _const_spec(cat.shape),
            _const_spec(lcum.shape), _const_spec(sel.shape)],
        out_specs=[row(DN_WIDTH), pl.BlockSpec((None, CHUNK, DN_WIDTH), lambda b, t: (b, 0, 0))],
        out_shape=[jax.ShapeDtypeStruct((nb, s_len, DN_WIDTH), bf16),
                   jax.ShapeDtypeStruct((nb // GDN_NSEQ, CHUNK, DN_WIDTH), bf16)],
        scratch_shapes=[pltpu.VMEM((GDN_NSEQ, DN_HEADS, DN_DK, DN_DV), f32)],
        compiler_params=pltpu.CompilerParams(
            dimension_semantics=("parallel", "arbitrary"), vmem_limit_bytes=VMEM_LIMIT),
        name="gdn",
    )(dq, dk, dv, dz, ba, mq, mk, mv, mz, mba, alog_row, dtb_row, ng_row, bd, cat, lcum, sel)


def _mix_and_norm(x, oda, odn, wout_ref, g2_ref):
    mix = jnp.concatenate([oda, odn], axis=1)
    h2 = x + jnp.dot(mix, wout_ref[...], preferred_element_type=f32)
    u2 = h2 * lax.rsqrt(jnp.mean(h2 * h2, axis=-1, keepdims=True) + EPS) * g2_ref[...]
    return h2, u2.astype(bf16)


def _ffn_halo_kernel(x_ref, oda_ref, odn_ref, wout_ref, g2_ref, wup_ref, halo_ref):
    _, u2 = _mix_and_norm(x_ref[...], oda_ref[...], odn_ref[...], wout_ref, g2_ref)
    halo_ref[...] = jnp.dot(u2, wup_ref[...], preferred_element_type=f32)


def _ffn_kernel(x_ref, oda_ref, odn_ref, halo_ref, wout_ref, g2_ref, wup_ref, cw_ref, cb_ref,
                wdown_ref, gf_ref, o_ref, *scratch):
    tm = FFN_TM
    carries = scratch[0:FFN_TILES + 1]
    accs = scratch[FFN_TILES + 1:2 * FFN_TILES + 1]
    hbufs = scratch[2 * FFN_TILES + 1:]
    t = pl.program_id(1)

    @pl.when(t == 0)
    def _():
        carries[0][...] = halo_ref[...]

    @pl.when(t > 0)
    def _():
        carries[0][...] = carries[FFN_TILES][...]

    n_chunks = D_FF // FFN_CW
    stages = [(a, c) for a in range(FFN_TILES) for c in range(n_chunks)]
    col_pair = lambda c: (slice(c * FFN_CW, (c + 1) * FFN_CW),
                          slice(D_FF + c * FFN_CW, D_FF + (c + 1) * FFN_CW))
    buf_pair = lambda s: hbufs[2 * (s % FFN_NBUF):2 * (s % FFN_NBUF) + 2]
    rows = lambda a: slice(a * tm, (a + 1) * tm)
    normed = {}

    def up_part(s):
        a, c = stages[s]
        if c == 0:
            normed[a] = _mix_and_norm(x_ref[rows(a), :], oda_ref[rows(a), :],
                                      odn_ref[rows(a), :], wout_ref, g2_ref)
        u2 = normed[a][1]
        for cols, hbuf in zip(col_pair(c), buf_pair(s)):
            hbuf[0:SUBLANES, :] = carries[a][:, cols]
            hup = jnp.dot(u2, wup_ref[:, cols], preferred_element_type=f32)
            hbuf[SUBLANES:SUBLANES + tm, :] = hup
            carries[a + 1][:, cols] = hup[tm - SUBLANES:, :]

    def conv_part(cols, hbuf):
        y = cb_ref[:, cols]
        for d in range(FFN_CONV):
            y = y + cw_ref[FFN_CONV - 1 - d:FFN_CONV - d, cols] * hbuf[SUBLANES - d:SUBLANES - d + tm, :]
        return y

    for s in range(FFN_AHEAD):
        up_part(s)
    for s, (a, c) in enumerate(stages):
        if s + FFN_AHEAD < len(stages):
            up_part(s + FFN_AHEAD)
        gate, val = (conv_part(cols, hbuf) for cols, hbuf in zip(col_pair(c), buf_pair(s)))
        act = (gate * _sigmoid(gate) * val).astype(bf16)
        part = jnp.dot(act, wdown_ref[c * FFN_CW:(c + 1) * FFN_CW, :], preferred_element_type=f32)
        if c == 0:
            accs[a][...] = part
        else:
            accs[a][...] += part
        if c == n_chunks - 1:
            y = normed[a][0] + accs[a][...]
            o_ref[rows(a), :] = (y * lax.rsqrt(jnp.mean(y * y, axis=-1, keepdims=True) + EPS)
                                 * gf_ref[...])


def _ffn_halo(xm, odam, odnm, w_out, g2, w_up):
    return pl.pallas_call(
        _ffn_halo_kernel,
        out_shape=jax.ShapeDtypeStruct((HALO_ROWS, 2 * D_FF), f32),
        compiler_params=pltpu.CompilerParams(vmem_limit_bytes=VMEM_LIMIT),
        name="ffn_halo",
    )(xm, odam, odnm, w_out, g2, w_up)


def _ffn(x, oda, odn, halo, w_out, g2, w_up, conv_w, conv_b, w_down, gf):
    nb, s_len, _ = x.shape
    tm = FFN_TM
    step_rows = FFN_TILES * tm
    row = lambda w: pl.BlockSpec((None, step_rows, w), lambda b, t: (b, t, 0))
    return pl.pallas_call(
        _ffn_kernel,
        grid=(nb, s_len // step_rows),
        in_specs=[row(D_MODEL), row(DA_WIDTH), row(DN_WIDTH),
                  _const_spec((SUBLANES, 2 * D_FF)), _const_spec((D_MODEL, D_MODEL)),
                  _const_spec((1, D_MODEL)), _const_spec((D_MODEL, 2 * D_FF)),
                  _const_spec((FFN_CONV, 2 * D_FF)), _const_spec((1, 2 * D_FF)),
                  _const_spec((D_FF, D_MODEL)), _const_spec((1, D_MODEL))],
        out_specs=row(D_MODEL),
        out_shape=jax.ShapeDtypeStruct((nb, s_len, D_MODEL), f32),
        scratch_shapes=([pltpu.VMEM((SUBLANES, 2 * D_FF), f32)] * (FFN_TILES + 1)
                        + [pltpu.VMEM((tm, D_MODEL), f32)] * FFN_TILES
                        + [pltpu.VMEM((SUBLANES + tm, FFN_CW), f32)] * (2 * FFN_NBUF)),
        compiler_params=pltpu.CompilerParams(
            dimension_semantics=("parallel", "arbitrary"), vmem_limit_bytes=VMEM_LIMIT),
        name="ffn",
    )(x, oda, odn, halo, w_out, g2, w_up, conv_w, conv_b, w_down, gf)


def kernel(x, meta_tokens, rel_bias, norm1_g, w_in, da_lambda, da_subln_g, dn_conv_w, dn_A_log,
           dn_dt_bias, dn_norm_g, w_out, norm2_g, w_up, ffn_conv_w, ffn_conv_b, w_down,
           final_norm_g):
    nb, s_len, _ = x.shape
    assert s_len % TQ == 0 and s_len % GDN_ROWS == 0 and s_len % (FFN_TILES * FFN_TM) == 0
    assert s_len % PROJ_TM == 0 and s_len % VT_CHUNK == 0
    w_all = jnp.pad(w_in[0], ((0, 0), (0, LANES - 2 * DN_HEADS))).astype(bf16)
    w_out_b = w_out[0].astype(bf16)
    w_up_b = w_up[0].astype(bf16)
    w_down_b = w_down[0].astype(bf16)
    g1 = norm1_g[0].reshape(1, D_MODEL).astype(f32)
    g2 = norm2_g[0].reshape(1, D_MODEL).astype(f32)
    gf = final_norm_g.reshape(1, D_MODEL).astype(f32)
    meta_pad = jnp.pad(meta_tokens.astype(x.dtype), ((MP - N_META, 0), (0, 0)))

    bnear, bmeta, bmm, lam_tile = _bias_tiles(rel_bias, da_lambda[0])
    lam1 = lam_tile[0, :1]

    conv_w = dn_conv_w[0].astype(f32)
    no_halo = jnp.zeros((SUBLANES, GDN_QKV), f32)
    mq, mk, mv, mdq, mdk, mdv, mdz, mba, mtail = [
        a[0] for a in _proj(meta_pad[None], no_halo, g1, w_all, conv_w, MP)]
    q, k, v, dq, dk, dv, dz, ba, _ = _proj(x, mtail, g1, w_all, conv_w, PROJ_TM)

    subln = da_subln_g[0].astype(f32)
    o_da = _attention(lam1, q, k, v, mk, mv, bnear, bmeta,
                      jnp.broadcast_to(subln[:, None], (DA_HEAD_DIM, TQ)))
    o_da_m = _attention_meta(lam1, mq, mk, mv, bmm,
                             jnp.broadcast_to(subln[:, None], (DA_HEAD_DIM, MP)))

    gate_row = lambda p: jnp.pad(p[0].astype(f32), (DN_HEADS, LANES - 2 * DN_HEADS)).reshape(1, LANES)
    o_dn, o_dn_m = _gdn(dq, dk, dv, dz, ba, mdq, mdk, mdv, mdz, mba,
                        gate_row(dn_A_log), gate_row(dn_dt_bias),
                        dn_norm_g[0].reshape(1, DN_DV).astype(f32))

    halo = _ffn_halo(meta_pad[MP - HALO_ROWS:], o_da_m[MP - HALO_ROWS:],
                     o_dn_m[0, CHUNK - HALO_ROWS:], w_out_b, g2, w_up_b)[HALO_ROWS - SUBLANES:]
    return _ffn(x, o_da, o_dn, halo, w_out_b, g2, w_up_b, ffn_conv_w[0].astype(f32),
                ffn_conv_b[0].reshape(1, 2 * D_FF).astype(f32), w_down_b, gf)
```

```python
import functools
import math

import numpy as np
import jax
import jax.numpy as jnp
from jax import lax
from jax.experimental import pallas as pl
from jax.experimental.pallas import tpu as pltpu

f32 = jnp.float32
bf16 = jnp.bfloat16

D_MODEL = 1024
CHUNK = 64
N_META = 16
EPS = 1e-6
NEG_INF = -1e30
LAMBDA_INIT = 0.8 - 0.6 * math.exp(-0.3 * 0)
LOG2E = math.log2(math.e)

DA_HEADS = 4
DA_HEAD_DIM = 128
DA_HALF = 64
DA_WIDTH = DA_HEADS * DA_HEAD_DIM
DN_HEADS = 4
DN_DK = 128
DN_DV = 128
DN_WIDTH = DN_HEADS * DN_DV
DN_CONV = 4
N_BUCKETS = 32
MAX_DISTANCE = 128
D_FF = 2816
FFN_CONV = 3

N_MAIN = 7 * 512
LANES = 128
SUBLANES = 8
MP = 128
TQ = 256
TK = 256
V_EXT = DA_HEAD_DIM + 16
VT_CHUNK = 512
PROJ_TM = 512
GDN_ROWS = 256
FFN_TILES = 1
FFN_TM = 256
FFN_CW = 256
FFN_AHEAD = 10
FFN_NBUF = FFN_AHEAD + 1
HALO_ROWS = 16
VMEM_LIMIT = 52 * 1024 * 1024


def _const_spec(shape):
    nd = len(shape)
    return pl.BlockSpec(shape, lambda *_: (0,) * nd, pipeline_mode=pl.Buffered(1))


def _prep_kernel(far_ref, table_ref, lam_in_ref, bnear_in, bmeta_in, bmm_in,
                 bnear_out, bmeta_out, bmm_out, lam_out):
    h = pl.program_id(0)
    c_far = table_ref[far_ref[0], h]

    def lookup(bkt):
        out = jnp.full(bkt.shape, NEG_INF, f32)
        for b in range(N_BUCKETS):
            out = jnp.where(bkt == b, (table_ref[b, h] - c_far) * LOG2E, out)
        return out

    for i in range(2):
        bnear_out[i] = lookup(bnear_in[i])
        bmeta_out[i] = lookup(bmeta_in[i])
    bmm_out[...] = lookup(bmm_in[...])
    lv = lam_in_ref[...]
    s1 = jnp.sum(lv[0:1] * lv[1:2], axis=-1, keepdims=True)
    s2 = jnp.sum(lv[2:3] * lv[3:4], axis=-1, keepdims=True)
    lam = jnp.exp(s1) - jnp.exp(s2) + LAMBDA_INIT
    lam_out[...] = jnp.broadcast_to(lam, lam_out.shape)


def _t5_bucket(rel):
    nb = N_BUCKETS // 2
    max_exact = nb // 2
    ret = jnp.where(rel > 0, nb, 0)
    n = jnp.abs(rel)
    nf = jnp.maximum(n, 1).astype(jnp.float32)
    large = max_exact + (jnp.log(nf / max_exact) / math.log(MAX_DISTANCE / max_exact)
                         * (nb - max_exact)).astype(jnp.int32)
    large = jnp.minimum(large, nb - 1)
    return ret + jnp.where(n < max_exact, n, large)


def _bias_tiles(rel_bias, da_lambda):
    r = jnp.arange(TK, dtype=jnp.int32)[:, None]
    c = jnp.arange(TQ, dtype=jnp.int32)[None, :]
    diag = jnp.where((r // CHUNK) <= (c // CHUNK), _t5_bucket(r - c), -1)
    prev = _t5_bucket(r - c - TK)
    bnear = jnp.stack([diag, prev]).astype(jnp.int32)
    far = _t5_bucket(jnp.full((1,), -(TK + 1), jnp.int32)).astype(jnp.int32)
    rm = jnp.arange(MP, dtype=jnp.int32)[:, None]
    valid = rm >= (MP - N_META)
    kpos = rm - (MP - N_META)
    m0 = jnp.where(valid, _t5_bucket(kpos - (N_META + c)), -1)
    m1 = jnp.where(valid, jnp.broadcast_to(far[0], (MP, TQ)), -1)
    bmeta = jnp.stack([m0, m1]).astype(jnp.int32)
    cm = jnp.arange(MP, dtype=jnp.int32)[None, :]
    bmm = jnp.where(valid, _t5_bucket(rm - cm), -1).astype(jnp.int32)

    smem = pl.BlockSpec(memory_space=pltpu.SMEM)
    return pl.pallas_call(
        _prep_kernel,
        grid=(DA_HEADS,),
        in_specs=[smem, smem,
                  pl.BlockSpec((4, DA_HALF), lambda h: (0, 0)),
                  pl.BlockSpec((2, TK, TQ), lambda h: (0, 0, 0)),
                  pl.BlockSpec((2, MP, TQ), lambda h: (0, 0, 0)),
                  pl.BlockSpec((MP, MP), lambda h: (0, 0))],
        out_specs=[pl.BlockSpec((None, 2, TK, TQ), lambda h: (h, 0, 0, 0)),
                   pl.BlockSpec((None, 2, MP, TQ), lambda h: (h, 0, 0, 0)),
                   pl.BlockSpec((None, MP, MP), lambda h: (h, 0, 0)),
                   pl.BlockSpec((SUBLANES, LANES), lambda h: (0, 0))],
        out_shape=[jax.ShapeDtypeStruct((DA_HEADS, 2, TK, TQ), f32),
                   jax.ShapeDtypeStruct((DA_HEADS, 2, MP, TQ), f32),
                   jax.ShapeDtypeStruct((DA_HEADS, MP, MP), f32),
                   jax.ShapeDtypeStruct((SUBLANES, LANES), f32)],
        name="prep",
    )(far, rel_bias.astype(f32), da_lambda.astype(f32), bnear, bmeta, bmm)


def _sigmoid(x):
    return 1.0 / (1.0 + jnp.exp(-x))


def _proj_kernel(x_ref, halo_ref, g_ref, w_ref, cw_ref, q_ref, k_ref, v_ref, dq_ref,
                 dk_ref, dv_ref, dz_ref, ba_ref, tail_ref, carry_ref, carry_next, u_ref):
    tm = x_ref.shape[0]
    t = pl.program_id(1)

    @pl.when(t == 0)
    def _():
        carry_ref[...] = halo_ref[...]

    @pl.when(t > 0)
    def _():
        carry_ref[...] = carry_next[...]

    x = x_ref[...]
    ms = jnp.mean(x * x, axis=-1, keepdims=True)
    u_ref[...] = (x * lax.rsqrt(ms + EPS) * g_ref[...]).astype(bf16)
    outs = (q_ref, k_ref, v_ref, dq_ref, dk_ref, dv_ref, dz_ref)
    n_out = len(outs)
    project = lambda j: jnp.dot(u_ref[...], w_ref[:, j * 512:(j + 1) * 512],
                                preferred_element_type=f32)

    def conv_silu(j, r):
        cols = slice((j - 3) * DN_WIDTH, (j - 2) * DN_WIDTH)
        carry_next[:, cols] = r[tm - SUBLANES:, :]
        ext = jnp.concatenate([carry_ref[:, cols], r], axis=0)
        y = cw_ref[0:1, cols] * ext
        for d in range(1, DN_CONV):
            y = cw_ref[d:d + 1, cols] * ext + pltpu.roll(y, 1, 0)
        y = y[SUBLANES:, :]
        return y * _sigmoid(y)

    def l2norm_heads(y, scale):
        parts = []
        for h in range(DN_HEADS):
            yh = y[:, h * DN_DK:(h + 1) * DN_DK]
            parts.append(yh * (lax.rsqrt(jnp.sum(yh * yh, axis=-1, keepdims=True) + EPS) * scale))
        return jnp.concatenate(parts, axis=1)

    order = (3, 0, 4, 1, 5, 2, 6)
    ahead = 2
    pending = [project(j) for j in order[:ahead]]
    for pos, j in enumerate(order):
        r = pending.pop(0)
        if pos + ahead < n_out:
            pending.append(project(order[pos + ahead]))
        elif pos + ahead == n_out:
            ba_ref[...] = jnp.dot(u_ref[...], w_ref[:, N_MAIN:N_MAIN + LANES],
                                  preferred_element_type=f32)
        if j == 0:
            r = r * (DA_HALF ** -0.5 * LOG2E)
        elif j in (3, 4):
            r = l2norm_heads(conv_silu(j, r), DN_DK ** -0.5 if j == 3 else 1.0)
        elif j == 5:
            r = conv_silu(j, r)
        if j == 0:
            r = r.T
        outs[j][...] = r.astype(outs[j].dtype)
    tail_ref[...] = carry_next[...]


def _proj(x3, halo, g1, w_all, conv_w, tm):
    nb, rows, _ = x3.shape
    row_spec = lambda w: pl.BlockSpec((None, tm, w), lambda b, t: (b, t, 0))
    outs = [jax.ShapeDtypeStruct((nb, 512, rows), bf16)] + [
        jax.ShapeDtypeStruct((nb, rows, 512), bf16)] * 6 + [
        jax.ShapeDtypeStruct((nb, rows, LANES), f32),
        jax.ShapeDtypeStruct((nb, SUBLANES, GDN_QKV), f32)]
    return pl.pallas_call(
        _proj_kernel,
        grid=(nb, rows // tm),
        in_specs=[row_spec(D_MODEL), _const_spec((SUBLANES, GDN_QKV)), _const_spec((1, D_MODEL)),
                  _const_spec((D_MODEL, N_MAIN + LANES)), _const_spec((DN_CONV, GDN_QKV))],
        out_specs=[pl.BlockSpec((None, 512, tm), lambda b, t: (b, 0, t))]
                  + [row_spec(512)] * 6 + [row_spec(LANES),
                   pl.BlockSpec((None, SUBLANES, GDN_QKV), lambda b, t: (b, 0, 0))],
        out_shape=outs,
        scratch_shapes=([pltpu.VMEM((SUBLANES, GDN_QKV), f32)] * 2
                        + [pltpu.VMEM((tm, D_MODEL), bf16)]),
        compiler_params=pltpu.CompilerParams(
            dimension_semantics=("parallel", "arbitrary"), vmem_limit_bytes=VMEM_LIMIT),
        name="proj",
    )(x3, halo, g1, w_all, conv_w)


def _dot_nt(a, b):
    return lax.dot_general(a, b, (((1,), (1,)), ((), ())), preferred_element_type=f32)


def _dot_tn(a, b):
    return lax.dot_general(a, b, (((0,), (0,)), ((), ())), preferred_element_type=f32)


def _stack_components(qt):
    zero = jnp.zeros((DA_HALF, qt.shape[1]), qt.dtype)
    q0 = jnp.concatenate([qt[:DA_HALF], zero], axis=0)
    q1 = jnp.concatenate([zero, qt[DA_HALF:]], axis=0)
    return jnp.concatenate([q0, q1], axis=1)


def _colmax(s):
    return jnp.max(s, axis=0, keepdims=True)


def _values_t_ext(v):
    vt = v.astype(f32).T.astype(bf16)
    return jnp.concatenate([vt, jnp.ones((V_EXT - DA_HEAD_DIM, v.shape[0]), bf16)], axis=0)


def _softmax_step(stats, s, smax, vt_pend, p_ref, acc_ref):
    m, alpha_pend = stats
    pv = jnp.dot(vt_pend, p_ref[...], preferred_element_type=f32)
    m_new = jnp.maximum(m, smax)
    alpha = jnp.exp2(m - m_new)
    p = jnp.exp2((s - m_new).astype(bf16))
    acc_ref[...] = alpha_pend * acc_ref[...] + pv
    return (m_new, alpha), p


def _attn_finish(stats, vt_pend, p_ref, acc_ref, lam, g, tq):
    m, alpha_pend = stats
    acc = alpha_pend * acc_ref[...] + jnp.dot(vt_pend, p_ref[...], preferred_element_type=f32)
    l = acc[DA_HEAD_DIM:DA_HEAD_DIM + 1, :]
    acc = acc[:DA_HEAD_DIM, :]
    o = acc[:, :tq] / l[:, :tq] - lam * (acc[:, tq:] / l[:, tq:])
    ms = jnp.mean(o * o, axis=0, keepdims=True)
    y = o * lax.rsqrt(ms + EPS) * g * (1.0 - LAMBDA_INIT)
    return y.T


def _attn_init(tq, p_ref, acc_ref):
    p_ref[...] = jnp.zeros(p_ref.shape, bf16)
    acc_ref[...] = jnp.zeros(acc_ref.shape, f32)
    return (jnp.full((1, 2 * tq), -3e38, f32), jnp.ones((1, 2 * tq), f32))


def _both(b):
    return jnp.concatenate([b, b], axis=1)


def _attn_kernel(lam_ref, q_ref, k_ref, v_ref, km_ref, vm_ref, bnear_ref, bmeta_ref, g_ref,
                 o_ref, *scratch):
    i = pl.program_id(1)
    n_far = jnp.maximum(i - 1, 0)
    heads = range(DA_HEADS)
    cols = lambda h: slice(h * DA_HEAD_DIM, (h + 1) * DA_HEAD_DIM)
    qz = [_stack_components(q_ref[cols(h), :]) for h in heads]

    def scores(h, j):
        start = pl.multiple_of(j * TK, TK)
        return jnp.dot(k_ref[pl.ds(start, TK), cols(h)], qz[h], preferred_element_type=f32)

    s_even = scratch[0:DA_HEADS]
    s_odd = scratch[DA_HEADS:2 * DA_HEADS]
    p_bufs = scratch[2 * DA_HEADS:3 * DA_HEADS]
    accs = scratch[3 * DA_HEADS:4 * DA_HEADS]
    vts = scratch[4 * DA_HEADS:5 * DA_HEADS]
    vmt_ref = scratch[5 * DA_HEADS]

    @pl.when(i == 0)
    def _():
        for h in heads:
            for c0 in range(0, k_ref.shape[0], VT_CHUNK):
                vts[h][:, c0:c0 + VT_CHUNK] = _values_t_ext(v_ref[c0:c0 + VT_CHUNK, cols(h)])
            vmt_ref[h] = _values_t_ext(vm_ref[:, cols(h)])

    def values(h, j):
        start = pl.multiple_of(jnp.maximum(j, 0) * TK, TK)
        return vts[h][:, pl.ds(start, TK)]

    def consume(j, bufs, stats, smax):
        new_stats = []
        for h in heads:
            st, p = _softmax_step(stats[h], bufs[h][...], smax[h], values(h, j - 1),
                                  p_bufs[h], accs[h])
            p_bufs[h][...] = p
            new_stats.append(st)
        return tuple(new_stats)

    def produce(j, bufs):
        new_smax = []
        for h in heads:
            s_next = scores(h, j)
            bufs[h][...] = s_next
            new_smax.append(_colmax(s_next))
        return tuple(new_smax)

    stats = tuple(_attn_init(TQ, p_bufs[h], accs[h]) for h in heads)
    smax_even = produce(0, s_even)
    smax_odd = produce(1, s_odd)

    def pair_body(jp, carry):
        stats, smax_even, smax_odd = carry
        j = 2 * jp
        stats = consume(j, s_even, stats, smax_even)
        smax_even = produce(j + 2, s_even)
        stats = consume(j + 1, s_odd, stats, smax_odd)
        smax_odd = produce(j + 3, s_odd)
        return stats, smax_even, smax_odd

    n_pairs = n_far // 2
    stats, smax_even, _ = lax.fori_loop(0, n_pairs, pair_body, (stats, smax_even, smax_odd))

    def single_body(_, stats):
        stats = consume(2 * n_pairs, s_even, stats, smax_even)
        for h in heads:
            s_even[h][...] = s_odd[h][...]
        return stats

    stats = lax.fori_loop(0, n_far - 2 * n_pairs, single_body, stats)
    gate = jnp.where(i >= 1, 0.0, NEG_INF).astype(f32)
    for h in heads:
        p_h, acc_h = p_bufs[h], accs[h]
        s = s_even[h][...] + _both(bnear_ref[h, 1] + gate)
        st, p = _softmax_step(stats[h], s, _colmax(s), values(h, n_far - 1), p_h, acc_h)
        p_h[...] = p
        s = scores(h, i) + _both(bnear_ref[h, 0])
        st, p = _softmax_step(st, s, _colmax(s), values(h, i - 1), p_h, acc_h)
        p_h[...] = p
        s = (jnp.dot(km_ref[:, cols(h)], qz[h], preferred_element_type=f32)
             + _both(bmeta_ref[h, jnp.minimum(i, 1)]))
        st, p = _softmax_step(st, s, _colmax(s), values(h, i), p_h, acc_h)
        p_meta = p_h.at[0:MP, :]
        p_meta[...] = p
        o_ref[:, cols(h)] = _attn_finish(st, vmt_ref[h], p_meta, acc_h, lam_ref[0],
                                         g_ref[...], TQ).astype(o_ref.dtype)


def _attn_meta_kernel(lam_ref, q_ref, km_ref, vm_ref, bmm_ref, g_ref, o_ref, p_buf, acc_ref):
    qz = _stack_components(q_ref[...])
    vmt = _values_t_ext(vm_ref[...])
    stats = _attn_init(MP, p_buf, acc_ref)
    s = jnp.dot(km_ref[...], qz, preferred_element_type=f32) + _both(bmm_ref[...])
    stats, p = _softmax_step(stats, s, _colmax(s), vmt, p_buf, acc_ref)
    p_buf[...] = p
    o_ref[...] = _attn_finish(stats, vmt, p_buf, acc_ref, lam_ref[0], g_ref[...],
                              MP).astype(o_ref.dtype)


def _attention(lam1, q, k, v, km, vm, bnear, bmeta, g_tile):
    nb, s_len, _ = k.shape
    smem = pl.BlockSpec(memory_space=pltpu.SMEM)
    return pl.pallas_call(
        _attn_kernel,
        grid=(nb, s_len // TQ),
        in_specs=[smem,
                  pl.BlockSpec((None, DA_WIDTH, TQ), lambda b, i: (b, 0, i)),
                  pl.BlockSpec((None, s_len, DA_WIDTH), lambda b, i: (b, 0, 0)),
                  pl.BlockSpec((None, s_len, DA_WIDTH), lambda b, i: (b, 0, 0)),
                  _const_spec((MP, DA_WIDTH)), _const_spec((MP, DA_WIDTH)),
                  _const_spec((DA_HEADS, 2, TK, TQ)), _const_spec((DA_HEADS, 2, MP, TQ)),
                  _const_spec((DA_HEAD_DIM, TQ))],
        out_specs=pl.BlockSpec((None, TQ, DA_WIDTH), lambda b, i: (b, i, 0)),
        out_shape=jax.ShapeDtypeStruct((nb, s_len, DA_WIDTH), bf16),
        scratch_shapes=([pltpu.VMEM((TK, 2 * TQ), f32)] * (2 * DA_HEADS)
                        + [pltpu.VMEM((TK, 2 * TQ), bf16)] * DA_HEADS
                        + [pltpu.VMEM((V_EXT, 2 * TQ), f32)] * DA_HEADS
                        + [pltpu.VMEM((V_EXT, s_len), bf16)] * DA_HEADS
                        + [pltpu.VMEM((DA_HEADS, V_EXT, MP), bf16)]),
        compiler_params=pltpu.CompilerParams(
            dimension_semantics=("parallel", "arbitrary"),
            vmem_limit_bytes=VMEM_LIMIT),
        name="attn",
    )(lam1, q, k, v, km, vm, bnear, bmeta, g_tile)


def _attention_meta(lam1, qm, km, vm, bmm, g_tile):
    smem = pl.BlockSpec(memory_space=pltpu.SMEM)
    head = pl.BlockSpec((MP, DA_HEAD_DIM), lambda h: (0, h))
    return pl.pallas_call(
        _attn_meta_kernel,
        grid=(DA_HEADS,),
        in_specs=[smem, pl.BlockSpec((DA_HEAD_DIM, MP), lambda h: (h, 0)), head, head,
                  pl.BlockSpec((None, MP, MP), lambda h: (h, 0, 0)),
                  pl.BlockSpec((DA_HEAD_DIM, MP), lambda h: (0, 0))],
        out_specs=head,
        out_shape=jax.ShapeDtypeStruct((MP, DA_WIDTH), bf16),
        scratch_shapes=[pltpu.VMEM((MP, 2 * MP), bf16), pltpu.VMEM((V_EXT, 2 * MP), f32)],
        name="attn_meta",
    )(lam1, qm, km, vm, bmm, g_tile)


GDN_R = DN_HEADS * CHUNK
GDN_LEVELS = (2, 4, 8, 16, 32)
GDN_QKV = 3 * DN_WIDTH
GDN_NCH = GDN_ROWS // CHUNK
GDN_NSEQ = 4


def _gdn_consts():
    r = np.arange(GDN_R)[:, None]
    c = np.arange(GDN_R)[None, :]
    same = (r // CHUNK) == (c // CHUNK)
    bd = np.stack([same & (r >= c), same & (r > c), same]).astype(np.float32)
    i = np.arange(CHUNK)[:, None]
    j = np.arange(GDN_R)[None, :] % CHUNK
    lv = [i == j, (i > j) & ((i // 2) == (j // 2))]
    for s in GDN_LEVELS:
        lv.append(((i // (2 * s)) == (j // (2 * s))) & ((i & s) != 0) & ((j & s) == 0))
    cat = np.stack(lv).astype(np.float32)
    rr = np.arange(GDN_ROWS)[:, None]
    cc = np.arange(GDN_ROWS)[None, :]
    lcum = (((rr // CHUNK) == (cc // CHUNK)) & (rr >= cc)).astype(np.float32)
    sel = np.zeros((2, LANES, DN_HEADS * LANES), np.float32)
    for h in range(DN_HEADS):
        sel[0, h, h * LANES:(h + 1) * LANES] = 1.0
        sel[1, DN_HEADS + h, h * LANES:(h + 1) * LANES] = 1.0
    return bd, cat, lcum, sel


def _split3(x):
    hi = x.astype(bf16)
    r1 = x - hi.astype(f32)
    mid = r1.astype(bf16)
    lo = (r1 - mid.astype(f32)).astype(bf16)
    return hi, mid, lo


def _gdn_block(n_per_seq, xq, xk, xv, ba, z, s_refs, alog_ref, dtb_ref, ng_ref, bd_ref,
               cat_ref, lcum_ref, sel_ref):
    n_seq = len(s_refs)
    n = n_seq * n_per_seq
    seq_rows = n_per_seq * CHUNK
    heads = range(DN_HEADS)
    chunks = range(n)
    dot = functools.partial(jnp.dot, preferred_element_type=f32)

    def stack(a):
        return jnp.concatenate(
            [a[c * CHUNK:(c + 1) * CHUNK, h * LANES:(h + 1) * LANES]
             for c in chunks for h in heads], axis=0)

    qn_b, kn_b = stack(xq), stack(xk)
    qn, kn, vs = qn_b.astype(f32), kn_b.astype(f32), stack(xv).astype(f32)
    blk = lambda c: slice(c * GDN_R, (c + 1) * GDN_R)

    beta_t = _sigmoid(ba)
    xg = ba + dtb_ref[...]
    softplus = jnp.maximum(xg, 0.0) + jnp.log1p(jnp.exp(-jnp.abs(xg)))
    g_t = -jnp.exp(alog_ref[...]) * softplus
    lcum = lcum_ref[0:seq_rows, 0:seq_rows]
    g_parts = _split3(g_t)[:2]
    gcum = jnp.concatenate(
        [sum(dot(lcum, part[q * seq_rows:(q + 1) * seq_rows]) for part in g_parts)
         for q in range(n_seq)], axis=0)
    g_rep = stack(sum(dot(part, sel_ref[1]) for part in _split3(gcum)[:2]))
    b_rep = stack(sum(dot(part, sel_ref[0]) for part in _split3(beta_t)[:2]))
    g_end = jnp.concatenate(
        [jnp.broadcast_to(g_rep[(b + 1) * CHUNK - 1:(b + 1) * CHUNK, :], (CHUNK, LANES))
         for b in range(n * DN_HEADS)], axis=0)
    exp_g = jnp.exp(g_rep)
    rhs = jnp.concatenate([vs * b_rep, kn * (b_rep * exp_g)], axis=1).astype(bf16)
    q_g = qn * exp_g
    k_g = (kn * jnp.exp(g_end - g_rep)).astype(bf16)
    g_last = jnp.exp(g_end)

    tri = bd_ref[0] > 0.5
    block_b = bd_ref[2].astype(bf16)
    both = lambda a: jnp.concatenate([a, a], axis=1)
    decay, m_b, x_cat = [], [], []
    for c in chunks:
        g_row = g_rep[blk(c)].T[0:1, :]
        dec = jnp.exp(jnp.where(tri, both(g_rep[blk(c)]) - g_row, -jnp.inf))
        m = bd_ref[1] * both(b_rep[blk(c)]) * _dot_nt(kn_b[blk(c)], kn_b[blk(c)]) * dec
        m_cat = sum(m[h * CHUNK:(h + 1) * CHUNK] for h in heads)
        decay.append(dec)
        m_b.append(m.astype(bf16))
        x_cat.append(cat_ref[0] - m_cat * cat_ref[1])

    def to_bd(x):
        return jnp.concatenate([x.astype(bf16)] * DN_HEADS, axis=0) * block_b

    for lvl in range(len(GDN_LEVELS)):
        ys = [dot(x_cat[c].astype(bf16), m_b[c]) for c in chunks]
        zs = [dot(ys[c].astype(bf16), to_bd(x_cat[c])) for c in chunks]
        x_cat = [x_cat[c] - zs[c] * cat_ref[2 + lvl] for c in chunks]
    sol = [dot(to_bd(x_cat[c]), rhs[blk(c)]).astype(bf16) for c in chunks]
    a_qk = [(_dot_nt(qn_b[blk(c)], kn_b[blk(c)]) * decay[c]).astype(bf16) for c in chunks]
    a_uw = [dot(a_qk[c], sol[c]) for c in chunks]
    hrows = lambda h: slice(h * CHUNK, (h + 1) * CHUNK)
    k_uw = [[_dot_tn(k_g[blk(c)][hrows(h)], sol[c][hrows(h)]) for h in heads] for c in chunks]

    out_rows = [None] * n
    for step in range(n_per_seq):
        for q in range(n_seq):
            c = q * n_per_seq + step
            s_ref = s_refs[q]
            out_cols = []
            for h in heads:
                s_old = s_ref[h]
                s_b = s_old.astype(bf16)
                q_eff = (q_g[blk(c)][hrows(h)] - a_uw[c][hrows(h), DN_DV:]).astype(bf16)
                o = a_uw[c][hrows(h), :DN_DV] + dot(q_eff, s_b)
                s_ref[h] = (s_old * g_last[blk(c)][hrows(h)][0:1, :] + k_uw[c][h][:, :DN_DV]
                            - dot(k_uw[c][h][:, DN_DV:].astype(bf16), s_b))
                o = o * lax.rsqrt(jnp.mean(o * o, axis=-1, keepdims=True) + EPS) * ng_ref[...]
                zh = z[c * CHUNK:(c + 1) * CHUNK, h * DN_DV:(h + 1) * DN_DV].astype(f32)
                out_cols.append(o * (zh * _sigmoid(zh)))
            out_rows[c] = jnp.concatenate(out_cols, axis=1)
    return jnp.concatenate(out_rows, axis=0)


def _gdn_kernel(dq_ref, dk_ref, dv_ref, dz_ref, ba_ref, mq_ref, mk_ref, mv_ref, mz_ref, mba_ref,
                alog_ref, dtb_ref, ng_ref, bd_ref, cat_ref, lcum_ref, sel_ref,
                o_ref, om_ref, s_ref):
    consts = (alog_ref, dtb_ref, ng_ref, bd_ref, cat_ref, lcum_ref, sel_ref)
    s_refs = [s_ref.at[q] for q in range(GDN_NSEQ)]

    @pl.when(pl.program_id(1) == 0)
    def _():
        s_ref[0] = jnp.zeros(s_ref.shape[1:], f32)
        om_ref[...] = _gdn_block(1, mq_ref[...], mk_ref[...], mv_ref[...], mba_ref[...],
                                 mz_ref[...], s_refs[:1], *consts).astype(om_ref.dtype)
        for q in range(1, GDN_NSEQ):
            s_ref[q] = s_ref[0]

    merge = lambda ref: ref[...].reshape(GDN_NSEQ * GDN_ROWS, ref.shape[-1])
    out = _gdn_block(GDN_NCH, merge(dq_ref), merge(dk_ref), merge(dv_ref), merge(ba_ref),
                     merge(dz_ref), s_refs, *consts)
    o_ref[...] = out.reshape(o_ref.shape).astype(o_ref.dtype)


def _gdn(dq, dk, dv, dz, ba, mq, mk, mv, mz, mba, alog_row, dtb_row, ng_row):
    nb, s_len, _ = dq.shape
    assert nb % GDN_NSEQ == 0
    row = lambda w: pl.BlockSpec((GDN_NSEQ, GDN_ROWS, w), lambda b, t: (b, t, 0))
    mrow = lambda w: pl.BlockSpec((CHUNK, w), lambda b, t: (MP // CHUNK - 1, 0))
    bd, cat, lcum, sel = _gdn_consts()
    bd, cat = jnp.asarray(bd), jnp.asarray(cat)
    lcum, sel = jnp.asarray(lcum).astype(bf16), jnp.asarray(sel).astype(bf16)
    return pl.pallas_call(
        _gdn_kernel,
        grid=(nb // GDN_NSEQ, s_len // GDN_ROWS),
        in_specs=[row(DN_WIDTH)] * 4 + [row(LANES)] + [mrow(DN_WIDTH)] * 4 + [mrow(LANES)] + [
            _const_spec((1, LANES)), _const_spec((1, LANES)),
            _const_spec((1, DN_DV)), _const_spec(bd.shape), _const_spec(cat.shape),
            _const_spec(lcum.shape), _const_spec(sel.shape)],
        out_specs=[row(DN_WIDTH), pl.BlockSpec((None, CHUNK, DN_WIDTH), lambda b, t: (b, 0, 0))],
        out_shape=[jax.ShapeDtypeStruct((nb, s_len, DN_WIDTH), bf16),
                   jax.ShapeDtypeStruct((nb // GDN_NSEQ, CHUNK, DN_WIDTH), bf16)],
        scratch_shapes=[pltpu.VMEM((GDN_NSEQ, DN_HEADS, DN_DK, DN_DV), f32)],
        compiler_params=pltpu.CompilerParams(
            dimension_semantics=("parallel", "arbitrary"), vmem_limit_bytes=VMEM_LIMIT),
        name="gdn",
    )(dq, dk, dv, dz, ba, mq, mk, mv, mz, mba, alog_row, dtb_row, ng_row, bd, cat, lcum, sel)


def _mix_and_norm(x, oda, odn, wout_ref, g2_ref):
    mix = jnp.concatenate([oda, odn], axis=1)
    h2 = x + jnp.dot(mix, wout_ref[...], preferred_element_type=f32)
    u2 = h2 * lax.rsqrt(jnp.mean(h2 * h2, axis=-1, keepdims=True) + EPS) * g2_ref[...]
    return h2, u2.astype(bf16)


def _ffn_halo_kernel(x_ref, oda_ref, odn_ref, wout_ref, g2_ref, wup_ref, halo_ref):
    _, u2 = _mix_and_norm(x_ref[...], oda_ref[...], odn_ref[...], wout_ref, g2_ref)
    halo_ref[...] = jnp.dot(u2, wup_ref[...], preferred_element_type=f32)


def _ffn_kernel(x_ref, oda_ref, odn_ref, halo_ref, wout_ref, g2_ref, wup_ref, cw_ref, cb_ref,
                wdown_ref, gf_ref, o_ref, *scratch):
    tm = FFN_TM
    carries = scratch[0:FFN_TILES + 1]
    accs = scratch[FFN_TILES + 1:2 * FFN_TILES + 1]
    hbufs = scratch[2 * FFN_TILES + 1:]
    t = pl.program_id(1)

    @pl.when(t == 0)
    def _():
        carries[0][...] = halo_ref[...]

    @pl.when(t > 0)
    def _():
        carries[0][...] = carries[FFN_TILES][...]

    n_chunks = D_FF // FFN_CW
    stages = [(a, c) for a in range(FFN_TILES) for c in range(n_chunks)]
    col_pair = lambda c: (slice(c * FFN_CW, (c + 1) * FFN_CW),
                          slice(D_FF + c * FFN_CW, D_FF + (c + 1) * FFN_CW))
    buf_pair = lambda s: hbufs[2 * (s % FFN_NBUF):2 * (s % FFN_NBUF) + 2]
    rows = lambda a: slice(a * tm, (a + 1) * tm)
    normed = {}

    def up_part(s):
        a, c = stages[s]
        if c == 0:
            normed[a] = _mix_and_norm(x_ref[rows(a), :], oda_ref[rows(a), :],
                                      odn_ref[rows(a), :], wout_ref, g2_ref)
        u2 = normed[a][1]
        for cols, hbuf in zip(col_pair(c), buf_pair(s)):
            hbuf[0:SUBLANES, :] = carries[a][:, cols]
            hup = jnp.dot(u2, wup_ref[:, cols], preferred_element_type=f32)
            hbuf[SUBLANES:SUBLANES + tm, :] = hup
            carries[a + 1][:, cols] = hup[tm - SUBLANES:, :]

    def conv_part(cols, hbuf):
        y = cb_ref[:, cols]
        for d in range(FFN_CONV):
            y = y + cw_ref[FFN_CONV - 1 - d:FFN_CONV - d, cols] * hbuf[SUBLANES - d:SUBLANES - d + tm, :]
        return y

    for s in range(FFN_AHEAD):
        up_part(s)
    for s, (a, c) in enumerate(stages):
        if s + FFN_AHEAD < len(stages):
            up_part(s + FFN_AHEAD)
        gate, val = (conv_part(cols, hbuf) for cols, hbuf in zip(col_pair(c), buf_pair(s)))
        act = (gate * _sigmoid(gate) * val).astype(bf16)
        part = jnp.dot(act, wdown_ref[c * FFN_CW:(c + 1) * FFN_CW, :], preferred_element_type=f32)
        if c == 0:
            accs[a][...] = part
        else:
            accs[a][...] += part
        if c == n_chunks - 1:
            y = normed[a][0] + accs[a][...]
            o_ref[rows(a), :] = (y * lax.rsqrt(jnp.mean(y * y, axis=-1, keepdims=True) + EPS)
                                 * gf_ref[...])


def _ffn_halo(xm, odam, odnm, w_out, g2, w_up):
    return pl.pallas_call(
        _ffn_halo_kernel,
        out_shape=jax.ShapeDtypeStruct((HALO_ROWS, 2 * D_FF), f32),
        compiler_params=pltpu.CompilerParams(vmem_limit_bytes=VMEM_LIMIT),
        name="ffn_halo",
    )(xm, odam, odnm, w_out, g2, w_up)


def _ffn(x, oda, odn, halo, w_out, g2, w_up, conv_w, conv_b, w_down, gf):
    nb, s_len, _ = x.shape
    tm = FFN_TM
    step_rows = FFN_TILES * tm
    row = lambda w: pl.BlockSpec((None, step_rows, w), lambda b, t: (b, t, 0))
    return pl.pallas_call(
        _ffn_kernel,
        grid=(nb, s_len // step_rows),
        in_specs=[row(D_MODEL), row(DA_WIDTH), row(DN_WIDTH),
                  _const_spec((SUBLANES, 2 * D_FF)), _const_spec((D_MODEL, D_MODEL)),
                  _const_spec((1, D_MODEL)), _const_spec((D_MODEL, 2 * D_FF)),
                  _const_spec((FFN_CONV, 2 * D_FF)), _const_spec((1, 2 * D_FF)),
                  _const_spec((D_FF, D_MODEL)), _const_spec((1, D_MODEL))],
        out_specs=row(D_MODEL),
        out_shape=jax.ShapeDtypeStruct((nb, s_len, D_MODEL), f32),
        scratch_shapes=([pltpu.VMEM((SUBLANES, 2 * D_FF), f32)] * (FFN_TILES + 1)
                        + [pltpu.VMEM((tm, D_MODEL), f32)] * FFN_TILES
                        + [pltpu.VMEM((SUBLANES + tm, FFN_CW), f32)] * (2 * FFN_NBUF)),
        compiler_params=pltpu.CompilerParams(
            dimension_semantics=("parallel", "arbitrary"), vmem_limit_bytes=VMEM_LIMIT),
        name="ffn",
    )(x, oda, odn, halo, w_out, g2, w_up, conv_w, conv_b, w_down, gf)


def kernel(x, meta_tokens, rel_bias, norm1_g, w_in, da_lambda, da_subln_g, dn_conv_w, dn_A_log,
           dn_dt_bias, dn_norm_g, w_out, norm2_g, w_up, ffn_conv_w, ffn_conv_b, w_down,
           final_norm_g):
    nb, s_len, _ = x.shape
    assert s_len % TQ == 0 and s_len % GDN_ROWS == 0 and s_len % (FFN_TILES * FFN_TM) == 0
    assert s_len % PROJ_TM == 0 and s_len % VT_CHUNK == 0
    w_all = jnp.pad(w_in[0], ((0, 0), (0, LANES - 2 * DN_HEADS))).astype(bf16)
    w_out_b = w_out[0].astype(bf16)
    w_up_b = w_up[0].astype(bf16)
    w_down_b = w_down[0].astype(bf16)
    g1 = norm1_g[0].reshape(1, D_MODEL).astype(f32)
    g2 = norm2_g[0].reshape(1, D_MODEL).astype(f32)
    gf = final_norm_g.reshape(1, D_MODEL).astype(f32)
    meta_pad = jnp.pad(meta_tokens.astype(x.dtype), ((MP - N_META, 0), (0, 0)))

    bnear, bmeta, bmm, lam_tile = _bias_tiles(rel_bias, da_lambda[0])
    lam1 = lam_tile[0, :1]

    conv_w = dn_conv_w[0].astype(f32)
    no_halo = jnp.zeros((SUBLANES, GDN_QKV), f32)
    mq, mk, mv, mdq, mdk, mdv, mdz, mba, mtail = [
        a[0] for a in _proj(meta_pad[None], no_halo, g1, w_all, conv_w, MP)]
    q, k, v, dq, dk, dv, dz, ba, _ = _proj(x, mtail, g1, w_all, conv_w, PROJ_TM)

    subln = da_subln_g[0].astype(f32)
    o_da = _attention(lam1, q, k, v, mk, mv, bnear, bmeta,
                      jnp.broadcast_to(subln[:, None], (DA_HEAD_DIM, TQ)))
    o_da_m = _attention_meta(lam1, mq, mk, mv, bmm,
                             jnp.broadcast_to(subln[:, None], (DA_HEAD_DIM, MP)))

    gate_row = lambda p: jnp.pad(p[0].astype(f32), (DN_HEADS, LANES - 2 * DN_HEADS)).reshape(1, LANES)
    o_dn, o_dn_m = _gdn(dq, dk, dv, dz, ba, mdq, mdk, mdv, mdz, mba,
                        gate_row(dn_A_log), gate_row(dn_dt_bias),
                        dn_norm_g[0].reshape(1, DN_DV).astype(f32))

    halo = _ffn_halo(meta_pad[MP - HALO_ROWS:], o_da_m[MP - HALO_ROWS:],
                     o_dn_m[0, CHUNK - HALO_ROWS:], w_out_b, g2, w_up_b)[HALO_ROWS - SUBLANES:]
    return _ffn(x, o_da, o_dn, halo, w_out_b, g2, w_up_b, ffn_conv_w[0].astype(f32),
                ffn_conv_b[0].reshape(1, 2 * D_FF).astype(f32), w_down_b, gf)
```

```python
import functools
import math

import numpy as np
import jax
import jax.numpy as jnp
from jax import lax
from jax.experimental import pallas as pl
from jax.experimental.pallas import tpu as pltpu

f32 = jnp.float32
bf16 = jnp.bfloat16

D_MODEL = 1024
CHUNK = 64
N_META = 16
EPS = 1e-6
NEG_INF = -1e30
LAMBDA_INIT = 0.8 - 0.6 * math.exp(-0.3 * 0)
LOG2E = math.log2(math.e)

DA_HEADS = 4
DA_HEAD_DIM = 128
DA_HALF = 64
DA_WIDTH = DA_HEADS * DA_HEAD_DIM
DN_HEADS = 4
DN_DK = 128
DN_DV = 128
DN_WIDTH = DN_HEADS * DN_DV
DN_CONV = 4
N_BUCKETS = 32
MAX_DISTANCE = 128
D_FF = 2816
FFN_CONV = 3

N_MAIN = 7 * 512
LANES = 128
SUBLANES = 8
MP = 128
TQ = 256
TK = 256
V_EXT = DA_HEAD_DIM + 16
PROJ_TM = 512
GDN_ROWS = 256
FFN_TILES = 1
FFN_TM = 256
FFN_CW = 256
FFN_AHEAD = 10
FFN_NBUF = FFN_AHEAD + 1
HALO_ROWS = 16
VMEM_LIMIT = 52 * 1024 * 1024


def _const_spec(shape):
    nd = len(shape)
    return pl.BlockSpec(shape, lambda *_: (0,) * nd, pipeline_mode=pl.Buffered(1))


def _prep_kernel(far_ref, table_ref, lam_in_ref, bnear_in, bmeta_in, bmm_in,
                 bnear_out, bmeta_out, bmm_out, lam_out):
    h = pl.program_id(0)
    c_far = table_ref[far_ref[0], h]

    def lookup(bkt):
        out = jnp.full(bkt.shape, NEG_INF, f32)
        for b in range(N_BUCKETS):
            out = jnp.where(bkt == b, (table_ref[b, h] - c_far) * LOG2E, out)
        return out

    for i in range(2):
        bnear_out[i] = lookup(bnear_in[i])
        bmeta_out[i] = lookup(bmeta_in[i])
    bmm_out[...] = lookup(bmm_in[...])
    lv = lam_in_ref[...]
    s1 = jnp.sum(lv[0:1] * lv[1:2], axis=-1, keepdims=True)
    s2 = jnp.sum(lv[2:3] * lv[3:4], axis=-1, keepdims=True)
    lam = jnp.exp(s1) - jnp.exp(s2) + LAMBDA_INIT
    lam_out[...] = jnp.broadcast_to(lam, lam_out.shape)


def _t5_bucket(rel):
    nb = N_BUCKETS // 2
    max_exact = nb // 2
    ret = jnp.where(rel > 0, nb, 0)
    n = jnp.abs(rel)
    nf = jnp.maximum(n, 1).astype(jnp.float32)
    large = max_exact + (jnp.log(nf / max_exact) / math.log(MAX_DISTANCE / max_exact)
                         * (nb - max_exact)).astype(jnp.int32)
    large = jnp.minimum(large, nb - 1)
    return ret + jnp.where(n < max_exact, n, large)


def _bias_tiles(rel_bias, da_lambda):
    r = jnp.arange(TK, dtype=jnp.int32)[:, None]
    c = jnp.arange(TQ, dtype=jnp.int32)[None, :]
    diag = jnp.where((r // CHUNK) <= (c // CHUNK), _t5_bucket(r - c), -1)
    prev = _t5_bucket(r - c - TK)
    bnear = jnp.stack([diag, prev]).astype(jnp.int32)
    far = _t5_bucket(jnp.full((1,), -(TK + 1), jnp.int32)).astype(jnp.int32)
    rm = jnp.arange(MP, dtype=jnp.int32)[:, None]
    valid = rm >= (MP - N_META)
    kpos = rm - (MP - N_META)
    m0 = jnp.where(valid, _t5_bucket(kpos - (N_META + c)), -1)
    m1 = jnp.where(valid, jnp.broadcast_to(far[0], (MP, TQ)), -1)
    bmeta = jnp.stack([m0, m1]).astype(jnp.int32)
    cm = jnp.arange(MP, dtype=jnp.int32)[None, :]
    bmm = jnp.where(valid, _t5_bucket(rm - cm), -1).astype(jnp.int32)

    smem = pl.BlockSpec(memory_space=pltpu.SMEM)
    return pl.pallas_call(
        _prep_kernel,
        grid=(DA_HEADS,),
        in_specs=[smem, smem,
                  pl.BlockSpec((4, DA_HALF), lambda h: (0, 0)),
                  pl.BlockSpec((2, TK, TQ), lambda h: (0, 0, 0)),
                  pl.BlockSpec((2, MP, TQ), lambda h: (0, 0, 0)),
                  pl.BlockSpec((MP, MP), lambda h: (0, 0))],
        out_specs=[pl.BlockSpec((None, 2, TK, TQ), lambda h: (h, 0, 0, 0)),
                   pl.BlockSpec((None, 2, MP, TQ), lambda h: (h, 0, 0, 0)),
                   pl.BlockSpec((None, MP, MP), lambda h: (h, 0, 0)),
                   pl.BlockSpec((SUBLANES, LANES), lambda h: (0, 0))],
        out_shape=[jax.ShapeDtypeStruct((DA_HEADS, 2, TK, TQ), f32),
                   jax.ShapeDtypeStruct((DA_HEADS, 2, MP, TQ), f32),
                   jax.ShapeDtypeStruct((DA_HEADS, MP, MP), f32),
                   jax.ShapeDtypeStruct((SUBLANES, LANES), f32)],
        name="prep",
    )(far, rel_bias.astype(f32), da_lambda.astype(f32), bnear, bmeta, bmm)


def _sigmoid(x):
    return 1.0 / (1.0 + jnp.exp(-x))


def _proj_kernel(x_ref, halo_ref, g_ref, w_ref, cw_ref, q_ref, k_ref, v_ref, dq_ref,
                 dk_ref, dv_ref, dz_ref, ba_ref, tail_ref, carry_ref, carry_next, u_ref):
    tm = x_ref.shape[0]
    t = pl.program_id(1)

    @pl.when(t == 0)
    def _():
        carry_ref[...] = halo_ref[...]

    @pl.when(t > 0)
    def _():
        carry_ref[...] = carry_next[...]

    x = x_ref[...]
    ms = jnp.mean(x * x, axis=-1, keepdims=True)
    u_ref[...] = (x * lax.rsqrt(ms + EPS) * g_ref[...]).astype(bf16)
    outs = (q_ref, k_ref, v_ref, dq_ref, dk_ref, dv_ref, dz_ref)
    n_out = len(outs)
    project = lambda j: jnp.dot(u_ref[...], w_ref[:, j * 512:(j + 1) * 512],
                                preferred_element_type=f32)

    def conv_silu(j, r):
        cols = slice((j - 3) * DN_WIDTH, (j - 2) * DN_WIDTH)
        carry_next[:, cols] = r[tm - SUBLANES:, :]
        ext = jnp.concatenate([carry_ref[:, cols], r], axis=0)
        y = cw_ref[0:1, cols] * ext
        for d in range(1, DN_CONV):
            y = cw_ref[d:d + 1, cols] * ext + pltpu.roll(y, 1, 0)
        y = y[SUBLANES:, :]
        return y * _sigmoid(y)

    def l2norm_heads(y, scale):
        parts = []
        for h in range(DN_HEADS):
            yh = y[:, h * DN_DK:(h + 1) * DN_DK]
            parts.append(yh * (lax.rsqrt(jnp.sum(yh * yh, axis=-1, keepdims=True) + EPS) * scale))
        return jnp.concatenate(parts, axis=1)

    order = (3, 0, 4, 1, 5, 2, 6)
    ahead = 2
    pending = [project(j) for j in order[:ahead]]
    for pos, j in enumerate(order):
        r = pending.pop(0)
        if pos + ahead < n_out:
            pending.append(project(order[pos + ahead]))
        elif pos + ahead == n_out:
            ba_ref[...] = jnp.dot(u_ref[...], w_ref[:, N_MAIN:N_MAIN + LANES],
                                  preferred_element_type=f32)
        if j == 0:
            r = r * (DA_HALF ** -0.5 * LOG2E)
        elif j in (3, 4):
            r = l2norm_heads(conv_silu(j, r), DN_DK ** -0.5 if j == 3 else 1.0)
        elif j == 5:
            r = conv_silu(j, r)
        if j in (0, 2):
            r = r.T
        outs[j][...] = r.astype(outs[j].dtype)
    tail_ref[...] = carry_next[...]


def _proj(x3, halo, g1, w_all, conv_w, tm):
    nb, rows, _ = x3.shape
    row_spec = lambda w: pl.BlockSpec((None, tm, w), lambda b, t: (b, t, 0))
    plain = jax.ShapeDtypeStruct((nb, rows, 512), bf16)
    transposed = jax.ShapeDtypeStruct((nb, 512, rows), bf16)
    outs = [transposed, plain, transposed] + [plain] * 4 + [
        jax.ShapeDtypeStruct((nb, rows, LANES), f32),
        jax.ShapeDtypeStruct((nb, SUBLANES, GDN_QKV), f32)]
    t_spec = pl.BlockSpec((None, 512, tm), lambda b, t: (b, 0, t))
    return pl.pallas_call(
        _proj_kernel,
        grid=(nb, rows // tm),
        in_specs=[row_spec(D_MODEL), _const_spec((SUBLANES, GDN_QKV)), _const_spec((1, D_MODEL)),
                  _const_spec((D_MODEL, N_MAIN + LANES)), _const_spec((DN_CONV, GDN_QKV))],
        out_specs=[t_spec, row_spec(512), t_spec]
                  + [row_spec(512)] * 4 + [row_spec(LANES),
                   pl.BlockSpec((None, SUBLANES, GDN_QKV), lambda b, t: (b, 0, 0))],
        out_shape=outs,
        scratch_shapes=([pltpu.VMEM((SUBLANES, GDN_QKV), f32)] * 2
                        + [pltpu.VMEM((tm, D_MODEL), bf16)]),
        compiler_params=pltpu.CompilerParams(
            dimension_semantics=("parallel", "arbitrary"), vmem_limit_bytes=VMEM_LIMIT),
        name="proj",
    )(x3, halo, g1, w_all, conv_w)


def _dot_nt(a, b):
    return lax.dot_general(a, b, (((1,), (1,)), ((), ())), preferred_element_type=f32)


def _dot_tn(a, b):
    return lax.dot_general(a, b, (((0,), (0,)), ((), ())), preferred_element_type=f32)


def _stack_components(qt):
    zero = jnp.zeros((DA_HALF, qt.shape[1]), qt.dtype)
    q0 = jnp.concatenate([qt[:DA_HALF], zero], axis=0)
    q1 = jnp.concatenate([zero, qt[DA_HALF:]], axis=0)
    return jnp.concatenate([q0, q1], axis=1)


def _colmax(s):
    return jnp.max(s, axis=0, keepdims=True)


def _values_t_ext(vt):
    return jnp.concatenate([vt, jnp.ones((V_EXT - DA_HEAD_DIM, vt.shape[1]), vt.dtype)], axis=0)


def _softmax_step(stats, s, smax, vt_pend, p_ref, acc_ref):
    m, alpha_pend = stats
    pv = jnp.dot(vt_pend, p_ref[...], preferred_element_type=f32)
    m_new = jnp.maximum(m, smax)
    alpha = jnp.exp2(m - m_new)
    p = jnp.exp2((s - m_new).astype(bf16))
    acc_ref[...] = alpha_pend * acc_ref[...] + pv
    return (m_new, alpha), p


def _attn_finish(stats, vt_pend, p_ref, acc_ref, lam, g, tq):
    m, alpha_pend = stats
    acc = alpha_pend * acc_ref[...] + jnp.dot(vt_pend, p_ref[...], preferred_element_type=f32)
    l = acc[DA_HEAD_DIM:DA_HEAD_DIM + 1, :]
    acc = acc[:DA_HEAD_DIM, :]
    o = acc[:, :tq] / l[:, :tq] - lam * (acc[:, tq:] / l[:, tq:])
    ms = jnp.mean(o * o, axis=0, keepdims=True)
    y = o * lax.rsqrt(ms + EPS) * g * (1.0 - LAMBDA_INIT)
    return y.T


def _attn_init(tq, p_ref, acc_ref):
    p_ref[...] = jnp.zeros(p_ref.shape, bf16)
    acc_ref[...] = jnp.zeros(acc_ref.shape, f32)
    return (jnp.full((1, 2 * tq), -3e38, f32), jnp.ones((1, 2 * tq), f32))


def _both(b):
    return jnp.concatenate([b, b], axis=1)


def _attn_kernel(lam_ref, q_ref, k_ref, v_ref, km_ref, vm_ref, bnear_ref, bmeta_ref, g_ref,
                 o_ref, *scratch):
    i = pl.program_id(1)
    n_far = jnp.maximum(i - 1, 0)
    heads = range(DA_HEADS)
    cols = lambda h: slice(h * DA_HEAD_DIM, (h + 1) * DA_HEAD_DIM)
    qz = [_stack_components(q_ref[cols(h), :]) for h in heads]

    def scores(h, j):
        start = pl.multiple_of(j * TK, TK)
        return jnp.dot(k_ref[pl.ds(start, TK), cols(h)], qz[h], preferred_element_type=f32)

    s_even = scratch[0:DA_HEADS]
    s_odd = scratch[DA_HEADS:2 * DA_HEADS]
    p_bufs = scratch[2 * DA_HEADS:3 * DA_HEADS]
    accs = scratch[3 * DA_HEADS:4 * DA_HEADS]

    def values(h, j):
        start = pl.multiple_of(jnp.maximum(j, 0) * TK, TK)
        return _values_t_ext(v_ref[cols(h), pl.ds(start, TK)])

    def consume(j, bufs, stats, smax):
        new_stats = []
        for h in heads:
            st, p = _softmax_step(stats[h], bufs[h][...], smax[h], values(h, j - 1),
                                  p_bufs[h], accs[h])
            p_bufs[h][...] = p
            new_stats.append(st)
        return tuple(new_stats)

    def produce(j, bufs):
        new_smax = []
        for h in heads:
            s_next = scores(h, j)
            bufs[h][...] = s_next
            new_smax.append(_colmax(s_next))
        return tuple(new_smax)

    stats = tuple(_attn_init(TQ, p_bufs[h], accs[h]) for h in heads)
    smax_even = produce(0, s_even)
    smax_odd = produce(1, s_odd)

    def pair_body(jp, carry):
        stats, smax_even, smax_odd = carry
        j = 2 * jp
        stats = consume(j, s_even, stats, smax_even)
        smax_even = produce(j + 2, s_even)
        stats = consume(j + 1, s_odd, stats, smax_odd)
        smax_odd = produce(j + 3, s_odd)
        return stats, smax_even, smax_odd

    n_pairs = n_far // 2
    stats, smax_even, _ = lax.fori_loop(0, n_pairs, pair_body, (stats, smax_even, smax_odd))

    def single_body(_, stats):
        stats = consume(2 * n_pairs, s_even, stats, smax_even)
        for h in heads:
            s_even[h][...] = s_odd[h][...]
        return stats

    stats = lax.fori_loop(0, n_far - 2 * n_pairs, single_body, stats)
    gate = jnp.where(i >= 1, 0.0, NEG_INF).astype(f32)
    for h in heads:
        p_h, acc_h = p_bufs[h], accs[h]
        s = s_even[h][...] + _both(bnear_ref[h, 1] + gate)
        st, p = _softmax_step(stats[h], s, _colmax(s), values(h, n_far - 1), p_h, acc_h)
        p_h[...] = p
        s = scores(h, i) + _both(bnear_ref[h, 0])
        st, p = _softmax_step(st, s, _colmax(s), values(h, i - 1), p_h, acc_h)
        p_h[...] = p
        s = (jnp.dot(km_ref[:, cols(h)], qz[h], preferred_element_type=f32)
             + _both(bmeta_ref[h, jnp.minimum(i, 1)]))
        st, p = _softmax_step(st, s, _colmax(s), values(h, i), p_h, acc_h)
        p_meta = p_h.at[0:MP, :]
        p_meta[...] = p
        o_ref[:, cols(h)] = _attn_finish(st, _values_t_ext(vm_ref[cols(h), :]), p_meta, acc_h,
                                         lam_ref[0],
                                         g_ref[...], TQ).astype(o_ref.dtype)


def _attn_meta_kernel(lam_ref, q_ref, km_ref, vm_ref, bmm_ref, g_ref, o_ref, p_buf, acc_ref):
    qz = _stack_components(q_ref[...])
    vmt = _values_t_ext(vm_ref[...])
    stats = _attn_init(MP, p_buf, acc_ref)
    s = jnp.dot(km_ref[...], qz, preferred_element_type=f32) + _both(bmm_ref[...])
    stats, p = _softmax_step(stats, s, _colmax(s), vmt, p_buf, acc_ref)
    p_buf[...] = p
    o_ref[...] = _attn_finish(stats, vmt, p_buf, acc_ref, lam_ref[0], g_ref[...],
                              MP).astype(o_ref.dtype)


def _attention(lam1, q, k, v, km, vm, bnear, bmeta, g_tile):
    nb, s_len, _ = k.shape
    smem = pl.BlockSpec(memory_space=pltpu.SMEM)
    return pl.pallas_call(
        _attn_kernel,
        grid=(nb, s_len // TQ),
        in_specs=[smem,
                  pl.BlockSpec((None, DA_WIDTH, TQ), lambda b, i: (b, 0, i)),
                  pl.BlockSpec((None, s_len, DA_WIDTH), lambda b, i: (b, 0, 0)),
                  pl.BlockSpec((None, DA_WIDTH, s_len), lambda b, i: (b, 0, 0)),
                  _const_spec((MP, DA_WIDTH)), _const_spec((DA_WIDTH, MP)),
                  _const_spec((DA_HEADS, 2, TK, TQ)), _const_spec((DA_HEADS, 2, MP, TQ)),
                  _const_spec((DA_HEAD_DIM, TQ))],
        out_specs=pl.BlockSpec((None, TQ, DA_WIDTH), lambda b, i: (b, i, 0)),
        out_shape=jax.ShapeDtypeStruct((nb, s_len, DA_WIDTH), bf16),
        scratch_shapes=([pltpu.VMEM((TK, 2 * TQ), f32)] * (2 * DA_HEADS)
                        + [pltpu.VMEM((TK, 2 * TQ), bf16)] * DA_HEADS
                        + [pltpu.VMEM((V_EXT, 2 * TQ), f32)] * DA_HEADS),
        compiler_params=pltpu.CompilerParams(
            dimension_semantics=("parallel", "arbitrary"),
            vmem_limit_bytes=VMEM_LIMIT),
        name="attn",
    )(lam1, q, k, v, km, vm, bnear, bmeta, g_tile)


def _attention_meta(lam1, qm, km, vm, bmm, g_tile):
    smem = pl.BlockSpec(memory_space=pltpu.SMEM)
    head = pl.BlockSpec((MP, DA_HEAD_DIM), lambda h: (0, h))
    return pl.pallas_call(
        _attn_meta_kernel,
        grid=(DA_HEADS,),
        in_specs=[smem, pl.BlockSpec((DA_HEAD_DIM, MP), lambda h: (h, 0)), head,
                  pl.BlockSpec((DA_HEAD_DIM, MP), lambda h: (h, 0)),
                  pl.BlockSpec((None, MP, MP), lambda h: (h, 0, 0)),
                  pl.BlockSpec((DA_HEAD_DIM, MP), lambda h: (0, 0))],
        out_specs=head,
        out_shape=jax.ShapeDtypeStruct((MP, DA_WIDTH), bf16),
        scratch_shapes=[pltpu.VMEM((MP, 2 * MP), bf16), pltpu.VMEM((V_EXT, 2 * MP), f32)],
        name="attn_meta",
    )(lam1, qm, km, vm, bmm, g_tile)


GDN_R = DN_HEADS * CHUNK
GDN_LEVELS = (2, 4, 8, 16, 32)
GDN_QKV = 3 * DN_WIDTH
GDN_NCH = GDN_ROWS // CHUNK
GDN_NSEQ = 4


def _gdn_consts():
    r = np.arange(GDN_R)[:, None]
    c = np.arange(GDN_R)[None, :]
    same = (r // CHUNK) == (c // CHUNK)
    bd = np.stack([same & (r >= c), same & (r > c), same]).astype(np.float32)
    i = np.arange(CHUNK)[:, None]
    j = np.arange(GDN_R)[None, :] % CHUNK
    lv = [i == j, (i > j) & ((i // 2) == (j // 2))]
    for s in GDN_LEVELS:
        lv.append(((i // (2 * s)) == (j // (2 * s))) & ((i & s) != 0) & ((j & s) == 0))
    cat = np.stack(lv).astype(np.float32)
    rr = np.arange(GDN_ROWS)[:, None]
    cc = np.arange(GDN_ROWS)[None, :]
    lcum = (((rr // CHUNK) == (cc // CHUNK)) & (rr >= cc)).astype(np.float32)
    sel = np.zeros((2, LANES, DN_HEADS * LANES), np.float32)
    for h in range(DN_HEADS):
        sel[0, h, h * LANES:(h + 1) * LANES] = 1.0
        sel[1, DN_HEADS + h, h * LANES:(h + 1) * LANES] = 1.0
    return bd, cat, lcum, sel


def _split3(x):
    hi = x.astype(bf16)
    r1 = x - hi.astype(f32)
    mid = r1.astype(bf16)
    lo = (r1 - mid.astype(f32)).astype(bf16)
    return hi, mid, lo


def _gdn_block(n_per_seq, xq, xk, xv, ba, z, s_refs, alog_ref, dtb_ref, ng_ref, bd_ref,
               cat_ref, lcum_ref, sel_ref):
    n_seq = len(s_refs)
    n = n_seq * n_per_seq
    seq_rows = n_per_seq * CHUNK
    heads = range(DN_HEADS)
    chunks = range(n)
    dot = functools.partial(jnp.dot, preferred_element_type=f32)

    def stack(a):
        return jnp.concatenate(
            [a[c * CHUNK:(c + 1) * CHUNK, h * LANES:(h + 1) * LANES]
             for c in chunks for h in heads], axis=0)

    qn_b, kn_b = stack(xq), stack(xk)
    qn, kn, vs = qn_b.astype(f32), kn_b.astype(f32), stack(xv).astype(f32)
    blk = lambda c: slice(c * GDN_R, (c + 1) * GDN_R)

    beta_t = _sigmoid(ba)
    xg = ba + dtb_ref[...]
    softplus = jnp.maximum(xg, 0.0) + jnp.log1p(jnp.exp(-jnp.abs(xg)))
    g_t = -jnp.exp(alog_ref[...]) * softplus
    lcum = lcum_ref[0:seq_rows, 0:seq_rows]
    g_parts = _split3(g_t)[:2]
    gcum = jnp.concatenate(
        [sum(dot(lcum, part[q * seq_rows:(q + 1) * seq_rows]) for part in g_parts)
         for q in range(n_seq)], axis=0)
    g_rep = stack(sum(dot(part, sel_ref[1]) for part in _split3(gcum)[:2]))
    b_rep = stack(sum(dot(part, sel_ref[0]) for part in _split3(beta_t)[:2]))
    g_end = jnp.concatenate(
        [jnp.broadcast_to(g_rep[(b + 1) * CHUNK - 1:(b + 1) * CHUNK, :], (CHUNK, LANES))
         for b in range(n * DN_HEADS)], axis=0)
    exp_g = jnp.exp(g_rep)
    rhs = jnp.concatenate([vs * b_rep, kn * (b_rep * exp_g)], axis=1).astype(bf16)
    q_g = qn * exp_g
    k_g = (kn * jnp.exp(g_end - g_rep)).astype(bf16)
    g_last = jnp.exp(g_end)

    tri = bd_ref[0] > 0.5
    block_b = bd_ref[2].astype(bf16)
    both = lambda a: jnp.concatenate([a, a], axis=1)
    decay, m_b, x_cat = [], [], []
    for c in chunks:
        g_row = g_rep[blk(c)].T[0:1, :]
        dec = jnp.exp(jnp.where(tri, both(g_rep[blk(c)]) - g_row, -jnp.inf))
        m = bd_ref[1] * both(b_rep[blk(c)]) * _dot_nt(kn_b[blk(c)], kn_b[blk(c)]) * dec
        m_cat = sum(m[h * CHUNK:(h + 1) * CHUNK] for h in heads)
        decay.append(dec)
        m_b.append(m.astype(bf16))
        x_cat.append(cat_ref[0] - m_cat * cat_ref[1])

    def to_bd(x):
        return jnp.concatenate([x.astype(bf16)] * DN_HEADS, axis=0) * block_b

    for lvl in range(len(GDN_LEVELS)):
        ys = [dot(x_cat[c].astype(bf16), m_b[c]) for c in chunks]
        zs = [dot(ys[c].astype(bf16), to_bd(x_cat[c])) for c in chunks]
        x_cat = [x_cat[c] - zs[c] * cat_ref[2 + lvl] for c in chunks]
    sol = [dot(to_bd(x_cat[c]), rhs[blk(c)]).astype(bf16) for c in chunks]
    a_qk = [(_dot_nt(qn_b[blk(c)], kn_b[blk(c)]) * decay[c]).astype(bf16) for c in chunks]
    a_uw = [dot(a_qk[c], sol[c]) for c in chunks]
    hrows = lambda h: slice(h * CHUNK, (h + 1) * CHUNK)
    k_uw = [[_dot_tn(k_g[blk(c)][hrows(h)], sol[c][hrows(h)]) for h in heads] for c in chunks]

    out_rows = [None] * n
    for step in range(n_per_seq):
        for q in range(n_seq):
            c = q * n_per_seq + step
            s_ref = s_refs[q]
            out_cols = []
            for h in heads:
                s_old = s_ref[h]
                s_b = s_old.astype(bf16)
                q_eff = (q_g[blk(c)][hrows(h)] - a_uw[c][hrows(h), DN_DV:]).astype(bf16)
                o = a_uw[c][hrows(h), :DN_DV] + dot(q_eff, s_b)
                s_ref[h] = (s_old * g_last[blk(c)][hrows(h)][0:1, :] + k_uw[c][h][:, :DN_DV]
                            - dot(k_uw[c][h][:, DN_DV:].astype(bf16), s_b))
                o = o * lax.rsqrt(jnp.mean(o * o, axis=-1, keepdims=True) + EPS) * ng_ref[...]
                zh = z[c * CHUNK:(c + 1) * CHUNK, h * DN_DV:(h + 1) * DN_DV].astype(f32)
                out_cols.append(o * (zh * _sigmoid(zh)))
            out_rows[c] = jnp.concatenate(out_cols, axis=1)
    return jnp.concatenate(out_rows, axis=0)


def _gdn_kernel(dq_ref, dk_ref, dv_ref, dz_ref, ba_ref, mq_ref, mk_ref, mv_ref, mz_ref, mba_ref,
                alog_ref, dtb_ref, ng_ref, bd_ref, cat_ref, lcum_ref, sel_ref,
                o_ref, om_ref, s_ref):
    consts = (alog_ref, dtb_ref, ng_ref, bd_ref, cat_ref, lcum_ref, sel_ref)
    s_refs = [s_ref.at[q] for q in range(GDN_NSEQ)]

    @pl.when(pl.program_id(1) == 0)
    def _():
        s_ref[0] = jnp.zeros(s_ref.shape[1:], f32)
        om_ref[...] = _gdn_block(1, mq_ref[...], mk_ref[...], mv_ref[...], mba_ref[...],
                                 mz_ref[...], s_refs[:1], *consts).astype(om_ref.dtype)
        for q in range(1, GDN_NSEQ):
            s_ref[q] = s_ref[0]

    merge = lambda ref: ref[...].reshape(GDN_NSEQ * GDN_ROWS, ref.shape[-1])
    out = _gdn_block(GDN_NCH, merge(dq_ref), merge(dk_ref), merge(dv_ref), merge(ba_ref),
                     merge(dz_ref), s_refs, *consts)
    o_ref[...] = out.reshape(o_ref.shape).astype(o_ref.dtype)


def _gdn(dq, dk, dv, dz, ba, mq, mk, mv, mz, mba, alog_row, dtb_row, ng_row):
    nb, s_len, _ = dq.shape
    assert nb % GDN_NSEQ == 0
    row = lambda w: pl.BlockSpec((GDN_NSEQ, GDN_ROWS, w), lambda b, t: (b, t, 0))
    mrow = lambda w: pl.BlockSpec((CHUNK, w), lambda b, t: (MP // CHUNK - 1, 0))
    bd, cat, lcum, sel = _gdn_consts()
    bd, cat = jnp.asarray(bd), jnp.asarray(cat)
    lcum, sel = jnp.asarray(lcum).astype(bf16), jnp.asarray(sel).astype(bf16)
    return pl.pallas_call(
        _gdn_kernel,
        grid=(nb // GDN_NSEQ, s_len // GDN_ROWS),
        in_specs=[row(DN_WIDTH)] * 4 + [row(LANES)] + [mrow(DN_WIDTH)] * 4 + [mrow(LANES)] + [
            _const_spec((1, LANES)), _const_spec((1, LANES)),
            _const_spec((1, DN_DV)), _const_spec(bd.shape), _const_spec(cat.shape),
            _const_spec(lcum.shape), _const_spec(sel.shape)],
        out_specs=[row(DN_WIDTH), pl.BlockSpec((None, CHUNK, DN_WIDTH), lambda b, t: (b, 0, 0))],
        out_shape=[jax.ShapeDtypeStruct((nb, s_len, DN_WIDTH), bf16),
                   jax.ShapeDtypeStruct((nb // GDN_NSEQ, CHUNK, DN_WIDTH), bf16)],
        scratch_shapes=[pltpu.VMEM((GDN_NSEQ, DN_HEADS, DN_DK, DN_DV), f32)],
        compiler_params=pltpu.CompilerParams(
            dimension_semantics=("parallel", "arbitrary"), vmem_limit_bytes=VMEM_LIMIT),
        name="gdn",
    )(dq, dk, dv, dz, ba, mq, mk, mv, mz, mba, alog_row, dtb_row, ng_row, bd, cat, lcum, sel)


def _mix_and_norm(x, oda, odn, wout_ref, g2_ref):
    mix = jnp.concatenate([oda, odn], axis=1)
    h2 = x + jnp.dot(mix, wout_ref[...], preferred_element_type=f32)
    u2 = h2 * lax.rsqrt(jnp.mean(h2 * h2, axis=-1, keepdims=True) + EPS) * g2_ref[...]
    return h2, u2.astype(bf16)


def _ffn_halo_kernel(x_ref, oda_ref, odn_ref, wout_ref, g2_ref, wup_ref, halo_ref):
    _, u2 = _mix_and_norm(x_ref[...], oda_ref[...], odn_ref[...], wout_ref, g2_ref)
    halo_ref[...] = jnp.dot(u2, wup_ref[...], preferred_element_type=f32)


def _ffn_kernel(x_ref, oda_ref, odn_ref, halo_ref, wout_ref, g2_ref, wup_ref, cw_ref, cb_ref,
                wdown_ref, gf_ref, o_ref, *scratch):
    tm = FFN_TM
    carries = scratch[0:FFN_TILES + 1]
    accs = scratch[FFN_TILES + 1:2 * FFN_TILES + 1]
    hbufs = scratch[2 * FFN_TILES + 1:]
    t = pl.program_id(1)

    @pl.when(t == 0)
    def _():
        carries[0][...] = halo_ref[...]

    @pl.when(t > 0)
    def _():
        carries[0][...] = carries[FFN_TILES][...]

    n_chunks = D_FF // FFN_CW
    stages = [(a, c) for a in range(FFN_TILES) for c in range(n_chunks)]
    col_pair = lambda c: (slice(c * FFN_CW, (c + 1) * FFN_CW),
                          slice(D_FF + c * FFN_CW, D_FF + (c + 1) * FFN_CW))
    buf_pair = lambda s: hbufs[2 * (s % FFN_NBUF):2 * (s % FFN_NBUF) + 2]
    rows = lambda a: slice(a * tm, (a + 1) * tm)
    normed = {}

    def up_part(s):
        a, c = stages[s]
        if c == 0:
            normed[a] = _mix_and_norm(x_ref[rows(a), :], oda_ref[rows(a), :],
                                      odn_ref[rows(a), :], wout_ref, g2_ref)
        u2 = normed[a][1]
        for cols, hbuf in zip(col_pair(c), buf_pair(s)):
            hbuf[0:SUBLANES, :] = carries[a][:, cols]
            hup = jnp.dot(u2, wup_ref[:, cols], preferred_element_type=f32)
            hbuf[SUBLANES:SUBLANES + tm, :] = hup
            carries[a + 1][:, cols] = hup[tm - SUBLANES:, :]

    def conv_part(cols, hbuf):
        y = cb_ref[:, cols]
        for d in range(FFN_CONV):
            y = y + cw_ref[FFN_CONV - 1 - d:FFN_CONV - d, cols] * hbuf[SUBLANES - d:SUBLANES - d + tm, :]
        return y

    for s in range(FFN_AHEAD):
        up_part(s)
    for s, (a, c) in enumerate(stages):
        if s + FFN_AHEAD < len(stages):
            up_part(s + FFN_AHEAD)
        gate, val = (conv_part(cols, hbuf) for cols, hbuf in zip(col_pair(c), buf_pair(s)))
        act = (gate * _sigmoid(gate) * val).astype(bf16)
        part = jnp.dot(act, wdown_ref[c * FFN_CW:(c + 1) * FFN_CW, :], preferred_element_type=f32)
        if c == 0:
            accs[a][...] = part
        else:
            accs[a][...] += part
        if c == n_chunks - 1:
            y = normed[a][0] + accs[a][...]
            o_ref[rows(a), :] = (y * lax.rsqrt(jnp.mean(y * y, axis=-1, keepdims=True) + EPS)
                                 * gf_ref[...])


def _ffn_halo(xm, odam, odnm, w_out, g2, w_up):
    return pl.pallas_call(
        _ffn_halo_kernel,
        out_shape=jax.ShapeDtypeStruct((HALO_ROWS, 2 * D_FF), f32),
        compiler_params=pltpu.CompilerParams(vmem_limit_bytes=VMEM_LIMIT),
        name="ffn_halo",
    )(xm, odam, odnm, w_out, g2, w_up)


def _ffn(x, oda, odn, halo, w_out, g2, w_up, conv_w, conv_b, w_down, gf):
    nb, s_len, _ = x.shape
    tm = FFN_TM
    step_rows = FFN_TILES * tm
    row = lambda w: pl.BlockSpec((None, step_rows, w), lambda b, t: (b, t, 0))
    return pl.pallas_call(
        _ffn_kernel,
        grid=(nb, s_len // step_rows),
        in_specs=[row(D_MODEL), row(DA_WIDTH), row(DN_WIDTH),
                  _const_spec((SUBLANES, 2 * D_FF)), _const_spec((D_MODEL, D_MODEL)),
                  _const_spec((1, D_MODEL)), _const_spec((D_MODEL, 2 * D_FF)),
                  _const_spec((FFN_CONV, 2 * D_FF)), _const_spec((1, 2 * D_FF)),
                  _const_spec((D_FF, D_MODEL)), _const_spec((1, D_MODEL))],
        out_specs=row(D_MODEL),
        out_shape=jax.ShapeDtypeStruct((nb, s_len, D_MODEL), f32),
        scratch_shapes=([pltpu.VMEM((SUBLANES, 2 * D_FF), f32)] * (FFN_TILES + 1)
                        + [pltpu.VMEM((tm, D_MODEL), f32)] * FFN_TILES
                        + [pltpu.VMEM((SUBLANES + tm, FFN_CW), f32)] * (2 * FFN_NBUF)),
        compiler_params=pltpu.CompilerParams(
            dimension_semantics=("parallel", "arbitrary"), vmem_limit_bytes=VMEM_LIMIT),
        name="ffn",
    )(x, oda, odn, halo, w_out, g2, w_up, conv_w, conv_b, w_down, gf)


def kernel(x, meta_tokens, rel_bias, norm1_g, w_in, da_lambda, da_subln_g, dn_conv_w, dn_A_log,
           dn_dt_bias, dn_norm_g, w_out, norm2_g, w_up, ffn_conv_w, ffn_conv_b, w_down,
           final_norm_g):
    nb, s_len, _ = x.shape
    assert s_len % TQ == 0 and s_len % GDN_ROWS == 0 and s_len % (FFN_TILES * FFN_TM) == 0
    assert s_len % PROJ_TM == 0 and s_len >= 2 * TK
    w_all = jnp.pad(w_in[0], ((0, 0), (0, LANES - 2 * DN_HEADS))).astype(bf16)
    w_out_b = w_out[0].astype(bf16)
    w_up_b = w_up[0].astype(bf16)
    w_down_b = w_down[0].astype(bf16)
    g1 = norm1_g[0].reshape(1, D_MODEL).astype(f32)
    g2 = norm2_g[0].reshape(1, D_MODEL).astype(f32)
    gf = final_norm_g.reshape(1, D_MODEL).astype(f32)
    meta_pad = jnp.pad(meta_tokens.astype(x.dtype), ((MP - N_META, 0), (0, 0)))

    bnear, bmeta, bmm, lam_tile = _bias_tiles(rel_bias, da_lambda[0])
    lam1 = lam_tile[0, :1]

    conv_w = dn_conv_w[0].astype(f32)
    no_halo = jnp.zeros((SUBLANES, GDN_QKV), f32)
    mq, mk, mv, mdq, mdk, mdv, mdz, mba, mtail = [
        a[0] for a in _proj(meta_pad[None], no_halo, g1, w_all, conv_w, MP)]
    q, k, v, dq, dk, dv, dz, ba, _ = _proj(x, mtail, g1, w_all, conv_w, PROJ_TM)

    subln = da_subln_g[0].astype(f32)
    o_da = _attention(lam1, q, k, v, mk, mv, bnear, bmeta,
                      jnp.broadcast_to(subln[:, None], (DA_HEAD_DIM, TQ)))
    o_da_m = _attention_meta(lam1, mq, mk, mv, bmm,
                             jnp.broadcast_to(subln[:, None], (DA_HEAD_DIM, MP)))

    gate_row = lambda p: jnp.pad(p[0].astype(f32), (DN_HEADS, LANES - 2 * DN_HEADS)).reshape(1, LANES)
    o_dn, o_dn_m = _gdn(dq, dk, dv, dz, ba, mdq, mdk, mdv, mdz, mba,
                        gate_row(dn_A_log), gate_row(dn_dt_bias),
                        dn_norm_g[0].reshape(1, DN_DV).astype(f32))

    halo = _ffn_halo(meta_pad[MP - HALO_ROWS:], o_da_m[MP - HALO_ROWS:],
                     o_dn_m[0, CHUNK - HALO_ROWS:], w_out_b, g2, w_up_b)[HALO_ROWS - SUBLANES:]
    return _ffn(x, o_da, o_dn, halo, w_out_b, g2, w_up_b, ffn_conv_w[0].astype(f32),
                ffn_conv_b[0].reshape(1, 2 * D_FF).astype(f32), w_down_b, gf)
```

```python
import functools
import math

import numpy as np
import jax
import jax.numpy as jnp
from jax import lax
from jax.experimental import pallas as pl
from jax.experimental.pallas import tpu as pltpu

f32 = jnp.float32
bf16 = jnp.bfloat16

D_MODEL = 1024
CHUNK = 64
N_META = 16
EPS = 1e-6
NEG_INF = -1e30
LAMBDA_INIT = 0.8 - 0.6 * math.exp(-0.3 * 0)
LOG2E = math.log2(math.e)

DA_HEADS = 4
DA_HEAD_DIM = 128
DA_HALF = 64
DA_WIDTH = DA_HEADS * DA_HEAD_DIM
DN_HEADS = 4
DN_DK = 128
DN_DV = 128
DN_WIDTH = DN_HEADS * DN_DV
DN_CONV = 4
N_BUCKETS = 32
MAX_DISTANCE = 128
D_FF = 2816
FFN_CONV = 3

N_MAIN = 7 * 512
LANES = 128
SUBLANES = 8
MP = 128
TQ = 256
TK = 256
V_EXT = DA_HEAD_DIM + 16
PROJ_TM = 512
GDN_ROWS = 256
FFN_TILES = 1
FFN_TM = 256
FFN_CW = 256
FFN_ACT_SPLIT = 2
FFN_AHEAD = 10
FFN_NBUF = FFN_AHEAD + 1
HALO_ROWS = 16
VMEM_LIMIT = 52 * 1024 * 1024


def _const_spec(shape):
    nd = len(shape)
    return pl.BlockSpec(shape, lambda *_: (0,) * nd, pipeline_mode=pl.Buffered(1))


def _prep_kernel(far_ref, table_ref, lam_in_ref, bnear_in, bmeta_in, bmm_in,
                 bnear_out, bmeta_out, bmm_out, lam_out):
    h = pl.program_id(0)
    c_far = table_ref[far_ref[0], h]

    def lookup(bkt):
        out = jnp.full(bkt.shape, NEG_INF, f32)
        for b in range(N_BUCKETS):
            out = jnp.where(bkt == b, (table_ref[b, h] - c_far) * LOG2E, out)
        return out

    for i in range(2):
        bnear_out[i] = lookup(bnear_in[i])
        bmeta_out[i] = lookup(bmeta_in[i])
    bmm_out[...] = lookup(bmm_in[...])
    lv = lam_in_ref[...]
    s1 = jnp.sum(lv[0:1] * lv[1:2], axis=-1, keepdims=True)
    s2 = jnp.sum(lv[2:3] * lv[3:4], axis=-1, keepdims=True)
    lam = jnp.exp(s1) - jnp.exp(s2) + LAMBDA_INIT
    lam_out[...] = jnp.broadcast_to(lam, lam_out.shape)


def _t5_bucket(rel):
    nb = N_BUCKETS // 2
    max_exact = nb // 2
    ret = jnp.where(rel > 0, nb, 0)
    n = jnp.abs(rel)
    nf = jnp.maximum(n, 1).astype(jnp.float32)
    large = max_exact + (jnp.log(nf / max_exact) / math.log(MAX_DISTANCE / max_exact)
                         * (nb - max_exact)).astype(jnp.int32)
    large = jnp.minimum(large, nb - 1)
    return ret + jnp.where(n < max_exact, n, large)


def _bias_tiles(rel_bias, da_lambda):
    r = jnp.arange(TK, dtype=jnp.int32)[:, None]
    c = jnp.arange(TQ, dtype=jnp.int32)[None, :]
    diag = jnp.where((r // CHUNK) <= (c // CHUNK), _t5_bucket(r - c), -1)
    prev = _t5_bucket(r - c - TK)
    bnear = jnp.stack([diag, prev]).astype(jnp.int32)
    far = _t5_bucket(jnp.full((1,), -(TK + 1), jnp.int32)).astype(jnp.int32)
    rm = jnp.arange(MP, dtype=jnp.int32)[:, None]
    valid = rm >= (MP - N_META)
    kpos = rm - (MP - N_META)
    m0 = jnp.where(valid, _t5_bucket(kpos - (N_META + c)), -1)
    m1 = jnp.where(valid, jnp.broadcast_to(far[0], (MP, TQ)), -1)
    bmeta = jnp.stack([m0, m1]).astype(jnp.int32)
    cm = jnp.arange(MP, dtype=jnp.int32)[None, :]
    bmm = jnp.where(valid, _t5_bucket(rm - cm), -1).astype(jnp.int32)

    smem = pl.BlockSpec(memory_space=pltpu.SMEM)
    return pl.pallas_call(
        _prep_kernel,
        grid=(DA_HEADS,),
        in_specs=[smem, smem,
                  pl.BlockSpec((4, DA_HALF), lambda h: (0, 0)),
                  pl.BlockSpec((2, TK, TQ), lambda h: (0, 0, 0)),
                  pl.BlockSpec((2, MP, TQ), lambda h: (0, 0, 0)),
                  pl.BlockSpec((MP, MP), lambda h: (0, 0))],
        out_specs=[pl.BlockSpec((None, 2, TK, TQ), lambda h: (h, 0, 0, 0)),
                   pl.BlockSpec((None, 2, MP, TQ), lambda h: (h, 0, 0, 0)),
                   pl.BlockSpec((None, MP, MP), lambda h: (h, 0, 0)),
                   pl.BlockSpec((SUBLANES, LANES), lambda h: (0, 0))],
        out_shape=[jax.ShapeDtypeStruct((DA_HEADS, 2, TK, TQ), f32),
                   jax.ShapeDtypeStruct((DA_HEADS, 2, MP, TQ), f32),
                   jax.ShapeDtypeStruct((DA_HEADS, MP, MP), f32),
                   jax.ShapeDtypeStruct((SUBLANES, LANES), f32)],
        name="prep",
    )(far, rel_bias.astype(f32), da_lambda.astype(f32), bnear, bmeta, bmm)


def _sigmoid(x):
    return 1.0 / (1.0 + jnp.exp(-x))


def _proj_kernel(x_ref, halo_ref, g_ref, w_ref, cw_ref, q_ref, k_ref, v_ref, dq_ref,
                 dk_ref, dv_ref, dz_ref, ba_ref, tail_ref, carry_ref, carry_next, u_ref):
    tm = x_ref.shape[0]
    t = pl.program_id(1)

    @pl.when(t == 0)
    def _():
        carry_ref[...] = halo_ref[...]

    @pl.when(t > 0)
    def _():
        carry_ref[...] = carry_next[...]

    x = x_ref[...]
    ms = jnp.mean(x * x, axis=-1, keepdims=True)
    u_ref[...] = (x * lax.rsqrt(ms + EPS) * g_ref[...]).astype(bf16)
    outs = (q_ref, k_ref, v_ref, dq_ref, dk_ref, dv_ref, dz_ref)
    n_out = len(outs)
    project = lambda j: jnp.dot(u_ref[...], w_ref[:, j * 512:(j + 1) * 512],
                                preferred_element_type=f32)

    def conv_silu(j, r):
        cols = slice((j - 3) * DN_WIDTH, (j - 2) * DN_WIDTH)
        carry_next[:, cols] = r[tm - SUBLANES:, :]
        ext = jnp.concatenate([carry_ref[:, cols], r], axis=0)
        y = cw_ref[0:1, cols] * ext
        for d in range(1, DN_CONV):
            y = cw_ref[d:d + 1, cols] * ext + pltpu.roll(y, 1, 0)
        y = y[SUBLANES:, :]
        return y * _sigmoid(y)

    def l2norm_heads(y, scale):
        parts = []
        for h in range(DN_HEADS):
            yh = y[:, h * DN_DK:(h + 1) * DN_DK]
            parts.append(yh * (lax.rsqrt(jnp.sum(yh * yh, axis=-1, keepdims=True) + EPS) * scale))
        return jnp.concatenate(parts, axis=1)

    order = (3, 0, 4, 1, 5, 2, 6)
    ahead = 2
    pending = [project(j) for j in order[:ahead]]
    for pos, j in enumerate(order):
        r = pending.pop(0)
        if pos + ahead < n_out:
            pending.append(project(order[pos + ahead]))
        elif pos + ahead == n_out:
            ba_ref[...] = jnp.dot(u_ref[...], w_ref[:, N_MAIN:N_MAIN + LANES],
                                  preferred_element_type=f32)
        if j == 0:
            r = r * (DA_HALF ** -0.5 * LOG2E)
        elif j in (3, 4):
            r = l2norm_heads(conv_silu(j, r), DN_DK ** -0.5 if j == 3 else 1.0)
        elif j == 5:
            r = conv_silu(j, r)
        if j in (0, 2):
            r = r.T
        outs[j][...] = r.astype(outs[j].dtype)
    tail_ref[...] = carry_next[...]


def _proj(x3, halo, g1, w_all, conv_w, tm):
    nb, rows, _ = x3.shape
    row_spec = lambda w: pl.BlockSpec((None, tm, w), lambda b, t: (b, t, 0))
    plain = jax.ShapeDtypeStruct((nb, rows, 512), bf16)
    transposed = jax.ShapeDtypeStruct((nb, 512, rows), bf16)
    outs = [transposed, plain, transposed] + [plain] * 4 + [
        jax.ShapeDtypeStruct((nb, rows, LANES), f32),
        jax.ShapeDtypeStruct((nb, SUBLANES, GDN_QKV), f32)]
    t_spec = pl.BlockSpec((None, 512, tm), lambda b, t: (b, 0, t))
    return pl.pallas_call(
        _proj_kernel,
        grid=(nb, rows // tm),
        in_specs=[row_spec(D_MODEL), _const_spec((SUBLANES, GDN_QKV)), _const_spec((1, D_MODEL)),
                  _const_spec((D_MODEL, N_MAIN + LANES)), _const_spec((DN_CONV, GDN_QKV))],
        out_specs=[t_spec, row_spec(512), t_spec]
                  + [row_spec(512)] * 4 + [row_spec(LANES),
                   pl.BlockSpec((None, SUBLANES, GDN_QKV), lambda b, t: (b, 0, 0))],
        out_shape=outs,
        scratch_shapes=([pltpu.VMEM((SUBLANES, GDN_QKV), f32)] * 2
                        + [pltpu.VMEM((tm, D_MODEL), bf16)]),
        compiler_params=pltpu.CompilerParams(
            dimension_semantics=("parallel", "arbitrary"), vmem_limit_bytes=VMEM_LIMIT),
        name="proj",
    )(x3, halo, g1, w_all, conv_w)


def _dot_nt(a, b):
    return lax.dot_general(a, b, (((1,), (1,)), ((), ())), preferred_element_type=f32)


def _dot_tn(a, b):
    return lax.dot_general(a, b, (((0,), (0,)), ((), ())), preferred_element_type=f32)


def _stack_components(qt):
    zero = jnp.zeros((DA_HALF, qt.shape[1]), qt.dtype)
    q0 = jnp.concatenate([qt[:DA_HALF], zero], axis=0)
    q1 = jnp.concatenate([zero, qt[DA_HALF:]], axis=0)
    return jnp.concatenate([q0, q1], axis=1)


def _colmax(s):
    return jnp.max(s, axis=0, keepdims=True)


def _values_t_ext(vt):
    return jnp.concatenate([vt, jnp.ones((V_EXT - DA_HEAD_DIM, vt.shape[1]), vt.dtype)], axis=0)


def _softmax_step(stats, s, smax, vt_pend, p_ref, acc_ref):
    m, alpha_pend = stats
    pv = jnp.dot(vt_pend, p_ref[...], preferred_element_type=f32)
    m_new = jnp.maximum(m, smax)
    alpha = jnp.exp2(m - m_new)
    p = jnp.exp2((s - m_new).astype(bf16))
    acc_ref[...] = alpha_pend * acc_ref[...] + pv
    return (m_new, alpha), p


def _attn_finish(stats, vt_pend, p_ref, acc_ref, lam, g, tq):
    m, alpha_pend = stats
    acc = alpha_pend * acc_ref[...] + jnp.dot(vt_pend, p_ref[...], preferred_element_type=f32)
    l = acc[DA_HEAD_DIM:DA_HEAD_DIM + 1, :]
    acc = acc[:DA_HEAD_DIM, :]
    o = acc[:, :tq] / l[:, :tq] - lam * (acc[:, tq:] / l[:, tq:])
    ms = jnp.mean(o * o, axis=0, keepdims=True)
    y = o * lax.rsqrt(ms + EPS) * g * (1.0 - LAMBDA_INIT)
    return y.T


def _attn_init(tq, p_ref, acc_ref):
    p_ref[...] = jnp.zeros(p_ref.shape, bf16)
    acc_ref[...] = jnp.zeros(acc_ref.shape, f32)
    return (jnp.full((1, 2 * tq), -3e38, f32), jnp.ones((1, 2 * tq), f32))


def _both(b):
    return jnp.concatenate([b, b], axis=1)


def _attn_kernel(lam_ref, q_ref, k_ref, v_ref, km_ref, vm_ref, bnear_ref, bmeta_ref, g_ref,
                 o_ref, *scratch):
    i = pl.program_id(1)
    n_far = jnp.maximum(i - 1, 0)
    heads = range(DA_HEADS)
    cols = lambda h: slice(h * DA_HEAD_DIM, (h + 1) * DA_HEAD_DIM)
    qz = [_stack_components(q_ref[cols(h), :]) for h in heads]

    def scores(h, j):
        start = pl.multiple_of(j * TK, TK)
        return jnp.dot(k_ref[pl.ds(start, TK), cols(h)], qz[h], preferred_element_type=f32)

    s_even = scratch[0:DA_HEADS]
    s_odd = scratch[DA_HEADS:2 * DA_HEADS]
    p_bufs = scratch[2 * DA_HEADS:3 * DA_HEADS]
    accs = scratch[3 * DA_HEADS:4 * DA_HEADS]

    def values(h, j):
        start = pl.multiple_of(jnp.maximum(j, 0) * TK, TK)
        return _values_t_ext(v_ref[cols(h), pl.ds(start, TK)])

    def consume(j, bufs, stats, smax):
        new_stats = []
        for h in heads:
            st, p = _softmax_step(stats[h], bufs[h][...], smax[h], values(h, j - 1),
                                  p_bufs[h], accs[h])
            p_bufs[h][...] = p
            new_stats.append(st)
        return tuple(new_stats)

    def produce(j, bufs):
        new_smax = []
        for h in heads:
            s_next = scores(h, j)
            bufs[h][...] = s_next
            new_smax.append(_colmax(s_next))
        return tuple(new_smax)

    stats = tuple(_attn_init(TQ, p_bufs[h], accs[h]) for h in heads)
    smax_even = produce(0, s_even)
    smax_odd = produce(1, s_odd)

    def pair_body(jp, carry):
        stats, smax_even, smax_odd = carry
        j = 2 * jp
        stats = consume(j, s_even, stats, smax_even)
        smax_even = produce(j + 2, s_even)
        stats = consume(j + 1, s_odd, stats, smax_odd)
        smax_odd = produce(j + 3, s_odd)
        return stats, smax_even, smax_odd

    n_pairs = n_far // 2
    stats, smax_even, _ = lax.fori_loop(0, n_pairs, pair_body, (stats, smax_even, smax_odd))

    def single_body(_, stats):
        stats = consume(2 * n_pairs, s_even, stats, smax_even)
        for h in heads:
            s_even[h][...] = s_odd[h][...]
        return stats

    stats = lax.fori_loop(0, n_far - 2 * n_pairs, single_body, stats)
    gate = jnp.where(i >= 1, 0.0, NEG_INF).astype(f32)
    for h in heads:
        p_h, acc_h = p_bufs[h], accs[h]
        s = s_even[h][...] + _both(bnear_ref[h, 1] + gate)
        st, p = _softmax_step(stats[h], s, _colmax(s), values(h, n_far - 1), p_h, acc_h)
        p_h[...] = p
        s = scores(h, i) + _both(bnear_ref[h, 0])
        st, p = _softmax_step(st, s, _colmax(s), values(h, i - 1), p_h, acc_h)
        p_h[...] = p
        s = (jnp.dot(km_ref[:, cols(h)], qz[h], preferred_element_type=f32)
             + _both(bmeta_ref[h, jnp.minimum(i, 1)]))
        st, p = _softmax_step(st, s, _colmax(s), values(h, i), p_h, acc_h)
        p_meta = p_h.at[0:MP, :]
        p_meta[...] = p
        o_ref[:, cols(h)] = _attn_finish(st, _values_t_ext(vm_ref[cols(h), :]), p_meta, acc_h,
                                         lam_ref[0],
                                         g_ref[...], TQ).astype(o_ref.dtype)


def _attn_meta_kernel(lam_ref, q_ref, km_ref, vm_ref, bmm_ref, g_ref, o_ref, p_buf, acc_ref):
    qz = _stack_components(q_ref[...])
    vmt = _values_t_ext(vm_ref[...])
    stats = _attn_init(MP, p_buf, acc_ref)
    s = jnp.dot(km_ref[...], qz, preferred_element_type=f32) + _both(bmm_ref[...])
    stats, p = _softmax_step(stats, s, _colmax(s), vmt, p_buf, acc_ref)
    p_buf[...] = p
    o_ref[...] = _attn_finish(stats, vmt, p_buf, acc_ref, lam_ref[0], g_ref[...],
                              MP).astype(o_ref.dtype)


def _attention(lam1, q, k, v, km, vm, bnear, bmeta, g_tile):
    nb, s_len, _ = k.shape
    smem = pl.BlockSpec(memory_space=pltpu.SMEM)
    return pl.pallas_call(
        _attn_kernel,
        grid=(nb, s_len // TQ),
        in_specs=[smem,
                  pl.BlockSpec((None, DA_WIDTH, TQ), lambda b, i: (b, 0, i)),
                  pl.BlockSpec((None, s_len, DA_WIDTH), lambda b, i: (b, 0, 0)),
                  pl.BlockSpec((None, DA_WIDTH, s_len), lambda b, i: (b, 0, 0)),
                  _const_spec((MP, DA_WIDTH)), _const_spec((DA_WIDTH, MP)),
                  _const_spec((DA_HEADS, 2, TK, TQ)), _const_spec((DA_HEADS, 2, MP, TQ)),
                  _const_spec((DA_HEAD_DIM, TQ))],
        out_specs=pl.BlockSpec((None, TQ, DA_WIDTH), lambda b, i: (b, i, 0)),
        out_shape=jax.ShapeDtypeStruct((nb, s_len, DA_WIDTH), bf16),
        scratch_shapes=([pltpu.VMEM((TK, 2 * TQ), f32)] * (2 * DA_HEADS)
                        + [pltpu.VMEM((TK, 2 * TQ), bf16)] * DA_HEADS
                        + [pltpu.VMEM((V_EXT, 2 * TQ), f32)] * DA_HEADS),
        compiler_params=pltpu.CompilerParams(
            dimension_semantics=("parallel", "arbitrary"),
            vmem_limit_bytes=VMEM_LIMIT),
        name="attn",
    )(lam1, q, k, v, km, vm, bnear, bmeta, g_tile)


def _attention_meta(lam1, qm, km, vm, bmm, g_tile):
    smem = pl.BlockSpec(memory_space=pltpu.SMEM)
    head = pl.BlockSpec((MP, DA_HEAD_DIM), lambda h: (0, h))
    return pl.pallas_call(
        _attn_meta_kernel,
        grid=(DA_HEADS,),
        in_specs=[smem, pl.BlockSpec((DA_HEAD_DIM, MP), lambda h: (h, 0)), head,
                  pl.BlockSpec((DA_HEAD_DIM, MP), lambda h: (h, 0)),
                  pl.BlockSpec((None, MP, MP), lambda h: (h, 0, 0)),
                  pl.BlockSpec((DA_HEAD_DIM, MP), lambda h: (0, 0))],
        out_specs=head,
        out_shape=jax.ShapeDtypeStruct((MP, DA_WIDTH), bf16),
        scratch_shapes=[pltpu.VMEM((MP, 2 * MP), bf16), pltpu.VMEM((V_EXT, 2 * MP), f32)],
        name="attn_meta",
    )(lam1, qm, km, vm, bmm, g_tile)


GDN_R = DN_HEADS * CHUNK
GDN_LEVELS = (2, 4, 8, 16, 32)
GDN_QKV = 3 * DN_WIDTH
GDN_NCH = GDN_ROWS // CHUNK
GDN_NSEQ = 4


def _gdn_consts():
    r = np.arange(GDN_R)[:, None]
    c = np.arange(GDN_R)[None, :]
    same = (r // CHUNK) == (c // CHUNK)
    bd = np.stack([same & (r >= c), same & (r > c), same]).astype(np.float32)
    i = np.arange(CHUNK)[:, None]
    j = np.arange(GDN_R)[None, :] % CHUNK
    lv = [i == j, (i > j) & ((i // 2) == (j // 2))]
    for s in GDN_LEVELS:
        lv.append(((i // (2 * s)) == (j // (2 * s))) & ((i & s) != 0) & ((j & s) == 0))
    cat = np.stack(lv).astype(np.float32)
    rr = np.arange(GDN_ROWS)[:, None]
    cc = np.arange(GDN_ROWS)[None, :]
    lcum = (((rr // CHUNK) == (cc // CHUNK)) & (rr >= cc)).astype(np.float32)
    sel = np.zeros((2, LANES, DN_HEADS * LANES), np.float32)
    for h in range(DN_HEADS):
        sel[0, h, h * LANES:(h + 1) * LANES] = 1.0
        sel[1, DN_HEADS + h, h * LANES:(h + 1) * LANES] = 1.0
    return bd, cat, lcum, sel


def _split3(x):
    hi = x.astype(bf16)
    r1 = x - hi.astype(f32)
    mid = r1.astype(bf16)
    lo = (r1 - mid.astype(f32)).astype(bf16)
    return hi, mid, lo


def _gdn_block(n_per_seq, xq, xk, xv, ba, z, s_refs, alog_ref, dtb_ref, ng_ref, bd_ref,
               cat_ref, lcum_ref, sel_ref):
    n_seq = len(s_refs)
    n = n_seq * n_per_seq
    seq_rows = n_per_seq * CHUNK
    heads = range(DN_HEADS)
    chunks = range(n)
    dot = functools.partial(jnp.dot, preferred_element_type=f32)

    def stack(a):
        return jnp.concatenate(
            [a[c * CHUNK:(c + 1) * CHUNK, h * LANES:(h + 1) * LANES]
             for c in chunks for h in heads], axis=0)

    qn_b, kn_b = stack(xq), stack(xk)
    qn, kn, vs = qn_b.astype(f32), kn_b.astype(f32), stack(xv).astype(f32)
    blk = lambda c: slice(c * GDN_R, (c + 1) * GDN_R)

    beta_t = _sigmoid(ba)
    xg = ba + dtb_ref[...]
    softplus = jnp.maximum(xg, 0.0) + jnp.log1p(jnp.exp(-jnp.abs(xg)))
    g_t = -jnp.exp(alog_ref[...]) * softplus
    lcum = lcum_ref[0:seq_rows, 0:seq_rows]
    g_parts = _split3(g_t)[:2]
    gcum = jnp.concatenate(
        [sum(dot(lcum, part[q * seq_rows:(q + 1) * seq_rows]) for part in g_parts)
         for q in range(n_seq)], axis=0)
    g_rep = stack(sum(dot(part, sel_ref[1]) for part in _split3(gcum)[:2]))
    b_rep = stack(sum(dot(part, sel_ref[0]) for part in _split3(beta_t)[:2]))
    g_end = jnp.concatenate(
        [jnp.broadcast_to(g_rep[(b + 1) * CHUNK - 1:(b + 1) * CHUNK, :], (CHUNK, LANES))
         for b in range(n * DN_HEADS)], axis=0)
    exp_g = jnp.exp(g_rep)
    rhs = jnp.concatenate([vs * b_rep, kn * (b_rep * exp_g)], axis=1).astype(bf16)
    q_g = qn * exp_g
    k_g = (kn * jnp.exp(g_end - g_rep)).astype(bf16)
    g_last = jnp.exp(g_end)

    tri = bd_ref[0] > 0.5
    block_b = bd_ref[2].astype(bf16)
    both = lambda a: jnp.concatenate([a, a], axis=1)
    decay, m_b, x_cat = [], [], []
    for c in chunks:
        g_row = g_rep[blk(c)].T[0:1, :]
        dec = jnp.exp(jnp.where(tri, both(g_rep[blk(c)]) - g_row, -jnp.inf))
        m = bd_ref[1] * both(b_rep[blk(c)]) * _dot_nt(kn_b[blk(c)], kn_b[blk(c)]) * dec
        m_cat = sum(m[h * CHUNK:(h + 1) * CHUNK] for h in heads)
        decay.append(dec)
        m_b.append(m.astype(bf16))
        x_cat.append(cat_ref[0] - m_cat * cat_ref[1])

    def to_bd(x):
        return jnp.concatenate([x.astype(bf16)] * DN_HEADS, axis=0) * block_b

    for lvl in range(len(GDN_LEVELS)):
        ys = [dot(x_cat[c].astype(bf16), m_b[c]) for c in chunks]
        zs = [dot(ys[c].astype(bf16), to_bd(x_cat[c])) for c in chunks]
        x_cat = [x_cat[c] - zs[c] * cat_ref[2 + lvl] for c in chunks]
    sol = [dot(to_bd(x_cat[c]), rhs[blk(c)]).astype(bf16) for c in chunks]
    a_qk = [(_dot_nt(qn_b[blk(c)], kn_b[blk(c)]) * decay[c]).astype(bf16) for c in chunks]
    a_uw = [dot(a_qk[c], sol[c]) for c in chunks]
    hrows = lambda h: slice(h * CHUNK, (h + 1) * CHUNK)
    k_uw = [[_dot_tn(k_g[blk(c)][hrows(h)], sol[c][hrows(h)]) for h in heads] for c in chunks]

    out_rows = [None] * n
    for step in range(n_per_seq):
        for q in range(n_seq):
            c = q * n_per_seq + step
            s_ref = s_refs[q]
            out_cols = []
            for h in heads:
                s_old = s_ref[h]
                s_b = s_old.astype(bf16)
                q_eff = (q_g[blk(c)][hrows(h)] - a_uw[c][hrows(h), DN_DV:]).astype(bf16)
                o = a_uw[c][hrows(h), :DN_DV] + dot(q_eff, s_b)
                s_ref[h] = (s_old * g_last[blk(c)][hrows(h)][0:1, :] + k_uw[c][h][:, :DN_DV]
                            - dot(k_uw[c][h][:, DN_DV:].astype(bf16), s_b))
                o = o * lax.rsqrt(jnp.mean(o * o, axis=-1, keepdims=True) + EPS) * ng_ref[...]
                zh = z[c * CHUNK:(c + 1) * CHUNK, h * DN_DV:(h + 1) * DN_DV].astype(f32)
                out_cols.append(o * (zh * _sigmoid(zh)))
            out_rows[c] = jnp.concatenate(out_cols, axis=1)
    return jnp.concatenate(out_rows, axis=0)


def _gdn_kernel(dq_ref, dk_ref, dv_ref, dz_ref, ba_ref, mq_ref, mk_ref, mv_ref, mz_ref, mba_ref,
                alog_ref, dtb_ref, ng_ref, bd_ref, cat_ref, lcum_ref, sel_ref,
                o_ref, om_ref, s_ref):
    consts = (alog_ref, dtb_ref, ng_ref, bd_ref, cat_ref, lcum_ref, sel_ref)
    s_refs = [s_ref.at[q] for q in range(GDN_NSEQ)]

    @pl.when(pl.program_id(1) == 0)
    def _():
        s_ref[0] = jnp.zeros(s_ref.shape[1:], f32)
        om_ref[...] = _gdn_block(1, mq_ref[...], mk_ref[...], mv_ref[...], mba_ref[...],
                                 mz_ref[...], s_refs[:1], *consts).astype(om_ref.dtype)
        for q in range(1, GDN_NSEQ):
            s_ref[q] = s_ref[0]

    merge = lambda ref: ref[...].reshape(GDN_NSEQ * GDN_ROWS, ref.shape[-1])
    out = _gdn_block(GDN_NCH, merge(dq_ref), merge(dk_ref), merge(dv_ref), merge(ba_ref),
                     merge(dz_ref), s_refs, *consts)
    o_ref[...] = out.reshape(o_ref.shape).astype(o_ref.dtype)


def _gdn(dq, dk, dv, dz, ba, mq, mk, mv, mz, mba, alog_row, dtb_row, ng_row):
    nb, s_len, _ = dq.shape
    assert nb % GDN_NSEQ == 0
    row = lambda w: pl.BlockSpec((GDN_NSEQ, GDN_ROWS, w), lambda b, t: (b, t, 0))
    mrow = lambda w: pl.BlockSpec((CHUNK, w), lambda b, t: (MP // CHUNK - 1, 0))
    bd, cat, lcum, sel = _gdn_consts()
    bd, cat = jnp.asarray(bd), jnp.asarray(cat)
    lcum, sel = jnp.asarray(lcum).astype(bf16), jnp.asarray(sel).astype(bf16)
    return pl.pallas_call(
        _gdn_kernel,
        grid=(nb // GDN_NSEQ, s_len // GDN_ROWS),
        in_specs=[row(DN_WIDTH)] * 4 + [row(LANES)] + [mrow(DN_WIDTH)] * 4 + [mrow(LANES)] + [
            _const_spec((1, LANES)), _const_spec((1, LANES)),
            _const_spec((1, DN_DV)), _const_spec(bd.shape), _const_spec(cat.shape),
            _const_spec(lcum.shape), _const_spec(sel.shape)],
        out_specs=[row(DN_WIDTH), pl.BlockSpec((None, CHUNK, DN_WIDTH), lambda b, t: (b, 0, 0))],
        out_shape=[jax.ShapeDtypeStruct((nb, s_len, DN_WIDTH), bf16),
                   jax.ShapeDtypeStruct((nb // GDN_NSEQ, CHUNK, DN_WIDTH), bf16)],
        scratch_shapes=[pltpu.VMEM((GDN_NSEQ, DN_HEADS, DN_DK, DN_DV), f32)],
        compiler_params=pltpu.CompilerParams(
            dimension_semantics=("parallel", "arbitrary"), vmem_limit_bytes=VMEM_LIMIT),
        name="gdn",
    )(dq, dk, dv, dz, ba, mq, mk, mv, mz, mba, alog_row, dtb_row, ng_row, bd, cat, lcum, sel)


def _mix_and_norm(x, oda, odn, wout_ref, g2_ref):
    mix = jnp.concatenate([oda, odn], axis=1)
    h2 = x + jnp.dot(mix, wout_ref[...], preferred_element_type=f32)
    u2 = h2 * lax.rsqrt(jnp.mean(h2 * h2, axis=-1, keepdims=True) + EPS) * g2_ref[...]
    return h2, u2.astype(bf16)


def _ffn_halo_kernel(x_ref, oda_ref, odn_ref, wout_ref, g2_ref, wup_ref, halo_ref):
    _, u2 = _mix_and_norm(x_ref[...], oda_ref[...], odn_ref[...], wout_ref, g2_ref)
    halo_ref[...] = jnp.dot(u2, wup_ref[...], preferred_element_type=f32)


def _ffn_kernel(x_ref, oda_ref, odn_ref, halo_ref, wout_ref, g2_ref, wup_ref, cw_ref, cb_ref,
                wdown_ref, gf_ref, o_ref, *scratch):
    tm = FFN_TM
    carries = scratch[0:FFN_TILES + 1]
    accs = scratch[FFN_TILES + 1:2 * FFN_TILES + 1]
    hbufs = scratch[2 * FFN_TILES + 1:]
    t = pl.program_id(1)

    @pl.when(t == 0)
    def _():
        carries[0][...] = halo_ref[...]

    @pl.when(t > 0)
    def _():
        carries[0][...] = carries[FFN_TILES][...]

    n_chunks = D_FF // FFN_CW
    stages = [(a, c) for a in range(FFN_TILES) for c in range(n_chunks)]
    col_pair = lambda c: (slice(c * FFN_CW, (c + 1) * FFN_CW),
                          slice(D_FF + c * FFN_CW, D_FF + (c + 1) * FFN_CW))
    buf_pair = lambda s: hbufs[2 * (s % FFN_NBUF):2 * (s % FFN_NBUF) + 2]
    rows = lambda a: slice(a * tm, (a + 1) * tm)
    normed = {}

    def up_part(s):
        a, c = stages[s]
        if c == 0:
            normed[a] = _mix_and_norm(x_ref[rows(a), :], oda_ref[rows(a), :],
                                      odn_ref[rows(a), :], wout_ref, g2_ref)
        u2 = normed[a][1]
        for cols, hbuf in zip(col_pair(c), buf_pair(s)):
            hbuf[0:SUBLANES, :] = carries[a][:, cols]
            hup = jnp.dot(u2, wup_ref[:, cols], preferred_element_type=f32)
            hbuf[SUBLANES:SUBLANES + tm, :] = hup
            carries[a + 1][:, cols] = hup[tm - SUBLANES:, :]

    def conv_part(cols, hbuf, r0, nr):
        y = cb_ref[:, cols]
        for d in range(FFN_CONV):
            y = y + (cw_ref[FFN_CONV - 1 - d:FFN_CONV - d, cols]
                     * hbuf[SUBLANES - d + r0:SUBLANES - d + r0 + nr, :])
        return y

    def activation(c, s):
        nr = tm // FFN_ACT_SPLIT
        blocks = []
        for r in range(FFN_ACT_SPLIT):
            gate, val = (conv_part(cols, hbuf, r * nr, nr)
                         for cols, hbuf in zip(col_pair(c), buf_pair(s)))
            blocks.append((gate * _sigmoid(gate) * val).astype(bf16))
        return jnp.concatenate(blocks, axis=0)

    for s in range(FFN_AHEAD):
        up_part(s)
    for s, (a, c) in enumerate(stages):
        if s + FFN_AHEAD < len(stages):
            up_part(s + FFN_AHEAD)
        part = jnp.dot(activation(c, s), wdown_ref[c * FFN_CW:(c + 1) * FFN_CW, :], preferred_element_type=f32)
        if c == 0:
            accs[a][...] = part
        else:
            accs[a][...] += part
        if c == n_chunks - 1:
            y = normed[a][0] + accs[a][...]
            o_ref[rows(a), :] = (y * lax.rsqrt(jnp.mean(y * y, axis=-1, keepdims=True) + EPS)
                                 * gf_ref[...])


def _ffn_halo(xm, odam, odnm, w_out, g2, w_up):
    return pl.pallas_call(
        _ffn_halo_kernel,
        out_shape=jax.ShapeDtypeStruct((HALO_ROWS, 2 * D_FF), f32),
        compiler_params=pltpu.CompilerParams(vmem_limit_bytes=VMEM_LIMIT),
        name="ffn_halo",
    )(xm, odam, odnm, w_out, g2, w_up)


def _ffn(x, oda, odn, halo, w_out, g2, w_up, conv_w, conv_b, w_down, gf):
    nb, s_len, _ = x.shape
    tm = FFN_TM
    step_rows = FFN_TILES * tm
    row = lambda w: pl.BlockSpec((None, step_rows, w), lambda b, t: (b, t, 0))
    return pl.pallas_call(
        _ffn_kernel,
        grid=(nb, s_len // step_rows),
        in_specs=[row(D_MODEL), row(DA_WIDTH), row(DN_WIDTH),
                  _const_spec((SUBLANES, 2 * D_FF)), _const_spec((D_MODEL, D_MODEL)),
                  _const_spec((1, D_MODEL)), _const_spec((D_MODEL, 2 * D_FF)),
                  _const_spec((FFN_CONV, 2 * D_FF)), _const_spec((1, 2 * D_FF)),
                  _const_spec((D_FF, D_MODEL)), _const_spec((1, D_MODEL))],
        out_specs=row(D_MODEL),
        out_shape=jax.ShapeDtypeStruct((nb, s_len, D_MODEL), f32),
        scratch_shapes=([pltpu.VMEM((SUBLANES, 2 * D_FF), f32)] * (FFN_TILES + 1)
                        + [pltpu.VMEM((tm, D_MODEL), f32)] * FFN_TILES
                        + [pltpu.VMEM((SUBLANES + tm, FFN_CW), f32)] * (2 * FFN_NBUF)),
        compiler_params=pltpu.CompilerParams(
            dimension_semantics=("parallel", "arbitrary"), vmem_limit_bytes=VMEM_LIMIT),
        name="ffn",
    )(x, oda, odn, halo, w_out, g2, w_up, conv_w, conv_b, w_down, gf)


def kernel(x, meta_tokens, rel_bias, norm1_g, w_in, da_lambda, da_subln_g, dn_conv_w, dn_A_log,
           dn_dt_bias, dn_norm_g, w_out, norm2_g, w_up, ffn_conv_w, ffn_conv_b, w_down,
           final_norm_g):
    nb, s_len, _ = x.shape
    assert s_len % TQ == 0 and s_len % GDN_ROWS == 0 and s_len % (FFN_TILES * FFN_TM) == 0
    assert s_len % PROJ_TM == 0 and s_len >= 2 * TK
    w_all = jnp.pad(w_in[0], ((0, 0), (0, LANES - 2 * DN_HEADS))).astype(bf16)
    w_out_b = w_out[0].astype(bf16)
    w_up_b = w_up[0].astype(bf16)
    w_down_b = w_down[0].astype(bf16)
    g1 = norm1_g[0].reshape(1, D_MODEL).astype(f32)
    g2 = norm2_g[0].reshape(1, D_MODEL).astype(f32)
    gf = final_norm_g.reshape(1, D_MODEL).astype(f32)
    meta_pad = jnp.pad(meta_tokens.astype(x.dtype), ((MP - N_META, 0), (0, 0)))

    bnear, bmeta, bmm, lam_tile = _bias_tiles(rel_bias, da_lambda[0])
    lam1 = lam_tile[0, :1]

    conv_w = dn_conv_w[0].astype(f32)
    no_halo = jnp.zeros((SUBLANES, GDN_QKV), f32)
    mq, mk, mv, mdq, mdk, mdv, mdz, mba, mtail = [
        a[0] for a in _proj(meta_pad[None], no_halo, g1, w_all, conv_w, MP)]
    q, k, v, dq, dk, dv, dz, ba, _ = _proj(x, mtail, g1, w_all, conv_w, PROJ_TM)

    subln = da_subln_g[0].astype(f32)
    o_da = _attention(lam1, q, k, v, mk, mv, bnear, bmeta,
                      jnp.broadcast_to(subln[:, None], (DA_HEAD_DIM, TQ)))
    o_da_m = _attention_meta(lam1, mq, mk, mv, bmm,
                             jnp.broadcast_to(subln[:, None], (DA_HEAD_DIM, MP)))

    gate_row = lambda p: jnp.pad(p[0].astype(f32), (DN_HEADS, LANES - 2 * DN_HEADS)).reshape(1, LANES)
    o_dn, o_dn_m = _gdn(dq, dk, dv, dz, ba, mdq, mdk, mdv, mdz, mba,
                        gate_row(dn_A_log), gate_row(dn_dt_bias),
                        dn_norm_g[0].reshape(1, DN_DV).astype(f32))

    halo = _ffn_halo(meta_pad[MP - HALO_ROWS:], o_da_m[MP - HALO_ROWS:],
                     o_dn_m[0, CHUNK - HALO_ROWS:], w_out_b, g2, w_up_b)[HALO_ROWS - SUBLANES:]
    return _ffn(x, o_da, o_dn, halo, w_out_b, g2, w_up_b, ffn_conv_w[0].astype(f32),
                ffn_conv_b[0].reshape(1, 2 * D_FF).astype(f32), w_down_b, gf)
```

```python
import functools
import math

import numpy as np
import jax
import jax.numpy as jnp
from jax import lax
from jax.experimental import pallas as pl
from jax.experimental.pallas import tpu as pltpu

f32 = jnp.float32
bf16 = jnp.bfloat16

D_MODEL = 1024
CHUNK = 64
N_META = 16
EPS = 1e-6
NEG_INF = -1e30
LAMBDA_INIT = 0.8 - 0.6 * math.exp(-0.3 * 0)
LOG2E = math.log2(math.e)

DA_HEADS = 4
DA_HEAD_DIM = 128
DA_HALF = 64
DA_WIDTH = DA_HEADS * DA_HEAD_DIM
DN_HEADS = 4
DN_DK = 128
DN_DV = 128
DN_WIDTH = DN_HEADS * DN_DV
DN_CONV = 4
N_BUCKETS = 32
MAX_DISTANCE = 128
D_FF = 2816
FFN_CONV = 3

N_MAIN = 7 * 512
LANES = 128
SUBLANES = 8
MP = 128
TQ = 256
TK = 256
V_EXT = DA_HEAD_DIM + 16
PROJ_TM = 512
GDN_ROWS = 256
FFN_TILES = 1
FFN_TM = 256
FFN_CW = 256
FFN_DOWN_GROUP = 2
FFN_AHEAD = 10
FFN_NBUF = FFN_AHEAD + 1
HALO_ROWS = 16
VMEM_LIMIT = 52 * 1024 * 1024


def _const_spec(shape):
    nd = len(shape)
    return pl.BlockSpec(shape, lambda *_: (0,) * nd, pipeline_mode=pl.Buffered(1))


def _prep_kernel(far_ref, table_ref, lam_in_ref, bnear_in, bmeta_in, bmm_in,
                 bnear_out, bmeta_out, bmm_out, lam_out):
    h = pl.program_id(0)
    c_far = table_ref[far_ref[0], h]

    def lookup(bkt):
        out = jnp.full(bkt.shape, NEG_INF, f32)
        for b in range(N_BUCKETS):
            out = jnp.where(bkt == b, (table_ref[b, h] - c_far) * LOG2E, out)
        return out

    for i in range(2):
        bnear_out[i] = lookup(bnear_in[i])
        bmeta_out[i] = lookup(bmeta_in[i])
    bmm_out[...] = lookup(bmm_in[...])
    lv = lam_in_ref[...]
    s1 = jnp.sum(lv[0:1] * lv[1:2], axis=-1, keepdims=True)
    s2 = jnp.sum(lv[2:3] * lv[3:4], axis=-1, keepdims=True)
    lam = jnp.exp(s1) - jnp.exp(s2) + LAMBDA_INIT
    lam_out[...] = jnp.broadcast_to(lam, lam_out.shape)


def _t5_bucket(rel):
    nb = N_BUCKETS // 2
    max_exact = nb // 2
    ret = jnp.where(rel > 0, nb, 0)
    n = jnp.abs(rel)
    nf = jnp.maximum(n, 1).astype(jnp.float32)
    large = max_exact + (jnp.log(nf / max_exact) / math.log(MAX_DISTANCE / max_exact)
                         * (nb - max_exact)).astype(jnp.int32)
    large = jnp.minimum(large, nb - 1)
    return ret + jnp.where(n < max_exact, n, large)


def _bias_tiles(rel_bias, da_lambda):
    r = jnp.arange(TK, dtype=jnp.int32)[:, None]
    c = jnp.arange(TQ, dtype=jnp.int32)[None, :]
    diag = jnp.where((r // CHUNK) <= (c // CHUNK), _t5_bucket(r - c), -1)
    prev = _t5_bucket(r - c - TK)
    bnear = jnp.stack([diag, prev]).astype(jnp.int32)
    far = _t5_bucket(jnp.full((1,), -(TK + 1), jnp.int32)).astype(jnp.int32)
    rm = jnp.arange(MP, dtype=jnp.int32)[:, None]
    valid = rm >= (MP - N_META)
    kpos = rm - (MP - N_META)
    m0 = jnp.where(valid, _t5_bucket(kpos - (N_META + c)), -1)
    m1 = jnp.where(valid, jnp.broadcast_to(far[0], (MP, TQ)), -1)
    bmeta = jnp.stack([m0, m1]).astype(jnp.int32)
    cm = jnp.arange(MP, dtype=jnp.int32)[None, :]
    bmm = jnp.where(valid, _t5_bucket(rm - cm), -1).astype(jnp.int32)

    smem = pl.BlockSpec(memory_space=pltpu.SMEM)
    return pl.pallas_call(
        _prep_kernel,
        grid=(DA_HEADS,),
        in_specs=[smem, smem,
                  pl.BlockSpec((4, DA_HALF), lambda h: (0, 0)),
                  pl.BlockSpec((2, TK, TQ), lambda h: (0, 0, 0)),
                  pl.BlockSpec((2, MP, TQ), lambda h: (0, 0, 0)),
                  pl.BlockSpec((MP, MP), lambda h: (0, 0))],
        out_specs=[pl.BlockSpec((None, 2, TK, TQ), lambda h: (h, 0, 0, 0)),
                   pl.BlockSpec((None, 2, MP, TQ), lambda h: (h, 0, 0, 0)),
                   pl.BlockSpec((None, MP, MP), lambda h: (h, 0, 0)),
                   pl.BlockSpec((SUBLANES, LANES), lambda h: (0, 0))],
        out_shape=[jax.ShapeDtypeStruct((DA_HEADS, 2, TK, TQ), f32),
                   jax.ShapeDtypeStruct((DA_HEADS, 2, MP, TQ), f32),
                   jax.ShapeDtypeStruct((DA_HEADS, MP, MP), f32),
                   jax.ShapeDtypeStruct((SUBLANES, LANES), f32)],
        name="prep",
    )(far, rel_bias.astype(f32), da_lambda.astype(f32), bnear, bmeta, bmm)


def _sigmoid(x):
    return 1.0 / (1.0 + jnp.exp(-x))


def _proj_kernel(x_ref, halo_ref, g_ref, w_ref, cw_ref, q_ref, k_ref, v_ref, dq_ref,
                 dk_ref, dv_ref, dz_ref, ba_ref, tail_ref, carry_ref, carry_next, u_ref):
    tm = x_ref.shape[0]
    t = pl.program_id(1)

    @pl.when(t == 0)
    def _():
        carry_ref[...] = halo_ref[...]

    @pl.when(t > 0)
    def _():
        carry_ref[...] = carry_next[...]

    x = x_ref[...]
    ms = jnp.mean(x * x, axis=-1, keepdims=True)
    u_ref[...] = (x * lax.rsqrt(ms + EPS) * g_ref[...]).astype(bf16)
    outs = (q_ref, k_ref, v_ref, dq_ref, dk_ref, dv_ref, dz_ref)
    n_out = len(outs)
    project = lambda j: jnp.dot(u_ref[...], w_ref[:, j * 512:(j + 1) * 512],
                                preferred_element_type=f32)

    def conv_silu(j, r):
        cols = slice((j - 3) * DN_WIDTH, (j - 2) * DN_WIDTH)
        carry_next[:, cols] = r[tm - SUBLANES:, :]
        ext = jnp.concatenate([carry_ref[:, cols], r], axis=0)
        y = cw_ref[0:1, cols] * ext
        for d in range(1, DN_CONV):
            y = cw_ref[d:d + 1, cols] * ext + pltpu.roll(y, 1, 0)
        y = y[SUBLANES:, :]
        return y * _sigmoid(y)

    def l2norm_heads(y, scale):
        parts = []
        for h in range(DN_HEADS):
            yh = y[:, h * DN_DK:(h + 1) * DN_DK]
            parts.append(yh * (lax.rsqrt(jnp.sum(yh * yh, axis=-1, keepdims=True) + EPS) * scale))
        return jnp.concatenate(parts, axis=1)

    order = (3, 0, 4, 1, 5, 2, 6)
    ahead = 2
    pending = [project(j) for j in order[:ahead]]
    for pos, j in enumerate(order):
        r = pending.pop(0)
        if pos + ahead < n_out:
            pending.append(project(order[pos + ahead]))
        elif pos + ahead == n_out:
            ba_ref[...] = jnp.dot(u_ref[...], w_ref[:, N_MAIN:N_MAIN + LANES],
                                  preferred_element_type=f32)
        if j == 0:
            r = r * (DA_HALF ** -0.5 * LOG2E)
        elif j in (3, 4):
            r = l2norm_heads(conv_silu(j, r), DN_DK ** -0.5 if j == 3 else 1.0)
        elif j == 5:
            r = conv_silu(j, r)
        if j in (0, 2):
            r = r.T
        outs[j][...] = r.astype(outs[j].dtype)
    tail_ref[...] = carry_next[...]


def _proj(x3, halo, g1, w_all, conv_w, tm):
    nb, rows, _ = x3.shape
    row_spec = lambda w: pl.BlockSpec((None, tm, w), lambda b, t: (b, t, 0))
    plain = jax.ShapeDtypeStruct((nb, rows, 512), bf16)
    transposed = jax.ShapeDtypeStruct((nb, 512, rows), bf16)
    outs = [transposed, plain, transposed] + [plain] * 4 + [
        jax.ShapeDtypeStruct((nb, rows, LANES), f32),
        jax.ShapeDtypeStruct((nb, SUBLANES, GDN_QKV), f32)]
    t_spec = pl.BlockSpec((None, 512, tm), lambda b, t: (b, 0, t))
    return pl.pallas_call(
        _proj_kernel,
        grid=(nb, rows // tm),
        in_specs=[row_spec(D_MODEL), _const_spec((SUBLANES, GDN_QKV)), _const_spec((1, D_MODEL)),
                  _const_spec((D_MODEL, N_MAIN + LANES)), _const_spec((DN_CONV, GDN_QKV))],
        out_specs=[t_spec, row_spec(512), t_spec]
                  + [row_spec(512)] * 4 + [row_spec(LANES),
                   pl.BlockSpec((None, SUBLANES, GDN_QKV), lambda b, t: (b, 0, 0))],
        out_shape=outs,
        scratch_shapes=([pltpu.VMEM((SUBLANES, GDN_QKV), f32)] * 2
                        + [pltpu.VMEM((tm, D_MODEL), bf16)]),
        compiler_params=pltpu.CompilerParams(
            dimension_semantics=("parallel", "arbitrary"), vmem_limit_bytes=VMEM_LIMIT),
        name="proj",
    )(x3, halo, g1, w_all, conv_w)


def _dot_nt(a, b):
    return lax.dot_general(a, b, (((1,), (1,)), ((), ())), preferred_element_type=f32)


def _dot_tn(a, b):
    return lax.dot_general(a, b, (((0,), (0,)), ((), ())), preferred_element_type=f32)


def _stack_components(qt):
    zero = jnp.zeros((DA_HALF, qt.shape[1]), qt.dtype)
    q0 = jnp.concatenate([qt[:DA_HALF], zero], axis=0)
    q1 = jnp.concatenate([zero, qt[DA_HALF:]], axis=0)
    return jnp.concatenate([q0, q1], axis=1)


def _colmax(s):
    return jnp.max(s, axis=0, keepdims=True)


def _values_t_ext(vt):
    return jnp.concatenate([vt, jnp.ones((V_EXT - DA_HEAD_DIM, vt.shape[1]), vt.dtype)], axis=0)


def _softmax_step(stats, s, smax, vt_pend, p_ref, acc_ref):
    m, alpha_pend = stats
    pv = jnp.dot(vt_pend, p_ref[...], preferred_element_type=f32)
    m_new = jnp.maximum(m, smax)
    alpha = jnp.exp2(m - m_new)
    p = jnp.exp2((s - m_new).astype(bf16))
    acc_ref[...] = alpha_pend * acc_ref[...] + pv
    return (m_new, alpha), p


def _attn_finish(stats, vt_pend, p_ref, acc_ref, lam, g, tq):
    m, alpha_pend = stats
    acc = alpha_pend * acc_ref[...] + jnp.dot(vt_pend, p_ref[...], preferred_element_type=f32)
    l = acc[DA_HEAD_DIM:DA_HEAD_DIM + 1, :]
    acc = acc[:DA_HEAD_DIM, :]
    o = acc[:, :tq] / l[:, :tq] - lam * (acc[:, tq:] / l[:, tq:])
    ms = jnp.mean(o * o, axis=0, keepdims=True)
    y = o * lax.rsqrt(ms + EPS) * g * (1.0 - LAMBDA_INIT)
    return y.T


def _attn_init(tq, p_ref, acc_ref):
    p_ref[...] = jnp.zeros(p_ref.shape, bf16)
    acc_ref[...] = jnp.zeros(acc_ref.shape, f32)
    return (jnp.full((1, 2 * tq), -3e38, f32), jnp.ones((1, 2 * tq), f32))


def _both(b):
    return jnp.concatenate([b, b], axis=1)


def _attn_kernel(lam_ref, q_ref, k_ref, v_ref, km_ref, vm_ref, bnear_ref, bmeta_ref, g_ref,
                 o_ref, *scratch):
    i = pl.program_id(1)
    n_far = jnp.maximum(i - 1, 0)
    heads = range(DA_HEADS)
    cols = lambda h: slice(h * DA_HEAD_DIM, (h + 1) * DA_HEAD_DIM)
    qz = [_stack_components(q_ref[cols(h), :]) for h in heads]

    def scores(h, j):
        start = pl.multiple_of(j * TK, TK)
        return jnp.dot(k_ref[pl.ds(start, TK), cols(h)], qz[h], preferred_element_type=f32)

    s_even = scratch[0:DA_HEADS]
    s_odd = scratch[DA_HEADS:2 * DA_HEADS]
    p_bufs = scratch[2 * DA_HEADS:3 * DA_HEADS]
    accs = scratch[3 * DA_HEADS:4 * DA_HEADS]

    def values(h, j):
        start = pl.multiple_of(jnp.maximum(j, 0) * TK, TK)
        return _values_t_ext(v_ref[cols(h), pl.ds(start, TK)])

    def consume(j, bufs, stats, smax):
        new_stats = []
        for h in heads:
            st, p = _softmax_step(stats[h], bufs[h][...], smax[h], values(h, j - 1),
                                  p_bufs[h], accs[h])
            p_bufs[h][...] = p
            new_stats.append(st)
        return tuple(new_stats)

    def produce(j, bufs):
        new_smax = []
        for h in heads:
            s_next = scores(h, j)
            bufs[h][...] = s_next
            new_smax.append(_colmax(s_next))
        return tuple(new_smax)

    stats = tuple(_attn_init(TQ, p_bufs[h], accs[h]) for h in heads)
    smax_even = produce(0, s_even)
    smax_odd = produce(1, s_odd)

    def pair_body(jp, carry):
        stats, smax_even, smax_odd = carry
        j = 2 * jp
        stats = consume(j, s_even, stats, smax_even)
        smax_even = produce(j + 2, s_even)
        stats = consume(j + 1, s_odd, stats, smax_odd)
        smax_odd = produce(j + 3, s_odd)
        return stats, smax_even, smax_odd

    n_pairs = n_far // 2
    stats, smax_even, _ = lax.fori_loop(0, n_pairs, pair_body, (stats, smax_even, smax_odd))

    def single_body(_, stats):
        stats = consume(2 * n_pairs, s_even, stats, smax_even)
        for h in heads:
            s_even[h][...] = s_odd[h][...]
        return stats

    stats = lax.fori_loop(0, n_far - 2 * n_pairs, single_body, stats)
    gate = jnp.where(i >= 1, 0.0, NEG_INF).astype(f32)
    for h in heads:
        p_h, acc_h = p_bufs[h], accs[h]
        s = s_even[h][...] + _both(bnear_ref[h, 1] + gate)
        st, p = _softmax_step(stats[h], s, _colmax(s), values(h, n_far - 1), p_h, acc_h)
        p_h[...] = p
        s = scores(h, i) + _both(bnear_ref[h, 0])
        st, p = _softmax_step(st, s, _colmax(s), values(h, i - 1), p_h, acc_h)
        p_h[...] = p
        s = (jnp.dot(km_ref[:, cols(h)], qz[h], preferred_element_type=f32)
             + _both(bmeta_ref[h, jnp.minimum(i, 1)]))
        st, p = _softmax_step(st, s, _colmax(s), values(h, i), p_h, acc_h)
        p_meta = p_h.at[0:MP, :]
        p_meta[...] = p
        o_ref[:, cols(h)] = _attn_finish(st, _values_t_ext(vm_ref[cols(h), :]), p_meta, acc_h,
                                         lam_ref[0],
                                         g_ref[...], TQ).astype(o_ref.dtype)


def _attn_meta_kernel(lam_ref, q_ref, km_ref, vm_ref, bmm_ref, g_ref, o_ref, p_buf, acc_ref):
    qz = _stack_components(q_ref[...])
    vmt = _values_t_ext(vm_ref[...])
    stats = _attn_init(MP, p_buf, acc_ref)
    s = jnp.dot(km_ref[...], qz, preferred_element_type=f32) + _both(bmm_ref[...])
    stats, p = _softmax_step(stats, s, _colmax(s), vmt, p_buf, acc_ref)
    p_buf[...] = p
    o_ref[...] = _attn_finish(stats, vmt, p_buf, acc_ref, lam_ref[0], g_ref[...],
                              MP).astype(o_ref.dtype)


def _attention(lam1, q, k, v, km, vm, bnear, bmeta, g_tile):
    nb, s_len, _ = k.shape
    smem = pl.BlockSpec(memory_space=pltpu.SMEM)
    return pl.pallas_call(
        _attn_kernel,
        grid=(nb, s_len // TQ),
        in_specs=[smem,
                  pl.BlockSpec((None, DA_WIDTH, TQ), lambda b, i: (b, 0, i)),
                  pl.BlockSpec((None, s_len, DA_WIDTH), lambda b, i: (b, 0, 0)),
                  pl.BlockSpec((None, DA_WIDTH, s_len), lambda b, i: (b, 0, 0)),
                  _const_spec((MP, DA_WIDTH)), _const_spec((DA_WIDTH, MP)),
                  _const_spec((DA_HEADS, 2, TK, TQ)), _const_spec((DA_HEADS, 2, MP, TQ)),
                  _const_spec((DA_HEAD_DIM, TQ))],
        out_specs=pl.BlockSpec((None, TQ, DA_WIDTH), lambda b, i: (b, i, 0)),
        out_shape=jax.ShapeDtypeStruct((nb, s_len, DA_WIDTH), bf16),
        scratch_shapes=([pltpu.VMEM((TK, 2 * TQ), f32)] * (2 * DA_HEADS)
                        + [pltpu.VMEM((TK, 2 * TQ), bf16)] * DA_HEADS
                        + [pltpu.VMEM((V_EXT, 2 * TQ), f32)] * DA_HEADS),
        compiler_params=pltpu.CompilerParams(
            dimension_semantics=("parallel", "arbitrary"),
            vmem_limit_bytes=VMEM_LIMIT),
        name="attn",
    )(lam1, q, k, v, km, vm, bnear, bmeta, g_tile)


def _attention_meta(lam1, qm, km, vm, bmm, g_tile):
    smem = pl.BlockSpec(memory_space=pltpu.SMEM)
    head = pl.BlockSpec((MP, DA_HEAD_DIM), lambda h: (0, h))
    return pl.pallas_call(
        _attn_meta_kernel,
        grid=(DA_HEADS,),
        in_specs=[smem, pl.BlockSpec((DA_HEAD_DIM, MP), lambda h: (h, 0)), head,
                  pl.BlockSpec((DA_HEAD_DIM, MP), lambda h: (h, 0)),
                  pl.BlockSpec((None, MP, MP), lambda h: (h, 0, 0)),
                  pl.BlockSpec((DA_HEAD_DIM, MP), lambda h: (0, 0))],
        out_specs=head,
        out_shape=jax.ShapeDtypeStruct((MP, DA_WIDTH), bf16),
        scratch_shapes=[pltpu.VMEM((MP, 2 * MP), bf16), pltpu.VMEM((V_EXT, 2 * MP), f32)],
        name="attn_meta",
    )(lam1, qm, km, vm, bmm, g_tile)


GDN_R = DN_HEADS * CHUNK
GDN_LEVELS = (2, 4, 8, 16, 32)
GDN_QKV = 3 * DN_WIDTH
GDN_NCH = GDN_ROWS // CHUNK
GDN_NSEQ = 4


def _gdn_consts():
    r = np.arange(GDN_R)[:, None]
    c = np.arange(GDN_R)[None, :]
    same = (r // CHUNK) == (c // CHUNK)
    bd = np.stack([same & (r >= c), same & (r > c), same]).astype(np.float32)
    i = np.arange(CHUNK)[:, None]
    j = np.arange(GDN_R)[None, :] % CHUNK
    lv = [i == j, (i > j) & ((i // 2) == (j // 2))]
    for s in GDN_LEVELS:
        lv.append(((i // (2 * s)) == (j // (2 * s))) & ((i & s) != 0) & ((j & s) == 0))
    cat = np.stack(lv).astype(np.float32)
    rr = np.arange(GDN_ROWS)[:, None]
    cc = np.arange(GDN_ROWS)[None, :]
    lcum = (((rr // CHUNK) == (cc // CHUNK)) & (rr >= cc)).astype(np.float32)
    sel = np.zeros((2, LANES, DN_HEADS * LANES), np.float32)
    for h in range(DN_HEADS):
        sel[0, h, h * LANES:(h + 1) * LANES] = 1.0
        sel[1, DN_HEADS + h, h * LANES:(h + 1) * LANES] = 1.0
    return bd, cat, lcum, sel


def _split3(x):
    hi = x.astype(bf16)
    r1 = x - hi.astype(f32)
    mid = r1.astype(bf16)
    lo = (r1 - mid.astype(f32)).astype(bf16)
    return hi, mid, lo


def _gdn_block(n_per_seq, xq, xk, xv, ba, z, s_refs, alog_ref, dtb_ref, ng_ref, bd_ref,
               cat_ref, lcum_ref, sel_ref):
    n_seq = len(s_refs)
    n = n_seq * n_per_seq
    seq_rows = n_per_seq * CHUNK
    heads = range(DN_HEADS)
    chunks = range(n)
    dot = functools.partial(jnp.dot, preferred_element_type=f32)

    def stack(a):
        return jnp.concatenate(
            [a[c * CHUNK:(c + 1) * CHUNK, h * LANES:(h + 1) * LANES]
             for c in chunks for h in heads], axis=0)

    qn_b, kn_b = stack(xq), stack(xk)
    qn, kn, vs = qn_b.astype(f32), kn_b.astype(f32), stack(xv).astype(f32)
    blk = lambda c: slice(c * GDN_R, (c + 1) * GDN_R)

    beta_t = _sigmoid(ba)
    xg = ba + dtb_ref[...]
    softplus = jnp.maximum(xg, 0.0) + jnp.log1p(jnp.exp(-jnp.abs(xg)))
    g_t = -jnp.exp(alog_ref[...]) * softplus
    lcum = lcum_ref[0:seq_rows, 0:seq_rows]
    g_parts = _split3(g_t)[:2]
    gcum = jnp.concatenate(
        [sum(dot(lcum, part[q * seq_rows:(q + 1) * seq_rows]) for part in g_parts)
         for q in range(n_seq)], axis=0)
    g_rep = stack(sum(dot(part, sel_ref[1]) for part in _split3(gcum)[:2]))
    b_rep = stack(sum(dot(part, sel_ref[0]) for part in _split3(beta_t)[:2]))
    g_end = jnp.concatenate(
        [jnp.broadcast_to(g_rep[(b + 1) * CHUNK - 1:(b + 1) * CHUNK, :], (CHUNK, LANES))
         for b in range(n * DN_HEADS)], axis=0)
    exp_g = jnp.exp(g_rep)
    rhs = jnp.concatenate([vs * b_rep, kn * (b_rep * exp_g)], axis=1).astype(bf16)
    q_g = qn * exp_g
    k_g = (kn * jnp.exp(g_end - g_rep)).astype(bf16)
    g_last = jnp.exp(g_end)

    tri = bd_ref[0] > 0.5
    block_b = bd_ref[2].astype(bf16)
    both = lambda a: jnp.concatenate([a, a], axis=1)
    decay, m_b, x_cat = [], [], []
    for c in chunks:
        g_row = g_rep[blk(c)].T[0:1, :]
        dec = jnp.exp(jnp.where(tri, both(g_rep[blk(c)]) - g_row, -jnp.inf))
        m = bd_ref[1] * both(b_rep[blk(c)]) * _dot_nt(kn_b[blk(c)], kn_b[blk(c)]) * dec
        m_cat = sum(m[h * CHUNK:(h + 1) * CHUNK] for h in heads)
        decay.append(dec)
        m_b.append(m.astype(bf16))
        x_cat.append(cat_ref[0] - m_cat * cat_ref[1])

    def to_bd(x):
        return jnp.concatenate([x.astype(bf16)] * DN_HEADS, axis=0) * block_b

    for lvl in range(len(GDN_LEVELS)):
        ys = [dot(x_cat[c].astype(bf16), m_b[c]) for c in chunks]
        zs = [dot(ys[c].astype(bf16), to_bd(x_cat[c])) for c in chunks]
        x_cat = [x_cat[c] - zs[c] * cat_ref[2 + lvl] for c in chunks]
    sol = [dot(to_bd(x_cat[c]), rhs[blk(c)]).astype(bf16) for c in chunks]
    a_qk = [(_dot_nt(qn_b[blk(c)], kn_b[blk(c)]) * decay[c]).astype(bf16) for c in chunks]
    a_uw = [dot(a_qk[c], sol[c]) for c in chunks]
    hrows = lambda h: slice(h * CHUNK, (h + 1) * CHUNK)
    k_uw = [[_dot_tn(k_g[blk(c)][hrows(h)], sol[c][hrows(h)]) for h in heads] for c in chunks]

    out_rows = [None] * n
    for step in range(n_per_seq):
        for q in range(n_seq):
            c = q * n_per_seq + step
            s_ref = s_refs[q]
            out_cols = []
            for h in heads:
                s_old = s_ref[h]
                s_b = s_old.astype(bf16)
                q_eff = (q_g[blk(c)][hrows(h)] - a_uw[c][hrows(h), DN_DV:]).astype(bf16)
                o = a_uw[c][hrows(h), :DN_DV] + dot(q_eff, s_b)
                s_ref[h] = (s_old * g_last[blk(c)][hrows(h)][0:1, :] + k_uw[c][h][:, :DN_DV]
                            - dot(k_uw[c][h][:, DN_DV:].astype(bf16), s_b))
                o = o * lax.rsqrt(jnp.mean(o * o, axis=-1, keepdims=True) + EPS) * ng_ref[...]
                zh = z[c * CHUNK:(c + 1) * CHUNK, h * DN_DV:(h + 1) * DN_DV].astype(f32)
                out_cols.append(o * (zh * _sigmoid(zh)))
            out_rows[c] = jnp.concatenate(out_cols, axis=1)
    return jnp.concatenate(out_rows, axis=0)


def _gdn_kernel(dq_ref, dk_ref, dv_ref, dz_ref, ba_ref, mq_ref, mk_ref, mv_ref, mz_ref, mba_ref,
                alog_ref, dtb_ref, ng_ref, bd_ref, cat_ref, lcum_ref, sel_ref,
                o_ref, om_ref, s_ref):
    consts = (alog_ref, dtb_ref, ng_ref, bd_ref, cat_ref, lcum_ref, sel_ref)
    s_refs = [s_ref.at[q] for q in range(GDN_NSEQ)]

    @pl.when(pl.program_id(1) == 0)
    def _():
        s_ref[0] = jnp.zeros(s_ref.shape[1:], f32)
        om_ref[...] = _gdn_block(1, mq_ref[...], mk_ref[...], mv_ref[...], mba_ref[...],
                                 mz_ref[...], s_refs[:1], *consts).astype(om_ref.dtype)
        for q in range(1, GDN_NSEQ):
            s_ref[q] = s_ref[0]

    merge = lambda ref: ref[...].reshape(GDN_NSEQ * GDN_ROWS, ref.shape[-1])
    out = _gdn_block(GDN_NCH, merge(dq_ref), merge(dk_ref), merge(dv_ref), merge(ba_ref),
                     merge(dz_ref), s_refs, *consts)
    o_ref[...] = out.reshape(o_ref.shape).astype(o_ref.dtype)


def _gdn(dq, dk, dv, dz, ba, mq, mk, mv, mz, mba, alog_row, dtb_row, ng_row):
    nb, s_len, _ = dq.shape
    assert nb % GDN_NSEQ == 0
    row = lambda w: pl.BlockSpec((GDN_NSEQ, GDN_ROWS, w), lambda b, t: (b, t, 0))
    mrow = lambda w: pl.BlockSpec((CHUNK, w), lambda b, t: (MP // CHUNK - 1, 0))
    bd, cat, lcum, sel = _gdn_consts()
    bd, cat = jnp.asarray(bd), jnp.asarray(cat)
    lcum, sel = jnp.asarray(lcum).astype(bf16), jnp.asarray(sel).astype(bf16)
    return pl.pallas_call(
        _gdn_kernel,
        grid=(nb // GDN_NSEQ, s_len // GDN_ROWS),
        in_specs=[row(DN_WIDTH)] * 4 + [row(LANES)] + [mrow(DN_WIDTH)] * 4 + [mrow(LANES)] + [
            _const_spec((1, LANES)), _const_spec((1, LANES)),
            _const_spec((1, DN_DV)), _const_spec(bd.shape), _const_spec(cat.shape),
            _const_spec(lcum.shape), _const_spec(sel.shape)],
        out_specs=[row(DN_WIDTH), pl.BlockSpec((None, CHUNK, DN_WIDTH), lambda b, t: (b, 0, 0))],
        out_shape=[jax.ShapeDtypeStruct((nb, s_len, DN_WIDTH), bf16),
                   jax.ShapeDtypeStruct((nb // GDN_NSEQ, CHUNK, DN_WIDTH), bf16)],
        scratch_shapes=[pltpu.VMEM((GDN_NSEQ, DN_HEADS, DN_DK, DN_DV), f32)],
        compiler_params=pltpu.CompilerParams(
            dimension_semantics=("parallel", "arbitrary"), vmem_limit_bytes=VMEM_LIMIT),
        name="gdn",
    )(dq, dk, dv, dz, ba, mq, mk, mv, mz, mba, alog_row, dtb_row, ng_row, bd, cat, lcum, sel)


def _mix_and_norm(x, oda, odn, wout_ref, g2_ref):
    mix = jnp.concatenate([oda, odn], axis=1)
    h2 = x + jnp.dot(mix, wout_ref[...], preferred_element_type=f32)
    u2 = h2 * lax.rsqrt(jnp.mean(h2 * h2, axis=-1, keepdims=True) + EPS) * g2_ref[...]
    return h2, u2.astype(bf16)


def _ffn_halo_kernel(x_ref, oda_ref, odn_ref, wout_ref, g2_ref, wup_ref, halo_ref):
    _, u2 = _mix_and_norm(x_ref[...], oda_ref[...], odn_ref[...], wout_ref, g2_ref)
    halo_ref[...] = jnp.dot(u2, wup_ref[...], preferred_element_type=f32)


def _ffn_kernel(x_ref, oda_ref, odn_ref, halo_ref, wout_ref, g2_ref, wup_ref, cw_ref, cb_ref,
                wdown_ref, gf_ref, o_ref, *scratch):
    tm = FFN_TM
    carries = scratch[0:FFN_TILES + 1]
    accs = scratch[FFN_TILES + 1:2 * FFN_TILES + 1]
    hbufs = scratch[2 * FFN_TILES + 1:]
    t = pl.program_id(1)

    @pl.when(t == 0)
    def _():
        carries[0][...] = halo_ref[...]

    @pl.when(t > 0)
    def _():
        carries[0][...] = carries[FFN_TILES][...]

    n_chunks = D_FF // FFN_CW
    stages = [(a, c) for a in range(FFN_TILES) for c in range(n_chunks)]
    col_pair = lambda c: (slice(c * FFN_CW, (c + 1) * FFN_CW),
                          slice(D_FF + c * FFN_CW, D_FF + (c + 1) * FFN_CW))
    buf_pair = lambda s: hbufs[2 * (s % FFN_NBUF):2 * (s % FFN_NBUF) + 2]
    rows = lambda a: slice(a * tm, (a + 1) * tm)
    normed = {}

    def up_part(s):
        a, c = stages[s]
        if c == 0:
            normed[a] = _mix_and_norm(x_ref[rows(a), :], oda_ref[rows(a), :],
                                      odn_ref[rows(a), :], wout_ref, g2_ref)
        u2 = normed[a][1]
        for cols, hbuf in zip(col_pair(c), buf_pair(s)):
            hbuf[0:SUBLANES, :] = carries[a][:, cols]
            hup = jnp.dot(u2, wup_ref[:, cols], preferred_element_type=f32)
            hbuf[SUBLANES:SUBLANES + tm, :] = hup
            carries[a + 1][:, cols] = hup[tm - SUBLANES:, :]

    def conv_part(cols, hbuf):
        y = cb_ref[:, cols]
        for d in range(FFN_CONV):
            y = y + cw_ref[FFN_CONV - 1 - d:FFN_CONV - d, cols] * hbuf[SUBLANES - d:SUBLANES - d + tm, :]
        return y

    for s in range(FFN_AHEAD):
        up_part(s)
    pending = []
    for s, (a, c) in enumerate(stages):
        if s + FFN_AHEAD < len(stages):
            up_part(s + FFN_AHEAD)
        gate, val = (conv_part(cols, hbuf) for cols, hbuf in zip(col_pair(c), buf_pair(s)))
        pending.append((gate * _sigmoid(gate) * val).astype(bf16))
        if len(pending) < FFN_DOWN_GROUP and c < n_chunks - 1:
            continue
        c0 = c + 1 - len(pending)
        act = pending[0] if len(pending) == 1 else jnp.concatenate(pending, axis=1)
        pending.clear()
        part = jnp.dot(act, wdown_ref[c0 * FFN_CW:(c + 1) * FFN_CW, :], preferred_element_type=f32)
        if c0 == 0:
            accs[a][...] = part
        else:
            accs[a][...] += part
        if c == n_chunks - 1:
            y = normed[a][0] + accs[a][...]
            o_ref[rows(a), :] = (y * lax.rsqrt(jnp.mean(y * y, axis=-1, keepdims=True) + EPS)
                                 * gf_ref[...])


def _ffn_halo(xm, odam, odnm, w_out, g2, w_up):
    return pl.pallas_call(
        _ffn_halo_kernel,
        out_shape=jax.ShapeDtypeStruct((HALO_ROWS, 2 * D_FF), f32),
        compiler_params=pltpu.CompilerParams(vmem_limit_bytes=VMEM_LIMIT),
        name="ffn_halo",
    )(xm, odam, odnm, w_out, g2, w_up)


def _ffn(x, oda, odn, halo, w_out, g2, w_up, conv_w, conv_b, w_down, gf):
    nb, s_len, _ = x.shape
    tm = FFN_TM
    step_rows = FFN_TILES * tm
    row = lambda w: pl.BlockSpec((None, step_rows, w), lambda b, t: (b, t, 0))
    return pl.pallas_call(
        _ffn_kernel,
        grid=(nb, s_len // step_rows),
        in_specs=[row(D_MODEL), row(DA_WIDTH), row(DN_WIDTH),
                  _const_spec((SUBLANES, 2 * D_FF)), _const_spec((D_MODEL, D_MODEL)),
                  _const_spec((1, D_MODEL)), _const_spec((D_MODEL, 2 * D_FF)),
                  _const_spec((FFN_CONV, 2 * D_FF)), _const_spec((1, 2 * D_FF)),
                  _const_spec((D_FF, D_MODEL)), _const_spec((1, D_MODEL))],
        out_specs=row(D_MODEL),
        out_shape=jax.ShapeDtypeStruct((nb, s_len, D_MODEL), f32),
        scratch_shapes=([pltpu.VMEM((SUBLANES, 2 * D_FF), f32)] * (FFN_TILES + 1)
                        + [pltpu.VMEM((tm, D_MODEL), f32)] * FFN_TILES
                        + [pltpu.VMEM((SUBLANES + tm, FFN_CW), f32)] * (2 * FFN_NBUF)),
        compiler_params=pltpu.CompilerParams(
            dimension_semantics=("parallel", "arbitrary"), vmem_limit_bytes=VMEM_LIMIT),
        name="ffn",
    )(x, oda, odn, halo, w_out, g2, w_up, conv_w, conv_b, w_down, gf)


def kernel(x, meta_tokens, rel_bias, norm1_g, w_in, da_lambda, da_subln_g, dn_conv_w, dn_A_log,
           dn_dt_bias, dn_norm_g, w_out, norm2_g, w_up, ffn_conv_w, ffn_conv_b, w_down,
           final_norm_g):
    nb, s_len, _ = x.shape
    assert s_len % TQ == 0 and s_len % GDN_ROWS == 0 and s_len % (FFN_TILES * FFN_TM) == 0
    assert s_len % PROJ_TM == 0 and s_len >= 2 * TK
    w_all = jnp.pad(w_in[0], ((0, 0), (0, LANES - 2 * DN_HEADS))).astype(bf16)
    w_out_b = w_out[0].astype(bf16)
    w_up_b = w_up[0].astype(bf16)
    w_down_b = w_down[0].astype(bf16)
    g1 = norm1_g[0].reshape(1, D_MODEL).astype(f32)
    g2 = norm2_g[0].reshape(1, D_MODEL).astype(f32)
    gf = final_norm_g.reshape(1, D_MODEL).astype(f32)
    meta_pad = jnp.pad(meta_tokens.astype(x.dtype), ((MP - N_META, 0), (0, 0)))

    bnear, bmeta, bmm, lam_tile = _bias_tiles(rel_bias, da_lambda[0])
    lam1 = lam_tile[0, :1]

    conv_w = dn_conv_w[0].astype(f32)
    no_halo = jnp.zeros((SUBLANES, GDN_QKV), f32)
    mq, mk, mv, mdq, mdk, mdv, mdz, mba, mtail = [
        a[0] for a in _proj(meta_pad[None], no_halo, g1, w_all, conv_w, MP)]
    q, k, v, dq, dk, dv, dz, ba, _ = _proj(x, mtail, g1, w_all, conv_w, PROJ_TM)

    subln = da_subln_g[0].astype(f32)
    o_da = _attention(lam1, q, k, v, mk, mv, bnear, bmeta,
                      jnp.broadcast_to(subln[:, None], (DA_HEAD_DIM, TQ)))
    o_da_m = _attention_meta(lam1, mq, mk, mv, bmm,
                             jnp.broadcast_to(subln[:, None], (DA_HEAD_DIM, MP)))

    gate_row = lambda p: jnp.pad(p[0].astype(f32), (DN_HEADS, LANES - 2 * DN_HEADS)).reshape(1, LANES)
    o_dn, o_dn_m = _gdn(dq, dk, dv, dz, ba, mdq, mdk, mdv, mdz, mba,
                        gate_row(dn_A_log), gate_row(dn_dt_bias),
                        dn_norm_g[0].reshape(1, DN_DV).astype(f32))

    halo = _ffn_halo(meta_pad[MP - HALO_ROWS:], o_da_m[MP - HALO_ROWS:],
                     o_dn_m[0, CHUNK - HALO_ROWS:], w_out_b, g2, w_up_b)[HALO_ROWS - SUBLANES:]
    return _ffn(x, o_da, o_dn, halo, w_out_b, g2, w_up_b, ffn_conv_w[0].astype(f32),
                ffn_conv_b[0].reshape(1, 2 * D_FF).astype(f32), w_down_b, gf)
```

```python
import functools
import math

import numpy as np
import jax
import jax.numpy as jnp
from jax import lax
from jax.experimental import pallas as pl
from jax.experimental.pallas import tpu as pltpu

f32 = jnp.float32
bf16 = jnp.bfloat16

D_MODEL = 1024
CHUNK = 64
N_META = 16
EPS = 1e-6
NEG_INF = -1e30
LAMBDA_INIT = 0.8 - 0.6 * math.exp(-0.3 * 0)
LOG2E = math.log2(math.e)

DA_HEADS = 4
DA_HEAD_DIM = 128
DA_HALF = 64
DA_WIDTH = DA_HEADS * DA_HEAD_DIM
DN_HEADS = 4
DN_DK = 128
DN_DV = 128
DN_WIDTH = DN_HEADS * DN_DV
DN_CONV = 4
N_BUCKETS = 32
MAX_DISTANCE = 128
D_FF = 2816
FFN_CONV = 3

N_MAIN = 7 * 512
LANES = 128
SUBLANES = 8
MP = 128
TQ = 256
TK = 256
V_EXT = DA_HEAD_DIM + 16
PROJ_TM = 512
GDN_ROWS = 256
FFN_TILES = 1
FFN_TM = 256
FFN_CW = 256
FFN_DOWN_GROUP = 4
FFN_AHEAD = 10
FFN_NBUF = FFN_AHEAD + 1
HALO_ROWS = 16
VMEM_LIMIT = 52 * 1024 * 1024


def _const_spec(shape):
    nd = len(shape)
    return pl.BlockSpec(shape, lambda *_: (0,) * nd, pipeline_mode=pl.Buffered(1))


def _prep_kernel(far_ref, table_ref, lam_in_ref, bnear_in, bmeta_in, bmm_in,
                 bnear_out, bmeta_out, bmm_out, lam_out):
    h = pl.program_id(0)
    c_far = table_ref[far_ref[0], h]

    def lookup(bkt):
        out = jnp.full(bkt.shape, NEG_INF, f32)
        for b in range(N_BUCKETS):
            out = jnp.where(bkt == b, (table_ref[b, h] - c_far) * LOG2E, out)
        return out

    for i in range(2):
        bnear_out[i] = lookup(bnear_in[i])
        bmeta_out[i] = lookup(bmeta_in[i])
    bmm_out[...] = lookup(bmm_in[...])
    lv = lam_in_ref[...]
    s1 = jnp.sum(lv[0:1] * lv[1:2], axis=-1, keepdims=True)
    s2 = jnp.sum(lv[2:3] * lv[3:4], axis=-1, keepdims=True)
    lam = jnp.exp(s1) - jnp.exp(s2) + LAMBDA_INIT
    lam_out[...] = jnp.broadcast_to(lam, lam_out.shape)


def _t5_bucket(rel):
    nb = N_BUCKETS // 2
    max_exact = nb // 2
    ret = jnp.where(rel > 0, nb, 0)
    n = jnp.abs(rel)
    nf = jnp.maximum(n, 1).astype(jnp.float32)
    large = max_exact + (jnp.log(nf / max_exact) / math.log(MAX_DISTANCE / max_exact)
                         * (nb - max_exact)).astype(jnp.int32)
    large = jnp.minimum(large, nb - 1)
    return ret + jnp.where(n < max_exact, n, large)


def _bias_tiles(rel_bias, da_lambda):
    r = jnp.arange(TK, dtype=jnp.int32)[:, None]
    c = jnp.arange(TQ, dtype=jnp.int32)[None, :]
    diag = jnp.where((r // CHUNK) <= (c // CHUNK), _t5_bucket(r - c), -1)
    prev = _t5_bucket(r - c - TK)
    bnear = jnp.stack([diag, prev]).astype(jnp.int32)
    far = _t5_bucket(jnp.full((1,), -(TK + 1), jnp.int32)).astype(jnp.int32)
    rm = jnp.arange(MP, dtype=jnp.int32)[:, None]
    valid = rm >= (MP - N_META)
    kpos = rm - (MP - N_META)
    m0 = jnp.where(valid, _t5_bucket(kpos - (N_META + c)), -1)
    m1 = jnp.where(valid, jnp.broadcast_to(far[0], (MP, TQ)), -1)
    bmeta = jnp.stack([m0, m1]).astype(jnp.int32)
    cm = jnp.arange(MP, dtype=jnp.int32)[None, :]
    bmm = jnp.where(valid, _t5_bucket(rm - cm), -1).astype(jnp.int32)

    smem = pl.BlockSpec(memory_space=pltpu.SMEM)
    return pl.pallas_call(
        _prep_kernel,
        grid=(DA_HEADS,),
        in_specs=[smem, smem,
                  pl.BlockSpec((4, DA_HALF), lambda h: (0, 0)),
                  pl.BlockSpec((2, TK, TQ), lambda h: (0, 0, 0)),
                  pl.BlockSpec((2, MP, TQ), lambda h: (0, 0, 0)),
                  pl.BlockSpec((MP, MP), lambda h: (0, 0))],
        out_specs=[pl.BlockSpec((None, 2, TK, TQ), lambda h: (h, 0, 0, 0)),
                   pl.BlockSpec((None, 2, MP, TQ), lambda h: (h, 0, 0, 0)),
                   pl.BlockSpec((None, MP, MP), lambda h: (h, 0, 0)),
                   pl.BlockSpec((SUBLANES, LANES), lambda h: (0, 0))],
        out_shape=[jax.ShapeDtypeStruct((DA_HEADS, 2, TK, TQ), f32),
                   jax.ShapeDtypeStruct((DA_HEADS, 2, MP, TQ), f32),
                   jax.ShapeDtypeStruct((DA_HEADS, MP, MP), f32),
                   jax.ShapeDtypeStruct((SUBLANES, LANES), f32)],
        name="prep",
    )(far, rel_bias.astype(f32), da_lambda.astype(f32), bnear, bmeta, bmm)


def _sigmoid(x):
    return 1.0 / (1.0 + jnp.exp(-x))


def _proj_kernel(x_ref, halo_ref, g_ref, w_ref, cw_ref, q_ref, k_ref, v_ref, dq_ref,
                 dk_ref, dv_ref, dz_ref, ba_ref, tail_ref, carry_ref, carry_next, u_ref):
    tm = x_ref.shape[0]
    t = pl.program_id(1)

    @pl.when(t == 0)
    def _():
        carry_ref[...] = halo_ref[...]

    @pl.when(t > 0)
    def _():
        carry_ref[...] = carry_next[...]

    x = x_ref[...]
    ms = jnp.mean(x * x, axis=-1, keepdims=True)
    u_ref[...] = (x * lax.rsqrt(ms + EPS) * g_ref[...]).astype(bf16)
    outs = (q_ref, k_ref, v_ref, dq_ref, dk_ref, dv_ref, dz_ref)
    n_out = len(outs)
    project = lambda j: jnp.dot(u_ref[...], w_ref[:, j * 512:(j + 1) * 512],
                                preferred_element_type=f32)

    def conv_silu(j, r):
        cols = slice((j - 3) * DN_WIDTH, (j - 2) * DN_WIDTH)
        carry_next[:, cols] = r[tm - SUBLANES:, :]
        ext = jnp.concatenate([carry_ref[:, cols], r], axis=0)
        y = cw_ref[0:1, cols] * ext
        for d in range(1, DN_CONV):
            y = cw_ref[d:d + 1, cols] * ext + pltpu.roll(y, 1, 0)
        y = y[SUBLANES:, :]
        return y * _sigmoid(y)

    def l2norm_heads(y, scale):
        parts = []
        for h in range(DN_HEADS):
            yh = y[:, h * DN_DK:(h + 1) * DN_DK]
            parts.append(yh * (lax.rsqrt(jnp.sum(yh * yh, axis=-1, keepdims=True) + EPS) * scale))
        return jnp.concatenate(parts, axis=1)

    order = (3, 0, 4, 1, 5, 2, 6)
    ahead = 2
    pending = [project(j) for j in order[:ahead]]
    for pos, j in enumerate(order):
        r = pending.pop(0)
        if pos + ahead < n_out:
            pending.append(project(order[pos + ahead]))
        elif pos + ahead == n_out:
            ba_ref[...] = jnp.dot(u_ref[...], w_ref[:, N_MAIN:N_MAIN + LANES],
                                  preferred_element_type=f32)
        if j == 0:
            r = r * (DA_HALF ** -0.5 * LOG2E)
        elif j in (3, 4):
            r = l2norm_heads(conv_silu(j, r), DN_DK ** -0.5 if j == 3 else 1.0)
        elif j == 5:
            r = conv_silu(j, r)
        if j in (0, 2):
            r = r.T
        outs[j][...] = r.astype(outs[j].dtype)
    tail_ref[...] = carry_next[...]


def _proj(x3, halo, g1, w_all, conv_w, tm):
    nb, rows, _ = x3.shape
    row_spec = lambda w: pl.BlockSpec((None, tm, w), lambda b, t: (b, t, 0))
    plain = jax.ShapeDtypeStruct((nb, rows, 512), bf16)
    transposed = jax.ShapeDtypeStruct((nb, 512, rows), bf16)
    outs = [transposed, plain, transposed] + [plain] * 4 + [
        jax.ShapeDtypeStruct((nb, rows, LANES), f32),
        jax.ShapeDtypeStruct((nb, SUBLANES, GDN_QKV), f32)]
    t_spec = pl.BlockSpec((None, 512, tm), lambda b, t: (b, 0, t))
    return pl.pallas_call(
        _proj_kernel,
        grid=(nb, rows // tm),
        in_specs=[row_spec(D_MODEL), _const_spec((SUBLANES, GDN_QKV)), _const_spec((1, D_MODEL)),
                  _const_spec((D_MODEL, N_MAIN + LANES)), _const_spec((DN_CONV, GDN_QKV))],
        out_specs=[t_spec, row_spec(512), t_spec]
                  + [row_spec(512)] * 4 + [row_spec(LANES),
                   pl.BlockSpec((None, SUBLANES, GDN_QKV), lambda b, t: (b, 0, 0))],
        out_shape=outs,
        scratch_shapes=([pltpu.VMEM((SUBLANES, GDN_QKV), f32)] * 2
                        + [pltpu.VMEM((tm, D_MODEL), bf16)]),
        compiler_params=pltpu.CompilerParams(
            dimension_semantics=("parallel", "arbitrary"), vmem_limit_bytes=VMEM_LIMIT),
        name="proj",
    )(x3, halo, g1, w_all, conv_w)


def _dot_nt(a, b):
    return lax.dot_general(a, b, (((1,), (1,)), ((), ())), preferred_element_type=f32)


def _dot_tn(a, b):
    return lax.dot_general(a, b, (((0,), (0,)), ((), ())), preferred_element_type=f32)


def _stack_components(qt):
    zero = jnp.zeros((DA_HALF, qt.shape[1]), qt.dtype)
    q0 = jnp.concatenate([qt[:DA_HALF], zero], axis=0)
    q1 = jnp.concatenate([zero, qt[DA_HALF:]], axis=0)
    return jnp.concatenate([q0, q1], axis=1)


def _colmax(s):
    return jnp.max(s, axis=0, keepdims=True)


def _values_t_ext(vt):
    return jnp.concatenate([vt, jnp.ones((V_EXT - DA_HEAD_DIM, vt.shape[1]), vt.dtype)], axis=0)


def _softmax_step(stats, s, smax, vt_pend, p_ref, acc_ref):
    m, alpha_pend = stats
    pv = jnp.dot(vt_pend, p_ref[...], preferred_element_type=f32)
    m_new = jnp.maximum(m, smax)
    alpha = jnp.exp2(m - m_new)
    p = jnp.exp2((s - m_new).astype(bf16))
    acc_ref[...] = alpha_pend * acc_ref[...] + pv
    return (m_new, alpha), p


def _attn_finish(stats, vt_pend, p_ref, acc_ref, lam, g, tq):
    m, alpha_pend = stats
    acc = alpha_pend * acc_ref[...] + jnp.dot(vt_pend, p_ref[...], preferred_element_type=f32)
    l = acc[DA_HEAD_DIM:DA_HEAD_DIM + 1, :]
    acc = acc[:DA_HEAD_DIM, :]
    o = acc[:, :tq] / l[:, :tq] - lam * (acc[:, tq:] / l[:, tq:])
    ms = jnp.mean(o * o, axis=0, keepdims=True)
    y = o * lax.rsqrt(ms + EPS) * g * (1.0 - LAMBDA_INIT)
    return y.T


def _attn_init(tq, p_ref, acc_ref):
    p_ref[...] = jnp.zeros(p_ref.shape, bf16)
    acc_ref[...] = jnp.zeros(acc_ref.shape, f32)
    return (jnp.full((1, 2 * tq), -3e38, f32), jnp.ones((1, 2 * tq), f32))


def _both(b):
    return jnp.concatenate([b, b], axis=1)


def _attn_kernel(lam_ref, q_ref, k_ref, v_ref, km_ref, vm_ref, bnear_ref, bmeta_ref, g_ref,
                 o_ref, *scratch):
    i = pl.program_id(1)
    n_far = jnp.maximum(i - 1, 0)
    heads = range(DA_HEADS)
    cols = lambda h: slice(h * DA_HEAD_DIM, (h + 1) * DA_HEAD_DIM)
    qz = [_stack_components(q_ref[cols(h), :]) for h in heads]

    def scores(h, j):
        start = pl.multiple_of(j * TK, TK)
        return jnp.dot(k_ref[pl.ds(start, TK), cols(h)], qz[h], preferred_element_type=f32)

    s_even = scratch[0:DA_HEADS]
    s_odd = scratch[DA_HEADS:2 * DA_HEADS]
    p_bufs = scratch[2 * DA_HEADS:3 * DA_HEADS]
    accs = scratch[3 * DA_HEADS:4 * DA_HEADS]

    def values(h, j):
        start = pl.multiple_of(jnp.maximum(j, 0) * TK, TK)
        return _values_t_ext(v_ref[cols(h), pl.ds(start, TK)])

    def consume(j, bufs, stats, smax):
        new_stats = []
        for h in heads:
            st, p = _softmax_step(stats[h], bufs[h][...], smax[h], values(h, j - 1),
                                  p_bufs[h], accs[h])
            p_bufs[h][...] = p
            new_stats.append(st)
        return tuple(new_stats)

    def produce(j, bufs):
        new_smax = []
        for h in heads:
            s_next = scores(h, j)
            bufs[h][...] = s_next
            new_smax.append(_colmax(s_next))
        return tuple(new_smax)

    stats = tuple(_attn_init(TQ, p_bufs[h], accs[h]) for h in heads)
    smax_even = produce(0, s_even)
    smax_odd = produce(1, s_odd)

    def pair_body(jp, carry):
        stats, smax_even, smax_odd = carry
        j = 2 * jp
        stats = consume(j, s_even, stats, smax_even)
        smax_even = produce(j + 2, s_even)
        stats = consume(j + 1, s_odd, stats, smax_odd)
        smax_odd = produce(j + 3, s_odd)
        return stats, smax_even, smax_odd

    n_pairs = n_far // 2
    stats, smax_even, _ = lax.fori_loop(0, n_pairs, pair_body, (stats, smax_even, smax_odd))

    def single_body(_, stats):
        stats = consume(2 * n_pairs, s_even, stats, smax_even)
        for h in heads:
            s_even[h][...] = s_odd[h][...]
        return stats

    stats = lax.fori_loop(0, n_far - 2 * n_pairs, single_body, stats)
    gate = jnp.where(i >= 1, 0.0, NEG_INF).astype(f32)
    for h in heads:
        p_h, acc_h = p_bufs[h], accs[h]
        s = s_even[h][...] + _both(bnear_ref[h, 1] + gate)
        st, p = _softmax_step(stats[h], s, _colmax(s), values(h, n_far - 1), p_h, acc_h)
        p_h[...] = p
        s = scores(h, i) + _both(bnear_ref[h, 0])
        st, p = _softmax_step(st, s, _colmax(s), values(h, i - 1), p_h, acc_h)
        p_h[...] = p
        s = (jnp.dot(km_ref[:, cols(h)], qz[h], preferred_element_type=f32)
             + _both(bmeta_ref[h, jnp.minimum(i, 1)]))
        st, p = _softmax_step(st, s, _colmax(s), values(h, i), p_h, acc_h)
        p_meta = p_h.at[0:MP, :]
        p_meta[...] = p
        o_ref[:, cols(h)] = _attn_finish(st, _values_t_ext(vm_ref[cols(h), :]), p_meta, acc_h,
                                         lam_ref[0],
                                         g_ref[...], TQ).astype(o_ref.dtype)


def _attn_meta_kernel(lam_ref, q_ref, km_ref, vm_ref, bmm_ref, g_ref, o_ref, p_buf, acc_ref):
    qz = _stack_components(q_ref[...])
    vmt = _values_t_ext(vm_ref[...])
    stats = _attn_init(MP, p_buf, acc_ref)
    s = jnp.dot(km_ref[...], qz, preferred_element_type=f32) + _both(bmm_ref[...])
    stats, p = _softmax_step(stats, s, _colmax(s), vmt, p_buf, acc_ref)
    p_buf[...] = p
    o_ref[...] = _attn_finish(stats, vmt, p_buf, acc_ref, lam_ref[0], g_ref[...],
                              MP).astype(o_ref.dtype)


def _attention(lam1, q, k, v, km, vm, bnear, bmeta, g_tile):
    nb, s_len, _ = k.shape
    smem = pl.BlockSpec(memory_space=pltpu.SMEM)
    return pl.pallas_call(
        _attn_kernel,
        grid=(nb, s_len // TQ),
        in_specs=[smem,
                  pl.BlockSpec((None, DA_WIDTH, TQ), lambda b, i: (b, 0, i)),
                  pl.BlockSpec((None, s_len, DA_WIDTH), lambda b, i: (b, 0, 0)),
                  pl.BlockSpec((None, DA_WIDTH, s_len), lambda b, i: (b, 0, 0)),
                  _const_spec((MP, DA_WIDTH)), _const_spec((DA_WIDTH, MP)),
                  _const_spec((DA_HEADS, 2, TK, TQ)), _const_spec((DA_HEADS, 2, MP, TQ)),
                  _const_spec((DA_HEAD_DIM, TQ))],
        out_specs=pl.BlockSpec((None, TQ, DA_WIDTH), lambda b, i: (b, i, 0)),
        out_shape=jax.ShapeDtypeStruct((nb, s_len, DA_WIDTH), bf16),
        scratch_shapes=([pltpu.VMEM((TK, 2 * TQ), f32)] * (2 * DA_HEADS)
                        + [pltpu.VMEM((TK, 2 * TQ), bf16)] * DA_HEADS
                        + [pltpu.VMEM((V_EXT, 2 * TQ), f32)] * DA_HEADS),
        compiler_params=pltpu.CompilerParams(
            dimension_semantics=("parallel", "arbitrary"),
            vmem_limit_bytes=VMEM_LIMIT),
        name="attn",
    )(lam1, q, k, v, km, vm, bnear, bmeta, g_tile)


def _attention_meta(lam1, qm, km, vm, bmm, g_tile):
    smem = pl.BlockSpec(memory_space=pltpu.SMEM)
    head = pl.BlockSpec((MP, DA_HEAD_DIM), lambda h: (0, h))
    return pl.pallas_call(
        _attn_meta_kernel,
        grid=(DA_HEADS,),
        in_specs=[smem, pl.BlockSpec((DA_HEAD_DIM, MP), lambda h: (h, 0)), head,
                  pl.BlockSpec((DA_HEAD_DIM, MP), lambda h: (h, 0)),
                  pl.BlockSpec((None, MP, MP), lambda h: (h, 0, 0)),
                  pl.BlockSpec((DA_HEAD_DIM, MP), lambda h: (0, 0))],
        out_specs=head,
        out_shape=jax.ShapeDtypeStruct((MP, DA_WIDTH), bf16),
        scratch_shapes=[pltpu.VMEM((MP, 2 * MP), bf16), pltpu.VMEM((V_EXT, 2 * MP), f32)],
        name="attn_meta",
    )(lam1, qm, km, vm, bmm, g_tile)


GDN_R = DN_HEADS * CHUNK
GDN_LEVELS = (2, 4, 8, 16, 32)
GDN_QKV = 3 * DN_WIDTH
GDN_NCH = GDN_ROWS // CHUNK
GDN_NSEQ = 4


def _gdn_consts():
    r = np.arange(GDN_R)[:, None]
    c = np.arange(GDN_R)[None, :]
    same = (r // CHUNK) == (c // CHUNK)
    bd = np.stack([same & (r >= c), same & (r > c), same]).astype(np.float32)
    i = np.arange(CHUNK)[:, None]
    j = np.arange(GDN_R)[None, :] % CHUNK
    lv = [i == j, (i > j) & ((i // 2) == (j // 2))]
    for s in GDN_LEVELS:
        lv.append(((i // (2 * s)) == (j // (2 * s))) & ((i & s) != 0) & ((j & s) == 0))
    cat = np.stack(lv).astype(np.float32)
    rr = np.arange(GDN_ROWS)[:, None]
    cc = np.arange(GDN_ROWS)[None, :]
    lcum = (((rr // CHUNK) == (cc // CHUNK)) & (rr >= cc)).astype(np.float32)
    sel = np.zeros((2, LANES, DN_HEADS * LANES), np.float32)
    for h in range(DN_HEADS):
        sel[0, h, h * LANES:(h + 1) * LANES] = 1.0
        sel[1, DN_HEADS + h, h * LANES:(h + 1) * LANES] = 1.0
    return bd, cat, lcum, sel


def _split3(x):
    hi = x.astype(bf16)
    r1 = x - hi.astype(f32)
    mid = r1.astype(bf16)
    lo = (r1 - mid.astype(f32)).astype(bf16)
    return hi, mid, lo


def _gdn_block(n_per_seq, xq, xk, xv, ba, z, s_refs, alog_ref, dtb_ref, ng_ref, bd_ref,
               cat_ref, lcum_ref, sel_ref):
    n_seq = len(s_refs)
    n = n_seq * n_per_seq
    seq_rows = n_per_seq * CHUNK
    heads = range(DN_HEADS)
    chunks = range(n)
    dot = functools.partial(jnp.dot, preferred_element_type=f32)

    def stack(a):
        return jnp.concatenate(
            [a[c * CHUNK:(c + 1) * CHUNK, h * LANES:(h + 1) * LANES]
             for c in chunks for h in heads], axis=0)

    qn_b, kn_b = stack(xq), stack(xk)
    qn, kn, vs = qn_b.astype(f32), kn_b.astype(f32), stack(xv).astype(f32)
    blk = lambda c: slice(c * GDN_R, (c + 1) * GDN_R)

    beta_t = _sigmoid(ba)
    xg = ba + dtb_ref[...]
    softplus = jnp.maximum(xg, 0.0) + jnp.log1p(jnp.exp(-jnp.abs(xg)))
    g_t = -jnp.exp(alog_ref[...]) * softplus
    lcum = lcum_ref[0:seq_rows, 0:seq_rows]
    g_parts = _split3(g_t)[:2]
    gcum = jnp.concatenate(
        [sum(dot(lcum, part[q * seq_rows:(q + 1) * seq_rows]) for part in g_parts)
         for q in range(n_seq)], axis=0)
    g_rep = stack(sum(dot(part, sel_ref[1]) for part in _split3(gcum)[:2]))
    b_rep = stack(sum(dot(part, sel_ref[0]) for part in _split3(beta_t)[:2]))
    g_end = jnp.concatenate(
        [jnp.broadcast_to(g_rep[(b + 1) * CHUNK - 1:(b + 1) * CHUNK, :], (CHUNK, LANES))
         for b in range(n * DN_HEADS)], axis=0)
    exp_g = jnp.exp(g_rep)
    rhs = jnp.concatenate([vs * b_rep, kn * (b_rep * exp_g)], axis=1).astype(bf16)
    q_g = qn * exp_g
    k_g = (kn * jnp.exp(g_end - g_rep)).astype(bf16)
    g_last = jnp.exp(g_end)

    tri = bd_ref[0] > 0.5
    block_b = bd_ref[2].astype(bf16)
    both = lambda a: jnp.concatenate([a, a], axis=1)
    decay, m_b, x_cat = [], [], []
    for c in chunks:
        g_row = g_rep[blk(c)].T[0:1, :]
        dec = jnp.exp(jnp.where(tri, both(g_rep[blk(c)]) - g_row, -jnp.inf))
        m = bd_ref[1] * both(b_rep[blk(c)]) * _dot_nt(kn_b[blk(c)], kn_b[blk(c)]) * dec
        m_cat = sum(m[h * CHUNK:(h + 1) * CHUNK] for h in heads)
        decay.append(dec)
        m_b.append(m.astype(bf16))
        x_cat.append(cat_ref[0] - m_cat * cat_ref[1])

    def to_bd(x):
        return jnp.concatenate([x.astype(bf16)] * DN_HEADS, axis=0) * block_b

    for lvl in range(len(GDN_LEVELS)):
        ys = [dot(x_cat[c].astype(bf16), m_b[c]) for c in chunks]
        zs = [dot(ys[c].astype(bf16), to_bd(x_cat[c])) for c in chunks]
        x_cat = [x_cat[c] - zs[c] * cat_ref[2 + lvl] for c in chunks]
    sol = [dot(to_bd(x_cat[c]), rhs[blk(c)]).astype(bf16) for c in chunks]
    a_qk = [(_dot_nt(qn_b[blk(c)], kn_b[blk(c)]) * decay[c]).astype(bf16) for c in chunks]
    a_uw = [dot(a_qk[c], sol[c]) for c in chunks]
    hrows = lambda h: slice(h * CHUNK, (h + 1) * CHUNK)
    k_uw = [[_dot_tn(k_g[blk(c)][hrows(h)], sol[c][hrows(h)]) for h in heads] for c in chunks]

    out_rows = [None] * n
    for step in range(n_per_seq):
        for q in range(n_seq):
            c = q * n_per_seq + step
            s_ref = s_refs[q]
            out_cols = []
            for h in heads:
                s_old = s_ref[h]
                s_b = s_old.astype(bf16)
                q_eff = (q_g[blk(c)][hrows(h)] - a_uw[c][hrows(h), DN_DV:]).astype(bf16)
                o = a_uw[c][hrows(h), :DN_DV] + dot(q_eff, s_b)
                s_ref[h] = (s_old * g_last[blk(c)][hrows(h)][0:1, :] + k_uw[c][h][:, :DN_DV]
                            - dot(k_uw[c][h][:, DN_DV:].astype(bf16), s_b))
                o = o * lax.rsqrt(jnp.mean(o * o, axis=-1, keepdims=True) + EPS) * ng_ref[...]
                zh = z[c * CHUNK:(c + 1) * CHUNK, h * DN_DV:(h + 1) * DN_DV].astype(f32)
                out_cols.append(o * (zh * _sigmoid(zh)))
            out_rows[c] = jnp.concatenate(out_cols, axis=1)
    return jnp.concatenate(out_rows, axis=0)


def _gdn_kernel(dq_ref, dk_ref, dv_ref, dz_ref, ba_ref, mq_ref, mk_ref, mv_ref, mz_ref, mba_ref,
                alog_ref, dtb_ref, ng_ref, bd_ref, cat_ref, lcum_ref, sel_ref,
                o_ref, om_ref, s_ref):
    consts = (alog_ref, dtb_ref, ng_ref, bd_ref, cat_ref, lcum_ref, sel_ref)
    s_refs = [s_ref.at[q] for q in range(GDN_NSEQ)]

    @pl.when(pl.program_id(1) == 0)
    def _():
        s_ref[0] = jnp.zeros(s_ref.shape[1:], f32)
        om_ref[...] = _gdn_block(1, mq_ref[...], mk_ref[...], mv_ref[...], mba_ref[...],
                                 mz_ref[...], s_refs[:1], *consts).astype(om_ref.dtype)
        for q in range(1, GDN_NSEQ):
            s_ref[q] = s_ref[0]

    merge = lambda ref: ref[...].reshape(GDN_NSEQ * GDN_ROWS, ref.shape[-1])
    out = _gdn_block(GDN_NCH, merge(dq_ref), merge(dk_ref), merge(dv_ref), merge(ba_ref),
                     merge(dz_ref), s_refs, *consts)
    o_ref[...] = out.reshape(o_ref.shape).astype(o_ref.dtype)


def _gdn(dq, dk, dv, dz, ba, mq, mk, mv, mz, mba, alog_row, dtb_row, ng_row):
    nb, s_len, _ = dq.shape
    assert nb % GDN_NSEQ == 0
    row = lambda w: pl.BlockSpec((GDN_NSEQ, GDN_ROWS, w), lambda b, t: (b, t, 0))
    mrow = lambda w: pl.BlockSpec((CHUNK, w), lambda b, t: (MP // CHUNK - 1, 0))
    bd, cat, lcum, sel = _gdn_consts()
    bd, cat = jnp.asarray(bd), jnp.asarray(cat)
    lcum, sel = jnp.asarray(lcum).astype(bf16), jnp.asarray(sel).astype(bf16)
    return pl.pallas_call(
        _gdn_kernel,
        grid=(nb // GDN_NSEQ, s_len // GDN_ROWS),
        in_specs=[row(DN_WIDTH)] * 4 + [row(LANES)] + [mrow(DN_WIDTH)] * 4 + [mrow(LANES)] + [
            _const_spec((1, LANES)), _const_spec((1, LANES)),
            _const_spec((1, DN_DV)), _const_spec(bd.shape), _const_spec(cat.shape),
            _const_spec(lcum.shape), _const_spec(sel.shape)],
        out_specs=[row(DN_WIDTH), pl.BlockSpec((None, CHUNK, DN_WIDTH), lambda b, t: (b, 0, 0))],
        out_shape=[jax.ShapeDtypeStruct((nb, s_len, DN_WIDTH), bf16),
                   jax.ShapeDtypeStruct((nb // GDN_NSEQ, CHUNK, DN_WIDTH), bf16)],
        scratch_shapes=[pltpu.VMEM((GDN_NSEQ, DN_HEADS, DN_DK, DN_DV), f32)],
        compiler_params=pltpu.CompilerParams(
            dimension_semantics=("parallel", "arbitrary"), vmem_limit_bytes=VMEM_LIMIT),
        name="gdn",
    )(dq, dk, dv, dz, ba, mq, mk, mv, mz, mba, alog_row, dtb_row, ng_row, bd, cat, lcum, sel)


def _mix_and_norm(x, oda, odn, wout_ref, g2_ref):
    mix = jnp.concatenate([oda, odn], axis=1)
    h2 = x + jnp.dot(mix, wout_ref[...], preferred_element_type=f32)
    u2 = h2 * lax.rsqrt(jnp.mean(h2 * h2, axis=-1, keepdims=True) + EPS) * g2_ref[...]
    return h2, u2.astype(bf16)


def _ffn_halo_kernel(x_ref, oda_ref, odn_ref, wout_ref, g2_ref, wup_ref, halo_ref):
    _, u2 = _mix_and_norm(x_ref[...], oda_ref[...], odn_ref[...], wout_ref, g2_ref)
    halo_ref[...] = jnp.dot(u2, wup_ref[...], preferred_element_type=f32)


def _ffn_kernel(x_ref, oda_ref, odn_ref, halo_ref, wout_ref, g2_ref, wup_ref, cw_ref, cb_ref,
                wdown_ref, gf_ref, o_ref, *scratch):
    tm = FFN_TM
    carries = scratch[0:FFN_TILES + 1]
    accs = scratch[FFN_TILES + 1:2 * FFN_TILES + 1]
    hbufs = scratch[2 * FFN_TILES + 1:]
    t = pl.program_id(1)

    @pl.when(t == 0)
    def _():
        carries[0][...] = halo_ref[...]

    @pl.when(t > 0)
    def _():
        carries[0][...] = carries[FFN_TILES][...]

    n_chunks = D_FF // FFN_CW
    stages = [(a, c) for a in range(FFN_TILES) for c in range(n_chunks)]
    col_pair = lambda c: (slice(c * FFN_CW, (c + 1) * FFN_CW),
                          slice(D_FF + c * FFN_CW, D_FF + (c + 1) * FFN_CW))
    buf_pair = lambda s: hbufs[2 * (s % FFN_NBUF):2 * (s % FFN_NBUF) + 2]
    rows = lambda a: slice(a * tm, (a + 1) * tm)
    normed = {}

    def up_part(s):
        a, c = stages[s]
        if c == 0:
            normed[a] = _mix_and_norm(x_ref[rows(a), :], oda_ref[rows(a), :],
                                      odn_ref[rows(a), :], wout_ref, g2_ref)
        u2 = normed[a][1]
        for cols, hbuf in zip(col_pair(c), buf_pair(s)):
            hbuf[0:SUBLANES, :] = carries[a][:, cols]
            hup = jnp.dot(u2, wup_ref[:, cols], preferred_element_type=f32)
            hbuf[SUBLANES:SUBLANES + tm, :] = hup
            carries[a + 1][:, cols] = hup[tm - SUBLANES:, :]

    def conv_part(cols, hbuf):
        y = cb_ref[:, cols]
        for d in range(FFN_CONV):
            y = y + cw_ref[FFN_CONV - 1 - d:FFN_CONV - d, cols] * hbuf[SUBLANES - d:SUBLANES - d + tm, :]
        return y

    for s in range(FFN_AHEAD):
        up_part(s)
    pending = []
    for s, (a, c) in enumerate(stages):
        if s + FFN_AHEAD < len(stages):
            up_part(s + FFN_AHEAD)
        gate, val = (conv_part(cols, hbuf) for cols, hbuf in zip(col_pair(c), buf_pair(s)))
        pending.append((gate * _sigmoid(gate) * val).astype(bf16))
        if len(pending) < FFN_DOWN_GROUP and c < n_chunks - 1:
            continue
        c0 = c + 1 - len(pending)
        act = pending[0] if len(pending) == 1 else jnp.concatenate(pending, axis=1)
        pending.clear()
        part = jnp.dot(act, wdown_ref[c0 * FFN_CW:(c + 1) * FFN_CW, :], preferred_element_type=f32)
        if c0 == 0:
            accs[a][...] = part
        else:
            accs[a][...] += part
        if c == n_chunks - 1:
            y = normed[a][0] + accs[a][...]
            o_ref[rows(a), :] = (y * lax.rsqrt(jnp.mean(y * y, axis=-1, keepdims=True) + EPS)
                                 * gf_ref[...])


def _ffn_halo(xm, odam, odnm, w_out, g2, w_up):
    return pl.pallas_call(
        _ffn_halo_kernel,
        out_shape=jax.ShapeDtypeStruct((HALO_ROWS, 2 * D_FF), f32),
        compiler_params=pltpu.CompilerParams(vmem_limit_bytes=VMEM_LIMIT),
        name="ffn_halo",
    )(xm, odam, odnm, w_out, g2, w_up)


def _ffn(x, oda, odn, halo, w_out, g2, w_up, conv_w, conv_b, w_down, gf):
    nb, s_len, _ = x.shape
    tm = FFN_TM
    step_rows = FFN_TILES * tm
    row = lambda w: pl.BlockSpec((None, step_rows, w), lambda b, t: (b, t, 0))
    return pl.pallas_call(
        _ffn_kernel,
        grid=(nb, s_len // step_rows),
        in_specs=[row(D_MODEL), row(DA_WIDTH), row(DN_WIDTH),
                  _const_spec((SUBLANES, 2 * D_FF)), _const_spec((D_MODEL, D_MODEL)),
                  _const_spec((1, D_MODEL)), _const_spec((D_MODEL, 2 * D_FF)),
                  _const_spec((FFN_CONV, 2 * D_FF)), _const_spec((1, 2 * D_FF)),
                  _const_spec((D_FF, D_MODEL)), _const_spec((1, D_MODEL))],
        out_specs=row(D_MODEL),
        out_shape=jax.ShapeDtypeStruct((nb, s_len, D_MODEL), f32),
        scratch_shapes=([pltpu.VMEM((SUBLANES, 2 * D_FF), f32)] * (FFN_TILES + 1)
                        + [pltpu.VMEM((tm, D_MODEL), f32)] * FFN_TILES
                        + [pltpu.VMEM((SUBLANES + tm, FFN_CW), f32)] * (2 * FFN_NBUF)),
        compiler_params=pltpu.CompilerParams(
            dimension_semantics=("parallel", "arbitrary"), vmem_limit_bytes=VMEM_LIMIT),
        name="ffn",
    )(x, oda, odn, halo, w_out, g2, w_up, conv_w, conv_b, w_down, gf)


def kernel(x, meta_tokens, rel_bias, norm1_g, w_in, da_lambda, da_subln_g, dn_conv_w, dn_A_log,
           dn_dt_bias, dn_norm_g, w_out, norm2_g, w_up, ffn_conv_w, ffn_conv_b, w_down,
           final_norm_g):
    nb, s_len, _ = x.shape
    assert s_len % TQ == 0 and s_len % GDN_ROWS == 0 and s_len % (FFN_TILES * FFN_TM) == 0
    assert s_len % PROJ_TM == 0 and s_len >= 2 * TK
    w_all = jnp.pad(w_in[0], ((0, 0), (0, LANES - 2 * DN_HEADS))).astype(bf16)
    w_out_b = w_out[0].astype(bf16)
    w_up_b = w_up[0].astype(bf16)
    w_down_b = w_down[0].astype(bf16)
    g1 = norm1_g[0].reshape(1, D_MODEL).astype(f32)
    g2 = norm2_g[0].reshape(1, D_MODEL).astype(f32)
    gf = final_norm_g.reshape(1, D_MODEL).astype(f32)
    meta_pad = jnp.pad(meta_tokens.astype(x.dtype), ((MP - N_META, 0), (0, 0)))

    bnear, bmeta, bmm, lam_tile = _bias_tiles(rel_bias, da_lambda[0])
    lam1 = lam_tile[0, :1]

    conv_w = dn_conv_w[0].astype(f32)
    no_halo = jnp.zeros((SUBLANES, GDN_QKV), f32)
    mq, mk, mv, mdq, mdk, mdv, mdz, mba, mtail = [
        a[0] for a in _proj(meta_pad[None], no_halo, g1, w_all, conv_w, MP)]
    q, k, v, dq, dk, dv, dz, ba, _ = _proj(x, mtail, g1, w_all, conv_w, PROJ_TM)

    subln = da_subln_g[0].astype(f32)
    o_da = _attention(lam1, q, k, v, mk, mv, bnear, bmeta,
                      jnp.broadcast_to(subln[:, None], (DA_HEAD_DIM, TQ)))
    o_da_m = _attention_meta(lam1, mq, mk, mv, bmm,
                             jnp.broadcast_to(subln[:, None], (DA_HEAD_DIM, MP)))

    gate_row = lambda p: jnp.pad(p[0].astype(f32), (DN_HEADS, LANES - 2 * DN_HEADS)).reshape(1, LANES)
    o_dn, o_dn_m = _gdn(dq, dk, dv, dz, ba, mdq, mdk, mdv, mdz, mba,
                        gate_row(dn_A_log), gate_row(dn_dt_bias),
                        dn_norm_g[0].reshape(1, DN_DV).astype(f32))

    halo = _ffn_halo(meta_pad[MP - HALO_ROWS:], o_da_m[MP - HALO_ROWS:],
                     o_dn_m[0, CHUNK - HALO_ROWS:], w_out_b, g2, w_up_b)[HALO_ROWS - SUBLANES:]
    return _ffn(x, o_da, o_dn, halo, w_out_b, g2, w_up_b, ffn_conv_w[0].astype(f32),
                ffn_conv_b[0].reshape(1, 2 * D_FF).astype(f32), w_down_b, gf)
```
